```python
import jax, jax.numpy as jnp
from jax import lax
import numpy as np

D_MODEL = 1024
BATCH = 2
SEQ = 8192
DEPTH = 1

GRID_W = 64
CTX_LEN = 256
NA_HEADS = 8
NA_HEAD_DIM = 64
NA_WIN_ROWS = 8
NA_WIN_COLS = 16
GLA_HEADS = 4
GLA_DK = 64
GLA_DV = 128
GLA_GATE_RANK = 16
GLA_GATE_TAU = 16.0
GLA_LOG_ALPHA_MIN = -1.0
GLA_CHUNK = 64
ROPE_THETA = 10000.0
D_FF = ((8 * D_MODEL + 3 * 256 - 1) // (3 * 256)) * 256
N_MOD = 6
NORM_EPS = 1e-6
NEG_INF = -1e30
NA_WIDTH = NA_HEADS * NA_HEAD_DIM
GLA_QK_WIDTH = GLA_HEADS * GLA_DK
GLA_V_WIDTH = GLA_HEADS * GLA_DV
IN_SPLITS = (NA_WIDTH, NA_WIDTH, NA_WIDTH, GLA_QK_WIDTH, GLA_QK_WIDTH, GLA_V_WIDTH, GLA_V_WIDTH,
             2 * GLA_GATE_RANK, D_MODEL, D_MODEL)
IN_OFFSETS = tuple(int(o) for o in np.cumsum(IN_SPLITS)[:-1])
D_IN = sum(IN_SPLITS)

kernel_name = "hybrid_na_gla_diffusion_block"


def rmsnorm(x, g):
    xf = x.astype(jnp.float32)
    y = xf * lax.rsqrt(jnp.mean(xf * xf, axis=-1, keepdims=True) + NORM_EPS)
    return (y * g.astype(jnp.float32)).astype(x.dtype)


def modulate(h, shift, scale):
    return h * (1 + scale) + shift


def heads(t, n):
    return t.reshape(*t.shape[:-1], n, -1)


def flip(t):
    return jnp.flip(t, axis=1)


def axial_rope(x, row_pos, col_pos):
    half = x.shape[-1] // 2
    n = half // 2
    freqs = ROPE_THETA ** (-jnp.arange(n, dtype=jnp.float32) / n)

    def rotate(xa, pos):
        ang = pos.astype(jnp.float32)[:, None] * freqs
        cos = jnp.cos(ang)[None, :, None, :]
        sin = jnp.sin(ang)[None, :, None, :]
        x1 = xa[..., :n].astype(jnp.float32)
        x2 = xa[..., n:].astype(jnp.float32)
        return jnp.concatenate([x1 * cos - x2 * sin, x1 * sin + x2 * cos], axis=-1)

    out = jnp.concatenate([rotate(x[..., :half], row_pos), rotate(x[..., half:], col_pos)], axis=-1)
    return out.astype(x.dtype)


def log_decay(z_lr, w_alpha, b_alpha):
    B, L, _ = z_lr.shape
    z = jnp.einsum('bldr,drk->bldk', z_lr.reshape(B, L, 2, GLA_GATE_RANK), w_alpha) + b_alpha
    la = jnp.maximum(jax.nn.log_sigmoid(z.astype(jnp.float32)) / GLA_GATE_TAU, GLA_LOG_ALPHA_MIN)
    la = la.reshape(B, L, 2, GLA_HEADS, GLA_DK)
    return la[:, :, 0], la[:, :, 1]


def gla_chunked(q, k, v, log_a, s0):
    B, L, H, dk = q.shape
    dv = v.shape[-1]
    n = L // GLA_CHUNK
    f32 = jnp.float32
    qc = q.astype(f32).reshape(B, n, GLA_CHUNK, H, dk)
    kc = k.astype(f32).reshape(B, n, GLA_CHUNK, H, dk)
    vc = v.astype(f32).reshape(B, n, GLA_CHUNK, H, dv)
    b = jnp.cumsum(log_a.astype(f32).reshape(B, n, GLA_CHUNK, H, dk), axis=2)
    b_last = b[:, :, -1:]
    q_dec = qc * jnp.exp(b)
    k_inv = kc * jnp.exp(-b)
    k_end = kc * jnp.exp(b_last - b)
    incl = jnp.tril(jnp.ones((GLA_CHUNK, GLA_CHUNK), dtype=bool))
    att = jnp.where(incl, jnp.einsum('bnihd,bnjhd->bnhij', q_dec, k_inv), 0.0)
    o_intra = jnp.einsum('bnhij,bnjhv->bnihv', att, vc)
    u = jnp.einsum('bnjhd,bnjhv->bnhdv', k_end, vc)
    decay = jnp.exp(b_last[:, :, 0])

    def step(s, inp):
        d_n, u_n = inp
        return d_n[..., None] * s + u_n, s

    _, s_prev = lax.scan(step, s0.astype(f32), (jnp.moveaxis(decay, 1, 0), jnp.moveaxis(u, 1, 0)))
    o_inter = jnp.einsum('bnihd,nbhdv->bnihv', q_dec, s_prev)
    return (o_intra + o_inter).reshape(B, L, H, dv)


def gla_final_state(k, v, log_a):
    b = jnp.cumsum(log_a.astype(jnp.float32), axis=1)
    w = jnp.exp(b[:, -1:] - b)
    return jnp.einsum('blhd,blhv->bhdv', k.astype(jnp.float32) * w, v.astype(jnp.float32))


def gla_bidir(q, k, v, la_f, la_b, s0_f, s0_b):
    o_f = gla_chunked(q, k, v, la_f, s0_f)
    o_b = flip(gla_chunked(flip(q), flip(k), flip(v), flip(la_b), s0_b))
    return o_f + o_b


def neighborhood_attention(q, k, v, k_ctx, v_ctx, rpb):
    B, L, H, Dh = q.shape
    rows = L // GRID_W
    kh = min(NA_WIN_ROWS, rows)
    kw = NA_WIN_COLS
    r = jnp.arange(rows)
    row_start = jnp.clip(r - kh // 2, 0, rows - kh)
    key_rows = row_start[:, None] + jnp.arange(kh)[None, :]
    cidx = jnp.arange(GRID_W)
    col_start = jnp.clip(cidx - kw // 2, 0, GRID_W - kw)
    in_win = (cidx[None, :] >= col_start[:, None]) & (cidx[None, :] < col_start[:, None] + kw)
    mask = jnp.broadcast_to(in_win[:, None, :], (GRID_W, kh, GRID_W)).reshape(GRID_W, kh * GRID_W)
    dr = key_rows - r[:, None]
    dc = jnp.clip(cidx[None, :] - cidx[:, None], -(kw - 1), kw - 1)
    bias = rpb.astype(jnp.float32)[:, dr[:, None, :, None] + NA_WIN_ROWS - 1,
                                   dc[None, :, None, :] + kw - 1]
    bias = bias.reshape(H, rows, GRID_W, kh * GRID_W)

    qg = q.reshape(B, rows, GRID_W, H, Dh)
    kb = k.reshape(B, rows, GRID_W, H, Dh)[:, key_rows].reshape(B, rows, kh * GRID_W, H, Dh)
    vb = v.reshape(B, rows, GRID_W, H, Dh)[:, key_rows].reshape(B, rows, kh * GRID_W, H, Dh)
    scale = NA_HEAD_DIM ** -0.5
    s_loc = jnp.einsum('brqhd,brkhd->bhrqk', qg, kb).astype(jnp.float32) * scale + bias
    s_loc = jnp.where(mask, s_loc, NEG_INF)
    s_ctx = jnp.einsum('brqhd,bkhd->bhrqk', qg, k_ctx).astype(jnp.float32) * scale
    p = jax.nn.softmax(jnp.concatenate([s_loc, s_ctx], axis=-1), axis=-1).astype(v.dtype)
    n_loc = kh * GRID_W
    o = (jnp.einsum('bhrqk,brkhd->brqhd', p[..., :n_loc], vb)
         + jnp.einsum('bhrqk,bkhd->brqhd', p[..., n_loc:], v_ctx))
    return o.reshape(B, L, H * Dh)


def context_attention(q, k, v):
    s = jnp.einsum('bqhd,bkhd->bhqk', q, k).astype(jnp.float32) * NA_HEAD_DIM ** -0.5
    p = jax.nn.softmax(s, axis=-1).astype(v.dtype)
    o = jnp.einsum('bhqk,bkhd->bqhd', p, v)
    return o.reshape(*o.shape[:2], -1)


def merge_branches(o_na, o_gla, og, m_a, m_b, gla_norm_g, w_br_na, w_br_gla, w_o):
    o_gla = rmsnorm(o_gla, gla_norm_g)
    o_gla = o_gla.reshape(*o_gla.shape[:2], -1).astype(og.dtype) * jax.nn.silu(og)
    y = jax.nn.sigmoid(m_a) * (o_na @ w_br_na) + jax.nn.sigmoid(m_b) * (o_gla @ w_br_gla)
    return y @ w_o


def swiglu(h, w_gate, w_up, w_down):
    return (jax.nn.silu(h @ w_gate) * (h @ w_up)) @ w_down


def setup_inputs(seed: int = 0) -> dict:
    key = jax.random.key(seed)
    ks = jax.random.split(key, 21)

    def nrm(k, shape, s):
        return jax.random.normal(k, shape, jnp.float32) * s

    L = DEPTH
    return {
        "x": nrm(ks[0], (BATCH, SEQ, D_MODEL), 1.0),
        "c": nrm(ks[1], (BATCH, D_MODEL), 1.0),
        "ctx": nrm(ks[2], (BATCH, CTX_LEN, D_MODEL), 1.0),
        "c_ctx": nrm(ks[3], (D_MODEL,), 1.0),
        "w_mod": nrm(ks[4], (L, D_MODEL, N_MOD * D_MODEL), 0.5 * D_MODEL ** -0.5),
        "b_mod": nrm(ks[5], (L, N_MOD * D_MODEL), 0.02),
        "norm1_g": 1.0 + nrm(ks[6], (L, D_MODEL), 0.05),
        "norm2_g": 1.0 + nrm(ks[7], (L, D_MODEL), 0.05),
        "w_in": nrm(ks[8], (L, D_MODEL, D_IN), D_MODEL ** -0.5),
        "na_q_norm_g": 1.0 + nrm(ks[9], (L, NA_HEAD_DIM), 0.05),
        "na_k_norm_g": 1.0 + nrm(ks[10], (L, NA_HEAD_DIM), 0.05),
        "na_rpb": nrm(ks[11], (L, NA_HEADS, 2 * NA_WIN_ROWS - 1, 2 * NA_WIN_COLS - 1), 0.5),
        "gla_w_alpha": nrm(ks[12], (L, 2, GLA_GATE_RANK, GLA_QK_WIDTH), GLA_GATE_RANK ** -0.5),
        "gla_b_alpha": 1.0 + nrm(ks[13], (L, 2, GLA_QK_WIDTH), 0.5),
        "gla_norm_g": 1.0 + nrm(ks[14], (L, GLA_DV), 0.05),
        "w_branch_na": nrm(ks[15], (L, NA_WIDTH, D_MODEL), NA_WIDTH ** -0.5),
        "w_branch_gla": nrm(ks[16], (L, GLA_V_WIDTH, D_MODEL), GLA_V_WIDTH ** -0.5),
        "w_out": nrm(ks[17], (L, D_MODEL, D_MODEL), D_MODEL ** -0.5),
        "w_ffn_gate": nrm(ks[18], (L, D_MODEL, D_FF), D_MODEL ** -0.5),
        "w_ffn_up": nrm(ks[19], (L, D_MODEL, D_FF), D_MODEL ** -0.5),
        "w_ffn_down": nrm(ks[20], (L, D_FF, D_MODEL), D_FF ** -0.5),
    }


def reference(x, c, ctx, c_ctx, w_mod, b_mod, norm1_g, norm2_g, w_in, na_q_norm_g, na_k_norm_g, na_rpb,
              gla_w_alpha, gla_b_alpha, gla_norm_g, w_branch_na, w_branch_gla, w_out,
              w_ffn_gate, w_ffn_up, w_ffn_down):
    B, L, _ = x.shape
    t = jnp.arange(L)
    row_pos, col_pos = t // GRID_W, t % GRID_W
    for layer in range(DEPTH):
        mod = jax.nn.silu(c) @ w_mod[layer] + b_mod[layer]
        mod_c = jax.nn.silu(c_ctx) @ w_mod[layer] + b_mod[layer]
        sh1, sc1, g1, sh2, sc2, g2 = jnp.split(mod[:, None, :], N_MOD, axis=-1)
        sh1c, sc1c, g1c, sh2c, sc2c, g2c = jnp.split(mod_c, N_MOD, axis=-1)
        (wq_na, wk_na, wv_na, wq_g, wk_g, wv_g, w_og, w_lr, w_ma, w_mb) = jnp.split(w_in[layer], IN_OFFSETS, axis=1)

        h = modulate(rmsnorm(x, norm1_g[layer]), sh1, sc1)
        hc = modulate(rmsnorm(ctx, norm1_g[layer]), sh1c, sc1c)

        k_na_c = rmsnorm(heads(hc @ wk_na, NA_HEADS), na_k_norm_g[layer])
        v_na_c = heads(hc @ wv_na, NA_HEADS)
        k_g_c = heads(hc @ wk_g, GLA_HEADS)
        v_g_c = heads(hc @ wv_g, GLA_HEADS)
        la_f_c, la_b_c = log_decay(hc @ w_lr, gla_w_alpha[layer], gla_b_alpha[layer])
        s_ctx_f = gla_final_state(k_g_c, v_g_c, la_f_c)
        s_ctx_b = gla_final_state(flip(k_g_c), flip(v_g_c), flip(la_b_c))

        (q_na, k_na, v_na, q_g, k_g, v_g, og, z_lr, m_a, m_b) = jnp.split(h @ w_in[layer], IN_OFFSETS, axis=-1)
        q_na = rmsnorm(heads(q_na, NA_HEADS), na_q_norm_g[layer])
        k_na = rmsnorm(heads(k_na, NA_HEADS), na_k_norm_g[layer])
        o_na = neighborhood_attention(q_na, k_na, heads(v_na, NA_HEADS), k_na_c, v_na_c, na_rpb[layer])
        q_g = axial_rope(heads(q_g, GLA_HEADS), row_pos, col_pos) * GLA_DK ** -0.5
        k_g = axial_rope(heads(k_g, GLA_HEADS), row_pos, col_pos)
        la_f, la_b = log_decay(z_lr, gla_w_alpha[layer], gla_b_alpha[layer])
        o_g = gla_bidir(q_g, k_g, heads(v_g, GLA_HEADS), la_f, la_b, s_ctx_f, s_ctx_b)
        mix = merge_branches(o_na, o_g, og, m_a, m_b, gla_norm_g[layer],
                             w_branch_na[layer], w_branch_gla[layer], w_out[layer])

        if layer + 1 < DEPTH:
            q_na_c = rmsnorm(heads(hc @ wq_na, NA_HEADS), na_q_norm_g[layer])
            o_na_c = context_attention(q_na_c, k_na_c, v_na_c)
            s_zero = jnp.zeros((B, GLA_HEADS, GLA_DK, GLA_DV), jnp.float32)
            q_g_c = heads(hc @ wq_g, GLA_HEADS) * GLA_DK ** -0.5
            o_g_c = gla_bidir(q_g_c, k_g_c, v_g_c, la_f_c, la_b_c, s_zero, s_zero)
            mix_c = merge_branches(o_na_c, o_g_c, hc @ w_og, hc @ w_ma, hc @ w_mb, gla_norm_g[layer],
                                   w_branch_na[layer], w_branch_gla[layer], w_out[layer])
            ctx = ctx + g1c * mix_c
            hc2 = modulate(rmsnorm(ctx, norm2_g[layer]), sh2c, sc2c)
            ctx = ctx + g2c * swiglu(hc2, w_ffn_gate[layer], w_ffn_up[layer], w_ffn_down[layer])

        x = x + g1 * mix
        h2 = modulate(rmsnorm(x, norm2_g[layer]), sh2, sc2)
        x = x + g2 * swiglu(h2, w_ffn_gate[layer], w_ffn_up[layer], w_ffn_down[layer])
    return x
```

```python
import functools

import numpy as np
import jax
import jax.numpy as jnp
from jax import lax
from jax.experimental import pallas as pl
from jax.experimental.pallas import tpu as pltpu

D = 1024
GRID_W = 64
CTX = 256
NA_H = 8
NA_DH = 64
NA_WR = 8
NA_WC = 16
GLA_H = 4
GLA_DK = 64
GLA_DV = 128
GLA_RANK = 16
GLA_TAU = 16.0
GLA_LA_MIN = -1.0
CHUNK = 64
ROPE_THETA = 10000.0
D_FF = 2816
EPS = 1e-6
NEG_INF = -1e30
NA_W = NA_H * NA_DH
GQK = GLA_H * GLA_DK
GV = GLA_H * GLA_DV

OFF_QNA, OFF_KNA, OFF_VNA = 0, 512, 1024
OFF_QG, OFF_KG, OFF_VG, OFF_OG = 1536, 1792, 2048, 2560
OFF_MA, OFF_MB, OFF_LR = 3072, 4096, 5120
D_INP = 5248

VMEM_LIMIT = 56 * 1024 * 1024

TM_PROJ = 512
TM_FFN = 256
NA_R = 4
GLA_CB = 8

f32 = jnp.float32
bf16 = jnp.bfloat16


def _const_spec(shape):
    nd = len(shape)
    return pl.BlockSpec(shape, lambda *_: (0,) * nd, pipeline_mode=pl.Buffered(1))


def _dot(a, b):
    return jnp.dot(a, b, preferred_element_type=f32)


def _dot_nt(a, b):
    return lax.dot_general(a, b, (((1,), (1,)), ((), ())), preferred_element_type=f32)


def _split2(x):
    hi = x.astype(bf16)
    lo = (x - hi.astype(f32)).astype(bf16)
    return hi, lo


def _log_sigmoid(z):
    return jnp.minimum(z, 0.0) - jnp.log(1.0 + jnp.exp(-jnp.abs(z)))


def _silu(x):
    return x * (1.0 / (1.0 + jnp.exp(-x)))


def _sigmoid(x):
    return 1.0 / (1.0 + jnp.exp(-x))


def _rpb_expand_kernel(r_ref, oh_ref, o_ref):
    r = r_ref[...]
    h1 = r.astype(bf16)
    r2 = r - h1.astype(f32)
    h2 = r2.astype(bf16)
    h3 = (r2 - h2.astype(f32)).astype(bf16)
    oh = oh_ref[...]
    o_ref[...] = _dot(h1, oh) + _dot(h2, oh) + _dot(h3, oh)


def _rpb_expand(rpb):
    nrow = NA_H * (2 * NA_WR - 1)
    ncol = 2 * NA_WC - 1
    r = jnp.zeros((128, 128), f32).at[:nrow, :ncol].set(rpb.reshape(nrow, ncol))
    q = np.arange(GRID_W)
    dc = np.clip(q[None, :] - q[:, None], -(NA_WC - 1), NA_WC - 1) + NA_WC - 1
    oh = np.zeros((128, GRID_W * GRID_W), np.float32)
    oh[dc.reshape(-1), np.arange(GRID_W * GRID_W)] = 1.0
    out = pl.pallas_call(
        _rpb_expand_kernel,
        out_shape=jax.ShapeDtypeStruct((128, GRID_W * GRID_W), f32),
        name="rpb_expand",
    )(r, jnp.asarray(oh, bf16))
    return out[:nrow].reshape(NA_H, 2 * NA_WR - 1, GRID_W, GRID_W)


def _na_bias_table(rpb):
    toe = _rpb_expand(rpb)
    q = np.arange(GRID_W)
    cs = np.clip(q - NA_WC // 2, 0, GRID_W - NA_WC)
    in_win = (q[None, :] >= cs[:, None]) & (q[None, :] < cs[:, None] + NA_WC)
    toe = jnp.where(jnp.asarray(in_win)[None, None], toe, NEG_INF)
    rows = 8192 // GRID_W
    variants = []
    for dr0 in range(0, -NA_WR, -1):
        t = toe[:, dr0 + NA_WR - 1: dr0 + NA_WR - 1 + NA_WR]
        t = jnp.transpose(t, (0, 2, 1, 3)).reshape(2, 4 * GRID_W, NA_WR * GRID_W)
        variants.append(t)
    del rows
    interior = NA_WR // 2
    first = [min(rr, interior) for rr in range(NA_R)]
    last = [max(interior, NA_WR - NA_R + rr) for rr in range(NA_R)]
    kinds = [first, [interior] * NA_R, last]
    return jnp.stack([jnp.stack([variants[v] for v in kind]) for kind in kinds])


def _mod_kernel(a_ref, w_ref, b_ref, o_ref):
    a = _silu(a_ref[...]).astype(bf16)
    o_ref[...] = _dot(a, w_ref[...].astype(bf16)) + b_ref[...]


def _adaln_mod(c, c_ctx, w_mod, b_mod):
    B = c.shape[0]
    a = jnp.zeros((8, D), f32).at[:B].set(c).at[B].set(c_ctx)
    n = w_mod.shape[1]
    return pl.pallas_call(
        _mod_kernel,
        grid=(n // D,),
        in_specs=[pl.BlockSpec((8, D), lambda j: (0, 0)),
                  pl.BlockSpec((D, D), lambda j: (0, j)),
                  pl.BlockSpec((1, D), lambda j: (0, j))],
        out_specs=pl.BlockSpec((8, D), lambda j: (0, j)),
        out_shape=jax.ShapeDtypeStruct((8, n), f32),
        name="adaln_mod",
    )(a, w_mod, b_mod.reshape(1, n))


def _norm_mod(x, g, sh, sc):
    ms = jnp.mean(x * x, axis=-1, keepdims=True)
    return (x * lax.rsqrt(ms + EPS) * g) * (1.0 + sc) + sh


def _head_rms(t, bd, g):
    sq = (t * t).astype(bf16)
    ms = jnp.concatenate([_dot(sq[:, :256], bd), _dot(sq[:, 256:], bd)], axis=1)
    return t * lax.rsqrt(ms + EPS) * g


def _log_alpha(zl, wa_ref, ba_ref):
    z = _dot(zl.astype(bf16), wa_ref[...]) + ba_ref[...]
    return jnp.maximum(_log_sigmoid(z) * (1.0 / GLA_TAU), GLA_LA_MIN)


def _ctx_kernel(ctx_ref, sh_ref, sc_ref, g1_ref, wk_ref, wv_ref, wkg_ref, wvg_ref, wlr_ref,
                wa_ref, ba_ref, gk_ref, bd_ref, su_ref, sl_ref,
                kc_ref, vc_ref, sf_ref, sb_ref):
    h = _norm_mod(ctx_ref[0], g1_ref[...], sh_ref[0], sc_ref[0]).astype(bf16)
    k = _dot(h, wk_ref[...])
    kc_ref[0] = _head_rms(k, bd_ref[...], gk_ref[...]).astype(bf16)
    vc_ref[0] = _dot(h, wv_ref[...]).astype(bf16)
    kg = _dot(h, wkg_ref[...])
    vg = _dot(h, wvg_ref[...]).astype(bf16)
    la = _log_alpha(_dot(h, wlr_ref[...]), wa_ref, ba_ref)
    hi, lo = _split2(la)
    ef = _dot(su_ref[...], hi[:, :GQK]) + _dot(su_ref[...], lo[:, :GQK])
    eb = _dot(sl_ref[...], hi[:, GQK:]) + _dot(sl_ref[...], lo[:, GQK:])
    r = lax.broadcasted_iota(jnp.int32, (GQK, GV), 0) >> 6
    cc = lax.broadcasted_iota(jnp.int32, (GQK, GV), 1) >> 7
    diag = r == cc
    for e, out in ((ef, sf_ref), (eb, sb_ref)):
        kw = (kg * jnp.exp(e)).astype(bf16)
        u = _dot(kw.T, vg)
        out[0] = jnp.where(diag, u, 0.0)


def _ctx_side(ctx, mod3, norm1_g, w_p, w_a, b_a, gk_t, bd, n_b):
    su = jnp.asarray(np.triu(np.ones((CTX, CTX), np.float32), 1), bf16)
    sl = jnp.asarray(np.tril(np.ones((CTX, CTX), np.float32), -1), bf16)
    B = ctx.shape[0]
    cst = lambda shape, idx: pl.BlockSpec(shape, lambda b: idx)
    return pl.pallas_call(
        _ctx_kernel,
        grid=(B,),
        in_specs=[pl.BlockSpec((1, CTX, D), lambda b: (b, 0, 0)),
                  cst((1, 1, D), (n_b, 0, 0)), cst((1, 1, D), (n_b, 0, 1)),
                  cst((1, D), (0, 0)),
                  cst((D, NA_W), (0, OFF_KNA // NA_W)), cst((D, NA_W), (0, OFF_VNA // NA_W)),
                  cst((D, GQK), (0, OFF_KG // GQK)), cst((D, GV), (0, OFF_VG // GV)),
                  cst((D, 128), (0, OFF_LR // 128)),
                  cst((128, 2 * GQK), (0, 0)), cst((1, 2 * GQK), (0, 0)),
                  cst((1, NA_W), (0, 0)), cst((256, 256), (0, 0)),
                  cst((CTX, CTX), (0, 0)), cst((CTX, CTX), (0, 0))],
        out_specs=[pl.BlockSpec((1, CTX, NA_W), lambda b: (b, 0, 0)),
                   pl.BlockSpec((1, CTX, NA_W), lambda b: (b, 0, 0)),
                   pl.BlockSpec((1, GQK, GV), lambda b: (b, 0, 0)),
                   pl.BlockSpec((1, GQK, GV), lambda b: (b, 0, 0))],
        out_shape=[jax.ShapeDtypeStruct((B, CTX, NA_W), bf16),
                   jax.ShapeDtypeStruct((B, CTX, NA_W), bf16),
                   jax.ShapeDtypeStruct((B, GQK, GV), f32),
                   jax.ShapeDtypeStruct((B, GQK, GV), f32)],
        compiler_params=pltpu.CompilerParams(vmem_limit_bytes=VMEM_LIMIT),
        name="ctx_side",
    )(ctx, mod3, mod3, norm1_g, w_p, w_p, w_p, w_p, w_p, w_a, b_a, gk_t, bd, su, sl)


def _proj_kernel(x_ref, sh_ref, sc_ref, g1_ref, w_ref, gq_ref, gk_ref, bd_ref,
                 rc_ref, rs_ref, cc_ref, cs_ref, wa_ref, ba_ref, tl_ref,
                 qna_ref, kna_ref, vna_ref, qf_ref, kif_ref, kef_ref, qb_ref, kib_ref, keb_ref,
                 vg_ref, og_ref, ma_ref, mb_ref, dec_ref):
    tm = x_ref.shape[1]
    nchunk = tm // CHUNK
    hb = _norm_mod(x_ref[0], g1_ref[...], sh_ref[0], sc_ref[0]).astype(bf16)

    def proj(off, n):
        return _dot(hb, w_ref[:, off:off + n])

    bd = bd_ref[...]
    qna_ref[0] = _head_rms(proj(OFF_QNA, NA_W), bd, gq_ref[...]).astype(bf16)
    kna_ref[0] = _head_rms(proj(OFF_KNA, NA_W), bd, gk_ref[...]).astype(bf16)
    vna_ref[0] = proj(OFF_VNA, NA_W).astype(bf16)
    vg_ref[0] = proj(OFF_VG, GV).astype(bf16)
    og_ref[0] = _silu(proj(OFF_OG, GV)).astype(bf16)
    ma_ref[0] = _sigmoid(proj(OFF_MA, D)).astype(bf16)
    mb_ref[0] = _sigmoid(proj(OFF_MB, D)).astype(bf16)

    cos_t = (rc_ref[...][:, None, :] + cc_ref[...][None]).reshape(tm, GQK)
    sin_t = (rs_ref[...][:, None, :] + cs_ref[...][None]).reshape(tm, GQK)
    lane = lax.broadcasted_iota(jnp.int32, (tm, GQK), 1)
    first_half = (lane & 31) < 16

    def rope(t):
        partner = jnp.where(first_half, pltpu.roll(t, GQK - 16, 1), pltpu.roll(t, 16, 1))
        return t * cos_t + partner * sin_t

    qg = rope(proj(OFF_QG, GQK)) * (GLA_DK ** -0.5)
    kg = rope(proj(OFF_KG, GQK))

    la = _log_alpha(proj(OFF_LR, 128), wa_ref, ba_ref)
    hi, lo = _split2(la)
    tl = tl_ref[...]
    cum = jnp.concatenate(
        [_dot(tl, hi[s:s + 256]) + _dot(tl, lo[s:s + 256]) for s in range(0, tm, 256)], axis=0)
    cum3 = cum.reshape(nchunk, CHUNK, 2 * GQK)
    tot3 = cum3[:, CHUNK - 1:CHUNK, :]
    la3 = la.reshape(nchunk, CHUNK, 2 * GQK)
    dec_ref[0] = jnp.exp(tot3).reshape(nchunk, 2 * GQK)

    b_f = cum3[:, :, :GQK].reshape(tm, GQK)
    e_f = (tot3 - cum3)[:, :, :GQK].reshape(tm, GQK)
    b_b = (tot3 - cum3 + la3)[:, :, GQK:].reshape(tm, GQK)
    e_b = (cum3 - la3)[:, :, GQK:].reshape(tm, GQK)
    qf_ref[0] = (qg * jnp.exp(b_f)).astype(bf16)
    kif_ref[0] = (kg * jnp.exp(-b_f)).astype(bf16)
    kef_ref[0] = (kg * jnp.exp(e_f)).astype(bf16)
    qb_ref[0] = (qg * jnp.exp(b_b)).astype(bf16)
    kib_ref[0] = (kg * jnp.exp(-b_b)).astype(bf16)
    keb_ref[0] = (kg * jnp.exp(e_b)).astype(bf16)


def _rope_tables(L):
    n = GLA_DK // 4
    freqs = ROPE_THETA ** (-np.arange(n, dtype=np.float64) / n)
    lane = np.arange(GQK)
    fl = freqs[lane % n]
    is_row = (lane % GLA_DK) < GLA_DK // 2
    sign = np.where((lane % 32) < 16, -1.0, 1.0)
    rows = np.arange(L // GRID_W)[:, None] * fl[None]
    cols = np.arange(GRID_W)[:, None] * fl[None]
    rc = np.where(is_row, np.cos(rows), 0.0)
    rs = np.where(is_row, np.sin(rows) * sign, 0.0)
    cc = np.where(~is_row, np.cos(cols), 0.0)
    cs = np.where(~is_row, np.sin(cols) * sign, 0.0)
    return [jnp.asarray(t, f32) for t in (rc, rs, cc, cs)]


def _in_proj(x, mod3, norm1_g, w_p, gq_t, gk_t, bd, w_a, b_a):
    B, L, _ = x.shape
    tm = TM_PROJ
    rows_per = tm // GRID_W
    rc, rs, cc, cs = _rope_tables(L)
    blk = np.kron(np.eye(256 // CHUNK), np.tril(np.ones((CHUNK, CHUNK)))).astype(np.float32)
    tl = jnp.asarray(blk, bf16)
    tok = lambda n: pl.BlockSpec((1, tm, n), lambda b, i: (b, i, 0))
    sds = lambda n, dt: jax.ShapeDtypeStruct((B, L, n), dt)
    return pl.pallas_call(
        _proj_kernel,
        grid=(B, L // tm),
        in_specs=[tok(D),
                  pl.BlockSpec((1, 1, D), lambda b, i: (b, 0, 0)),
                  pl.BlockSpec((1, 1, D), lambda b, i: (b, 0, 1)),
                  _const_spec((1, D)), _const_spec((D, D_INP)),
                  _const_spec((1, NA_W)), _const_spec((1, NA_W)), _const_spec((256, 256)),
                  pl.BlockSpec((rows_per, GQK), lambda b, i: (i, 0)),
                  pl.BlockSpec((rows_per, GQK), lambda b, i: (i, 0)),
                  _const_spec((GRID_W, GQK)), _const_spec((GRID_W, GQK)),
                  _const_spec((128, 2 * GQK)), _const_spec((1, 2 * GQK)), _const_spec((256, 256))],
        out_specs=[tok(NA_W), tok(NA_W), tok(NA_W)] + [tok(GQK)] * 6
                  + [tok(GV), tok(GV), tok(D), tok(D),
                     pl.BlockSpec((1, tm // CHUNK, 2 * GQK), lambda b, i: (b, i, 0))],
        out_shape=[sds(NA_W, bf16)] * 3 + [sds(GQK, bf16)] * 6
                  + [sds(GV, bf16), sds(GV, bf16), sds(D, bf16), sds(D, bf16),
                     jax.ShapeDtypeStruct((B, L // CHUNK, 2 * GQK), f32)],
        compiler_params=pltpu.CompilerParams(
            dimension_semantics=("parallel", "parallel"), vmem_limit_bytes=VMEM_LIMIT),
        name="in_proj",
    )(x, mod3, mod3, norm1_g, w_p, gq_t, gk_t, bd, rc, rs, cc, cs, w_a, b_a, tl)


def _na_kernel(q_ref, k_ref, v_ref, kc_ref, vc_ref, bias_ref, o_ref):
    i = pl.program_id(1)
    rows = k_ref.shape[1] // GRID_W
    lane_head = lax.broadcasted_iota(jnp.int32, (GRID_W, 256), 1) >> 6
    hmask = [lane_head == h for h in range(4)]
    for rr in range(NA_R):
        r = i * NA_R + rr
        rs = jnp.clip(r - NA_WR // 2, 0, rows - NA_WR)
        start = pl.multiple_of(rs * GRID_W, GRID_W)
        outs = []
        for g in range(2):
            ls = slice(256 * g, 256 * g + 256)
            q = q_ref[0, rr * GRID_W:(rr + 1) * GRID_W, ls]
            qs = jnp.concatenate([jnp.where(m, q, jnp.zeros_like(q)) for m in hmask], axis=0)
            kw = k_ref[0, pl.ds(start, NA_WR * GRID_W), ls]
            vw = v_ref[0, pl.ds(start, NA_WR * GRID_W), ls]
            s_loc = _dot_nt(qs, kw) + bias_ref[0, rr, g]
            s_ctx = _dot_nt(qs, kc_ref[0, :, ls])
            m = jnp.maximum(jnp.max(s_loc, axis=-1, keepdims=True),
                            jnp.max(s_ctx, axis=-1, keepdims=True))
            p_loc = jnp.exp(s_loc - m)
            p_ctx = jnp.exp(s_ctx - m)
            den = jnp.sum(p_loc, axis=-1, keepdims=True) + jnp.sum(p_ctx, axis=-1, keepdims=True)
            acc = _dot(p_loc.astype(bf16), vw) + _dot(p_ctx.astype(bf16), vc_ref[0, :, ls])
            acc = acc * (1.0 / den)
            o = jnp.zeros((GRID_W, 256), f32)
            for h in range(4):
                o = o + jnp.where(hmask[h], acc[h * GRID_W:(h + 1) * GRID_W], 0.0)
            outs.append(o)
        o_ref[0, rr * GRID_W:(rr + 1) * GRID_W, :] = jnp.concatenate(outs, axis=1).astype(bf16)


def _na_attn(q, k, v, kc, vc, bias):
    B, L, _ = q.shape
    nblk = L // (NA_R * GRID_W)
    kind = lambda b, i: (jnp.where(i == 0, 0, jnp.where(i == nblk - 1, 2, 1)), 0, 0, 0, 0)
    return pl.pallas_call(
        _na_kernel,
        grid=(B, nblk),
        in_specs=[pl.BlockSpec((1, NA_R * GRID_W, NA_W), lambda b, i: (b, i, 0)),
                  pl.BlockSpec((1, L, NA_W), lambda b, i: (b, 0, 0), pipeline_mode=pl.Buffered(1)),
                  pl.BlockSpec((1, L, NA_W), lambda b, i: (b, 0, 0), pipeline_mode=pl.Buffered(1)),
                  pl.BlockSpec((1, CTX, NA_W), lambda b, i: (b, 0, 0)),
                  pl.BlockSpec((1, CTX, NA_W), lambda b, i: (b, 0, 0)),
                  pl.BlockSpec((1, NA_R, 2, 256, NA_WR * GRID_W), kind)],
        out_specs=pl.BlockSpec((1, NA_R * GRID_W, NA_W), lambda b, i: (b, i, 0)),
        out_shape=jax.ShapeDtypeStruct((B, L, NA_W), bf16),
        compiler_params=pltpu.CompilerParams(
            dimension_semantics=("parallel", "parallel"), vmem_limit_bytes=VMEM_LIMIT),
        name="na_attn",
    )(q, k, v, kc, vc, bias)


def _gla_chunk(q, ki, ke, v, dcol, s_ref, tri, hmask):
    qs = jnp.concatenate([jnp.where(m, q, jnp.zeros_like(q)) for m in hmask], axis=0)
    att = jnp.where(tri, _dot_nt(qs, ki), 0.0).astype(bf16)
    o = _dot(q, s_ref[...].astype(bf16))
    intra = [_dot(att[h * CHUNK:(h + 1) * CHUNK], v[:, h * GLA_DV:(h + 1) * GLA_DV]) for h in range(GLA_H)]
    o = o + jnp.concatenate(intra, axis=1)
    ket = ke.T
    for h in range(GLA_H):
        rs = slice(h * GLA_DK, (h + 1) * GLA_DK)
        cs = slice(h * GLA_DV, (h + 1) * GLA_DV)
        s_ref[rs, cs] = dcol[rs] * s_ref[rs, cs] + _dot(ket[rs], v[:, cs])
    return o


def _gla_kernel(qf_ref, kif_ref, kef_ref, vf_ref, decf_ref, qb_ref, kib_ref, keb_ref, vb_ref, decb_ref,
                s0f_ref, s0b_ref, of_ref, ob_ref, sf_ref, sb_ref):
    @pl.when(pl.program_id(1) == 0)
    def _():
        sf_ref[...] = s0f_ref[0]
        sb_ref[...] = s0b_ref[0]

    lane_head = lax.broadcasted_iota(jnp.int32, (CHUNK, GQK), 1) >> 6
    hmask = [lane_head == h for h in range(GLA_H)]
    ri = lax.broadcasted_iota(jnp.int32, (GLA_H * CHUNK, CHUNK), 0) & (CHUNK - 1)
    ci = lax.broadcasted_iota(jnp.int32, (GLA_H * CHUNK, CHUNK), 1)
    tri_f = ri >= ci
    tri_b = ri <= ci
    dect_f = decf_ref[0][:, :GQK].T
    dect_b = decb_ref[0][:, GQK:].T
    for c in range(GLA_CB):
        ts = slice(c * CHUNK, (c + 1) * CHUNK)
        of_ref[0, ts, :] = _gla_chunk(qf_ref[0, ts, :], kif_ref[0, ts, :], kef_ref[0, ts, :], vf_ref[0, ts, :],
                                      dect_f[:, c:c + 1], sf_ref, tri_f, hmask)
        cb = GLA_CB - 1 - c
        tb = slice(cb * CHUNK, (cb + 1) * CHUNK)
        ob_ref[0, tb, :] = _gla_chunk(qb_ref[0, tb, :], kib_ref[0, tb, :], keb_ref[0, tb, :], vb_ref[0, tb, :],
                                      dect_b[:, cb:cb + 1], sb_ref, tri_b, hmask)


def _gla_scan(qf, kif, kef, qb, kib, keb, vg, dec, s0f, s0b):
    B, L, _ = vg.shape
    tm = GLA_CB * CHUNK
    nb = L // tm
    fw = lambda n: pl.BlockSpec((1, tm, n), lambda b, i: (b, i, 0))
    bw = lambda n: pl.BlockSpec((1, tm, n), lambda b, i: (b, nb - 1 - i, 0))
    st = pl.BlockSpec((1, GQK, GV), lambda b, i: (b, 0, 0))
    return pl.pallas_call(
        _gla_kernel,
        grid=(B, nb),
        in_specs=[fw(GQK), fw(GQK), fw(GQK), fw(GV),
                  pl.BlockSpec((1, GLA_CB, 2 * GQK), lambda b, i: (b, i, 0)),
                  bw(GQK), bw(GQK), bw(GQK), bw(GV),
                  pl.BlockSpec((1, GLA_CB, 2 * GQK), lambda b, i: (b, nb - 1 - i, 0)),
                  st, st],
        out_specs=[fw(GV), bw(GV)],
        out_shape=[jax.ShapeDtypeStruct((B, L, GV), f32)] * 2,
        scratch_shapes=[pltpu.VMEM((GQK, GV), f32), pltpu.VMEM((GQK, GV), f32)],
        compiler_params=pltpu.CompilerParams(
            dimension_semantics=("parallel", "arbitrary"), vmem_limit_bytes=VMEM_LIMIT),
        name="gla_scan",
    )(qf, kif, kef, vg, dec, qb, kib, keb, vg, dec, s0f, s0b)


def _merge_ffn_kernel(x_ref, ona_ref, of_ref, ob_ref, og_ref, ma_ref, mb_ref,
                      g1_ref, sh2_ref, sc2_ref, g2_ref, n2_ref, gg_ref,
                      wna_ref, wgl_ref, wo_ref, wg_ref, wu_ref, wd_ref, o_ref):
    og = of_ref[0] + ob_ref[0]
    parts = []
    for h in range(GLA_H):
        seg = og[:, h * GLA_DV:(h + 1) * GLA_DV]
        ms = jnp.mean(seg * seg, axis=-1, keepdims=True)
        parts.append(seg * lax.rsqrt(ms + EPS) * gg_ref[...])
    ogl = (jnp.concatenate(parts, axis=1) * og_ref[0].astype(f32)).astype(bf16)
    y = (ma_ref[0].astype(f32) * _dot(ona_ref[0], wna_ref[...])
         + mb_ref[0].astype(f32) * _dot(ogl, wgl_ref[...]))
    x1 = x_ref[0] + g1_ref[0] * _dot(y.astype(bf16), wo_ref[...])
    h2 = _norm_mod(x1, n2_ref[...], sh2_ref[0], sc2_ref[0]).astype(bf16)
    act = (_silu(_dot(h2, wg_ref[...])) * _dot(h2, wu_ref[...])).astype(bf16)
    o_ref[0] = x1 + g2_ref[0] * _dot(act, wd_ref[...])


def _merge_ffn(x, ona, of, ob, og, ma, mb, mod3, norm2_g, gg, wna, wgl, wo, wg, wu, wd):
    B, L, _ = x.shape
    tm = TM_FFN
    tok = lambda n: pl.BlockSpec((1, tm, n), lambda b, i: (b, i, 0))
    modv = lambda j: pl.BlockSpec((1, 1, D), lambda b, i: (b, 0, j))
    return pl.pallas_call(
        _merge_ffn_kernel,
        grid=(B, L // tm),
        in_specs=[tok(D), tok(NA_W), tok(GV), tok(GV), tok(GV), tok(D), tok(D),
                  modv(2), modv(3), modv(4), modv(5),
                  _const_spec((1, D)), _const_spec((1, GLA_DV)),
                  _const_spec((NA_W, D)), _const_spec((GV, D)), _const_spec((D, D)),
                  _const_spec((D, D_FF)), _const_spec((D, D_FF)), _const_spec((D_FF, D))],
        out_specs=tok(D),
        out_shape=jax.ShapeDtypeStruct((B, L, D), f32),
        compiler_params=pltpu.CompilerParams(
            dimension_semantics=("parallel", "parallel"), vmem_limit_bytes=VMEM_LIMIT),
        name="merge_ffn",
    )(x, ona, of, ob, og, ma, mb, mod3, mod3, mod3, mod3, norm2_g, gg, wna, wgl, wo, wg, wu, wd)


def kernel(x, c, ctx, c_ctx, w_mod, b_mod, norm1_g, norm2_g, w_in, na_q_norm_g, na_k_norm_g, na_rpb,
           gla_w_alpha, gla_b_alpha, gla_norm_g, w_branch_na, w_branch_gla, w_out,
           w_ffn_gate, w_ffn_up, w_ffn_down):
    B, L, _ = x.shape
    assert w_mod.shape[0] == 1 and L % (NA_R * GRID_W) == 0 and L % TM_PROJ == 0

    w = w_in[0]
    w_p = jnp.concatenate([w[:, :3072], w[:, 3104:], w[:, 3072:3104],
                           jnp.zeros((D, D_INP - 5152), w.dtype)], axis=1).astype(bf16)
    w_a = jnp.zeros((128, 2 * GQK), f32)
    w_a = w_a.at[:GLA_RANK, :GQK].set(gla_w_alpha[0, 0]).at[GLA_RANK:2 * GLA_RANK, GQK:].set(gla_w_alpha[0, 1])
    w_a = w_a.astype(bf16)
    b_a = gla_b_alpha[0].reshape(1, 2 * GQK)
    gq_t = jnp.tile(na_q_norm_g[0] * (NA_DH ** -0.5), NA_H).reshape(1, NA_W)
    gk_t = jnp.tile(na_k_norm_g[0], NA_H).reshape(1, NA_W)
    bd = jnp.asarray(np.kron(np.eye(4), np.full((NA_DH, NA_DH), 1.0 / NA_DH)), bf16)

    mod = _adaln_mod(c, c_ctx, w_mod[0], b_mod[0])
    mod3 = mod.reshape(8, 1, 6 * D)
    bias = _na_bias_table(na_rpb[0])

    kc, vc, s0f, s0b = _ctx_side(ctx, mod3, norm1_g, w_p, w_a, b_a, gk_t, bd, B)
    (qna, kna, vna, qf, kif, kef, qb, kib, keb, vg, og, ma, mb, dec) = _in_proj(
        x, mod3, norm1_g, w_p, gq_t, gk_t, bd, w_a, b_a)
    ona = _na_attn(qna, kna, vna, kc, vc, bias)
    of, ob = _gla_scan(qf, kif, kef, qb, kib, keb, vg, dec, s0f, s0b)
    return _merge_ffn(x, ona, of, ob, og, ma, mb, mod3, norm2_g, gla_norm_g,
                      w_branch_na[0].astype(bf16), w_branch_gla[0].astype(bf16), w_out[0].astype(bf16),
                      w_ffn_gate[0].astype(bf16), w_ffn_up[0].astype(bf16), w_ffn_down[0].astype(bf16))
```

```python
import functools

import numpy as np
import jax
import jax.numpy as jnp
from jax import lax
from jax.experimental import pallas as pl
from jax.experimental.pallas import tpu as pltpu

D = 1024
GRID_W = 64
CTX = 256
NA_H = 8
NA_DH = 64
NA_WR = 8
NA_WC = 16
GLA_H = 4
GLA_DK = 64
GLA_DV = 128
GLA_RANK = 16
GLA_TAU = 16.0
GLA_LA_MIN = -1.0
CHUNK = 64
ROPE_THETA = 10000.0
D_FF = 2816
EPS = 1e-6
NEG_INF = -1e30
NA_W = NA_H * NA_DH
GQK = GLA_H * GLA_DK
GV = GLA_H * GLA_DV

OFF_QNA, OFF_KNA, OFF_VNA = 0, 512, 1024
OFF_QG, OFF_KG, OFF_VG, OFF_OG = 1536, 1792, 2048, 2560
W_HEAD = 3072
OFF_MA, OFF_MB, OFF_LR = 0, 1024, 2048
W_TAIL = 2176
D_IN = 5152
LR_SHIFT = 32
LOG2E = 1.4426950408889634

VMEM_LIMIT = 56 * 1024 * 1024

TM_PROJ = 512
TM_FFN = 256
NA_R = 4
GLA_CB = 8

f32 = jnp.float32
bf16 = jnp.bfloat16


def _const_spec(shape):
    nd = len(shape)
    return pl.BlockSpec(shape, lambda *_: (0,) * nd, pipeline_mode=pl.Buffered(1))


def _dot(a, b):
    return jnp.dot(a, b, preferred_element_type=f32)


def _dot_nt(a, b):
    return lax.dot_general(a, b, (((1,), (1,)), ((), ())), preferred_element_type=f32)


def _split2(x):
    hi = x.astype(bf16)
    lo = (x - hi.astype(f32)).astype(bf16)
    return hi, lo


def _log_sigmoid(z):
    return jnp.minimum(z, 0.0) - jnp.log(1.0 + jnp.exp(-jnp.abs(z)))


def _silu(x):
    return x * (1.0 / (1.0 + jnp.exp(-x)))


def _sigmoid(x):
    return 1.0 / (1.0 + jnp.exp(-x))


def _split3_dot(r, oh):
    h1 = r.astype(bf16)
    r2 = r - h1.astype(f32)
    h2 = r2.astype(bf16)
    h3 = (r2 - h2.astype(f32)).astype(bf16)
    return _dot(h1, oh) + _dot(h2, oh) + _dot(h3, oh)


def _rpb_expand_kernel(r0_ref, r1_ref, oh0_ref, oh1_ref, mask_ref, o_ref):
    gathered = _split3_dot(r0_ref[...], oh0_ref[...]) + _split3_dot(r1_ref[...], oh1_ref[...])
    o_ref[...] = gathered * LOG2E + mask_ref[...]


def _rpb_expand(rpb):
    nd = 2 * NA_WR - 1
    nrow = NA_H * nd
    ncol = 2 * NA_WC - 1
    nxt = jnp.concatenate([rpb[:, 1:], jnp.zeros_like(rpb[:, :1])], axis=1)
    r0 = jnp.zeros((128, 128), f32).at[:nrow, :ncol].set(rpb.reshape(nrow, ncol))
    r1 = jnp.zeros((128, 128), f32).at[:nrow, :ncol].set(nxt.reshape(nrow, ncol))
    q = np.arange(GRID_W)
    cs = np.clip(q - NA_WC // 2, 0, GRID_W - NA_WC)
    in_win = (q[None, :] >= cs[:, None]) & (q[None, :] < cs[:, None] + NA_WC)
    dc = np.clip(q[None, :] - q[:, None], -(NA_WC - 1), NA_WC - 1) + NA_WC - 1
    oh = np.zeros((2, 128, GRID_W, 128), np.float32)
    qq, kk = np.nonzero(in_win)
    oh[0, dc[qq, kk], qq, kk] = 1.0
    oh[1, dc[qq, kk], qq, kk + GRID_W] = 1.0
    oh = oh.reshape(2, 128, GRID_W * 128)
    mask = np.where(np.concatenate([in_win, in_win], axis=1), 0.0, NEG_INF).reshape(1, GRID_W * 128)
    out = pl.pallas_call(
        _rpb_expand_kernel,
        out_shape=jax.ShapeDtypeStruct((128, GRID_W * 128), f32),
        name="rpb_expand",
    )(r0, r1, jnp.asarray(oh[0], bf16), jnp.asarray(oh[1], bf16), jnp.asarray(mask, f32))
    return out[:nrow].reshape(NA_H, nd, GRID_W, 128)


def _cast_kernel(x_ref, o_ref):
    o_ref[...] = x_ref[0].astype(bf16)


def _w_head(w_in):
    blk = 1024
    return pl.pallas_call(
        _cast_kernel,
        grid=(W_HEAD // blk,),
        in_specs=[pl.BlockSpec((1, D, blk), lambda j: (0, 0, j))],
        out_specs=pl.BlockSpec((D, blk), lambda j: (0, j)),
        out_shape=jax.ShapeDtypeStruct((D, W_HEAD), bf16),
        name="w_head",
    )(w_in)


def _w_tail_kernel(a_ref, b_ref, o_ref):
    n = pl.program_id(0)
    lane = lax.broadcasted_iota(jnp.int32, (D, 128), 1)
    col = (W_HEAD // 128 + n) * 128 + lane
    a = jnp.where(col < D_IN, a_ref[0], 0.0)
    x = jnp.where(lane >= LR_SHIFT, a, b_ref[0])
    o_ref[...] = pltpu.roll(x, 128 - LR_SHIFT, 1).astype(bf16)


def _w_tail(w_in):
    nblk = W_TAIL // 128
    first = W_HEAD // 128
    return pl.pallas_call(
        _w_tail_kernel,
        grid=(nblk,),
        in_specs=[pl.BlockSpec((1, D, 128), lambda n: (0, 0, first + n)),
                  pl.BlockSpec((1, D, 128), lambda n: (0, 0, first + (n + 1) % nblk))],
        out_specs=pl.BlockSpec((D, 128), lambda n: (0, n)),
        out_shape=jax.ShapeDtypeStruct((D, W_TAIL), bf16),
        name="w_tail",
    )(w_in, w_in)


def _mod_kernel(a_ref, w_ref, b_ref, o_ref):
    a = _silu(a_ref[...]).astype(bf16)
    o_ref[...] = _dot(a, w_ref[...].astype(bf16)) + b_ref[...]


def _adaln_mod(c, c_ctx, w_mod, b_mod):
    B = c.shape[0]
    a = jnp.zeros((8, D), f32).at[:B].set(c).at[B].set(c_ctx)
    n = w_mod.shape[1]
    return pl.pallas_call(
        _mod_kernel,
        grid=(n // D,),
        in_specs=[pl.BlockSpec((8, D), lambda j: (0, 0)),
                  pl.BlockSpec((D, D), lambda j: (0, j)),
                  pl.BlockSpec((1, D), lambda j: (0, j))],
        out_specs=pl.BlockSpec((8, D), lambda j: (0, j)),
        out_shape=jax.ShapeDtypeStruct((8, n), f32),
        name="adaln_mod",
    )(a, w_mod, b_mod.reshape(1, n))


def _norm_mod(x, g, sh, sc):
    ms = jnp.mean(x * x, axis=-1, keepdims=True)
    return (x * lax.rsqrt(ms + EPS) * g) * (1.0 + sc) + sh


def _head_rms(t, bd, g):
    sq = (t * t).astype(bf16)
    ms = jnp.concatenate([_dot(sq[:, :256], bd), _dot(sq[:, 256:], bd)], axis=1)
    return t * lax.rsqrt(ms + EPS) * g


def _log_alpha(zl, wa_ref, ba_ref):
    z = _dot(zl.astype(bf16), wa_ref[...]) + ba_ref[...]
    return jnp.maximum(_log_sigmoid(z) * (1.0 / GLA_TAU), GLA_LA_MIN)


def _ctx_kernel(ctx_ref, sh_ref, sc_ref, g1_ref, wk_ref, wv_ref, wkg_ref, wvg_ref, wlr_ref,
                wa_ref, ba_ref, gk_ref, bd_ref, su_ref, sl_ref,
                kc_ref, vc_ref, sf_ref, sb_ref):
    h = _norm_mod(ctx_ref[0], g1_ref[...], sh_ref[0], sc_ref[0]).astype(bf16)
    k = _dot(h, wk_ref[...])
    kc_ref[0] = _head_rms(k, bd_ref[...], gk_ref[...]).astype(bf16)
    vc_ref[0] = _dot(h, wv_ref[...]).astype(bf16)
    kg = _dot(h, wkg_ref[...])
    vg = _dot(h, wvg_ref[...]).astype(bf16)
    la = _log_alpha(_dot(h, wlr_ref[...]), wa_ref, ba_ref)
    hi, lo = _split2(la)
    ef = _dot(su_ref[...], hi[:, :GQK]) + _dot(su_ref[...], lo[:, :GQK])
    eb = _dot(sl_ref[...], hi[:, GQK:]) + _dot(sl_ref[...], lo[:, GQK:])
    r = lax.broadcasted_iota(jnp.int32, (GQK, GV), 0) >> 6
    cc = lax.broadcasted_iota(jnp.int32, (GQK, GV), 1) >> 7
    diag = r == cc
    for e, out in ((ef, sf_ref), (eb, sb_ref)):
        kw = (kg * jnp.exp(e)).astype(bf16)
        u = _dot(kw.T, vg)
        out[0] = jnp.where(diag, u, 0.0)


def _ctx_side(ctx, mod3, norm1_g, w_h, w_t, w_a, b_a, gk_t, bd, n_b):
    su = jnp.asarray(np.triu(np.ones((CTX, CTX), np.float32), 1), bf16)
    sl = jnp.asarray(np.tril(np.ones((CTX, CTX), np.float32), -1), bf16)
    B = ctx.shape[0]
    cst = lambda shape, idx: pl.BlockSpec(shape, lambda b: idx)
    return pl.pallas_call(
        _ctx_kernel,
        grid=(B,),
        in_specs=[pl.BlockSpec((1, CTX, D), lambda b: (b, 0, 0)),
                  cst((1, 1, D), (n_b, 0, 0)), cst((1, 1, D), (n_b, 0, 1)),
                  cst((1, D), (0, 0)),
                  cst((D, NA_W), (0, OFF_KNA // NA_W)), cst((D, NA_W), (0, OFF_VNA // NA_W)),
                  cst((D, GQK), (0, OFF_KG // GQK)), cst((D, GV), (0, OFF_VG // GV)),
                  cst((D, 128), (0, OFF_LR // 128)),
                  cst((128, 2 * GQK), (0, 0)), cst((1, 2 * GQK), (0, 0)),
                  cst((1, NA_W), (0, 0)), cst((256, 256), (0, 0)),
                  cst((CTX, CTX), (0, 0)), cst((CTX, CTX), (0, 0))],
        out_specs=[pl.BlockSpec((1, CTX, NA_W), lambda b: (b, 0, 0)),
                   pl.BlockSpec((1, CTX, NA_W), lambda b: (b, 0, 0)),
                   pl.BlockSpec((1, GQK, GV), lambda b: (b, 0, 0)),
                   pl.BlockSpec((1, GQK, GV), lambda b: (b, 0, 0))],
        out_shape=[jax.ShapeDtypeStruct((B, CTX, NA_W), bf16),
                   jax.ShapeDtypeStruct((B, CTX, NA_W), bf16),
                   jax.ShapeDtypeStruct((B, GQK, GV), f32),
                   jax.ShapeDtypeStruct((B, GQK, GV), f32)],
        compiler_params=pltpu.CompilerParams(vmem_limit_bytes=VMEM_LIMIT),
        name="ctx_side",
    )(ctx, mod3, mod3, norm1_g, w_h, w_h, w_h, w_h, w_t, w_a, b_a, gk_t, bd, su, sl)


def _proj_kernel(x_ref, sh_ref, sc_ref, g1_ref, w_ref, wt_ref, gq_ref, gk_ref, bd_ref,
                 rc_ref, rs_ref, cc_ref, cs_ref, wa_ref, ba_ref, tl_ref,
                 qna_ref, kna_ref, vna_ref, qf_ref, kif_ref, kef_ref, qb_ref, kib_ref, keb_ref,
                 vg_ref, og_ref, ma_ref, mb_ref, dec_ref):
    tm = x_ref.shape[1]
    nchunk = tm // CHUNK
    hb = _norm_mod(x_ref[0], g1_ref[...], sh_ref[0], sc_ref[0]).astype(bf16)

    def proj(off, n):
        return _dot(hb, w_ref[:, off:off + n])

    def proj_tail(off, n):
        return _dot(hb, wt_ref[:, off:off + n])

    bd = bd_ref[...]
    qna_ref[0] = _head_rms(proj(OFF_QNA, NA_W), bd, gq_ref[...]).astype(bf16)
    kna_ref[0] = _head_rms(proj(OFF_KNA, NA_W), bd, gk_ref[...]).astype(bf16)
    vna_ref[0] = proj(OFF_VNA, NA_W).astype(bf16)
    vg_ref[0] = proj(OFF_VG, GV).astype(bf16)
    og_ref[0] = _silu(proj(OFF_OG, GV)).astype(bf16)
    ma_ref[0] = _sigmoid(proj_tail(OFF_MA, D)).astype(bf16)
    mb_ref[0] = _sigmoid(proj_tail(OFF_MB, D)).astype(bf16)

    cos_t = (rc_ref[...][:, None, :] + cc_ref[...][None]).reshape(tm, GQK)
    sin_t = (rs_ref[...][:, None, :] + cs_ref[...][None]).reshape(tm, GQK)
    lane = lax.broadcasted_iota(jnp.int32, (tm, GQK), 1)
    first_half = (lane & 31) < 16

    def rope(t):
        partner = jnp.where(first_half, pltpu.roll(t, GQK - 16, 1), pltpu.roll(t, 16, 1))
        return t * cos_t + partner * sin_t

    qg = rope(proj(OFF_QG, GQK)) * (GLA_DK ** -0.5)
    kg = rope(proj(OFF_KG, GQK))

    la = _log_alpha(proj_tail(OFF_LR, 128), wa_ref, ba_ref)
    hi, lo = _split2(la)
    tl = tl_ref[...]
    cum = jnp.concatenate(
        [_dot(tl, hi[s:s + 256]) + _dot(tl, lo[s:s + 256]) for s in range(0, tm, 256)], axis=0)
    cum3 = cum.reshape(nchunk, CHUNK, 2 * GQK)
    tot3 = cum3[:, CHUNK - 1:CHUNK, :]
    la3 = la.reshape(nchunk, CHUNK, 2 * GQK)
    dec_ref[0] = jnp.exp(tot3).reshape(nchunk, 2 * GQK)

    b_f = cum3[:, :, :GQK].reshape(tm, GQK)
    e_f = (tot3 - cum3)[:, :, :GQK].reshape(tm, GQK)
    b_b = (tot3 - cum3 + la3)[:, :, GQK:].reshape(tm, GQK)
    e_b = (cum3 - la3)[:, :, GQK:].reshape(tm, GQK)
    qf_ref[0] = (qg * jnp.exp(b_f)).astype(bf16)
    kif_ref[0] = (kg * jnp.exp(-b_f)).astype(bf16)
    kef_ref[0] = (kg * jnp.exp(e_f)).astype(bf16)
    qb_ref[0] = (qg * jnp.exp(b_b)).astype(bf16)
    kib_ref[0] = (kg * jnp.exp(-b_b)).astype(bf16)
    keb_ref[0] = (kg * jnp.exp(e_b)).astype(bf16)


def _rope_tables(L):
    n = GLA_DK // 4
    freqs = ROPE_THETA ** (-np.arange(n, dtype=np.float64) / n)
    lane = np.arange(GQK)
    fl = freqs[lane % n]
    is_row = (lane % GLA_DK) < GLA_DK // 2
    sign = np.where((lane % 32) < 16, -1.0, 1.0)
    rows = np.arange(L // GRID_W)[:, None] * fl[None]
    cols = np.arange(GRID_W)[:, None] * fl[None]
    rc = np.where(is_row, np.cos(rows), 0.0)
    rs = np.where(is_row, np.sin(rows) * sign, 0.0)
    cc = np.where(~is_row, np.cos(cols), 0.0)
    cs = np.where(~is_row, np.sin(cols) * sign, 0.0)
    return [jnp.asarray(t, f32) for t in (rc, rs, cc, cs)]


def _in_proj(x, mod3, norm1_g, w_h, w_t, gq_t, gk_t, bd, w_a, b_a):
    B, L, _ = x.shape
    tm = TM_PROJ
    rows_per = tm // GRID_W
    rc, rs, cc, cs = _rope_tables(L)
    blk = np.kron(np.eye(256 // CHUNK), np.tril(np.ones((CHUNK, CHUNK)))).astype(np.float32)
    tl = jnp.asarray(blk, bf16)
    tok = lambda n: pl.BlockSpec((1, tm, n), lambda b, i: (b, i, 0))
    sds = lambda n, dt: jax.ShapeDtypeStruct((B, L, n), dt)
    return pl.pallas_call(
        _proj_kernel,
        grid=(B, L // tm),
        in_specs=[tok(D),
                  pl.BlockSpec((1, 1, D), lambda b, i: (b, 0, 0)),
                  pl.BlockSpec((1, 1, D), lambda b, i: (b, 0, 1)),
                  _const_spec((1, D)), _const_spec((D, W_HEAD)), _const_spec((D, W_TAIL)),
                  _const_spec((1, NA_W)), _const_spec((1, NA_W)), _const_spec((256, 256)),
                  pl.BlockSpec((rows_per, GQK), lambda b, i: (i, 0)),
                  pl.BlockSpec((rows_per, GQK), lambda b, i: (i, 0)),
                  _const_spec((GRID_W, GQK)), _const_spec((GRID_W, GQK)),
                  _const_spec((128, 2 * GQK)), _const_spec((1, 2 * GQK)), _const_spec((256, 256))],
        out_specs=[tok(NA_W), tok(NA_W), tok(NA_W)] + [tok(GQK)] * 6
                  + [tok(GV), tok(GV), tok(D), tok(D),
                     pl.BlockSpec((1, tm // CHUNK, 2 * GQK), lambda b, i: (b, i, 0))],
        out_shape=[sds(NA_W, bf16)] * 3 + [sds(GQK, bf16)] * 6
                  + [sds(GV, bf16), sds(GV, bf16), sds(D, bf16), sds(D, bf16),
                     jax.ShapeDtypeStruct((B, L // CHUNK, 2 * GQK), f32)],
        compiler_params=pltpu.CompilerParams(
            dimension_semantics=("parallel", "parallel"), vmem_limit_bytes=VMEM_LIMIT),
        name="in_proj",
    )(x, mod3, mod3, norm1_g, w_h, w_t, gq_t, gk_t, bd, rc, rs, cc, cs, w_a, b_a, tl)


def _lane_blocks(t):
    return [t[:, c:c + 128] for c in range(0, t.shape[1], 128)]


def _na_kernel(q_ref, k_ref, v_ref, kc_ref, vc_ref, t2_ref, o_ref):
    i = pl.program_id(1)
    rows = k_ref.shape[1] // GRID_W
    lane_head = lax.broadcasted_iota(jnp.int32, (GRID_W, 256), 1) >> 6
    hmask = [lane_head == h for h in range(4)]
    for rr in range(NA_R):
        r = i * NA_R + rr
        rs = jnp.clip(r - NA_WR // 2, 0, rows - NA_WR)
        d0 = rs - r + (NA_WR - 1)
        start = pl.multiple_of(rs * GRID_W, GRID_W)
        outs = []
        for g in range(2):
            ls = slice(256 * g, 256 * g + 256)
            q = q_ref[0, rr * GRID_W:(rr + 1) * GRID_W, ls]
            qs = jnp.concatenate([jnp.where(m, q, jnp.zeros_like(q)) for m in hmask], axis=0)
            kw = k_ref[0, pl.ds(start, NA_WR * GRID_W), ls]
            vw = v_ref[0, pl.ds(start, NA_WR * GRID_W), ls]
            bias = jnp.concatenate(
                [jnp.concatenate([t2_ref[4 * g + h, d0 + 2 * jj] for jj in range(NA_WR // 2)], axis=1)
                 for h in range(4)], axis=0)
            s_loc = _dot_nt(qs, kw) + bias
            s_ctx = _dot_nt(qs, kc_ref[0, :, ls])
            m = jnp.max(functools.reduce(jnp.maximum, _lane_blocks(s_loc) + _lane_blocks(s_ctx)),
                        axis=-1, keepdims=True)
            p_loc = jnp.exp2(s_loc - m)
            p_ctx = jnp.exp2(s_ctx - m)
            den = jnp.sum(functools.reduce(jnp.add, _lane_blocks(p_loc) + _lane_blocks(p_ctx)),
                          axis=-1, keepdims=True)
            acc = _dot(p_loc.astype(bf16), vw) + _dot(p_ctx.astype(bf16), vc_ref[0, :, ls])
            acc = acc * (1.0 / den)
            o = jnp.zeros((GRID_W, 256), f32)
            for h in range(4):
                o = o + jnp.where(hmask[h], acc[h * GRID_W:(h + 1) * GRID_W], 0.0)
            outs.append(o)
        o_ref[0, rr * GRID_W:(rr + 1) * GRID_W, :] = jnp.concatenate(outs, axis=1).astype(bf16)


def _na_attn(q, k, v, kc, vc, t2):
    B, L, _ = q.shape
    nblk = L // (NA_R * GRID_W)
    return pl.pallas_call(
        _na_kernel,
        grid=(B, nblk),
        in_specs=[pl.BlockSpec((1, NA_R * GRID_W, NA_W), lambda b, i: (b, i, 0)),
                  pl.BlockSpec((1, L, NA_W), lambda b, i: (b, 0, 0), pipeline_mode=pl.Buffered(1)),
                  pl.BlockSpec((1, L, NA_W), lambda b, i: (b, 0, 0), pipeline_mode=pl.Buffered(1)),
                  pl.BlockSpec((1, CTX, NA_W), lambda b, i: (b, 0, 0)),
                  pl.BlockSpec((1, CTX, NA_W), lambda b, i: (b, 0, 0)),
                  _const_spec((NA_H, 2 * NA_WR - 1, GRID_W, 128))],
        out_specs=pl.BlockSpec((1, NA_R * GRID_W, NA_W), lambda b, i: (b, i, 0)),
        out_shape=jax.ShapeDtypeStruct((B, L, NA_W), bf16),
        compiler_params=pltpu.CompilerParams(
            dimension_semantics=("parallel", "parallel"), vmem_limit_bytes=VMEM_LIMIT),
        name="na_attn",
    )(q, k, v, kc, vc, t2)


def _gla_chunk(q, ki, ke, v, dcol, s_ref, tri, hmask):
    qs = jnp.concatenate([jnp.where(m, q, jnp.zeros_like(q)) for m in hmask], axis=0)
    att = jnp.where(tri, _dot_nt(qs, ki), 0.0).astype(bf16)
    o = _dot(q, s_ref[...].astype(bf16))
    intra = [_dot(att[h * CHUNK:(h + 1) * CHUNK], v[:, h * GLA_DV:(h + 1) * GLA_DV]) for h in range(GLA_H)]
    o = o + jnp.concatenate(intra, axis=1)
    ket = ke.T
    for h in range(GLA_H):
        rs = slice(h * GLA_DK, (h + 1) * GLA_DK)
        cs = slice(h * GLA_DV, (h + 1) * GLA_DV)
        s_ref[rs, cs] = dcol[rs] * s_ref[rs, cs] + _dot(ket[rs], v[:, cs])
    return o


def _gla_kernel(qf_ref, kif_ref, kef_ref, vf_ref, decf_ref, qb_ref, kib_ref, keb_ref, vb_ref, decb_ref,
                s0f_ref, s0b_ref, of_ref, ob_ref, sf_ref, sb_ref):
    @pl.when(pl.program_id(1) == 0)
    def _():
        sf_ref[...] = s0f_ref[0]
        sb_ref[...] = s0b_ref[0]

    lane_head = lax.broadcasted_iota(jnp.int32, (CHUNK, GQK), 1) >> 6
    hmask = [lane_head == h for h in range(GLA_H)]
    ri = lax.broadcasted_iota(jnp.int32, (GLA_H * CHUNK, CHUNK), 0) & (CHUNK - 1)
    ci = lax.broadcasted_iota(jnp.int32, (GLA_H * CHUNK, CHUNK), 1)
    tri_f = ri >= ci
    tri_b = ri <= ci
    dect_f = decf_ref[0][:, :GQK].T
    dect_b = decb_ref[0][:, GQK:].T
    for c in range(GLA_CB):
        ts = slice(c * CHUNK, (c + 1) * CHUNK)
        of_ref[0, ts, :] = _gla_chunk(qf_ref[0, ts, :], kif_ref[0, ts, :], kef_ref[0, ts, :], vf_ref[0, ts, :],
                                      dect_f[:, c:c + 1], sf_ref, tri_f, hmask)
        cb = GLA_CB - 1 - c
        tb = slice(cb * CHUNK, (cb + 1) * CHUNK)
        ob_ref[0, tb, :] = _gla_chunk(qb_ref[0, tb, :], kib_ref[0, tb, :], keb_ref[0, tb, :], vb_ref[0, tb, :],
                                      dect_b[:, cb:cb + 1], sb_ref, tri_b, hmask)


def _gla_scan(qf, kif, kef, qb, kib, keb, vg, dec, s0f, s0b):
    B, L, _ = vg.shape
    tm = GLA_CB * CHUNK
    nb = L // tm
    fw = lambda n: pl.BlockSpec((1, tm, n), lambda b, i: (b, i, 0))
    bw = lambda n: pl.BlockSpec((1, tm, n), lambda b, i: (b, nb - 1 - i, 0))
    st = pl.BlockSpec((1, GQK, GV), lambda b, i: (b, 0, 0))
    return pl.pallas_call(
        _gla_kernel,
        grid=(B, nb),
        in_specs=[fw(GQK), fw(GQK), fw(GQK), fw(GV),
                  pl.BlockSpec((1, GLA_CB, 2 * GQK), lambda b, i: (b, i, 0)),
                  bw(GQK), bw(GQK), bw(GQK), bw(GV),
                  pl.BlockSpec((1, GLA_CB, 2 * GQK), lambda b, i: (b, nb - 1 - i, 0)),
                  st, st],
        out_specs=[fw(GV), bw(GV)],
        out_shape=[jax.ShapeDtypeStruct((B, L, GV), f32)] * 2,
        scratch_shapes=[pltpu.VMEM((GQK, GV), f32), pltpu.VMEM((GQK, GV), f32)],
        compiler_params=pltpu.CompilerParams(
            dimension_semantics=("parallel", "arbitrary"), vmem_limit_bytes=VMEM_LIMIT),
        name="gla_scan",
    )(qf, kif, kef, vg, dec, qb, kib, keb, vg, dec, s0f, s0b)


def _merge_ffn_kernel(x_ref, ona_ref, of_ref, ob_ref, og_ref, ma_ref, mb_ref,
                      g1_ref, sh2_ref, sc2_ref, g2_ref, n2_ref, gg_ref,
                      wna_ref, wgl_ref, wo_ref, wg_ref, wu_ref, wd_ref, o_ref):
    og = of_ref[0] + ob_ref[0]
    parts = []
    for h in range(GLA_H):
        seg = og[:, h * GLA_DV:(h + 1) * GLA_DV]
        ms = jnp.mean(seg * seg, axis=-1, keepdims=True)
        parts.append(seg * lax.rsqrt(ms + EPS) * gg_ref[...])
    ogl = (jnp.concatenate(parts, axis=1) * og_ref[0].astype(f32)).astype(bf16)
    y = (ma_ref[0].astype(f32) * _dot(ona_ref[0], wna_ref[...])
         + mb_ref[0].astype(f32) * _dot(ogl, wgl_ref[...]))
    x1 = x_ref[0] + g1_ref[0] * _dot(y.astype(bf16), wo_ref[...])
    h2 = _norm_mod(x1, n2_ref[...], sh2_ref[0], sc2_ref[0]).astype(bf16)
    act = (_silu(_dot(h2, wg_ref[...])) * _dot(h2, wu_ref[...])).astype(bf16)
    o_ref[0] = x1 + g2_ref[0] * _dot(act, wd_ref[...])


def _merge_ffn(x, ona, of, ob, og, ma, mb, mod3, norm2_g, gg, wna, wgl, wo, wg, wu, wd):
    B, L, _ = x.shape
    tm = TM_FFN
    tok = lambda n: pl.BlockSpec((1, tm, n), lambda b, i: (b, i, 0))
    modv = lambda j: pl.BlockSpec((1, 1, D), lambda b, i: (b, 0, j))
    return pl.pallas_call(
        _merge_ffn_kernel,
        grid=(B, L // tm),
        in_specs=[tok(D), tok(NA_W), tok(GV), tok(GV), tok(GV), tok(D), tok(D),
                  modv(2), modv(3), modv(4), modv(5),
                  _const_spec((1, D)), _const_spec((1, GLA_DV)),
                  _const_spec((NA_W, D)), _const_spec((GV, D)), _const_spec((D, D)),
                  _const_spec((D, D_FF)), _const_spec((D, D_FF)), _const_spec((D_FF, D))],
        out_specs=tok(D),
        out_shape=jax.ShapeDtypeStruct((B, L, D), f32),
        compiler_params=pltpu.CompilerParams(
            dimension_semantics=("parallel", "parallel"), vmem_limit_bytes=VMEM_LIMIT),
        name="merge_ffn",
    )(x, ona, of, ob, og, ma, mb, mod3, mod3, mod3, mod3, norm2_g, gg, wna, wgl, wo, wg, wu, wd)


def kernel(x, c, ctx, c_ctx, w_mod, b_mod, norm1_g, norm2_g, w_in, na_q_norm_g, na_k_norm_g, na_rpb,
           gla_w_alpha, gla_b_alpha, gla_norm_g, w_branch_na, w_branch_gla, w_out,
           w_ffn_gate, w_ffn_up, w_ffn_down):
    B, L, _ = x.shape
    assert w_mod.shape[0] == 1 and L % (NA_R * GRID_W) == 0 and L % TM_PROJ == 0

    w_h = _w_head(w_in)
    w_t = _w_tail(w_in)
    lr0 = 128 - LR_SHIFT
    w_a = jnp.zeros((128, 2 * GQK), f32)
    w_a = w_a.at[lr0:lr0 + GLA_RANK, :GQK].set(gla_w_alpha[0, 0])
    w_a = w_a.at[lr0 + GLA_RANK:lr0 + 2 * GLA_RANK, GQK:].set(gla_w_alpha[0, 1])
    w_a = w_a.astype(bf16)
    b_a = gla_b_alpha[0].reshape(1, 2 * GQK)
    gq_t = jnp.tile(na_q_norm_g[0] * (NA_DH ** -0.5 * LOG2E), NA_H).reshape(1, NA_W)
    gk_t = jnp.tile(na_k_norm_g[0], NA_H).reshape(1, NA_W)
    bd = jnp.asarray(np.kron(np.eye(4), np.full((NA_DH, NA_DH), 1.0 / NA_DH)), bf16)

    mod = _adaln_mod(c, c_ctx, w_mod[0], b_mod[0])
    mod3 = mod.reshape(8, 1, 6 * D)
    t2 = _rpb_expand(na_rpb[0])

    kc, vc, s0f, s0b = _ctx_side(ctx, mod3, norm1_g, w_h, w_t, w_a, b_a, gk_t, bd, B)
    (qna, kna, vna, qf, kif, kef, qb, kib, keb, vg, og, ma, mb, dec) = _in_proj(
        x, mod3, norm1_g, w_h, w_t, gq_t, gk_t, bd, w_a, b_a)
    ona = _na_attn(qna, kna, vna, kc, vc, t2)
    of, ob = _gla_scan(qf, kif, kef, qb, kib, keb, vg, dec, s0f, s0b)
    return _merge_ffn(x, ona, of, ob, og, ma, mb, mod3, norm2_g, gla_norm_g,
                      w_branch_na[0].astype(bf16), w_branch_gla[0].astype(bf16), w_out[0].astype(bf16),
                      w_ffn_gate[0].astype(bf16), w_ffn_up[0].astype(bf16), w_ffn_down[0].astype(bf16))
```

```python
import functools

import numpy as np
import jax
import jax.numpy as jnp
from jax import lax
from jax.experimental import pallas as pl
from jax.experimental.pallas import tpu as pltpu

D = 1024
GRID_W = 64
CTX = 256
NA_H = 8
NA_DH = 64
NA_WR = 8
NA_WC = 16
GLA_H = 4
GLA_DK = 64
GLA_DV = 128
GLA_RANK = 16
GLA_TAU = 16.0
GLA_LA_MIN = -1.0
CHUNK = 64
ROPE_THETA = 10000.0
D_FF = 2816
EPS = 1e-6
NEG_INF = -1e30
NA_W = NA_H * NA_DH
GQK = GLA_H * GLA_DK
GV = GLA_H * GLA_DV

OFF_QNA, OFF_KNA, OFF_VNA = 0, 512, 1024
OFF_QG, OFF_KG, OFF_VG, OFF_OG = 1536, 1792, 2048, 2560
OFF_LR, OFF_MA, OFF_MB = 3072, 3104, 4128
N_LR = 2 * GLA_RANK
D_IN = 5152
LOG2E = 1.4426950408889634

VMEM_LIMIT = 56 * 1024 * 1024

TM_PROJ = 512
TM_FFN = 256
NA_R = 4
GLA_CB = 8

f32 = jnp.float32
bf16 = jnp.bfloat16


def _const_spec(shape):
    nd = len(shape)
    return pl.BlockSpec(shape, lambda *_: (0,) * nd, pipeline_mode=pl.Buffered(1))


def _dot(a, b):
    return jnp.dot(a, b, preferred_element_type=f32)


def _dot_nt(a, b):
    return lax.dot_general(a, b, (((1,), (1,)), ((), ())), preferred_element_type=f32)


def _split2(x):
    hi = x.astype(bf16)
    lo = (x - hi.astype(f32)).astype(bf16)
    return hi, lo


def _log_sigmoid(z):
    return jnp.minimum(z, 0.0) - jnp.log(1.0 + jnp.exp(-jnp.abs(z)))


def _silu(x):
    return x * (1.0 / (1.0 + jnp.exp(-x)))


def _sigmoid(x):
    return 1.0 / (1.0 + jnp.exp(-x))


def _split3_dot(r, oh):
    h1 = r.astype(bf16)
    r2 = r - h1.astype(f32)
    h2 = r2.astype(bf16)
    h3 = (r2 - h2.astype(f32)).astype(bf16)
    return _dot(h1, oh) + _dot(h2, oh) + _dot(h3, oh)


def _rpb_expand_kernel(r0_ref, r1_ref, oh0_ref, oh1_ref, mask_ref, o_ref):
    gathered = _split3_dot(r0_ref[...], oh0_ref[...]) + _split3_dot(r1_ref[...], oh1_ref[...])
    o_ref[...] = gathered * LOG2E + mask_ref[...]


def _rpb_expand(rpb):
    nd = 2 * NA_WR - 1
    nrow = NA_H * nd
    ncol = 2 * NA_WC - 1
    nxt = jnp.concatenate([rpb[:, 1:], jnp.zeros_like(rpb[:, :1])], axis=1)
    r0 = jnp.zeros((128, 128), f32).at[:nrow, :ncol].set(rpb.reshape(nrow, ncol))
    r1 = jnp.zeros((128, 128), f32).at[:nrow, :ncol].set(nxt.reshape(nrow, ncol))
    q = np.arange(GRID_W)
    cs = np.clip(q - NA_WC // 2, 0, GRID_W - NA_WC)
    in_win = (q[None, :] >= cs[:, None]) & (q[None, :] < cs[:, None] + NA_WC)
    dc = np.clip(q[None, :] - q[:, None], -(NA_WC - 1), NA_WC - 1) + NA_WC - 1
    oh = np.zeros((2, 128, GRID_W, 128), np.float32)
    qq, kk = np.nonzero(in_win)
    oh[0, dc[qq, kk], qq, kk] = 1.0
    oh[1, dc[qq, kk], qq, kk + GRID_W] = 1.0
    oh = oh.reshape(2, 128, GRID_W * 128)
    mask = np.where(np.concatenate([in_win, in_win], axis=1), 0.0, NEG_INF).reshape(1, GRID_W * 128)
    out = pl.pallas_call(
        _rpb_expand_kernel,
        out_shape=jax.ShapeDtypeStruct((128, GRID_W * 128), f32),
        name="rpb_expand",
    )(r0, r1, jnp.asarray(oh[0], bf16), jnp.asarray(oh[1], bf16), jnp.asarray(mask, f32))
    return out[:nrow].reshape(NA_H, nd, GRID_W, 128)


def _cast_kernel(x_ref, o_ref):
    o_ref[...] = x_ref[...].astype(bf16)


def _cast_bf16(w, rows):
    R, C = w.shape
    return pl.pallas_call(
        _cast_kernel,
        grid=(R // rows,),
        in_specs=[pl.BlockSpec((rows, C), lambda j: (j, 0))],
        out_specs=pl.BlockSpec((rows, C), lambda j: (j, 0)),
        out_shape=jax.ShapeDtypeStruct((R, C), bf16),
        name="cast_bf16",
    )(w)


def _mod_kernel(a_ref, w_ref, b_ref, o_ref):
    a = _silu(a_ref[...]).astype(bf16)
    o_ref[...] = _dot(a, w_ref[...].astype(bf16)) + b_ref[...]


def _adaln_mod(c, c_ctx, w_mod, b_mod):
    B = c.shape[0]
    a = jnp.zeros((8, D), f32).at[:B].set(c).at[B].set(c_ctx)
    n = w_mod.shape[1]
    return pl.pallas_call(
        _mod_kernel,
        grid=(n // D,),
        in_specs=[pl.BlockSpec((8, D), lambda j: (0, 0)),
                  pl.BlockSpec((D, D), lambda j: (0, j)),
                  pl.BlockSpec((1, D), lambda j: (0, j))],
        out_specs=pl.BlockSpec((8, D), lambda j: (0, j)),
        out_shape=jax.ShapeDtypeStruct((8, n), f32),
        name="adaln_mod",
    )(a, w_mod, b_mod.reshape(1, n))


def _norm_mod(x, g, sh, sc):
    ms = jnp.mean(x * x, axis=-1, keepdims=True)
    return (x * lax.rsqrt(ms + EPS) * g) * (1.0 + sc) + sh


def _head_rms(t, bd, g):
    sq = (t * t).astype(bf16)
    ms = jnp.concatenate([_dot(sq[:, :256], bd), _dot(sq[:, 256:], bd)], axis=1)
    return t * lax.rsqrt(ms + EPS) * g


def _log_alpha(zl, wa_ref, ba_ref):
    z = _dot(zl.astype(bf16), wa_ref[...]) + ba_ref[...]
    return jnp.maximum(_log_sigmoid(z) * (1.0 / GLA_TAU), GLA_LA_MIN)


def _ctx_kernel(ctx_ref, sh_ref, sc_ref, g1_ref, wk_ref, wv_ref, wkg_ref, wvg_ref, wlr_ref,
                wa_ref, ba_ref, gk_ref, bd_ref, su_ref, sl_ref,
                kc_ref, vc_ref, sf_ref, sb_ref):
    h = _norm_mod(ctx_ref[0], g1_ref[...], sh_ref[0], sc_ref[0]).astype(bf16)
    k = _dot_nt(h, wk_ref[...])
    kc_ref[0] = _head_rms(k, bd_ref[...], gk_ref[...]).astype(bf16)
    vc_ref[0] = _dot_nt(h, wv_ref[...]).astype(bf16)
    kg = _dot_nt(h, wkg_ref[...])
    vg = _dot_nt(h, wvg_ref[...]).astype(bf16)
    la = _log_alpha(_dot_nt(h, wlr_ref[...]), wa_ref, ba_ref)
    hi, lo = _split2(la)
    ef = _dot(su_ref[...], hi[:, :GQK]) + _dot(su_ref[...], lo[:, :GQK])
    eb = _dot(sl_ref[...], hi[:, GQK:]) + _dot(sl_ref[...], lo[:, GQK:])
    r = lax.broadcasted_iota(jnp.int32, (GQK, GV), 0) >> 6
    cc = lax.broadcasted_iota(jnp.int32, (GQK, GV), 1) >> 7
    diag = r == cc
    for e, out in ((ef, sf_ref), (eb, sb_ref)):
        kw = (kg * jnp.exp(e)).astype(bf16)
        u = _dot(kw.T, vg)
        out[0] = jnp.where(diag, u, 0.0)


def _ctx_side(ctx, mod3, norm1_g, w_t, w_a, b_a, gk_t, bd, n_b):
    su = jnp.asarray(np.triu(np.ones((CTX, CTX), np.float32), 1), bf16)
    sl = jnp.asarray(np.tril(np.ones((CTX, CTX), np.float32), -1), bf16)
    B = ctx.shape[0]
    cst = lambda shape, idx: pl.BlockSpec(shape, lambda b: idx)
    return pl.pallas_call(
        _ctx_kernel,
        grid=(B,),
        in_specs=[pl.BlockSpec((1, CTX, D), lambda b: (b, 0, 0)),
                  cst((1, 1, D), (n_b, 0, 0)), cst((1, 1, D), (n_b, 0, 1)),
                  cst((1, D), (0, 0)),
                  cst((NA_W, D), (OFF_KNA // NA_W, 0)), cst((NA_W, D), (OFF_VNA // NA_W, 0)),
                  cst((GQK, D), (OFF_KG // GQK, 0)), cst((GV, D), (OFF_VG // GV, 0)),
                  cst((N_LR, D), (OFF_LR // N_LR, 0)),
                  cst((N_LR, 2 * GQK), (0, 0)), cst((1, 2 * GQK), (0, 0)),
                  cst((1, NA_W), (0, 0)), cst((256, 256), (0, 0)),
                  cst((CTX, CTX), (0, 0)), cst((CTX, CTX), (0, 0))],
        out_specs=[pl.BlockSpec((1, CTX, NA_W), lambda b: (b, 0, 0)),
                   pl.BlockSpec((1, CTX, NA_W), lambda b: (b, 0, 0)),
                   pl.BlockSpec((1, GQK, GV), lambda b: (b, 0, 0)),
                   pl.BlockSpec((1, GQK, GV), lambda b: (b, 0, 0))],
        out_shape=[jax.ShapeDtypeStruct((B, CTX, NA_W), bf16),
                   jax.ShapeDtypeStruct((B, CTX, NA_W), bf16),
                   jax.ShapeDtypeStruct((B, GQK, GV), f32),
                   jax.ShapeDtypeStruct((B, GQK, GV), f32)],
        compiler_params=pltpu.CompilerParams(vmem_limit_bytes=VMEM_LIMIT),
        name="ctx_side",
    )(ctx, mod3, mod3, norm1_g, w_t, w_t, w_t, w_t, w_t, w_a, b_a, gk_t, bd, su, sl)


def _proj_kernel(x_ref, sh_ref, sc_ref, g1_ref, w_ref, gq_ref, gk_ref, bd_ref,
                 rc_ref, rs_ref, cc_ref, cs_ref, wa_ref, ba_ref, tl_ref,
                 qna_ref, kna_ref, vna_ref, qf_ref, kif_ref, kef_ref, qb_ref, kib_ref, keb_ref,
                 vg_ref, og_ref, ma_ref, mb_ref, dec_ref):
    tm = x_ref.shape[1]
    nchunk = tm // CHUNK
    hb = _norm_mod(x_ref[0], g1_ref[...], sh_ref[0], sc_ref[0]).astype(bf16)

    def proj(off, n):
        return _dot_nt(hb, w_ref[off:off + n, :])

    bd = bd_ref[...]
    qna_ref[0] = _head_rms(proj(OFF_QNA, NA_W), bd, gq_ref[...]).astype(bf16)
    kna_ref[0] = _head_rms(proj(OFF_KNA, NA_W), bd, gk_ref[...]).astype(bf16)
    vna_ref[0] = proj(OFF_VNA, NA_W).astype(bf16)
    vg_ref[0] = proj(OFF_VG, GV).astype(bf16)
    og_ref[0] = _silu(proj(OFF_OG, GV)).astype(bf16)
    ma_ref[0] = _sigmoid(proj(OFF_MA, D)).astype(bf16)
    mb_ref[0] = _sigmoid(proj(OFF_MB, D)).astype(bf16)

    cos_t = (rc_ref[...][:, None, :] + cc_ref[...][None]).reshape(tm, GQK)
    sin_t = (rs_ref[...][:, None, :] + cs_ref[...][None]).reshape(tm, GQK)
    lane = lax.broadcasted_iota(jnp.int32, (tm, GQK), 1)
    first_half = (lane & 31) < 16

    def rope(t):
        partner = jnp.where(first_half, pltpu.roll(t, GQK - 16, 1), pltpu.roll(t, 16, 1))
        return t * cos_t + partner * sin_t

    qg = rope(proj(OFF_QG, GQK)) * (GLA_DK ** -0.5)
    kg = rope(proj(OFF_KG, GQK))

    la = _log_alpha(proj(OFF_LR, N_LR), wa_ref, ba_ref)
    hi, lo = _split2(la)
    tl = tl_ref[...]
    cum = jnp.concatenate(
        [_dot(tl, hi[s:s + 256]) + _dot(tl, lo[s:s + 256]) for s in range(0, tm, 256)], axis=0)
    cum3 = cum.reshape(nchunk, CHUNK, 2 * GQK)
    tot3 = cum3[:, CHUNK - 1:CHUNK, :]
    la3 = la.reshape(nchunk, CHUNK, 2 * GQK)
    dec_ref[0] = jnp.exp(tot3).reshape(nchunk, 2 * GQK)

    b_f = cum3[:, :, :GQK].reshape(tm, GQK)
    e_f = (tot3 - cum3)[:, :, :GQK].reshape(tm, GQK)
    b_b = (tot3 - cum3 + la3)[:, :, GQK:].reshape(tm, GQK)
    e_b = (cum3 - la3)[:, :, GQK:].reshape(tm, GQK)
    qf_ref[0] = (qg * jnp.exp(b_f)).astype(bf16)
    kif_ref[0] = (kg * jnp.exp(-b_f)).astype(bf16)
    kef_ref[0] = (kg * jnp.exp(e_f)).astype(bf16)
    qb_ref[0] = (qg * jnp.exp(b_b)).astype(bf16)
    kib_ref[0] = (kg * jnp.exp(-b_b)).astype(bf16)
    keb_ref[0] = (kg * jnp.exp(e_b)).astype(bf16)


def _rope_tables(L):
    n = GLA_DK // 4
    freqs = ROPE_THETA ** (-np.arange(n, dtype=np.float64) / n)
    lane = np.arange(GQK)
    fl = freqs[lane % n]
    is_row = (lane % GLA_DK) < GLA_DK // 2
    sign = np.where((lane % 32) < 16, -1.0, 1.0)
    rows = np.arange(L // GRID_W)[:, None] * fl[None]
    cols = np.arange(GRID_W)[:, None] * fl[None]
    rc = np.where(is_row, np.cos(rows), 0.0)
    rs = np.where(is_row, np.sin(rows) * sign, 0.0)
    cc = np.where(~is_row, np.cos(cols), 0.0)
    cs = np.where(~is_row, np.sin(cols) * sign, 0.0)
    return [jnp.asarray(t, f32) for t in (rc, rs, cc, cs)]


def _in_proj(x, mod3, norm1_g, w_t, gq_t, gk_t, bd, w_a, b_a):
    B, L, _ = x.shape
    tm = TM_PROJ
    rows_per = tm // GRID_W
    rc, rs, cc, cs = _rope_tables(L)
    blk = np.kron(np.eye(256 // CHUNK), np.tril(np.ones((CHUNK, CHUNK)))).astype(np.float32)
    tl = jnp.asarray(blk, bf16)
    tok = lambda n: pl.BlockSpec((1, tm, n), lambda b, i: (b, i, 0))
    sds = lambda n, dt: jax.ShapeDtypeStruct((B, L, n), dt)
    return pl.pallas_call(
        _proj_kernel,
        grid=(B, L // tm),
        in_specs=[tok(D),
                  pl.BlockSpec((1, 1, D), lambda b, i: (b, 0, 0)),
                  pl.BlockSpec((1, 1, D), lambda b, i: (b, 0, 1)),
                  _const_spec((1, D)), _const_spec((D_IN, D)),
                  _const_spec((1, NA_W)), _const_spec((1, NA_W)), _const_spec((256, 256)),
                  pl.BlockSpec((rows_per, GQK), lambda b, i: (i, 0)),
                  pl.BlockSpec((rows_per, GQK), lambda b, i: (i, 0)),
                  _const_spec((GRID_W, GQK)), _const_spec((GRID_W, GQK)),
                  _const_spec((N_LR, 2 * GQK)), _const_spec((1, 2 * GQK)), _const_spec((256, 256))],
        out_specs=[tok(NA_W), tok(NA_W), tok(NA_W)] + [tok(GQK)] * 6
                  + [tok(GV), tok(GV), tok(D), tok(D),
                     pl.BlockSpec((1, tm // CHUNK, 2 * GQK), lambda b, i: (b, i, 0))],
        out_shape=[sds(NA_W, bf16)] * 3 + [sds(GQK, bf16)] * 6
                  + [sds(GV, bf16), sds(GV, bf16), sds(D, bf16), sds(D, bf16),
                     jax.ShapeDtypeStruct((B, L // CHUNK, 2 * GQK), f32)],
        compiler_params=pltpu.CompilerParams(
            dimension_semantics=("parallel", "parallel"), vmem_limit_bytes=VMEM_LIMIT),
        name="in_proj",
    )(x, mod3, mod3, norm1_g, w_t, gq_t, gk_t, bd, rc, rs, cc, cs, w_a, b_a, tl)


def _lane_blocks(t):
    return [t[:, c:c + 128] for c in range(0, t.shape[1], 128)]


def _na_kernel(q_ref, k_ref, v_ref, kc_ref, vc_ref, t2_ref, o_ref):
    i = pl.program_id(1)
    rows = k_ref.shape[1] // GRID_W
    lane_head = lax.broadcasted_iota(jnp.int32, (GRID_W, 256), 1) >> 6
    hmask = [lane_head == h for h in range(4)]
    for rr in range(NA_R):
        r = i * NA_R + rr
        rs = jnp.clip(r - NA_WR // 2, 0, rows - NA_WR)
        d0 = rs - r + (NA_WR - 1)
        start = pl.multiple_of(rs * GRID_W, GRID_W)
        outs = []
        for g in range(2):
            ls = slice(256 * g, 256 * g + 256)
            q = q_ref[0, rr * GRID_W:(rr + 1) * GRID_W, ls]
            qs = jnp.concatenate([jnp.where(m, q, jnp.zeros_like(q)) for m in hmask], axis=0)
            kw = k_ref[0, pl.ds(start, NA_WR * GRID_W), ls]
            vw = v_ref[0, pl.ds(start, NA_WR * GRID_W), ls]
            bias = jnp.concatenate(
                [jnp.concatenate([t2_ref[4 * g + h, d0 + 2 * jj] for jj in range(NA_WR // 2)], axis=1)
                 for h in range(4)], axis=0)
            s_loc = _dot_nt(qs, kw) + bias
            s_ctx = _dot_nt(qs, kc_ref[0, :, ls])
            m = jnp.max(functools.reduce(jnp.maximum, _lane_blocks(s_loc) + _lane_blocks(s_ctx)),
                        axis=-1, keepdims=True)
            p_loc = jnp.exp2(s_loc - m)
            p_ctx = jnp.exp2(s_ctx - m)
            den = jnp.sum(functools.reduce(jnp.add, _lane_blocks(p_loc) + _lane_blocks(p_ctx)),
                          axis=-1, keepdims=True)
            acc = _dot(p_loc.astype(bf16), vw) + _dot(p_ctx.astype(bf16), vc_ref[0, :, ls])
            acc = acc * (1.0 / den)
            o = jnp.zeros((GRID_W, 256), f32)
            for h in range(4):
                o = o + jnp.where(hmask[h], acc[h * GRID_W:(h + 1) * GRID_W], 0.0)
            outs.append(o)
        o_ref[0, rr * GRID_W:(rr + 1) * GRID_W, :] = jnp.concatenate(outs, axis=1).astype(bf16)


def _na_attn(q, k, v, kc, vc, t2):
    B, L, _ = q.shape
    nblk = L // (NA_R * GRID_W)
    return pl.pallas_call(
        _na_kernel,
        grid=(B, nblk),
        in_specs=[pl.BlockSpec((1, NA_R * GRID_W, NA_W), lambda b, i: (b, i, 0)),
                  pl.BlockSpec((1, L, NA_W), lambda b, i: (b, 0, 0), pipeline_mode=pl.Buffered(1)),
                  pl.BlockSpec((1, L, NA_W), lambda b, i: (b, 0, 0), pipeline_mode=pl.Buffered(1)),
                  pl.BlockSpec((1, CTX, NA_W), lambda b, i: (b, 0, 0)),
                  pl.BlockSpec((1, CTX, NA_W), lambda b, i: (b, 0, 0)),
                  _const_spec((NA_H, 2 * NA_WR - 1, GRID_W, 128))],
        out_specs=pl.BlockSpec((1, NA_R * GRID_W, NA_W), lambda b, i: (b, i, 0)),
        out_shape=jax.ShapeDtypeStruct((B, L, NA_W), bf16),
        compiler_params=pltpu.CompilerParams(
            dimension_semantics=("parallel", "parallel"), vmem_limit_bytes=VMEM_LIMIT),
        name="na_attn",
    )(q, k, v, kc, vc, t2)


def _gla_chunk(q, ki, ke, v, dcol, s_ref, tri, hmask):
    qs = jnp.concatenate([jnp.where(m, q, jnp.zeros_like(q)) for m in hmask], axis=0)
    att = jnp.where(tri, _dot_nt(qs, ki), 0.0).astype(bf16)
    o = _dot(q, s_ref[...].astype(bf16))
    intra = [_dot(att[h * CHUNK:(h + 1) * CHUNK], v[:, h * GLA_DV:(h + 1) * GLA_DV]) for h in range(GLA_H)]
    o = o + jnp.concatenate(intra, axis=1)
    ket = ke.T
    for h in range(GLA_H):
        rs = slice(h * GLA_DK, (h + 1) * GLA_DK)
        cs = slice(h * GLA_DV, (h + 1) * GLA_DV)
        s_ref[rs, cs] = dcol[rs] * s_ref[rs, cs] + _dot(ket[rs], v[:, cs])
    return o


def _gla_kernel(qf_ref, kif_ref, kef_ref, vf_ref, decf_ref, qb_ref, kib_ref, keb_ref, vb_ref, decb_ref,
                s0f_ref, s0b_ref, of_ref, ob_ref, sf_ref, sb_ref):
    @pl.when(pl.program_id(1) == 0)
    def _():
        sf_ref[...] = s0f_ref[0]
        sb_ref[...] = s0b_ref[0]

    lane_head = lax.broadcasted_iota(jnp.int32, (CHUNK, GQK), 1) >> 6
    hmask = [lane_head == h for h in range(GLA_H)]
    ri = lax.broadcasted_iota(jnp.int32, (GLA_H * CHUNK, CHUNK), 0) & (CHUNK - 1)
    ci = lax.broadcasted_iota(jnp.int32, (GLA_H * CHUNK, CHUNK), 1)
    tri_f = ri >= ci
    tri_b = ri <= ci
    dect_f = decf_ref[0][:, :GQK].T
    dect_b = decb_ref[0][:, GQK:].T
    for c in range(GLA_CB):
        ts = slice(c * CHUNK, (c + 1) * CHUNK)
        of_ref[0, ts, :] = _gla_chunk(qf_ref[0, ts, :], kif_ref[0, ts, :], kef_ref[0, ts, :], vf_ref[0, ts, :],
                                      dect_f[:, c:c + 1], sf_ref, tri_f, hmask)
        cb = GLA_CB - 1 - c
        tb = slice(cb * CHUNK, (cb + 1) * CHUNK)
        ob_ref[0, tb, :] = _gla_chunk(qb_ref[0, tb, :], kib_ref[0, tb, :], keb_ref[0, tb, :], vb_ref[0, tb, :],
                                      dect_b[:, cb:cb + 1], sb_ref, tri_b, hmask)


def _gla_scan(qf, kif, kef, qb, kib, keb, vg, dec, s0f, s0b):
    B, L, _ = vg.shape
    tm = GLA_CB * CHUNK
    nb = L // tm
    fw = lambda n: pl.BlockSpec((1, tm, n), lambda b, i: (b, i, 0))
    bw = lambda n: pl.BlockSpec((1, tm, n), lambda b, i: (b, nb - 1 - i, 0))
    st = pl.BlockSpec((1, GQK, GV), lambda b, i: (b, 0, 0))
    return pl.pallas_call(
        _gla_kernel,
        grid=(B, nb),
        in_specs=[fw(GQK), fw(GQK), fw(GQK), fw(GV),
                  pl.BlockSpec((1, GLA_CB, 2 * GQK), lambda b, i: (b, i, 0)),
                  bw(GQK), bw(GQK), bw(GQK), bw(GV),
                  pl.BlockSpec((1, GLA_CB, 2 * GQK), lambda b, i: (b, nb - 1 - i, 0)),
                  st, st],
        out_specs=[fw(GV), bw(GV)],
        out_shape=[jax.ShapeDtypeStruct((B, L, GV), f32)] * 2,
        scratch_shapes=[pltpu.VMEM((GQK, GV), f32), pltpu.VMEM((GQK, GV), f32)],
        compiler_params=pltpu.CompilerParams(
            dimension_semantics=("parallel", "arbitrary"), vmem_limit_bytes=VMEM_LIMIT),
        name="gla_scan",
    )(qf, kif, kef, vg, dec, qb, kib, keb, vg, dec, s0f, s0b)


def _merge_ffn_kernel(x_ref, ona_ref, of_ref, ob_ref, og_ref, ma_ref, mb_ref,
                      g1_ref, sh2_ref, sc2_ref, g2_ref, n2_ref, gg_ref,
                      wna_ref, wgl_ref, wo_ref, wg_ref, wu_ref, wd_ref, o_ref):
    og = of_ref[0] + ob_ref[0]
    parts = []
    for h in range(GLA_H):
        seg = og[:, h * GLA_DV:(h + 1) * GLA_DV]
        ms = jnp.mean(seg * seg, axis=-1, keepdims=True)
        parts.append(seg * lax.rsqrt(ms + EPS) * gg_ref[...])
    ogl = (jnp.concatenate(parts, axis=1) * og_ref[0].astype(f32)).astype(bf16)
    y = (ma_ref[0].astype(f32) * _dot(ona_ref[0], wna_ref[...])
         + mb_ref[0].astype(f32) * _dot(ogl, wgl_ref[...]))
    x1 = x_ref[0] + g1_ref[0] * _dot(y.astype(bf16), wo_ref[...])
    h2 = _norm_mod(x1, n2_ref[...], sh2_ref[0], sc2_ref[0]).astype(bf16)
    act = (_silu(_dot(h2, wg_ref[...])) * _dot(h2, wu_ref[...])).astype(bf16)
    o_ref[0] = x1 + g2_ref[0] * _dot(act, wd_ref[...])


def _merge_ffn(x, ona, of, ob, og, ma, mb, mod3, norm2_g, gg, wna, wgl, wo, wg, wu, wd):
    B, L, _ = x.shape
    tm = TM_FFN
    tok = lambda n: pl.BlockSpec((1, tm, n), lambda b, i: (b, i, 0))
    modv = lambda j: pl.BlockSpec((1, 1, D), lambda b, i: (b, 0, j))
    return pl.pallas_call(
        _merge_ffn_kernel,
        grid=(B, L // tm),
        in_specs=[tok(D), tok(NA_W), tok(GV), tok(GV), tok(GV), tok(D), tok(D),
                  modv(2), modv(3), modv(4), modv(5),
                  _const_spec((1, D)), _const_spec((1, GLA_DV)),
                  _const_spec((NA_W, D)), _const_spec((GV, D)), _const_spec((D, D)),
                  _const_spec((D, D_FF)), _const_spec((D, D_FF)), _const_spec((D_FF, D))],
        out_specs=tok(D),
        out_shape=jax.ShapeDtypeStruct((B, L, D), f32),
        compiler_params=pltpu.CompilerParams(
            dimension_semantics=("parallel", "parallel"), vmem_limit_bytes=VMEM_LIMIT),
        name="merge_ffn",
    )(x, ona, of, ob, og, ma, mb, mod3, mod3, mod3, mod3, norm2_g, gg, wna, wgl, wo, wg, wu, wd)


def kernel(x, c, ctx, c_ctx, w_mod, b_mod, norm1_g, norm2_g, w_in, na_q_norm_g, na_k_norm_g, na_rpb,
           gla_w_alpha, gla_b_alpha, gla_norm_g, w_branch_na, w_branch_gla, w_out,
           w_ffn_gate, w_ffn_up, w_ffn_down):
    B, L, _ = x.shape
    assert w_mod.shape[0] == 1 and L % (NA_R * GRID_W) == 0 and L % TM_PROJ == 0

    w_t = _cast_bf16(jnp.transpose(w_in[0]), D_IN // 7)
    w_a = jnp.zeros((N_LR, 2 * GQK), f32)
    w_a = w_a.at[:GLA_RANK, :GQK].set(gla_w_alpha[0, 0]).at[GLA_RANK:, GQK:].set(gla_w_alpha[0, 1])
    w_a = w_a.astype(bf16)
    b_a = gla_b_alpha[0].reshape(1, 2 * GQK)
    gq_t = jnp.tile(na_q_norm_g[0] * (NA_DH ** -0.5 * LOG2E), NA_H).reshape(1, NA_W)
    gk_t = jnp.tile(na_k_norm_g[0], NA_H).reshape(1, NA_W)
    bd = jnp.asarray(np.kron(np.eye(4), np.full((NA_DH, NA_DH), 1.0 / NA_DH)), bf16)

    mod = _adaln_mod(c, c_ctx, w_mod[0], b_mod[0])
    mod3 = mod.reshape(8, 1, 6 * D)
    t2 = _rpb_expand(na_rpb[0])

    kc, vc, s0f, s0b = _ctx_side(ctx, mod3, norm1_g, w_t, w_a, b_a, gk_t, bd, B)
    (qna, kna, vna, qf, kif, kef, qb, kib, keb, vg, og, ma, mb, dec) = _in_proj(
        x, mod3, norm1_g, w_t, gq_t, gk_t, bd, w_a, b_a)
    ona = _na_attn(qna, kna, vna, kc, vc, t2)
    of, ob = _gla_scan(qf, kif, kef, qb, kib, keb, vg, dec, s0f, s0b)
    return _merge_ffn(x, ona, of, ob, og, ma, mb, mod3, norm2_g, gla_norm_g,
                      w_branch_na[0].astype(bf16), w_branch_gla[0].astype(bf16), w_out[0].astype(bf16),
                      w_ffn_gate[0].astype(bf16), w_ffn_up[0].astype(bf16), w_ffn_down[0].astype(bf16))
```

```python
import functools

import numpy as np
import jax
import jax.numpy as jnp
from jax import lax
from jax.experimental import pallas as pl
from jax.experimental.pallas import tpu as pltpu

D = 1024
GRID_W = 64
CTX = 256
NA_H = 8
NA_DH = 64
NA_WR = 8
NA_WC = 16
GLA_H = 4
GLA_DK = 64
GLA_DV = 128
GLA_RANK = 16
GLA_TAU = 16.0
GLA_LA_MIN = -1.0
CHUNK = 64
ROPE_THETA = 10000.0
D_FF = 2816
EPS = 1e-6
NEG_INF = -1e30
NA_W = NA_H * NA_DH
GQK = GLA_H * GLA_DK
GV = GLA_H * GLA_DV

OFF_QNA, OFF_KNA, OFF_VNA = 0, 512, 1024
OFF_QG, OFF_KG, OFF_VG, OFF_OG = 1536, 1792, 2048, 2560
OFF_LR, OFF_MA, OFF_MB = 3072, 3104, 4128
N_LR = 2 * GLA_RANK
D_IN = 5152
LOG2E = 1.4426950408889634

VMEM_LIMIT = 56 * 1024 * 1024

TM_PROJ = 512
TM_FFN = 512
NA_R = 32
NA_RI = 4
GLA_CB = 16

f32 = jnp.float32
bf16 = jnp.bfloat16


def _const_spec(shape):
    nd = len(shape)
    return pl.BlockSpec(shape, lambda *_: (0,) * nd, pipeline_mode=pl.Buffered(1))


def _dot(a, b):
    return jnp.dot(a, b, preferred_element_type=f32)


def _dot_nt(a, b):
    return lax.dot_general(a, b, (((1,), (1,)), ((), ())), preferred_element_type=f32)


def _split2(x):
    hi = x.astype(bf16)
    lo = (x - hi.astype(f32)).astype(bf16)
    return hi, lo


def _log_sigmoid(z):
    return jnp.minimum(z, 0.0) - jnp.log(1.0 + jnp.exp(-jnp.abs(z)))


def _silu(x):
    return x * (1.0 / (1.0 + jnp.exp(-x)))


def _sigmoid(x):
    return 1.0 / (1.0 + jnp.exp(-x))


def _split3_dot(r, oh):
    h1 = r.astype(bf16)
    r2 = r - h1.astype(f32)
    h2 = r2.astype(bf16)
    h3 = (r2 - h2.astype(f32)).astype(bf16)
    return _dot(h1, oh) + _dot(h2, oh) + _dot(h3, oh)


def _rpb_expand_kernel(r0_ref, r1_ref, oh0_ref, oh1_ref, mask_ref, o_ref):
    gathered = _split3_dot(r0_ref[...], oh0_ref[...]) + _split3_dot(r1_ref[...], oh1_ref[...])
    o_ref[...] = gathered * LOG2E + mask_ref[...]


def _rpb_expand(rpb):
    nd = 2 * NA_WR - 1
    nrow = NA_H * nd
    ncol = 2 * NA_WC - 1
    nxt = jnp.concatenate([rpb[:, 1:], jnp.zeros_like(rpb[:, :1])], axis=1)
    r0 = jnp.zeros((128, 128), f32).at[:nrow, :ncol].set(rpb.reshape(nrow, ncol))
    r1 = jnp.zeros((128, 128), f32).at[:nrow, :ncol].set(nxt.reshape(nrow, ncol))
    q = np.arange(GRID_W)
    cs = np.clip(q - NA_WC // 2, 0, GRID_W - NA_WC)
    in_win = (q[None, :] >= cs[:, None]) & (q[None, :] < cs[:, None] + NA_WC)
    dc = np.clip(q[None, :] - q[:, None], -(NA_WC - 1), NA_WC - 1) + NA_WC - 1
    oh = np.zeros((2, 128, GRID_W, 128), np.float32)
    qq, kk = np.nonzero(in_win)
    oh[0, dc[qq, kk], qq, kk] = 1.0
    oh[1, dc[qq, kk], qq, kk + GRID_W] = 1.0
    oh = oh.reshape(2, 128, GRID_W * 128)
    mask = np.where(np.concatenate([in_win, in_win], axis=1), 0.0, NEG_INF).reshape(1, GRID_W * 128)
    out = pl.pallas_call(
        _rpb_expand_kernel,
        out_shape=jax.ShapeDtypeStruct((128, GRID_W * 128), f32),
        name="rpb_expand",
    )(r0, r1, jnp.asarray(oh[0], bf16), jnp.asarray(oh[1], bf16), jnp.asarray(mask, f32))
    return out[:nrow].reshape(NA_H, nd, GRID_W, 128)


def _cast_kernel(x_ref, o_ref):
    o_ref[...] = x_ref[...].astype(bf16)


def _cast_bf16(w, rows):
    R, C = w.shape
    return pl.pallas_call(
        _cast_kernel,
        grid=(R // rows,),
        in_specs=[pl.BlockSpec((rows, C), lambda j: (j, 0))],
        out_specs=pl.BlockSpec((rows, C), lambda j: (j, 0)),
        out_shape=jax.ShapeDtypeStruct((R, C), bf16),
        name="cast_bf16",
    )(w)


def _mod_kernel(a_ref, w_ref, b_ref, o_ref):
    a = _silu(a_ref[...]).astype(bf16)
    o_ref[...] = _dot(a, w_ref[...].astype(bf16)) + b_ref[...]


def _adaln_mod(c, c_ctx, w_mod, b_mod):
    B = c.shape[0]
    a = jnp.zeros((8, D), f32).at[:B].set(c).at[B].set(c_ctx)
    n = w_mod.shape[1]
    return pl.pallas_call(
        _mod_kernel,
        grid=(n // D,),
        in_specs=[pl.BlockSpec((8, D), lambda j: (0, 0)),
                  pl.BlockSpec((D, D), lambda j: (0, j)),
                  pl.BlockSpec((1, D), lambda j: (0, j))],
        out_specs=pl.BlockSpec((8, D), lambda j: (0, j)),
        out_shape=jax.ShapeDtypeStruct((8, n), f32),
        name="adaln_mod",
    )(a, w_mod, b_mod.reshape(1, n))


def _norm_mod(x, g, sh, sc):
    ms = jnp.mean(x * x, axis=-1, keepdims=True)
    return (x * lax.rsqrt(ms + EPS) * g) * (1.0 + sc) + sh


def _head_rms(t, bd, g):
    sq = (t * t).astype(bf16)
    ms = jnp.concatenate([_dot(sq[:, :256], bd), _dot(sq[:, 256:], bd)], axis=1)
    return t * lax.rsqrt(ms + EPS) * g


def _log_alpha(zl, wa_ref, ba_ref):
    z = _dot(zl.astype(bf16), wa_ref[...]) + ba_ref[...]
    return jnp.maximum(_log_sigmoid(z) * (1.0 / GLA_TAU), GLA_LA_MIN)


def _ctx_kernel(ctx_ref, sh_ref, sc_ref, g1_ref, wk_ref, wv_ref, wkg_ref, wvg_ref, wlr_ref,
                wa_ref, ba_ref, gk_ref, bd_ref, su_ref, sl_ref,
                kc_ref, vc_ref, sf_ref, sb_ref):
    h = _norm_mod(ctx_ref[0], g1_ref[...], sh_ref[0], sc_ref[0]).astype(bf16)
    k = _dot_nt(h, wk_ref[...])
    kc_ref[0] = _head_rms(k, bd_ref[...], gk_ref[...]).astype(bf16)
    vc_ref[0] = _dot_nt(h, wv_ref[...]).astype(bf16)
    kg = _dot_nt(h, wkg_ref[...])
    vg = _dot_nt(h, wvg_ref[...]).astype(bf16)
    la = _log_alpha(_dot_nt(h, wlr_ref[...]), wa_ref, ba_ref)
    hi, lo = _split2(la)
    ef = _dot(su_ref[...], hi[:, :GQK]) + _dot(su_ref[...], lo[:, :GQK])
    eb = _dot(sl_ref[...], hi[:, GQK:]) + _dot(sl_ref[...], lo[:, GQK:])
    r = lax.broadcasted_iota(jnp.int32, (GQK, GV), 0) >> 6
    cc = lax.broadcasted_iota(jnp.int32, (GQK, GV), 1) >> 7
    diag = r == cc
    for e, out in ((ef, sf_ref), (eb, sb_ref)):
        kw = (kg * jnp.exp(e)).astype(bf16)
        u = _dot(kw.T, vg)
        out[0] = jnp.where(diag, u, 0.0)


def _ctx_side(ctx, mod3, norm1_g, w_t, w_a, b_a, gk_t, bd, n_b):
    su = jnp.asarray(np.triu(np.ones((CTX, CTX), np.float32), 1), bf16)
    sl = jnp.asarray(np.tril(np.ones((CTX, CTX), np.float32), -1), bf16)
    B = ctx.shape[0]
    cst = lambda shape, idx: pl.BlockSpec(shape, lambda b: idx)
    return pl.pallas_call(
        _ctx_kernel,
        grid=(B,),
        in_specs=[pl.BlockSpec((1, CTX, D), lambda b: (b, 0, 0)),
                  cst((1, 1, D), (n_b, 0, 0)), cst((1, 1, D), (n_b, 0, 1)),
                  cst((1, D), (0, 0)),
                  cst((NA_W, D), (OFF_KNA // NA_W, 0)), cst((NA_W, D), (OFF_VNA // NA_W, 0)),
                  cst((GQK, D), (OFF_KG // GQK, 0)), cst((GV, D), (OFF_VG // GV, 0)),
                  cst((N_LR, D), (OFF_LR // N_LR, 0)),
                  cst((N_LR, 2 * GQK), (0, 0)), cst((1, 2 * GQK), (0, 0)),
                  cst((1, NA_W), (0, 0)), cst((256, 256), (0, 0)),
                  cst((CTX, CTX), (0, 0)), cst((CTX, CTX), (0, 0))],
        out_specs=[pl.BlockSpec((1, CTX, NA_W), lambda b: (b, 0, 0)),
                   pl.BlockSpec((1, CTX, NA_W), lambda b: (b, 0, 0)),
                   pl.BlockSpec((1, GQK, GV), lambda b: (b, 0, 0)),
                   pl.BlockSpec((1, GQK, GV), lambda b: (b, 0, 0))],
        out_shape=[jax.ShapeDtypeStruct((B, CTX, NA_W), bf16),
                   jax.ShapeDtypeStruct((B, CTX, NA_W), bf16),
                   jax.ShapeDtypeStruct((B, GQK, GV), f32),
                   jax.ShapeDtypeStruct((B, GQK, GV), f32)],
        compiler_params=pltpu.CompilerParams(vmem_limit_bytes=VMEM_LIMIT),
        name="ctx_side",
    )(ctx, mod3, mod3, norm1_g, w_t, w_t, w_t, w_t, w_t, w_a, b_a, gk_t, bd, su, sl)


def _proj_kernel(x_ref, sh_ref, sc_ref, g1_ref, w_ref, gq_ref, gk_ref, bd_ref,
                 rc_ref, rs_ref, cc_ref, cs_ref, wa_ref, ba_ref, tl_ref,
                 qna_ref, kna_ref, vna_ref, qf_ref, kif_ref, kef_ref, qb_ref, kib_ref, keb_ref,
                 vg_ref, og_ref, ma_ref, mb_ref, dec_ref):
    tm = x_ref.shape[1]
    nchunk = tm // CHUNK
    hb = _norm_mod(x_ref[0], g1_ref[...], sh_ref[0], sc_ref[0]).astype(bf16)

    def proj(off, n):
        return _dot_nt(hb, w_ref[off:off + n, :])

    bd = bd_ref[...]
    zl = proj(OFF_LR, N_LR)
    q_raw = proj(OFF_QNA, NA_W)
    la = _log_alpha(zl, wa_ref, ba_ref)
    k_raw = proj(OFF_KNA, NA_W)
    hi, lo = _split2(la)
    qna_ref[0] = _head_rms(q_raw, bd, gq_ref[...]).astype(bf16)
    tl = tl_ref[...]
    cum = jnp.concatenate(
        [_dot(tl, hi[s:s + 256]) + _dot(tl, lo[s:s + 256]) for s in range(0, tm, 256)], axis=0)
    kna_ref[0] = _head_rms(k_raw, bd, gk_ref[...]).astype(bf16)
    qg_raw = proj(OFF_QG, GQK)
    kg_raw = proj(OFF_KG, GQK)
    vna_ref[0] = proj(OFF_VNA, NA_W).astype(bf16)
    vg_ref[0] = proj(OFF_VG, GV).astype(bf16)

    cos_t = (rc_ref[...][:, None, :] + cc_ref[...][None]).reshape(tm, GQK)
    sin_t = (rs_ref[...][:, None, :] + cs_ref[...][None]).reshape(tm, GQK)
    lane = lax.broadcasted_iota(jnp.int32, (tm, GQK), 1)
    first_half = (lane & 31) < 16

    def rope(t):
        partner = jnp.where(first_half, pltpu.roll(t, GQK - 16, 1), pltpu.roll(t, 16, 1))
        return t * cos_t + partner * sin_t

    qg = rope(qg_raw) * (GLA_DK ** -0.5)
    kg = rope(kg_raw)

    cum3 = cum.reshape(nchunk, CHUNK, 2 * GQK)
    tot3 = cum3[:, CHUNK - 1:CHUNK, :]
    la3 = la.reshape(nchunk, CHUNK, 2 * GQK)
    dec_ref[0] = jnp.exp(tot3).reshape(nchunk, 2 * GQK)
    b_f = cum3[:, :, :GQK].reshape(tm, GQK)
    e_f = (tot3 - cum3)[:, :, :GQK].reshape(tm, GQK)
    b_b = (tot3 - cum3 + la3)[:, :, GQK:].reshape(tm, GQK)
    e_b = (cum3 - la3)[:, :, GQK:].reshape(tm, GQK)
    og_ref[0] = _silu(proj(OFF_OG, GV)).astype(bf16)
    qf_ref[0] = (qg * jnp.exp(b_f)).astype(bf16)
    kif_ref[0] = (kg * jnp.exp(-b_f)).astype(bf16)
    kef_ref[0] = (kg * jnp.exp(e_f)).astype(bf16)
    ma_ref[0] = _sigmoid(proj(OFF_MA, D)).astype(bf16)
    qb_ref[0] = (qg * jnp.exp(b_b)).astype(bf16)
    kib_ref[0] = (kg * jnp.exp(-b_b)).astype(bf16)
    keb_ref[0] = (kg * jnp.exp(e_b)).astype(bf16)
    mb_ref[0] = _sigmoid(proj(OFF_MB, D)).astype(bf16)


def _rope_tables(L):
    n = GLA_DK // 4
    freqs = ROPE_THETA ** (-np.arange(n, dtype=np.float64) / n)
    lane = np.arange(GQK)
    fl = freqs[lane % n]
    is_row = (lane % GLA_DK) < GLA_DK // 2
    sign = np.where((lane % 32) < 16, -1.0, 1.0)
    rows = np.arange(L // GRID_W)[:, None] * fl[None]
    cols = np.arange(GRID_W)[:, None] * fl[None]
    rc = np.where(is_row, np.cos(rows), 0.0)
    rs = np.where(is_row, np.sin(rows) * sign, 0.0)
    cc = np.where(~is_row, np.cos(cols), 0.0)
    cs = np.where(~is_row, np.sin(cols) * sign, 0.0)
    return [jnp.asarray(t, f32) for t in (rc, rs, cc, cs)]


def _in_proj(x, mod3, norm1_g, w_t, gq_t, gk_t, bd, w_a, b_a):
    B, L, _ = x.shape
    tm = TM_PROJ
    rows_per = tm // GRID_W
    rc, rs, cc, cs = _rope_tables(L)
    blk = np.kron(np.eye(256 // CHUNK), np.tril(np.ones((CHUNK, CHUNK)))).astype(np.float32)
    tl = jnp.asarray(blk, bf16)
    tok = lambda n: pl.BlockSpec((1, tm, n), lambda b, i: (b, i, 0))
    sds = lambda n, dt: jax.ShapeDtypeStruct((B, L, n), dt)
    return pl.pallas_call(
        _proj_kernel,
        grid=(B, L // tm),
        in_specs=[tok(D),
                  pl.BlockSpec((1, 1, D), lambda b, i: (b, 0, 0)),
                  pl.BlockSpec((1, 1, D), lambda b, i: (b, 0, 1)),
                  _const_spec((1, D)), _const_spec((D_IN, D)),
                  _const_spec((1, NA_W)), _const_spec((1, NA_W)), _const_spec((256, 256)),
                  pl.BlockSpec((rows_per, GQK), lambda b, i: (i, 0)),
                  pl.BlockSpec((rows_per, GQK), lambda b, i: (i, 0)),
                  _const_spec((GRID_W, GQK)), _const_spec((GRID_W, GQK)),
                  _const_spec((N_LR, 2 * GQK)), _const_spec((1, 2 * GQK)), _const_spec((256, 256))],
        out_specs=[tok(NA_W), tok(NA_W), tok(NA_W)] + [tok(GQK)] * 6
                  + [tok(GV), tok(GV), tok(D), tok(D),
                     pl.BlockSpec((1, tm // CHUNK, 2 * GQK), lambda b, i: (b, i, 0))],
        out_shape=[sds(NA_W, bf16)] * 3 + [sds(GQK, bf16)] * 6
                  + [sds(GV, bf16), sds(GV, bf16), sds(D, bf16), sds(D, bf16),
                     jax.ShapeDtypeStruct((B, L // CHUNK, 2 * GQK), f32)],
        compiler_params=pltpu.CompilerParams(
            dimension_semantics=("parallel", "parallel"), vmem_limit_bytes=VMEM_LIMIT),
        name="in_proj",
    )(x, mod3, mod3, norm1_g, w_t, gq_t, gk_t, bd, rc, rs, cc, cs, w_a, b_a, tl)


def _lane_blocks(t):
    return [t[:, c:c + 128] for c in range(0, t.shape[1], 128)]


def _na_kernel(q_ref, k_ref, v_ref, kc_ref, vc_ref, t2_ref, o_ref, sl_ref, sc_ref, m_ref):
    i = pl.program_id(1)
    rows = k_ref.shape[1] // GRID_W
    lane_head = lax.broadcasted_iota(jnp.int32, (GRID_W, 256), 1) >> 6
    hmask = [lane_head == h for h in range(4)]

    def window_start(r):
        rs = jnp.clip(r - NA_WR // 2, 0, rows - NA_WR)
        return rs, pl.multiple_of(rs * GRID_W, GRID_W)

    def produce(rr, bank, j):
        r = i * NA_R + rr
        rs, start = window_start(r)
        d0 = rs - r + (NA_WR - 1)
        row0 = pl.multiple_of(rr * GRID_W, GRID_W)
        for g in range(2):
            ls = slice(256 * g, 256 * g + 256)
            q = q_ref[0, pl.ds(row0, GRID_W), ls]
            qs = jnp.concatenate([jnp.where(m, q, jnp.zeros_like(q)) for m in hmask], axis=0)
            kw = k_ref[0, pl.ds(start, NA_WR * GRID_W), ls]
            bias = jnp.concatenate(
                [jnp.concatenate([t2_ref[4 * g + h, d0 + 2 * jj] for jj in range(NA_WR // 2)], axis=1)
                 for h in range(4)], axis=0)
            s_loc = _dot_nt(qs, kw) + bias
            s_ctx = _dot_nt(qs, kc_ref[0, :, ls])
            mx = functools.reduce(jnp.maximum, _lane_blocks(s_loc) + _lane_blocks(s_ctx))
            sl_ref[bank, j, g] = s_loc
            sc_ref[bank, j, g] = s_ctx
            m_ref[bank, j, g] = jnp.broadcast_to(jnp.max(mx, axis=-1, keepdims=True), (4 * GRID_W, 128))

    def consume(rr, bank, j):
        _, start = window_start(i * NA_R + rr)
        row0 = pl.multiple_of(rr * GRID_W, GRID_W)
        outs = []
        for g in range(2):
            ls = slice(256 * g, 256 * g + 256)
            m = m_ref[bank, j, g]
            p_loc = [jnp.exp2(sl_ref[bank, j, g, :, c:c + 128] - m) for c in range(0, NA_WR * GRID_W, 128)]
            p_ctx = [jnp.exp2(sc_ref[bank, j, g, :, c:c + 128] - m) for c in range(0, CTX, 128)]
            den = jnp.sum(functools.reduce(jnp.add, p_loc + p_ctx), axis=-1, keepdims=True)
            vw = v_ref[0, pl.ds(start, NA_WR * GRID_W), ls]
            acc = (_dot(jnp.concatenate(p_loc, axis=1).astype(bf16), vw)
                   + _dot(jnp.concatenate(p_ctx, axis=1).astype(bf16), vc_ref[0, :, ls]))
            acc = acc * (1.0 / den)
            o = jnp.zeros((GRID_W, 256), f32)
            for h in range(4):
                o = o + jnp.where(hmask[h], acc[h * GRID_W:(h + 1) * GRID_W], 0.0)
            outs.append(o)
        o_ref[0, pl.ds(row0, GRID_W), :] = jnp.concatenate(outs, axis=1).astype(bf16)

    for j in range(NA_RI):
        produce(jnp.int32(j), 0, j)

    def step(t, carry):
        bank = t & 1
        for j in range(NA_RI):
            consume(t * NA_RI + j, bank, j)
        for j in range(NA_RI):
            produce(jnp.minimum((t + 1) * NA_RI + j, NA_R - 1), 1 - bank, j)
        return carry

    lax.fori_loop(0, NA_R // NA_RI, step, 0)


def _na_attn(q, k, v, kc, vc, t2):
    B, L, _ = q.shape
    nblk = L // (NA_R * GRID_W)
    return pl.pallas_call(
        _na_kernel,
        grid=(B, nblk),
        in_specs=[pl.BlockSpec((1, NA_R * GRID_W, NA_W), lambda b, i: (b, i, 0)),
                  pl.BlockSpec((1, L, NA_W), lambda b, i: (b, 0, 0), pipeline_mode=pl.Buffered(1)),
                  pl.BlockSpec((1, L, NA_W), lambda b, i: (b, 0, 0), pipeline_mode=pl.Buffered(1)),
                  pl.BlockSpec((1, CTX, NA_W), lambda b, i: (b, 0, 0)),
                  pl.BlockSpec((1, CTX, NA_W), lambda b, i: (b, 0, 0)),
                  _const_spec((NA_H, 2 * NA_WR - 1, GRID_W, 128))],
        out_specs=pl.BlockSpec((1, NA_R * GRID_W, NA_W), lambda b, i: (b, i, 0)),
        out_shape=jax.ShapeDtypeStruct((B, L, NA_W), bf16),
        scratch_shapes=[pltpu.VMEM((2, NA_RI, 2, 4 * GRID_W, NA_WR * GRID_W), f32),
                        pltpu.VMEM((2, NA_RI, 2, 4 * GRID_W, CTX), f32),
                        pltpu.VMEM((2, NA_RI, 2, 4 * GRID_W, 128), f32)],
        compiler_params=pltpu.CompilerParams(
            dimension_semantics=("parallel", "parallel"), vmem_limit_bytes=VMEM_LIMIT),
        name="na_attn",
    )(q, k, v, kc, vc, t2)


def _gla_chunk(q, ki, ke, v, dcol, s_ref, tri, hmask):
    qs = jnp.concatenate([jnp.where(m, q, jnp.zeros_like(q)) for m in hmask], axis=0)
    att = jnp.where(tri, _dot_nt(qs, ki), 0.0).astype(bf16)
    o = _dot(q, s_ref[...].astype(bf16))
    intra = [_dot(att[h * CHUNK:(h + 1) * CHUNK], v[:, h * GLA_DV:(h + 1) * GLA_DV]) for h in range(GLA_H)]
    o = o + jnp.concatenate(intra, axis=1)
    ket = ke.T
    for h in range(GLA_H):
        rs = slice(h * GLA_DK, (h + 1) * GLA_DK)
        cs = slice(h * GLA_DV, (h + 1) * GLA_DV)
        s_ref[rs, cs] = dcol[rs] * s_ref[rs, cs] + _dot(ket[rs], v[:, cs])
    return o


def _gla_kernel(qf_ref, kif_ref, kef_ref, vf_ref, decf_ref, qb_ref, kib_ref, keb_ref, vb_ref, decb_ref,
                s0f_ref, s0b_ref, of_ref, ob_ref, sf_ref, sb_ref):
    @pl.when(pl.program_id(1) == 0)
    def _():
        sf_ref[...] = s0f_ref[0]
        sb_ref[...] = s0b_ref[0]

    lane_head = lax.broadcasted_iota(jnp.int32, (CHUNK, GQK), 1) >> 6
    hmask = [lane_head == h for h in range(GLA_H)]
    ri = lax.broadcasted_iota(jnp.int32, (GLA_H * CHUNK, CHUNK), 0) & (CHUNK - 1)
    ci = lax.broadcasted_iota(jnp.int32, (GLA_H * CHUNK, CHUNK), 1)
    tri_f = ri >= ci
    tri_b = ri <= ci
    dect_f = decf_ref[0][:, :GQK].T
    dect_b = decb_ref[0][:, GQK:].T
    for c in range(GLA_CB):
        ts = slice(c * CHUNK, (c + 1) * CHUNK)
        of_ref[0, ts, :] = _gla_chunk(qf_ref[0, ts, :], kif_ref[0, ts, :], kef_ref[0, ts, :], vf_ref[0, ts, :],
                                      dect_f[:, c:c + 1], sf_ref, tri_f, hmask)
        cb = GLA_CB - 1 - c
        tb = slice(cb * CHUNK, (cb + 1) * CHUNK)
        ob_ref[0, tb, :] = _gla_chunk(qb_ref[0, tb, :], kib_ref[0, tb, :], keb_ref[0, tb, :], vb_ref[0, tb, :],
                                      dect_b[:, cb:cb + 1], sb_ref, tri_b, hmask)


def _gla_scan(qf, kif, kef, qb, kib, keb, vg, dec, s0f, s0b):
    B, L, _ = vg.shape
    tm = GLA_CB * CHUNK
    nb = L // tm
    fw = lambda n: pl.BlockSpec((1, tm, n), lambda b, i: (b, i, 0))
    bw = lambda n: pl.BlockSpec((1, tm, n), lambda b, i: (b, nb - 1 - i, 0))
    st = pl.BlockSpec((1, GQK, GV), lambda b, i: (b, 0, 0))
    return pl.pallas_call(
        _gla_kernel,
        grid=(B, nb),
        in_specs=[fw(GQK), fw(GQK), fw(GQK), fw(GV),
                  pl.BlockSpec((1, GLA_CB, 2 * GQK), lambda b, i: (b, i, 0)),
                  bw(GQK), bw(GQK), bw(GQK), bw(GV),
                  pl.BlockSpec((1, GLA_CB, 2 * GQK), lambda b, i: (b, nb - 1 - i, 0)),
                  st, st],
        out_specs=[fw(GV), bw(GV)],
        out_shape=[jax.ShapeDtypeStruct((B, L, GV), f32)] * 2,
        scratch_shapes=[pltpu.VMEM((GQK, GV), f32), pltpu.VMEM((GQK, GV), f32)],
        compiler_params=pltpu.CompilerParams(
            dimension_semantics=("parallel", "arbitrary"), vmem_limit_bytes=VMEM_LIMIT),
        name="gla_scan",
    )(qf, kif, kef, vg, dec, qb, kib, keb, vg, dec, s0f, s0b)


def _merge_ffn_kernel(x_ref, ona_ref, of_ref, ob_ref, og_ref, ma_ref, mb_ref,
                      g1_ref, sh2_ref, sc2_ref, g2_ref, n2_ref, gg_ref,
                      wna_ref, wgl_ref, wo_ref, wg_ref, wu_ref, wd_ref, o_ref):
    og = of_ref[0] + ob_ref[0]
    parts = []
    for h in range(GLA_H):
        seg = og[:, h * GLA_DV:(h + 1) * GLA_DV]
        ms = jnp.mean(seg * seg, axis=-1, keepdims=True)
        parts.append(seg * lax.rsqrt(ms + EPS) * gg_ref[...])
    ogl = (jnp.concatenate(parts, axis=1) * og_ref[0].astype(f32)).astype(bf16)
    y = (ma_ref[0].astype(f32) * _dot(ona_ref[0], wna_ref[...])
         + mb_ref[0].astype(f32) * _dot(ogl, wgl_ref[...]))
    x1 = x_ref[0] + g1_ref[0] * _dot(y.astype(bf16), wo_ref[...])
    h2 = _norm_mod(x1, n2_ref[...], sh2_ref[0], sc2_ref[0]).astype(bf16)
    act = (_silu(_dot(h2, wg_ref[...])) * _dot(h2, wu_ref[...])).astype(bf16)
    o_ref[0] = x1 + g2_ref[0] * _dot(act, wd_ref[...])


def _merge_ffn(x, ona, of, ob, og, ma, mb, mod3, norm2_g, gg, wna, wgl, wo, wg, wu, wd):
    B, L, _ = x.shape
    tm = TM_FFN
    tok = lambda n: pl.BlockSpec((1, tm, n), lambda b, i: (b, i, 0))
    modv = lambda j: pl.BlockSpec((1, 1, D), lambda b, i: (b, 0, j))
    return pl.pallas_call(
        _merge_ffn_kernel,
        grid=(B, L // tm),
        in_specs=[tok(D), tok(NA_W), tok(GV), tok(GV), tok(GV), tok(D), tok(D),
                  modv(2), modv(3), modv(4), modv(5),
                  _const_spec((1, D)), _const_spec((1, GLA_DV)),
                  _const_spec((NA_W, D)), _const_spec((GV, D)), _const_spec((D, D)),
                  _const_spec((D, D_FF)), _const_spec((D, D_FF)), _const_spec((D_FF, D))],
        out_specs=tok(D),
        out_shape=jax.ShapeDtypeStruct((B, L, D), f32),
        compiler_params=pltpu.CompilerParams(
            dimension_semantics=("parallel", "parallel"), vmem_limit_bytes=VMEM_LIMIT),
        name="merge_ffn",
    )(x, ona, of, ob, og, ma, mb, mod3, mod3, mod3, mod3, norm2_g, gg, wna, wgl, wo, wg, wu, wd)


def kernel(x, c, ctx, c_ctx, w_mod, b_mod, norm1_g, norm2_g, w_in, na_q_norm_g, na_k_norm_g, na_rpb,
           gla_w_alpha, gla_b_alpha, gla_norm_g, w_branch_na, w_branch_gla, w_out,
           w_ffn_gate, w_ffn_up, w_ffn_down):
    B, L, _ = x.shape
    assert w_mod.shape[0] == 1 and L % (NA_R * GRID_W) == 0 and L % TM_PROJ == 0

    w_t = _cast_bf16(jnp.transpose(w_in[0]), D_IN // 7)
    w_a = jnp.zeros((N_LR, 2 * GQK), f32)
    w_a = w_a.at[:GLA_RANK, :GQK].set(gla_w_alpha[0, 0]).at[GLA_RANK:, GQK:].set(gla_w_alpha[0, 1])
    w_a = w_a.astype(bf16)
    b_a = gla_b_alpha[0].reshape(1, 2 * GQK)
    gq_t = jnp.tile(na_q_norm_g[0] * (NA_DH ** -0.5 * LOG2E), NA_H).reshape(1, NA_W)
    gk_t = jnp.tile(na_k_norm_g[0], NA_H).reshape(1, NA_W)
    bd = jnp.asarray(np.kron(np.eye(4), np.full((NA_DH, NA_DH), 1.0 / NA_DH)), bf16)

    mod = _adaln_mod(c, c_ctx, w_mod[0], b_mod[0])
    mod3 = mod.reshape(8, 1, 6 * D)
    t2 = _rpb_expand(na_rpb[0])

    kc, vc, s0f, s0b = _ctx_side(ctx, mod3, norm1_g, w_t, w_a, b_a, gk_t, bd, B)
    (qna, kna, vna, qf, kif, kef, qb, kib, keb, vg, og, ma, mb, dec) = _in_proj(
        x, mod3, norm1_g, w_t, gq_t, gk_t, bd, w_a, b_a)
    ona = _na_attn(qna, kna, vna, kc, vc, t2)
    of, ob = _gla_scan(qf, kif, kef, qb, kib, keb, vg, dec, s0f, s0b)
    return _merge_ffn(x, ona, of, ob, og, ma, mb, mod3, norm2_g, gla_norm_g,
                      w_branch_na[0].astype(bf16), w_branch_gla[0].astype(bf16), w_out[0].astype(bf16),
                      w_ffn_gate[0].astype(bf16), w_ffn_up[0].astype(bf16), w_ffn_down[0].astype(bf16))
```

```python
import functools

import numpy as np
import jax
import jax.numpy as jnp
from jax import lax
from jax.experimental import pallas as pl
from jax.experimental.pallas import tpu as pltpu

D = 1024
GRID_W = 64
CTX = 256
NA_H = 8
NA_DH = 64
NA_WR = 8
NA_WC = 16
GLA_H = 4
GLA_DK = 64
GLA_DV = 128
GLA_RANK = 16
GLA_TAU = 16.0
GLA_LA_MIN = -1.0
CHUNK = 64
ROPE_THETA = 10000.0
D_FF = 2816
EPS = 1e-6
NEG_INF = -1e30
NA_W = NA_H * NA_DH
GQK = GLA_H * GLA_DK
GV = GLA_H * GLA_DV

OFF_QNA, OFF_KNA, OFF_VNA = 0, 512, 1024
OFF_QG, OFF_KG, OFF_VG, OFF_OG = 1536, 1792, 2048, 2560
OFF_LR, OFF_MA, OFF_MB = 3072, 3104, 4128
N_LR = 2 * GLA_RANK
D_IN = 5152
LOG2E = 1.4426950408889634

BF16_SUBLANES = 16
VMEM_LIMIT = 56 * 1024 * 1024

TM_PROJ = 512
PROJ_SUB = 256
TM_FFN = 512
FFN_SUB = 512
NA_R = 32
NA_RI = 4
GLA_CB = 16

f32 = jnp.float32
bf16 = jnp.bfloat16


def _const_spec(shape):
    nd = len(shape)
    return pl.BlockSpec(shape, lambda *_: (0,) * nd, pipeline_mode=pl.Buffered(1))


def _dot(a, b):
    return jnp.dot(a, b, preferred_element_type=f32)


def _dot_nt(a, b):
    return lax.dot_general(a, b, (((1,), (1,)), ((), ())), preferred_element_type=f32)


def _split2(x):
    hi = x.astype(bf16)
    lo = (x - hi.astype(f32)).astype(bf16)
    return hi, lo


def _log_sigmoid(z):
    return jnp.minimum(z, 0.0) - jnp.log(1.0 + jnp.exp(-jnp.abs(z)))


def _silu(x):
    return x * (1.0 / (1.0 + jnp.exp(-x)))


def _sigmoid(x):
    return 1.0 / (1.0 + jnp.exp(-x))


def _split3_dot(r, oh):
    h1 = r.astype(bf16)
    r2 = r - h1.astype(f32)
    h2 = r2.astype(bf16)
    h3 = (r2 - h2.astype(f32)).astype(bf16)
    return _dot(h1, oh) + _dot(h2, oh) + _dot(h3, oh)


def _rpb_expand_kernel(r0_ref, r1_ref, oh0_ref, oh1_ref, mask_ref, o_ref):
    gathered = _split3_dot(r0_ref[...], oh0_ref[...]) + _split3_dot(r1_ref[...], oh1_ref[...])
    o_ref[...] = gathered * LOG2E + mask_ref[...]


def _rpb_expand(rpb):
    nd = 2 * NA_WR - 1
    nrow = NA_H * nd
    ncol = 2 * NA_WC - 1
    nxt = jnp.concatenate([rpb[:, 1:], jnp.zeros_like(rpb[:, :1])], axis=1)
    r0 = jnp.zeros((128, 128), f32).at[:nrow, :ncol].set(rpb.reshape(nrow, ncol))
    r1 = jnp.zeros((128, 128), f32).at[:nrow, :ncol].set(nxt.reshape(nrow, ncol))
    q = np.arange(GRID_W)
    cs = np.clip(q - NA_WC // 2, 0, GRID_W - NA_WC)
    in_win = (q[None, :] >= cs[:, None]) & (q[None, :] < cs[:, None] + NA_WC)
    dc = np.clip(q[None, :] - q[:, None], -(NA_WC - 1), NA_WC - 1) + NA_WC - 1
    oh = np.zeros((2, 128, GRID_W, 128), np.float32)
    qq, kk = np.nonzero(in_win)
    oh[0, dc[qq, kk], qq, kk] = 1.0
    oh[1, dc[qq, kk], qq, kk + GRID_W] = 1.0
    oh = oh.reshape(2, 128, GRID_W * 128)
    mask = np.where(np.concatenate([in_win, in_win], axis=1), 0.0, NEG_INF).reshape(1, GRID_W * 128)
    out = pl.pallas_call(
        _rpb_expand_kernel,
        out_shape=jax.ShapeDtypeStruct((128, GRID_W * 128), f32),
        name="rpb_expand",
    )(r0, r1, jnp.asarray(oh[0], bf16), jnp.asarray(oh[1], bf16), jnp.asarray(mask, f32))
    return out[:nrow].reshape(NA_H, nd, GRID_W, 128)


def _cast_kernel(x_ref, o_ref):
    o_ref[...] = x_ref[...].astype(bf16)


def _cast_bf16(w, rows):
    R, C = w.shape
    return pl.pallas_call(
        _cast_kernel,
        grid=(R // rows,),
        in_specs=[pl.BlockSpec((rows, C), lambda j: (j, 0))],
        out_specs=pl.BlockSpec((rows, C), lambda j: (j, 0)),
        out_shape=jax.ShapeDtypeStruct((R, C), bf16),
        name="cast_bf16",
    )(w)


def _mod_kernel(a_ref, w_ref, b_ref, o_ref):
    a = _silu(a_ref[...]).astype(bf16)
    o_ref[...] = _dot(a, w_ref[...].astype(bf16)) + b_ref[...]


def _adaln_mod(c, c_ctx, w_mod, b_mod):
    B = c.shape[0]
    a = jnp.zeros((8, D), f32).at[:B].set(c).at[B].set(c_ctx)
    n = w_mod.shape[1]
    return pl.pallas_call(
        _mod_kernel,
        grid=(n // D,),
        in_specs=[pl.BlockSpec((8, D), lambda j: (0, 0)),
                  pl.BlockSpec((D, D), lambda j: (0, j)),
                  pl.BlockSpec((1, D), lambda j: (0, j))],
        out_specs=pl.BlockSpec((8, D), lambda j: (0, j)),
        out_shape=jax.ShapeDtypeStruct((8, n), f32),
        name="adaln_mod",
    )(a, w_mod, b_mod.reshape(1, n))


def _norm_mod(x, g, sh, sc):
    ms = jnp.mean(x * x, axis=-1, keepdims=True)
    return (x * lax.rsqrt(ms + EPS) * g) * (1.0 + sc) + sh


def _head_rms(t, bd, g):
    sq = (t * t).astype(bf16)
    ms = jnp.concatenate([_dot(sq[:, :256], bd), _dot(sq[:, 256:], bd)], axis=1)
    return t * lax.rsqrt(ms + EPS) * g


def _log_alpha(zl, wa_ref, ba_ref):
    z = _dot(zl.astype(bf16), wa_ref[...]) + ba_ref[...]
    return jnp.maximum(_log_sigmoid(z) * (1.0 / GLA_TAU), GLA_LA_MIN)


def _ctx_kernel(ctx_ref, sh_ref, sc_ref, g1_ref, wk_ref, wv_ref, wkg_ref, wvg_ref, wlr_ref,
                wa_ref, ba_ref, gk_ref, bd_ref, su_ref, sl_ref,
                kc_ref, vc_ref, sf_ref, sb_ref):
    h = _norm_mod(ctx_ref[0], g1_ref[...], sh_ref[0], sc_ref[0]).astype(bf16)
    k = _dot_nt(h, wk_ref[...])
    kc_ref[0] = _head_rms(k, bd_ref[...], gk_ref[...]).astype(bf16)
    vc_ref[0] = _dot_nt(h, wv_ref[...]).astype(bf16)
    kg = _dot_nt(h, wkg_ref[...])
    vg = _dot_nt(h, wvg_ref[...]).astype(bf16)
    la = _log_alpha(_dot_nt(h, wlr_ref[...]), wa_ref, ba_ref)
    hi, lo = _split2(la)
    ef = _dot(su_ref[...], hi[:, :GQK]) + _dot(su_ref[...], lo[:, :GQK])
    eb = _dot(sl_ref[...], hi[:, GQK:]) + _dot(sl_ref[...], lo[:, GQK:])
    r = lax.broadcasted_iota(jnp.int32, (GQK, GV), 0) >> 6
    cc = lax.broadcasted_iota(jnp.int32, (GQK, GV), 1) >> 7
    diag = r == cc
    for e, out in ((ef, sf_ref), (eb, sb_ref)):
        kw = (kg * jnp.exp(e)).astype(bf16)
        u = _dot(kw.T, vg)
        out[0] = jnp.where(diag, u, 0.0)


def _ctx_side(ctx, mod3, norm1_g, w_t, w_a, b_a, gk_t, bd, n_b):
    su = jnp.asarray(np.triu(np.ones((CTX, CTX), np.float32), 1), bf16)
    sl = jnp.asarray(np.tril(np.ones((CTX, CTX), np.float32), -1), bf16)
    B = ctx.shape[0]
    cst = lambda shape, idx: pl.BlockSpec(shape, lambda b: idx)
    return pl.pallas_call(
        _ctx_kernel,
        grid=(B,),
        in_specs=[pl.BlockSpec((1, CTX, D), lambda b: (b, 0, 0)),
                  cst((1, 1, D), (n_b, 0, 0)), cst((1, 1, D), (n_b, 0, 1)),
                  cst((1, D), (0, 0)),
                  cst((NA_W, D), (OFF_KNA // NA_W, 0)), cst((NA_W, D), (OFF_VNA // NA_W, 0)),
                  cst((GQK, D), (OFF_KG // GQK, 0)), cst((GV, D), (OFF_VG // GV, 0)),
                  cst((N_LR, D), (OFF_LR // N_LR, 0)),
                  cst((N_LR, 2 * GQK), (0, 0)), cst((1, 2 * GQK), (0, 0)),
                  cst((1, NA_W), (0, 0)), cst((256, 256), (0, 0)),
                  cst((CTX, CTX), (0, 0)), cst((CTX, CTX), (0, 0))],
        out_specs=[pl.BlockSpec((1, CTX, NA_W), lambda b: (b, 0, 0)),
                   pl.BlockSpec((1, CTX, NA_W), lambda b: (b, 0, 0)),
                   pl.BlockSpec((1, GQK, GV), lambda b: (b, 0, 0)),
                   pl.BlockSpec((1, GQK, GV), lambda b: (b, 0, 0))],
        out_shape=[jax.ShapeDtypeStruct((B, CTX, NA_W), bf16),
                   jax.ShapeDtypeStruct((B, CTX, NA_W), bf16),
                   jax.ShapeDtypeStruct((B, GQK, GV), f32),
                   jax.ShapeDtypeStruct((B, GQK, GV), f32)],
        compiler_params=pltpu.CompilerParams(vmem_limit_bytes=VMEM_LIMIT),
        name="ctx_side",
    )(ctx, mod3, mod3, norm1_g, w_t, w_t, w_t, w_t, w_t, w_a, b_a, gk_t, bd, su, sl)


def _proj_kernel(x_ref, sh_ref, sc_ref, g1_ref, w_ref, gq_ref, gk_ref, bd_ref,
                 rc_ref, rs_ref, cc_ref, cs_ref, wa_ref, ba_ref, tl_ref, *refs):
    n_cast = (len(refs) - 14) // 2
    cast_in, refs = refs[:n_cast], refs[n_cast:]
    (qna_ref, kna_ref, vna_ref, qf_ref, kif_ref, kef_ref, qb_ref, kib_ref, keb_ref,
     vg_ref, og_ref, ma_ref, mb_ref, dec_ref) = refs[:14]
    cast_out = refs[14:]
    for src, dst in zip(cast_in, cast_out):
        dst[...] = src[...].astype(bf16)

    bd = bd_ref[...]
    tl = tl_ref[...]
    nrow = PROJ_SUB // GRID_W
    nchunk = PROJ_SUB // CHUNK
    lane = lax.broadcasted_iota(jnp.int32, (PROJ_SUB, GQK), 1)
    first_half = (lane & 31) < 16

    for hs in range(0, x_ref.shape[1], PROJ_SUB):
        rows = slice(hs, hs + PROJ_SUB)
        hb = _norm_mod(x_ref[0, rows], g1_ref[...], sh_ref[0], sc_ref[0]).astype(bf16)

        def proj(off, n):
            return _dot_nt(hb, w_ref[off:off + n, :])

        zl = proj(OFF_LR, N_LR)
        q_raw = proj(OFF_QNA, NA_W)
        la = _log_alpha(zl, wa_ref, ba_ref)
        k_raw = proj(OFF_KNA, NA_W)
        hi, lo = _split2(la)
        qna_ref[0, rows] = _head_rms(q_raw, bd, gq_ref[...]).astype(bf16)
        cum = jnp.concatenate(
            [_dot(tl, hi[s:s + 256]) + _dot(tl, lo[s:s + 256]) for s in range(0, PROJ_SUB, 256)], axis=0)
        kna_ref[0, rows] = _head_rms(k_raw, bd, gk_ref[...]).astype(bf16)
        qg_raw = proj(OFF_QG, GQK)
        kg_raw = proj(OFF_KG, GQK)
        vna_ref[0, rows] = proj(OFF_VNA, NA_W).astype(bf16)
        vg_ref[0, rows] = proj(OFF_VG, GV).astype(bf16)

        gr = slice(hs // GRID_W, hs // GRID_W + nrow)
        cos_t = (rc_ref[gr][:, None, :] + cc_ref[...][None]).reshape(PROJ_SUB, GQK)
        sin_t = (rs_ref[gr][:, None, :] + cs_ref[...][None]).reshape(PROJ_SUB, GQK)

        def rope(t):
            partner = jnp.where(first_half, pltpu.roll(t, GQK - 16, 1), pltpu.roll(t, 16, 1))
            return t * cos_t + partner * sin_t

        qg = rope(qg_raw) * (GLA_DK ** -0.5)
        kg = rope(kg_raw)

        cum3 = cum.reshape(nchunk, CHUNK, 2 * GQK)
        tot3 = cum3[:, CHUNK - 1:CHUNK, :]
        la3 = la.reshape(nchunk, CHUNK, 2 * GQK)
        dec_ref[0, hs // CHUNK:hs // CHUNK + nchunk] = jnp.exp(tot3).reshape(nchunk, 2 * GQK)
        b_f = cum3[:, :, :GQK].reshape(PROJ_SUB, GQK)
        e_f = (tot3 - cum3)[:, :, :GQK].reshape(PROJ_SUB, GQK)
        b_b = (tot3 - cum3 + la3)[:, :, GQK:].reshape(PROJ_SUB, GQK)
        e_b = (cum3 - la3)[:, :, GQK:].reshape(PROJ_SUB, GQK)
        og_ref[0, rows] = _silu(proj(OFF_OG, GV)).astype(bf16)
        qf_ref[0, rows] = (qg * jnp.exp(b_f)).astype(bf16)
        kif_ref[0, rows] = (kg * jnp.exp(-b_f)).astype(bf16)
        kef_ref[0, rows] = (kg * jnp.exp(e_f)).astype(bf16)
        ma_ref[0, rows] = _sigmoid(proj(OFF_MA, D)).astype(bf16)
        qb_ref[0, rows] = (qg * jnp.exp(b_b)).astype(bf16)
        kib_ref[0, rows] = (kg * jnp.exp(-b_b)).astype(bf16)
        keb_ref[0, rows] = (kg * jnp.exp(e_b)).astype(bf16)
        mb_ref[0, rows] = _sigmoid(proj(OFF_MB, D)).astype(bf16)


def _rope_tables(L):
    n = GLA_DK // 4
    freqs = ROPE_THETA ** (-np.arange(n, dtype=np.float64) / n)
    lane = np.arange(GQK)
    fl = freqs[lane % n]
    is_row = (lane % GLA_DK) < GLA_DK // 2
    sign = np.where((lane % 32) < 16, -1.0, 1.0)
    rows = np.arange(L // GRID_W)[:, None] * fl[None]
    cols = np.arange(GRID_W)[:, None] * fl[None]
    rc = np.where(is_row, np.cos(rows), 0.0)
    rs = np.where(is_row, np.sin(rows) * sign, 0.0)
    cc = np.where(~is_row, np.cos(cols), 0.0)
    cs = np.where(~is_row, np.sin(cols) * sign, 0.0)
    return [jnp.asarray(t, f32) for t in (rc, rs, cc, cs)]


def _in_proj(x, mod3, norm1_g, w_t, gq_t, gk_t, bd, w_a, b_a, cast_ws):
    B, L, _ = x.shape
    tm = TM_PROJ
    rows_per = tm // GRID_W
    nstep = L // tm
    total = B * nstep

    def slab(w):
        R, C = w.shape
        if R % (total * BF16_SUBLANES) == 0:
            return pl.BlockSpec((R // total, C), lambda b, i: (b * nstep + i, 0))
        assert (2 * R) % (total * BF16_SUBLANES) == 0 and C % 256 == 0
        return pl.BlockSpec((2 * R // total, C // 2), lambda b, i: ((b * nstep + i) // 2, (b * nstep + i) % 2))
    rc, rs, cc, cs = _rope_tables(L)
    blk = np.kron(np.eye(256 // CHUNK), np.tril(np.ones((CHUNK, CHUNK)))).astype(np.float32)
    tl = jnp.asarray(blk, bf16)
    tok = lambda n: pl.BlockSpec((1, tm, n), lambda b, i: (b, i, 0))
    sds = lambda n, dt: jax.ShapeDtypeStruct((B, L, n), dt)
    return pl.pallas_call(
        _proj_kernel,
        grid=(B, L // tm),
        in_specs=[tok(D),
                  pl.BlockSpec((1, 1, D), lambda b, i: (b, 0, 0)),
                  pl.BlockSpec((1, 1, D), lambda b, i: (b, 0, 1)),
                  _const_spec((1, D)), _const_spec((D_IN, D)),
                  _const_spec((1, NA_W)), _const_spec((1, NA_W)), _const_spec((256, 256)),
                  pl.BlockSpec((rows_per, GQK), lambda b, i: (i, 0)),
                  pl.BlockSpec((rows_per, GQK), lambda b, i: (i, 0)),
                  _const_spec((GRID_W, GQK)), _const_spec((GRID_W, GQK)),
                  _const_spec((N_LR, 2 * GQK)), _const_spec((1, 2 * GQK)), _const_spec((256, 256))]
                 + [slab(w) for w in cast_ws],
        out_specs=[tok(NA_W), tok(NA_W), tok(NA_W)] + [tok(GQK)] * 6
                  + [tok(GV), tok(GV), tok(D), tok(D),
                     pl.BlockSpec((1, tm // CHUNK, 2 * GQK), lambda b, i: (b, i, 0))]
                  + [slab(w) for w in cast_ws],
        out_shape=[sds(NA_W, bf16)] * 3 + [sds(GQK, bf16)] * 6
                  + [sds(GV, bf16), sds(GV, bf16), sds(D, bf16), sds(D, bf16),
                     jax.ShapeDtypeStruct((B, L // CHUNK, 2 * GQK), f32)]
                  + [jax.ShapeDtypeStruct(w.shape, bf16) for w in cast_ws],
        compiler_params=pltpu.CompilerParams(
            dimension_semantics=("parallel", "parallel"), vmem_limit_bytes=VMEM_LIMIT),
        name="in_proj",
    )(x, mod3, mod3, norm1_g, w_t, gq_t, gk_t, bd, rc, rs, cc, cs, w_a, b_a, tl, *cast_ws)


def _lane_blocks(t):
    return [t[:, c:c + 128] for c in range(0, t.shape[1], 128)]


def _na_kernel(q_ref, k_ref, v_ref, kc_ref, vc_ref, t2_ref, o_ref, sl_ref, sc_ref, m_ref):
    i = pl.program_id(1)
    rows = k_ref.shape[1] // GRID_W
    lane_head = lax.broadcasted_iota(jnp.int32, (GRID_W, 256), 1) >> 6
    hmask = [lane_head == h for h in range(4)]

    def window_start(r):
        rs = jnp.clip(r - NA_WR // 2, 0, rows - NA_WR)
        return rs, pl.multiple_of(rs * GRID_W, GRID_W)

    def produce(rr, bank, j):
        r = i * NA_R + rr
        rs, start = window_start(r)
        d0 = rs - r + (NA_WR - 1)
        row0 = pl.multiple_of(rr * GRID_W, GRID_W)
        for g in range(2):
            ls = slice(256 * g, 256 * g + 256)
            q = q_ref[0, pl.ds(row0, GRID_W), ls]
            qs = jnp.concatenate([jnp.where(m, q, jnp.zeros_like(q)) for m in hmask], axis=0)
            kw = k_ref[0, pl.ds(start, NA_WR * GRID_W), ls]
            bias = jnp.concatenate(
                [jnp.concatenate([t2_ref[4 * g + h, d0 + 2 * jj] for jj in range(NA_WR // 2)], axis=1)
                 for h in range(4)], axis=0)
            s_loc = _dot_nt(qs, kw) + bias
            s_ctx = _dot_nt(qs, kc_ref[0, :, ls])
            mx = functools.reduce(jnp.maximum, _lane_blocks(s_loc) + _lane_blocks(s_ctx))
            sl_ref[bank, j, g] = s_loc
            sc_ref[bank, j, g] = s_ctx
            m_ref[bank, j, g] = jnp.broadcast_to(jnp.max(mx, axis=-1, keepdims=True), (4 * GRID_W, 128))

    def consume(rr, bank, j):
        _, start = window_start(i * NA_R + rr)
        row0 = pl.multiple_of(rr * GRID_W, GRID_W)
        outs = []
        for g in range(2):
            ls = slice(256 * g, 256 * g + 256)
            m = m_ref[bank, j, g]
            p_loc = [jnp.exp2(sl_ref[bank, j, g, :, c:c + 128] - m) for c in range(0, NA_WR * GRID_W, 128)]
            p_ctx = [jnp.exp2(sc_ref[bank, j, g, :, c:c + 128] - m) for c in range(0, CTX, 128)]
            den = jnp.sum(functools.reduce(jnp.add, p_loc + p_ctx), axis=-1, keepdims=True)
            vw = v_ref[0, pl.ds(start, NA_WR * GRID_W), ls]
            acc = (_dot(jnp.concatenate(p_loc, axis=1).astype(bf16), vw)
                   + _dot(jnp.concatenate(p_ctx, axis=1).astype(bf16), vc_ref[0, :, ls]))
            acc = acc * (1.0 / den)
            o = jnp.zeros((GRID_W, 256), f32)
            for h in range(4):
                o = o + jnp.where(hmask[h], acc[h * GRID_W:(h + 1) * GRID_W], 0.0)
            outs.append(o)
        o_ref[0, pl.ds(row0, GRID_W), :] = jnp.concatenate(outs, axis=1).astype(bf16)

    for j in range(NA_RI):
        produce(jnp.int32(j), 0, j)

    def step(t, carry):
        bank = t & 1
        for j in range(NA_RI):
            consume(t * NA_RI + j, bank, j)
        for j in range(NA_RI):
            produce(jnp.minimum((t + 1) * NA_RI + j, NA_R - 1), 1 - bank, j)
        return carry

    lax.fori_loop(0, NA_R // NA_RI, step, 0)


def _na_attn(q, k, v, kc, vc, t2):
    B, L, _ = q.shape
    nblk = L // (NA_R * GRID_W)
    return pl.pallas_call(
        _na_kernel,
        grid=(B, nblk),
        in_specs=[pl.BlockSpec((1, NA_R * GRID_W, NA_W), lambda b, i: (b, i, 0)),
                  pl.BlockSpec((1, L, NA_W), lambda b, i: (b, 0, 0), pipeline_mode=pl.Buffered(1)),
                  pl.BlockSpec((1, L, NA_W), lambda b, i: (b, 0, 0), pipeline_mode=pl.Buffered(1)),
                  pl.BlockSpec((1, CTX, NA_W), lambda b, i: (b, 0, 0)),
                  pl.BlockSpec((1, CTX, NA_W), lambda b, i: (b, 0, 0)),
                  _const_spec((NA_H, 2 * NA_WR - 1, GRID_W, 128))],
        out_specs=pl.BlockSpec((1, NA_R * GRID_W, NA_W), lambda b, i: (b, i, 0)),
        out_shape=jax.ShapeDtypeStruct((B, L, NA_W), bf16),
        scratch_shapes=[pltpu.VMEM((2, NA_RI, 2, 4 * GRID_W, NA_WR * GRID_W), f32),
                        pltpu.VMEM((2, NA_RI, 2, 4 * GRID_W, CTX), f32),
                        pltpu.VMEM((2, NA_RI, 2, 4 * GRID_W, 128), f32)],
        compiler_params=pltpu.CompilerParams(
            dimension_semantics=("parallel", "parallel"), vmem_limit_bytes=VMEM_LIMIT),
        name="na_attn",
    )(q, k, v, kc, vc, t2)


def _gla_chunk(q, ki, ke, v, dcol, s_ref, tri, hmask):
    qs = jnp.concatenate([jnp.where(m, q, jnp.zeros_like(q)) for m in hmask], axis=0)
    att = jnp.where(tri, _dot_nt(qs, ki), 0.0).astype(bf16)
    o = _dot(q, s_ref[...].astype(bf16))
    intra = [_dot(att[h * CHUNK:(h + 1) * CHUNK], v[:, h * GLA_DV:(h + 1) * GLA_DV]) for h in range(GLA_H)]
    o = o + jnp.concatenate(intra, axis=1)
    ket = ke.T
    for h in range(GLA_H):
        rs = slice(h * GLA_DK, (h + 1) * GLA_DK)
        cs = slice(h * GLA_DV, (h + 1) * GLA_DV)
        s_ref[rs, cs] = dcol[rs] * s_ref[rs, cs] + _dot(ket[rs], v[:, cs])
    return o


def _gla_kernel(qf_ref, kif_ref, kef_ref, vf_ref, decf_ref, qb_ref, kib_ref, keb_ref, vb_ref, decb_ref,
                s0f_ref, s0b_ref, of_ref, ob_ref, sf_ref, sb_ref):
    @pl.when(pl.program_id(1) == 0)
    def _():
        sf_ref[...] = s0f_ref[0]
        sb_ref[...] = s0b_ref[0]

    lane_head = lax.broadcasted_iota(jnp.int32, (CHUNK, GQK), 1) >> 6
    hmask = [lane_head == h for h in range(GLA_H)]
    ri = lax.broadcasted_iota(jnp.int32, (GLA_H * CHUNK, CHUNK), 0) & (CHUNK - 1)
    ci = lax.broadcasted_iota(jnp.int32, (GLA_H * CHUNK, CHUNK), 1)
    tri_f = ri >= ci
    tri_b = ri <= ci
    dect_f = decf_ref[0][:, :GQK].T
    dect_b = decb_ref[0][:, GQK:].T
    for c in range(GLA_CB):
        ts = slice(c * CHUNK, (c + 1) * CHUNK)
        of_ref[0, ts, :] = _gla_chunk(qf_ref[0, ts, :], kif_ref[0, ts, :], kef_ref[0, ts, :], vf_ref[0, ts, :],
                                      dect_f[:, c:c + 1], sf_ref, tri_f, hmask)
        cb = GLA_CB - 1 - c
        tb = slice(cb * CHUNK, (cb + 1) * CHUNK)
        ob_ref[0, tb, :] = _gla_chunk(qb_ref[0, tb, :], kib_ref[0, tb, :], keb_ref[0, tb, :], vb_ref[0, tb, :],
                                      dect_b[:, cb:cb + 1], sb_ref, tri_b, hmask)


def _gla_scan(qf, kif, kef, qb, kib, keb, vg, dec, s0f, s0b):
    B, L, _ = vg.shape
    tm = GLA_CB * CHUNK
    nb = L // tm
    fw = lambda n: pl.BlockSpec((1, tm, n), lambda b, i: (b, i, 0))
    bw = lambda n: pl.BlockSpec((1, tm, n), lambda b, i: (b, nb - 1 - i, 0))
    st = pl.BlockSpec((1, GQK, GV), lambda b, i: (b, 0, 0))
    return pl.pallas_call(
        _gla_kernel,
        grid=(B, nb),
        in_specs=[fw(GQK), fw(GQK), fw(GQK), fw(GV),
                  pl.BlockSpec((1, GLA_CB, 2 * GQK), lambda b, i: (b, i, 0)),
                  bw(GQK), bw(GQK), bw(GQK), bw(GV),
                  pl.BlockSpec((1, GLA_CB, 2 * GQK), lambda b, i: (b, nb - 1 - i, 0)),
                  st, st],
        out_specs=[fw(GV), bw(GV)],
        out_shape=[jax.ShapeDtypeStruct((B, L, GV), f32)] * 2,
        scratch_shapes=[pltpu.VMEM((GQK, GV), f32), pltpu.VMEM((GQK, GV), f32)],
        compiler_params=pltpu.CompilerParams(
            dimension_semantics=("parallel", "arbitrary"), vmem_limit_bytes=VMEM_LIMIT),
        name="gla_scan",
    )(qf, kif, kef, vg, dec, qb, kib, keb, vg, dec, s0f, s0b)


def _merge_ffn_kernel(x_ref, ona_ref, of_ref, ob_ref, og_ref, ma_ref, mb_ref,
                      g1_ref, sh2_ref, sc2_ref, g2_ref, n2_ref, gg_ref,
                      wna_ref, wgl_ref, wo_ref, wg_ref, wu_ref, wd_ref, o_ref):
    for hs in range(0, x_ref.shape[1], FFN_SUB):
        rows = slice(hs, hs + FFN_SUB)
        og = of_ref[0, rows] + ob_ref[0, rows]
        parts = []
        for h in range(GLA_H):
            seg = og[:, h * GLA_DV:(h + 1) * GLA_DV]
            ms = jnp.mean(seg * seg, axis=-1, keepdims=True)
            parts.append(seg * lax.rsqrt(ms + EPS) * gg_ref[...])
        ogl = (jnp.concatenate(parts, axis=1) * og_ref[0, rows].astype(f32)).astype(bf16)
        y = (ma_ref[0, rows].astype(f32) * _dot(ona_ref[0, rows], wna_ref[...])
             + mb_ref[0, rows].astype(f32) * _dot(ogl, wgl_ref[...]))
        x1 = x_ref[0, rows] + g1_ref[0] * _dot(y.astype(bf16), wo_ref[...])
        h2 = _norm_mod(x1, n2_ref[...], sh2_ref[0], sc2_ref[0]).astype(bf16)
        act = (_silu(_dot(h2, wg_ref[...])) * _dot(h2, wu_ref[...])).astype(bf16)
        o_ref[0, rows] = x1 + g2_ref[0] * _dot(act, wd_ref[...])


def _merge_ffn(x, ona, of, ob, og, ma, mb, mod3, norm2_g, gg, wna, wgl, wo, wg, wu, wd):
    B, L, _ = x.shape
    tm = TM_FFN
    tok = lambda n: pl.BlockSpec((1, tm, n), lambda b, i: (b, i, 0))
    modv = lambda j: pl.BlockSpec((1, 1, D), lambda b, i: (b, 0, j))
    return pl.pallas_call(
        _merge_ffn_kernel,
        grid=(B, L // tm),
        in_specs=[tok(D), tok(NA_W), tok(GV), tok(GV), tok(GV), tok(D), tok(D),
                  modv(2), modv(3), modv(4), modv(5),
                  _const_spec((1, D)), _const_spec((1, GLA_DV)),
                  _const_spec((NA_W, D)), _const_spec((GV, D)), _const_spec((D, D)),
                  _const_spec((D, D_FF)), _const_spec((D, D_FF)), _const_spec((D_FF, D))],
        out_specs=tok(D),
        out_shape=jax.ShapeDtypeStruct((B, L, D), f32),
        compiler_params=pltpu.CompilerParams(
            dimension_semantics=("parallel", "parallel"), vmem_limit_bytes=VMEM_LIMIT),
        name="merge_ffn",
    )(x, ona, of, ob, og, ma, mb, mod3, mod3, mod3, mod3, norm2_g, gg, wna, wgl, wo, wg, wu, wd)


def kernel(x, c, ctx, c_ctx, w_mod, b_mod, norm1_g, norm2_g, w_in, na_q_norm_g, na_k_norm_g, na_rpb,
           gla_w_alpha, gla_b_alpha, gla_norm_g, w_branch_na, w_branch_gla, w_out,
           w_ffn_gate, w_ffn_up, w_ffn_down):
    B, L, _ = x.shape
    assert w_mod.shape[0] == 1 and L % (NA_R * GRID_W) == 0 and L % TM_PROJ == 0

    w_t = _cast_bf16(jnp.transpose(w_in[0]), D_IN // 7)
    w_a = jnp.zeros((N_LR, 2 * GQK), f32)
    w_a = w_a.at[:GLA_RANK, :GQK].set(gla_w_alpha[0, 0]).at[GLA_RANK:, GQK:].set(gla_w_alpha[0, 1])
    w_a = w_a.astype(bf16)
    b_a = gla_b_alpha[0].reshape(1, 2 * GQK)
    gq_t = jnp.tile(na_q_norm_g[0] * (NA_DH ** -0.5 * LOG2E), NA_H).reshape(1, NA_W)
    gk_t = jnp.tile(na_k_norm_g[0], NA_H).reshape(1, NA_W)
    bd = jnp.asarray(np.kron(np.eye(4), np.full((NA_DH, NA_DH), 1.0 / NA_DH)), bf16)

    mod = _adaln_mod(c, c_ctx, w_mod[0], b_mod[0])
    mod3 = mod.reshape(8, 1, 6 * D)
    t2 = _rpb_expand(na_rpb[0])

    kc, vc, s0f, s0b = _ctx_side(ctx, mod3, norm1_g, w_t, w_a, b_a, gk_t, bd, B)
    later_ws = [w_branch_na[0], w_branch_gla[0], w_out[0], w_ffn_gate[0], w_ffn_up[0], w_ffn_down[0]]
    (qna, kna, vna, qf, kif, kef, qb, kib, keb, vg, og, ma, mb, dec, *later_bf) = _in_proj(
        x, mod3, norm1_g, w_t, gq_t, gk_t, bd, w_a, b_a, later_ws)
    ona = _na_attn(qna, kna, vna, kc, vc, t2)
    of, ob = _gla_scan(qf, kif, kef, qb, kib, keb, vg, dec, s0f, s0b)
    return _merge_ffn(x, ona, of, ob, og, ma, mb, mod3, norm2_g, gla_norm_g, *later_bf)
```

```python
import functools

import numpy as np
import jax
import jax.numpy as jnp
from jax import lax
from jax.experimental import pallas as pl
from jax.experimental.pallas import tpu as pltpu

D = 1024
GRID_W = 64
CTX = 256
NA_H = 8
NA_DH = 64
NA_WR = 8
NA_WC = 16
GLA_H = 4
GLA_DK = 64
GLA_DV = 128
GLA_RANK = 16
GLA_TAU = 16.0
GLA_LA_MIN = -1.0
CHUNK = 64
ROPE_THETA = 10000.0
D_FF = 2816
EPS = 1e-6
NEG_INF = -1e30
NA_W = NA_H * NA_DH
GQK = GLA_H * GLA_DK
GV = GLA_H * GLA_DV

OFF_QNA, OFF_KNA, OFF_VNA = 0, 512, 1024
OFF_QG, OFF_KG, OFF_VG, OFF_OG = 1536, 1792, 2048, 2560
OFF_LR, OFF_MA, OFF_MB = 3072, 3104, 4128
N_LR = 2 * GLA_RANK
D_IN = 5152
LOG2E = 1.4426950408889634

BF16_SUBLANES = 16
VMEM_LIMIT = 56 * 1024 * 1024

TM_PROJ = 512
PROJ_SUB = 256
TM_FFN = 512
FFN_SUB = 512
NA_R = 32
NA_RI = 4
GLA_CB = 32

f32 = jnp.float32
bf16 = jnp.bfloat16


def _const_spec(shape):
    nd = len(shape)
    return pl.BlockSpec(shape, lambda *_: (0,) * nd, pipeline_mode=pl.Buffered(1))


def _dot(a, b):
    return jnp.dot(a, b, preferred_element_type=f32)


def _dot_nt(a, b):
    return lax.dot_general(a, b, (((1,), (1,)), ((), ())), preferred_element_type=f32)


def _split2(x):
    hi = x.astype(bf16)
    lo = (x - hi.astype(f32)).astype(bf16)
    return hi, lo


def _sigmoid(x):
    return 0.5 * jnp.tanh(0.5 * x) + 0.5


def _silu(x):
    return x * _sigmoid(x)


def _split3_dot(r, oh):
    h1 = r.astype(bf16)
    r2 = r - h1.astype(f32)
    h2 = r2.astype(bf16)
    h3 = (r2 - h2.astype(f32)).astype(bf16)
    return _dot(h1, oh) + _dot(h2, oh) + _dot(h3, oh)


def _rpb_expand_kernel(r0_ref, r1_ref, oh0_ref, oh1_ref, mask_ref, o_ref):
    gathered = _split3_dot(r0_ref[...], oh0_ref[...]) + _split3_dot(r1_ref[...], oh1_ref[...])
    o_ref[...] = gathered * LOG2E + mask_ref[...]


def _rpb_expand(rpb):
    nd = 2 * NA_WR - 1
    nrow = NA_H * nd
    ncol = 2 * NA_WC - 1
    nxt = jnp.concatenate([rpb[:, 1:], jnp.zeros_like(rpb[:, :1])], axis=1)
    r0 = jnp.zeros((128, 128), f32).at[:nrow, :ncol].set(rpb.reshape(nrow, ncol))
    r1 = jnp.zeros((128, 128), f32).at[:nrow, :ncol].set(nxt.reshape(nrow, ncol))
    q = np.arange(GRID_W)
    cs = np.clip(q - NA_WC // 2, 0, GRID_W - NA_WC)
    in_win = (q[None, :] >= cs[:, None]) & (q[None, :] < cs[:, None] + NA_WC)
    dc = np.clip(q[None, :] - q[:, None], -(NA_WC - 1), NA_WC - 1) + NA_WC - 1
    oh = np.zeros((2, 128, GRID_W, 128), np.float32)
    qq, kk = np.nonzero(in_win)
    oh[0, dc[qq, kk], qq, kk] = 1.0
    oh[1, dc[qq, kk], qq, kk + GRID_W] = 1.0
    oh = oh.reshape(2, 128, GRID_W * 128)
    mask = np.where(np.concatenate([in_win, in_win], axis=1), 0.0, NEG_INF).reshape(1, GRID_W * 128)
    out = pl.pallas_call(
        _rpb_expand_kernel,
        out_shape=jax.ShapeDtypeStruct((128, GRID_W * 128), f32),
        name="rpb_expand",
    )(r0, r1, jnp.asarray(oh[0], bf16), jnp.asarray(oh[1], bf16), jnp.asarray(mask, f32))
    return out[:nrow].reshape(NA_H, nd, GRID_W, 128)


def _cast_kernel(x_ref, o_ref):
    o_ref[...] = x_ref[...].astype(bf16)


def _cast_bf16(w, rows):
    R, C = w.shape
    return pl.pallas_call(
        _cast_kernel,
        grid=(R // rows,),
        in_specs=[pl.BlockSpec((rows, C), lambda j: (j, 0))],
        out_specs=pl.BlockSpec((rows, C), lambda j: (j, 0)),
        out_shape=jax.ShapeDtypeStruct((R, C), bf16),
        name="cast_bf16",
    )(w)


def _mod_kernel(a_ref, w_ref, b_ref, o_ref):
    a = _silu(a_ref[...]).astype(bf16)
    o_ref[...] = _dot(a, w_ref[...].astype(bf16)) + b_ref[...]


def _adaln_mod(c, c_ctx, w_mod, b_mod):
    B = c.shape[0]
    a = jnp.zeros((8, D), f32).at[:B].set(c).at[B].set(c_ctx)
    n = w_mod.shape[1]
    return pl.pallas_call(
        _mod_kernel,
        grid=(n // D,),
        in_specs=[pl.BlockSpec((8, D), lambda j: (0, 0)),
                  pl.BlockSpec((D, D), lambda j: (0, j)),
                  pl.BlockSpec((1, D), lambda j: (0, j))],
        out_specs=pl.BlockSpec((8, D), lambda j: (0, j)),
        out_shape=jax.ShapeDtypeStruct((8, n), f32),
        name="adaln_mod",
    )(a, w_mod, b_mod.reshape(1, n))


def _norm_mod(x, g, sh, sc):
    ms = jnp.mean(x * x, axis=-1, keepdims=True)
    return (x * lax.rsqrt(ms + EPS)) * (g * (1.0 + sc)) + sh


def _head_rms(t, bd, g):
    sq = (t * t).astype(bf16)
    ms = jnp.concatenate([_dot(sq[:, :256], bd), _dot(sq[:, 256:], bd)], axis=1)
    return t * lax.rsqrt(ms + EPS) * g


def _log_alpha(zl, wa_ref, ba_ref):
    z = _dot(zl.astype(bf16), wa_ref[...]) + ba_ref[...]
    soft = jnp.log2(1.0 + jnp.exp2(z * -LOG2E))
    return jnp.maximum(soft * (-1.0 / GLA_TAU), GLA_LA_MIN * LOG2E)


def _ctx_kernel(ctx_ref, sh_ref, sc_ref, g1_ref, wk_ref, wv_ref, wkg_ref, wvg_ref, wlr_ref,
                wa_ref, ba_ref, gk_ref, bd_ref, su_ref, sl_ref,
                kc_ref, vc_ref, sf_ref, sb_ref):
    h = _norm_mod(ctx_ref[0], g1_ref[...], sh_ref[0], sc_ref[0]).astype(bf16)
    k = _dot_nt(h, wk_ref[...])
    kc_ref[0] = _head_rms(k, bd_ref[...], gk_ref[...]).astype(bf16)
    vc_ref[0] = _dot_nt(h, wv_ref[...]).astype(bf16)
    kg = _dot_nt(h, wkg_ref[...])
    vg = _dot_nt(h, wvg_ref[...]).astype(bf16)
    la = _log_alpha(_dot_nt(h, wlr_ref[...]), wa_ref, ba_ref)
    hi, lo = _split2(la)
    ef = _dot(su_ref[...], hi[:, :GQK]) + _dot(su_ref[...], lo[:, :GQK])
    eb = _dot(sl_ref[...], hi[:, GQK:]) + _dot(sl_ref[...], lo[:, GQK:])
    r = lax.broadcasted_iota(jnp.int32, (GQK, GV), 0) >> 6
    cc = lax.broadcasted_iota(jnp.int32, (GQK, GV), 1) >> 7
    diag = r == cc
    for e, out in ((ef, sf_ref), (eb, sb_ref)):
        kw = (kg * jnp.exp2(e)).astype(bf16)
        u = _dot(kw.T, vg)
        out[0] = jnp.where(diag, u, 0.0)


def _ctx_side(ctx, mod3, norm1_g, w_t, w_a, b_a, gk_t, bd, n_b):
    su = jnp.asarray(np.triu(np.ones((CTX, CTX), np.float32), 1), bf16)
    sl = jnp.asarray(np.tril(np.ones((CTX, CTX), np.float32), -1), bf16)
    B = ctx.shape[0]
    cst = lambda shape, idx: pl.BlockSpec(shape, lambda b: idx)
    return pl.pallas_call(
        _ctx_kernel,
        grid=(B,),
        in_specs=[pl.BlockSpec((1, CTX, D), lambda b: (b, 0, 0)),
                  cst((1, 1, D), (n_b, 0, 0)), cst((1, 1, D), (n_b, 0, 1)),
                  cst((1, D), (0, 0)),
                  cst((NA_W, D), (OFF_KNA // NA_W, 0)), cst((NA_W, D), (OFF_VNA // NA_W, 0)),
                  cst((GQK, D), (OFF_KG // GQK, 0)), cst((GV, D), (OFF_VG // GV, 0)),
                  cst((N_LR, D), (OFF_LR // N_LR, 0)),
                  cst((N_LR, 2 * GQK), (0, 0)), cst((1, 2 * GQK), (0, 0)),
                  cst((1, NA_W), (0, 0)), cst((256, 256), (0, 0)),
                  cst((CTX, CTX), (0, 0)), cst((CTX, CTX), (0, 0))],
        out_specs=[pl.BlockSpec((1, CTX, NA_W), lambda b: (b, 0, 0)),
                   pl.BlockSpec((1, CTX, NA_W), lambda b: (b, 0, 0)),
                   pl.BlockSpec((1, GQK, GV), lambda b: (b, 0, 0)),
                   pl.BlockSpec((1, GQK, GV), lambda b: (b, 0, 0))],
        out_shape=[jax.ShapeDtypeStruct((B, CTX, NA_W), bf16),
                   jax.ShapeDtypeStruct((B, CTX, NA_W), bf16),
                   jax.ShapeDtypeStruct((B, GQK, GV), f32),
                   jax.ShapeDtypeStruct((B, GQK, GV), f32)],
        compiler_params=pltpu.CompilerParams(vmem_limit_bytes=VMEM_LIMIT),
        name="ctx_side",
    )(ctx, mod3, mod3, norm1_g, w_t, w_t, w_t, w_t, w_t, w_a, b_a, gk_t, bd, su, sl)


def _proj_kernel(x_ref, sh_ref, sc_ref, g1_ref, w_ref, gq_ref, gk_ref, bd_ref,
                 rc_ref, rs_ref, cc_ref, cs_ref, wa_ref, ba_ref, tl_ref, *refs):
    n_cast = (len(refs) - 14) // 2
    cast_in, refs = refs[:n_cast], refs[n_cast:]
    (qna_ref, kna_ref, vna_ref, qf_ref, kif_ref, kef_ref, qb_ref, kib_ref, keb_ref,
     vg_ref, og_ref, ma_ref, mb_ref, dec_ref) = refs[:14]
    cast_out = refs[14:]
    for src, dst in zip(cast_in, cast_out):
        dst[...] = src[...].astype(bf16)

    bd = bd_ref[...]
    tl = tl_ref[...]
    nrow = PROJ_SUB // GRID_W
    nchunk = PROJ_SUB // CHUNK
    lane = lax.broadcasted_iota(jnp.int32, (PROJ_SUB, GQK), 1)
    first_half = (lane & 31) < 16

    for hs in range(0, x_ref.shape[1], PROJ_SUB):
        rows = slice(hs, hs + PROJ_SUB)
        hb = _norm_mod(x_ref[0, rows], g1_ref[...], sh_ref[0], sc_ref[0]).astype(bf16)

        def proj(off, n):
            return _dot_nt(hb, w_ref[off:off + n, :])

        zl = proj(OFF_LR, N_LR)
        q_raw = proj(OFF_QNA, NA_W)
        la = _log_alpha(zl, wa_ref, ba_ref)
        k_raw = proj(OFF_KNA, NA_W)
        hi, lo = _split2(la)
        qna_ref[0, rows] = _head_rms(q_raw, bd, gq_ref[...]).astype(bf16)
        cum = jnp.concatenate(
            [_dot(tl, hi[s:s + 256]) + _dot(tl, lo[s:s + 256]) for s in range(0, PROJ_SUB, 256)], axis=0)
        kna_ref[0, rows] = _head_rms(k_raw, bd, gk_ref[...]).astype(bf16)
        qg_raw = proj(OFF_QG, GQK)
        kg_raw = proj(OFF_KG, GQK)
        vna_ref[0, rows] = proj(OFF_VNA, NA_W).astype(bf16)
        vg_ref[0, rows] = proj(OFF_VG, GV).astype(bf16)

        gr = slice(hs // GRID_W, hs // GRID_W + nrow)
        cos_t = (rc_ref[gr][:, None, :] + cc_ref[...][None]).reshape(PROJ_SUB, GQK)
        sin_t = (rs_ref[gr][:, None, :] + cs_ref[...][None]).reshape(PROJ_SUB, GQK)

        def rope(t):
            partner = jnp.where(first_half, pltpu.roll(t, GQK - 16, 1), pltpu.roll(t, 16, 1))
            return t * cos_t + partner * sin_t

        qg = rope(qg_raw) * (GLA_DK ** -0.5)
        kg = rope(kg_raw)

        cum3 = cum.reshape(nchunk, CHUNK, 2 * GQK)
        tot3 = cum3[:, CHUNK - 1:CHUNK, :]
        la3 = la.reshape(nchunk, CHUNK, 2 * GQK)
        dec_ref[0, hs // CHUNK:hs // CHUNK + nchunk] = jnp.exp2(tot3).reshape(nchunk, 2 * GQK)
        b_f = cum3[:, :, :GQK].reshape(PROJ_SUB, GQK)
        e_f = (tot3 - cum3)[:, :, :GQK].reshape(PROJ_SUB, GQK)
        b_b = (tot3 - cum3 + la3)[:, :, GQK:].reshape(PROJ_SUB, GQK)
        e_b = (cum3 - la3)[:, :, GQK:].reshape(PROJ_SUB, GQK)
        og_ref[0, rows] = _silu(proj(OFF_OG, GV)).astype(bf16)
        qf_ref[0, rows] = (qg * jnp.exp2(b_f)).astype(bf16)
        kif_ref[0, rows] = (kg * jnp.exp2(-b_f)).astype(bf16)
        kef_ref[0, rows] = (kg * jnp.exp2(e_f)).astype(bf16)
        ma_ref[0, rows] = _sigmoid(proj(OFF_MA, D)).astype(bf16)
        qb_ref[0, rows] = (qg * jnp.exp2(b_b)).astype(bf16)
        kib_ref[0, rows] = (kg * jnp.exp2(-b_b)).astype(bf16)
        keb_ref[0, rows] = (kg * jnp.exp2(e_b)).astype(bf16)
        mb_ref[0, rows] = _sigmoid(proj(OFF_MB, D)).astype(bf16)


def _rope_tables(L):
    n = GLA_DK // 4
    freqs = ROPE_THETA ** (-np.arange(n, dtype=np.float64) / n)
    lane = np.arange(GQK)
    fl = freqs[lane % n]
    is_row = (lane % GLA_DK) < GLA_DK // 2
    sign = np.where((lane % 32) < 16, -1.0, 1.0)
    rows = np.arange(L // GRID_W)[:, None] * fl[None]
    cols = np.arange(GRID_W)[:, None] * fl[None]
    rc = np.where(is_row, np.cos(rows), 0.0)
    rs = np.where(is_row, np.sin(rows) * sign, 0.0)
    cc = np.where(~is_row, np.cos(cols), 0.0)
    cs = np.where(~is_row, np.sin(cols) * sign, 0.0)
    return [jnp.asarray(t, f32) for t in (rc, rs, cc, cs)]


def _in_proj(x, mod3, norm1_g, w_t, gq_t, gk_t, bd, w_a, b_a, cast_ws):
    B, L, _ = x.shape
    tm = TM_PROJ
    rows_per = tm // GRID_W
    nstep = L // tm
    total = B * nstep

    def slab(w):
        R, C = w.shape
        if R % (total * BF16_SUBLANES) == 0:
            return pl.BlockSpec((R // total, C), lambda b, i: (b * nstep + i, 0))
        assert (2 * R) % (total * BF16_SUBLANES) == 0 and C % 256 == 0
        return pl.BlockSpec((2 * R // total, C // 2), lambda b, i: ((b * nstep + i) // 2, (b * nstep + i) % 2))
    rc, rs, cc, cs = _rope_tables(L)
    blk = np.kron(np.eye(256 // CHUNK), np.tril(np.ones((CHUNK, CHUNK)))).astype(np.float32)
    tl = jnp.asarray(blk, bf16)
    tok = lambda n: pl.BlockSpec((1, tm, n), lambda b, i: (b, i, 0))
    sds = lambda n, dt: jax.ShapeDtypeStruct((B, L, n), dt)
    return pl.pallas_call(
        _proj_kernel,
        grid=(B, L // tm),
        in_specs=[tok(D),
                  pl.BlockSpec((1, 1, D), lambda b, i: (b, 0, 0)),
                  pl.BlockSpec((1, 1, D), lambda b, i: (b, 0, 1)),
                  _const_spec((1, D)), _const_spec((D_IN, D)),
                  _const_spec((1, NA_W)), _const_spec((1, NA_W)), _const_spec((256, 256)),
                  pl.BlockSpec((rows_per, GQK), lambda b, i: (i, 0)),
                  pl.BlockSpec((rows_per, GQK), lambda b, i: (i, 0)),
                  _const_spec((GRID_W, GQK)), _const_spec((GRID_W, GQK)),
                  _const_spec((N_LR, 2 * GQK)), _const_spec((1, 2 * GQK)), _const_spec((256, 256))]
                 + [slab(w) for w in cast_ws],
        out_specs=[tok(NA_W), tok(NA_W), tok(NA_W)] + [tok(GQK)] * 6
                  + [tok(GV), tok(GV), tok(D), tok(D),
                     pl.BlockSpec((1, tm // CHUNK, 2 * GQK), lambda b, i: (b, i, 0))]
                  + [slab(w) for w in cast_ws],
        out_shape=[sds(NA_W, bf16)] * 3 + [sds(GQK, bf16)] * 6
                  + [sds(GV, bf16), sds(GV, bf16), sds(D, bf16), sds(D, bf16),
                     jax.ShapeDtypeStruct((B, L // CHUNK, 2 * GQK), f32)]
                  + [jax.ShapeDtypeStruct(w.shape, bf16) for w in cast_ws],
        compiler_params=pltpu.CompilerParams(
            dimension_semantics=("parallel", "parallel"), vmem_limit_bytes=VMEM_LIMIT),
        name="in_proj",
    )(x, mod3, mod3, norm1_g, w_t, gq_t, gk_t, bd, rc, rs, cc, cs, w_a, b_a, tl, *cast_ws)


def _lane_blocks(t):
    return [t[:, c:c + 128] for c in range(0, t.shape[1], 128)]


def _na_kernel(q_ref, k_ref, v_ref, kc_ref, vc_ref, t2_ref, o_ref, sl_ref, sc_ref, m_ref):
    i = pl.program_id(1)
    rows = k_ref.shape[1] // GRID_W
    lane_head = lax.broadcasted_iota(jnp.int32, (GRID_W, 256), 1) >> 6
    hmask = [lane_head == h for h in range(4)]

    def window_start(r):
        rs = jnp.clip(r - NA_WR // 2, 0, rows - NA_WR)
        return rs, pl.multiple_of(rs * GRID_W, GRID_W)

    def produce(rr, bank, j):
        r = i * NA_R + rr
        rs, start = window_start(r)
        d0 = rs - r + (NA_WR - 1)
        row0 = pl.multiple_of(rr * GRID_W, GRID_W)
        for g in range(2):
            ls = slice(256 * g, 256 * g + 256)
            q = q_ref[0, pl.ds(row0, GRID_W), ls]
            qs = jnp.concatenate([jnp.where(m, q, jnp.zeros_like(q)) for m in hmask], axis=0)
            kw = k_ref[0, pl.ds(start, NA_WR * GRID_W), ls]
            bias = jnp.concatenate(
                [jnp.concatenate([t2_ref[4 * g + h, d0 + 2 * jj] for jj in range(NA_WR // 2)], axis=1)
                 for h in range(4)], axis=0)
            s_loc = _dot_nt(qs, kw) + bias
            s_ctx = _dot_nt(qs, kc_ref[0, :, ls])
            mx = functools.reduce(jnp.maximum, _lane_blocks(s_loc) + _lane_blocks(s_ctx))
            sl_ref[bank, j, g] = s_loc
            sc_ref[bank, j, g] = s_ctx
            m_ref[bank, j, g] = jnp.broadcast_to(jnp.max(mx, axis=-1, keepdims=True), (4 * GRID_W, 128))

    def consume(rr, bank, j):
        _, start = window_start(i * NA_R + rr)
        row0 = pl.multiple_of(rr * GRID_W, GRID_W)
        outs = []
        for g in range(2):
            ls = slice(256 * g, 256 * g + 256)
            m = m_ref[bank, j, g]
            p_loc = [jnp.exp2(sl_ref[bank, j, g, :, c:c + 128] - m) for c in range(0, NA_WR * GRID_W, 128)]
            p_ctx = [jnp.exp2(sc_ref[bank, j, g, :, c:c + 128] - m) for c in range(0, CTX, 128)]
            den = jnp.sum(functools.reduce(jnp.add, p_loc + p_ctx), axis=-1, keepdims=True)
            vw = v_ref[0, pl.ds(start, NA_WR * GRID_W), ls]
            acc = (_dot(jnp.concatenate(p_loc, axis=1).astype(bf16), vw)
                   + _dot(jnp.concatenate(p_ctx, axis=1).astype(bf16), vc_ref[0, :, ls]))
            acc = acc * (1.0 / den)
            o = jnp.zeros((GRID_W, 256), f32)
            for h in range(4):
                o = o + jnp.where(hmask[h], acc[h * GRID_W:(h + 1) * GRID_W], 0.0)
            outs.append(o)
        o_ref[0, pl.ds(row0, GRID_W), :] = jnp.concatenate(outs, axis=1).astype(bf16)

    for j in range(NA_RI):
        produce(jnp.int32(j), 0, j)

    def step(t, carry):
        bank = t & 1
        for j in range(NA_RI):
            consume(t * NA_RI + j, bank, j)
        for j in range(NA_RI):
            produce(jnp.minimum((t + 1) * NA_RI + j, NA_R - 1), 1 - bank, j)
        return carry

    lax.fori_loop(0, NA_R // NA_RI, step, 0)


def _na_attn(q, k, v, kc, vc, t2):
    B, L, _ = q.shape
    nblk = L // (NA_R * GRID_W)
    return pl.pallas_call(
        _na_kernel,
        grid=(B, nblk),
        in_specs=[pl.BlockSpec((1, NA_R * GRID_W, NA_W), lambda b, i: (b, i, 0)),
                  pl.BlockSpec((1, L, NA_W), lambda b, i: (b, 0, 0), pipeline_mode=pl.Buffered(1)),
                  pl.BlockSpec((1, L, NA_W), lambda b, i: (b, 0, 0), pipeline_mode=pl.Buffered(1)),
                  pl.BlockSpec((1, CTX, NA_W), lambda b, i: (b, 0, 0)),
                  pl.BlockSpec((1, CTX, NA_W), lambda b, i: (b, 0, 0)),
                  _const_spec((NA_H, 2 * NA_WR - 1, GRID_W, 128))],
        out_specs=pl.BlockSpec((1, NA_R * GRID_W, NA_W), lambda b, i: (b, i, 0)),
        out_shape=jax.ShapeDtypeStruct((B, L, NA_W), bf16),
        scratch_shapes=[pltpu.VMEM((2, NA_RI, 2, 4 * GRID_W, NA_WR * GRID_W), f32),
                        pltpu.VMEM((2, NA_RI, 2, 4 * GRID_W, CTX), f32),
                        pltpu.VMEM((2, NA_RI, 2, 4 * GRID_W, 128), f32)],
        compiler_params=pltpu.CompilerParams(
            dimension_semantics=("parallel", "parallel"), vmem_limit_bytes=VMEM_LIMIT),
        name="na_attn",
    )(q, k, v, kc, vc, t2)


def _gla_chunk(q, ki, ke, v, dcol, s_ref, tri, hmask):
    qs = jnp.concatenate([jnp.where(m, q, jnp.zeros_like(q)) for m in hmask], axis=0)
    att = jnp.where(tri, _dot_nt(qs, ki), 0.0).astype(bf16)
    o = _dot(q, s_ref[...].astype(bf16))
    intra = [_dot(att[h * CHUNK:(h + 1) * CHUNK], v[:, h * GLA_DV:(h + 1) * GLA_DV]) for h in range(GLA_H)]
    o = o + jnp.concatenate(intra, axis=1)
    ket = ke.T
    for h in range(GLA_H):
        rs = slice(h * GLA_DK, (h + 1) * GLA_DK)
        cs = slice(h * GLA_DV, (h + 1) * GLA_DV)
        s_ref[rs, cs] = dcol[rs] * s_ref[rs, cs] + _dot(ket[rs], v[:, cs])
    return o


def _gla_kernel(qf_ref, kif_ref, kef_ref, vf_ref, decf_ref, qb_ref, kib_ref, keb_ref, vb_ref, decb_ref,
                s0f_ref, s0b_ref, of_ref, ob_ref, sf_ref, sb_ref):
    @pl.when(pl.program_id(1) == 0)
    def _():
        sf_ref[...] = s0f_ref[0]
        sb_ref[...] = s0b_ref[0]

    lane_head = lax.broadcasted_iota(jnp.int32, (CHUNK, GQK), 1) >> 6
    hmask = [lane_head == h for h in range(GLA_H)]
    ri = lax.broadcasted_iota(jnp.int32, (GLA_H * CHUNK, CHUNK), 0) & (CHUNK - 1)
    ci = lax.broadcasted_iota(jnp.int32, (GLA_H * CHUNK, CHUNK), 1)
    tri_f = ri >= ci
    tri_b = ri <= ci
    dect_f = decf_ref[0][:, :GQK].T
    dect_b = decb_ref[0][:, GQK:].T
    for c in range(GLA_CB):
        ts = slice(c * CHUNK, (c + 1) * CHUNK)
        of_ref[0, ts, :] = _gla_chunk(qf_ref[0, ts, :], kif_ref[0, ts, :], kef_ref[0, ts, :], vf_ref[0, ts, :],
                                      dect_f[:, c:c + 1], sf_ref, tri_f, hmask)
        cb = GLA_CB - 1 - c
        tb = slice(cb * CHUNK, (cb + 1) * CHUNK)
        ob_ref[0, tb, :] = _gla_chunk(qb_ref[0, tb, :], kib_ref[0, tb, :], keb_ref[0, tb, :], vb_ref[0, tb, :],
                                      dect_b[:, cb:cb + 1], sb_ref, tri_b, hmask)


def _gla_scan(qf, kif, kef, qb, kib, keb, vg, dec, s0f, s0b):
    B, L, _ = vg.shape
    tm = GLA_CB * CHUNK
    nb = L // tm
    fw = lambda n: pl.BlockSpec((1, tm, n), lambda b, i: (b, i, 0))
    bw = lambda n: pl.BlockSpec((1, tm, n), lambda b, i: (b, nb - 1 - i, 0))
    st = pl.BlockSpec((1, GQK, GV), lambda b, i: (b, 0, 0))
    return pl.pallas_call(
        _gla_kernel,
        grid=(B, nb),
        in_specs=[fw(GQK), fw(GQK), fw(GQK), fw(GV),
                  pl.BlockSpec((1, GLA_CB, 2 * GQK), lambda b, i: (b, i, 0)),
                  bw(GQK), bw(GQK), bw(GQK), bw(GV),
                  pl.BlockSpec((1, GLA_CB, 2 * GQK), lambda b, i: (b, nb - 1 - i, 0)),
                  st, st],
        out_specs=[fw(GV), bw(GV)],
        out_shape=[jax.ShapeDtypeStruct((B, L, GV), f32)] * 2,
        scratch_shapes=[pltpu.VMEM((GQK, GV), f32), pltpu.VMEM((GQK, GV), f32)],
        compiler_params=pltpu.CompilerParams(
            dimension_semantics=("parallel", "arbitrary"), vmem_limit_bytes=VMEM_LIMIT),
        name="gla_scan",
    )(qf, kif, kef, vg, dec, qb, kib, keb, vg, dec, s0f, s0b)


def _merge_ffn_kernel(x_ref, ona_ref, of_ref, ob_ref, og_ref, ma_ref, mb_ref,
                      g1_ref, sh2_ref, sc2_ref, g2_ref, n2_ref, gg_ref,
                      wna_ref, wgl_ref, wo_ref, wg_ref, wu_ref, wd_ref, o_ref):
    for hs in range(0, x_ref.shape[1], FFN_SUB):
        rows = slice(hs, hs + FFN_SUB)
        og = of_ref[0, rows] + ob_ref[0, rows]
        parts = []
        for h in range(GLA_H):
            seg = og[:, h * GLA_DV:(h + 1) * GLA_DV]
            ms = jnp.mean(seg * seg, axis=-1, keepdims=True)
            parts.append(seg * lax.rsqrt(ms + EPS) * gg_ref[...])
        ogl = (jnp.concatenate(parts, axis=1) * og_ref[0, rows].astype(f32)).astype(bf16)
        y = (ma_ref[0, rows].astype(f32) * _dot(ona_ref[0, rows], wna_ref[...])
             + mb_ref[0, rows].astype(f32) * _dot(ogl, wgl_ref[...]))
        x1 = x_ref[0, rows] + g1_ref[0] * _dot(y.astype(bf16), wo_ref[...])
        h2 = _norm_mod(x1, n2_ref[...], sh2_ref[0], sc2_ref[0]).astype(bf16)
        act = (_silu(_dot(h2, wg_ref[...])) * _dot(h2, wu_ref[...])).astype(bf16)
        o_ref[0, rows] = x1 + g2_ref[0] * _dot(act, wd_ref[...])


def _merge_ffn(x, ona, of, ob, og, ma, mb, mod3, norm2_g, gg, wna, wgl, wo, wg, wu, wd):
    B, L, _ = x.shape
    tm = TM_FFN
    tok = lambda n: pl.BlockSpec((1, tm, n), lambda b, i: (b, i, 0))
    modv = lambda j: pl.BlockSpec((1, 1, D), lambda b, i: (b, 0, j))
    return pl.pallas_call(
        _merge_ffn_kernel,
        grid=(B, L // tm),
        in_specs=[tok(D), tok(NA_W), tok(GV), tok(GV), tok(GV), tok(D), tok(D),
                  modv(2), modv(3), modv(4), modv(5),
                  _const_spec((1, D)), _const_spec((1, GLA_DV)),
                  _const_spec((NA_W, D)), _const_spec((GV, D)), _const_spec((D, D)),
                  _const_spec((D, D_FF)), _const_spec((D, D_FF)), _const_spec((D_FF, D))],
        out_specs=tok(D),
        out_shape=jax.ShapeDtypeStruct((B, L, D), f32),
        compiler_params=pltpu.CompilerParams(
            dimension_semantics=("parallel", "parallel"), vmem_limit_bytes=VMEM_LIMIT),
        name="merge_ffn",
    )(x, ona, of, ob, og, ma, mb, mod3, mod3, mod3, mod3, norm2_g, gg, wna, wgl, wo, wg, wu, wd)


def kernel(x, c, ctx, c_ctx, w_mod, b_mod, norm1_g, norm2_g, w_in, na_q_norm_g, na_k_norm_g, na_rpb,
           gla_w_alpha, gla_b_alpha, gla_norm_g, w_branch_na, w_branch_gla, w_out,
           w_ffn_gate, w_ffn_up, w_ffn_down):
    B, L, _ = x.shape
    assert w_mod.shape[0] == 1 and L % (NA_R * GRID_W) == 0 and L % TM_PROJ == 0

    w_t = _cast_bf16(jnp.transpose(w_in[0]), D_IN // 7)
    w_a = jnp.zeros((N_LR, 2 * GQK), f32)
    w_a = w_a.at[:GLA_RANK, :GQK].set(gla_w_alpha[0, 0]).at[GLA_RANK:, GQK:].set(gla_w_alpha[0, 1])
    w_a = w_a.astype(bf16)
    b_a = gla_b_alpha[0].reshape(1, 2 * GQK)
    gq_t = jnp.tile(na_q_norm_g[0] * (NA_DH ** -0.5 * LOG2E), NA_H).reshape(1, NA_W)
    gk_t = jnp.tile(na_k_norm_g[0], NA_H).reshape(1, NA_W)
    bd = jnp.asarray(np.kron(np.eye(4), np.full((NA_DH, NA_DH), 1.0 / NA_DH)), bf16)

    mod = _adaln_mod(c, c_ctx, w_mod[0], b_mod[0])
    mod3 = mod.reshape(8, 1, 6 * D)
    t2 = _rpb_expand(na_rpb[0])

    kc, vc, s0f, s0b = _ctx_side(ctx, mod3, norm1_g, w_t, w_a, b_a, gk_t, bd, B)
    later_ws = [w_branch_na[0], w_branch_gla[0], w_out[0], w_ffn_gate[0], w_ffn_up[0], w_ffn_down[0]]
    (qna, kna, vna, qf, kif, kef, qb, kib, keb, vg, og, ma, mb, dec, *later_bf) = _in_proj(
        x, mod3, norm1_g, w_t, gq_t, gk_t, bd, w_a, b_a, later_ws)
    ona = _na_attn(qna, kna, vna, kc, vc, t2)
    of, ob = _gla_scan(qf, kif, kef, qb, kib, keb, vg, dec, s0f, s0b)
    return _merge_ffn(x, ona, of, ob, og, ma, mb, mod3, norm2_g, gla_norm_g, *later_bf)
```

```python
import functools

import numpy as np
import jax
import jax.numpy as jnp
from jax import lax
from jax.experimental import pallas as pl
from jax.experimental.pallas import tpu as pltpu

D = 1024
GRID_W = 64
CTX = 256
NA_H = 8
NA_DH = 64
NA_WR = 8
NA_WC = 16
GLA_H = 4
GLA_DK = 64
GLA_DV = 128
GLA_RANK = 16
GLA_TAU = 16.0
GLA_LA_MIN = -1.0
CHUNK = 64
ROPE_THETA = 10000.0
D_FF = 2816
EPS = 1e-6
NEG_INF = -1e30
NA_W = NA_H * NA_DH
GQK = GLA_H * GLA_DK
GV = GLA_H * GLA_DV

OFF_QNA, OFF_KNA, OFF_VNA = 0, 512, 1024
OFF_QG, OFF_KG, OFF_VG, OFF_OG = 1536, 1792, 2048, 2560
OFF_LR, OFF_MA, OFF_MB = 3072, 3104, 4128
N_LR = 2 * GLA_RANK
N_PROJ_OUT = 5
NA_Q, NA_K, NA_V, W_NA = 0, 512, 1024, 1536
G_V, G_Q, G_KI, G_KE, W_GF = 0, 512, 768, 1024, 1280
W_GB = W_GF - GV
MG_A, MG_B, MG_OG, W_MG = 0, 1024, 2048, 2560
D_IN = 5152
LOG2E = 1.4426950408889634

BF16_SUBLANES = 16
VMEM_LIMIT = 56 * 1024 * 1024

TM_PROJ = 512
PROJ_SUB = 256
TM_FFN = 512
FFN_SUB = 512
NA_R = 32
NA_RI = 4
GLA_CB = 32

f32 = jnp.float32
bf16 = jnp.bfloat16


def _const_spec(shape):
    nd = len(shape)
    return pl.BlockSpec(shape, lambda *_: (0,) * nd, pipeline_mode=pl.Buffered(1))


def _dot(a, b):
    return jnp.dot(a, b, preferred_element_type=f32)


def _dot_nt(a, b):
    return lax.dot_general(a, b, (((1,), (1,)), ((), ())), preferred_element_type=f32)


def _split2(x):
    hi = x.astype(bf16)
    lo = (x - hi.astype(f32)).astype(bf16)
    return hi, lo


def _sigmoid(x):
    return 0.5 * jnp.tanh(0.5 * x) + 0.5


def _silu(x):
    return x * _sigmoid(x)


def _split3_dot(r, oh):
    h1 = r.astype(bf16)
    r2 = r - h1.astype(f32)
    h2 = r2.astype(bf16)
    h3 = (r2 - h2.astype(f32)).astype(bf16)
    return _dot(h1, oh) + _dot(h2, oh) + _dot(h3, oh)


def _rpb_expand_kernel(r0_ref, r1_ref, oh0_ref, oh1_ref, mask_ref, o_ref):
    gathered = _split3_dot(r0_ref[...], oh0_ref[...]) + _split3_dot(r1_ref[...], oh1_ref[...])
    o_ref[...] = gathered * LOG2E + mask_ref[...]


def _rpb_expand(rpb):
    nd = 2 * NA_WR - 1
    nrow = NA_H * nd
    ncol = 2 * NA_WC - 1
    nxt = jnp.concatenate([rpb[:, 1:], jnp.zeros_like(rpb[:, :1])], axis=1)
    r0 = jnp.zeros((128, 128), f32).at[:nrow, :ncol].set(rpb.reshape(nrow, ncol))
    r1 = jnp.zeros((128, 128), f32).at[:nrow, :ncol].set(nxt.reshape(nrow, ncol))
    q = np.arange(GRID_W)
    cs = np.clip(q - NA_WC // 2, 0, GRID_W - NA_WC)
    in_win = (q[None, :] >= cs[:, None]) & (q[None, :] < cs[:, None] + NA_WC)
    dc = np.clip(q[None, :] - q[:, None], -(NA_WC - 1), NA_WC - 1) + NA_WC - 1
    oh = np.zeros((2, 128, GRID_W, 128), np.float32)
    qq, kk = np.nonzero(in_win)
    oh[0, dc[qq, kk], qq, kk] = 1.0
    oh[1, dc[qq, kk], qq, kk + GRID_W] = 1.0
    oh = oh.reshape(2, 128, GRID_W * 128)
    mask = np.where(np.concatenate([in_win, in_win], axis=1), 0.0, NEG_INF).reshape(1, GRID_W * 128)
    out = pl.pallas_call(
        _rpb_expand_kernel,
        out_shape=jax.ShapeDtypeStruct((128, GRID_W * 128), f32),
        name="rpb_expand",
    )(r0, r1, jnp.asarray(oh[0], bf16), jnp.asarray(oh[1], bf16), jnp.asarray(mask, f32))
    return out[:nrow].reshape(NA_H, nd, GRID_W, 128)


def _cast_kernel(x_ref, o_ref):
    o_ref[...] = x_ref[...].astype(bf16)


def _cast_bf16(w, rows):
    R, C = w.shape
    return pl.pallas_call(
        _cast_kernel,
        grid=(R // rows,),
        in_specs=[pl.BlockSpec((rows, C), lambda j: (j, 0))],
        out_specs=pl.BlockSpec((rows, C), lambda j: (j, 0)),
        out_shape=jax.ShapeDtypeStruct((R, C), bf16),
        name="cast_bf16",
    )(w)


def _mod_kernel(a_ref, w_ref, b_ref, o_ref):
    a = _silu(a_ref[...]).astype(bf16)
    o_ref[...] = _dot(a, w_ref[...].astype(bf16)) + b_ref[...]


def _adaln_mod(c, c_ctx, w_mod, b_mod):
    B = c.shape[0]
    a = jnp.zeros((8, D), f32).at[:B].set(c).at[B].set(c_ctx)
    n = w_mod.shape[1]
    return pl.pallas_call(
        _mod_kernel,
        grid=(n // D,),
        in_specs=[pl.BlockSpec((8, D), lambda j: (0, 0)),
                  pl.BlockSpec((D, D), lambda j: (0, j)),
                  pl.BlockSpec((1, D), lambda j: (0, j))],
        out_specs=pl.BlockSpec((8, D), lambda j: (0, j)),
        out_shape=jax.ShapeDtypeStruct((8, n), f32),
        name="adaln_mod",
    )(a, w_mod, b_mod.reshape(1, n))


def _norm_mod(x, g, sh, sc):
    ms = jnp.mean(x * x, axis=-1, keepdims=True)
    return (x * lax.rsqrt(ms + EPS)) * (g * (1.0 + sc)) + sh


def _head_rms(t, bd, g):
    sq = (t * t).astype(bf16)
    ms = jnp.concatenate([_dot(sq[:, :256], bd), _dot(sq[:, 256:], bd)], axis=1)
    return t * lax.rsqrt(ms + EPS) * g


def _log_alpha(zl, wa_ref, ba_ref):
    z = _dot(zl.astype(bf16), wa_ref[...]) + ba_ref[...]
    soft = jnp.log2(1.0 + jnp.exp2(z * -LOG2E))
    return jnp.maximum(soft * (-1.0 / GLA_TAU), GLA_LA_MIN * LOG2E)


def _ctx_kernel(ctx_ref, sh_ref, sc_ref, g1_ref, wk_ref, wv_ref, wkg_ref, wvg_ref, wlr_ref,
                wa_ref, ba_ref, gk_ref, bd_ref, su_ref, sl_ref,
                kc_ref, vc_ref, sf_ref, sb_ref):
    h = _norm_mod(ctx_ref[0], g1_ref[...], sh_ref[0], sc_ref[0]).astype(bf16)
    k = _dot_nt(h, wk_ref[...])
    kc_ref[0] = _head_rms(k, bd_ref[...], gk_ref[...]).astype(bf16)
    vc_ref[0] = _dot_nt(h, wv_ref[...]).astype(bf16)
    kg = _dot_nt(h, wkg_ref[...])
    vg = _dot_nt(h, wvg_ref[...]).astype(bf16)
    la = _log_alpha(_dot_nt(h, wlr_ref[...]), wa_ref, ba_ref)
    hi, lo = _split2(la)
    ef = _dot(su_ref[...], hi[:, :GQK]) + _dot(su_ref[...], lo[:, :GQK])
    eb = _dot(sl_ref[...], hi[:, GQK:]) + _dot(sl_ref[...], lo[:, GQK:])
    r = lax.broadcasted_iota(jnp.int32, (GQK, GV), 0) >> 6
    cc = lax.broadcasted_iota(jnp.int32, (GQK, GV), 1) >> 7
    diag = r == cc
    for e, out in ((ef, sf_ref), (eb, sb_ref)):
        kw = (kg * jnp.exp2(e)).astype(bf16)
        u = _dot(kw.T, vg)
        out[0] = jnp.where(diag, u, 0.0)


def _ctx_side(ctx, mod3, norm1_g, w_t, w_a, b_a, gk_t, bd, n_b):
    su = jnp.asarray(np.triu(np.ones((CTX, CTX), np.float32), 1), bf16)
    sl = jnp.asarray(np.tril(np.ones((CTX, CTX), np.float32), -1), bf16)
    B = ctx.shape[0]
    cst = lambda shape, idx: pl.BlockSpec(shape, lambda b: idx)
    return pl.pallas_call(
        _ctx_kernel,
        grid=(B,),
        in_specs=[pl.BlockSpec((1, CTX, D), lambda b: (b, 0, 0)),
                  cst((1, 1, D), (n_b, 0, 0)), cst((1, 1, D), (n_b, 0, 1)),
                  cst((1, D), (0, 0)),
                  cst((NA_W, D), (OFF_KNA // NA_W, 0)), cst((NA_W, D), (OFF_VNA // NA_W, 0)),
                  cst((GQK, D), (OFF_KG // GQK, 0)), cst((GV, D), (OFF_VG // GV, 0)),
                  cst((N_LR, D), (OFF_LR // N_LR, 0)),
                  cst((N_LR, 2 * GQK), (0, 0)), cst((1, 2 * GQK), (0, 0)),
                  cst((1, NA_W), (0, 0)), cst((256, 256), (0, 0)),
                  cst((CTX, CTX), (0, 0)), cst((CTX, CTX), (0, 0))],
        out_specs=[pl.BlockSpec((1, CTX, NA_W), lambda b: (b, 0, 0)),
                   pl.BlockSpec((1, CTX, NA_W), lambda b: (b, 0, 0)),
                   pl.BlockSpec((1, GQK, GV), lambda b: (b, 0, 0)),
                   pl.BlockSpec((1, GQK, GV), lambda b: (b, 0, 0))],
        out_shape=[jax.ShapeDtypeStruct((B, CTX, NA_W), bf16),
                   jax.ShapeDtypeStruct((B, CTX, NA_W), bf16),
                   jax.ShapeDtypeStruct((B, GQK, GV), f32),
                   jax.ShapeDtypeStruct((B, GQK, GV), f32)],
        compiler_params=pltpu.CompilerParams(vmem_limit_bytes=VMEM_LIMIT),
        name="ctx_side",
    )(ctx, mod3, mod3, norm1_g, w_t, w_t, w_t, w_t, w_t, w_a, b_a, gk_t, bd, su, sl)


def _proj_kernel(x_ref, sh_ref, sc_ref, g1_ref, w_ref, gq_ref, gk_ref, bd_ref,
                 rc_ref, rs_ref, cc_ref, cs_ref, wa_ref, ba_ref, tl_ref, *refs):
    n_cast = (len(refs) - N_PROJ_OUT) // 2
    cast_in, refs = refs[:n_cast], refs[n_cast:]
    na_ref, gf_ref, gb_ref, mg_ref, dec_ref = refs[:N_PROJ_OUT]
    cast_out = refs[N_PROJ_OUT:]
    for src, dst in zip(cast_in, cast_out):
        dst[...] = src[...].astype(bf16)

    bd = bd_ref[...]
    tl = tl_ref[...]
    nrow = PROJ_SUB // GRID_W
    nchunk = PROJ_SUB // CHUNK
    lane = lax.broadcasted_iota(jnp.int32, (PROJ_SUB, GQK), 1)
    first_half = (lane & 31) < 16

    for hs in range(0, x_ref.shape[1], PROJ_SUB):
        rows = slice(hs, hs + PROJ_SUB)
        hb = _norm_mod(x_ref[0, rows], g1_ref[...], sh_ref[0], sc_ref[0]).astype(bf16)

        def proj(off, n):
            return _dot_nt(hb, w_ref[off:off + n, :])

        zl = proj(OFF_LR, N_LR)
        q_raw = proj(OFF_QNA, NA_W)
        la = _log_alpha(zl, wa_ref, ba_ref)
        k_raw = proj(OFF_KNA, NA_W)
        hi, lo = _split2(la)
        na_ref[0, rows, NA_Q:NA_Q + NA_W] = _head_rms(q_raw, bd, gq_ref[...]).astype(bf16)
        cum = jnp.concatenate(
            [_dot(tl, hi[s:s + 256]) + _dot(tl, lo[s:s + 256]) for s in range(0, PROJ_SUB, 256)], axis=0)
        na_ref[0, rows, NA_K:NA_K + NA_W] = _head_rms(k_raw, bd, gk_ref[...]).astype(bf16)
        qg_raw = proj(OFF_QG, GQK)
        kg_raw = proj(OFF_KG, GQK)
        na_ref[0, rows, NA_V:NA_V + NA_W] = proj(OFF_VNA, NA_W).astype(bf16)
        gf_ref[0, rows, G_V:G_V + GV] = proj(OFF_VG, GV).astype(bf16)

        gr = slice(hs // GRID_W, hs // GRID_W + nrow)
        cos_t = (rc_ref[gr][:, None, :] + cc_ref[...][None]).reshape(PROJ_SUB, GQK)
        sin_t = (rs_ref[gr][:, None, :] + cs_ref[...][None]).reshape(PROJ_SUB, GQK)

        def rope(t):
            partner = jnp.where(first_half, pltpu.roll(t, GQK - 16, 1), pltpu.roll(t, 16, 1))
            return t * cos_t + partner * sin_t

        qg = rope(qg_raw) * (GLA_DK ** -0.5)
        kg = rope(kg_raw)

        cum3 = cum.reshape(nchunk, CHUNK, 2 * GQK)
        tot3 = cum3[:, CHUNK - 1:CHUNK, :]
        la3 = la.reshape(nchunk, CHUNK, 2 * GQK)
        dec_ref[0, hs // CHUNK:hs // CHUNK + nchunk] = jnp.exp2(tot3).reshape(nchunk, 2 * GQK)
        b_f = cum3[:, :, :GQK].reshape(PROJ_SUB, GQK)
        e_f = (tot3 - cum3)[:, :, :GQK].reshape(PROJ_SUB, GQK)
        b_b = (tot3 - cum3 + la3)[:, :, GQK:].reshape(PROJ_SUB, GQK)
        e_b = (cum3 - la3)[:, :, GQK:].reshape(PROJ_SUB, GQK)
        mg_ref[0, rows, MG_OG:MG_OG + GV] = _silu(proj(OFF_OG, GV)).astype(bf16)
        gf_ref[0, rows, G_Q:G_Q + GQK] = (qg * jnp.exp2(b_f)).astype(bf16)
        gf_ref[0, rows, G_KI:G_KI + GQK] = (kg * jnp.exp2(-b_f)).astype(bf16)
        gf_ref[0, rows, G_KE:G_KE + GQK] = (kg * jnp.exp2(e_f)).astype(bf16)
        mg_ref[0, rows, MG_A:MG_A + D] = _sigmoid(proj(OFF_MA, D)).astype(bf16)
        gb_ref[0, rows, G_Q - GV:G_Q - GV + GQK] = (qg * jnp.exp2(b_b)).astype(bf16)
        gb_ref[0, rows, G_KI - GV:G_KI - GV + GQK] = (kg * jnp.exp2(-b_b)).astype(bf16)
        gb_ref[0, rows, G_KE - GV:G_KE - GV + GQK] = (kg * jnp.exp2(e_b)).astype(bf16)
        mg_ref[0, rows, MG_B:MG_B + D] = _sigmoid(proj(OFF_MB, D)).astype(bf16)


def _rope_tables(L):
    n = GLA_DK // 4
    freqs = ROPE_THETA ** (-np.arange(n, dtype=np.float64) / n)
    lane = np.arange(GQK)
    fl = freqs[lane % n]
    is_row = (lane % GLA_DK) < GLA_DK // 2
    sign = np.where((lane % 32) < 16, -1.0, 1.0)
    rows = np.arange(L // GRID_W)[:, None] * fl[None]
    cols = np.arange(GRID_W)[:, None] * fl[None]
    rc = np.where(is_row, np.cos(rows), 0.0)
    rs = np.where(is_row, np.sin(rows) * sign, 0.0)
    cc = np.where(~is_row, np.cos(cols), 0.0)
    cs = np.where(~is_row, np.sin(cols) * sign, 0.0)
    return [jnp.asarray(t, f32) for t in (rc, rs, cc, cs)]


def _in_proj(x, mod3, norm1_g, w_t, gq_t, gk_t, bd, w_a, b_a, cast_ws):
    B, L, _ = x.shape
    tm = TM_PROJ
    rows_per = tm // GRID_W
    nstep = L // tm
    total = B * nstep

    def slab(w):
        R, C = w.shape
        if R % (total * BF16_SUBLANES) == 0:
            return pl.BlockSpec((R // total, C), lambda b, i: (b * nstep + i, 0))
        assert (2 * R) % (total * BF16_SUBLANES) == 0 and C % 256 == 0
        return pl.BlockSpec((2 * R // total, C // 2), lambda b, i: ((b * nstep + i) // 2, (b * nstep + i) % 2))
    rc, rs, cc, cs = _rope_tables(L)
    blk = np.kron(np.eye(256 // CHUNK), np.tril(np.ones((CHUNK, CHUNK)))).astype(np.float32)
    tl = jnp.asarray(blk, bf16)
    tok = lambda n: pl.BlockSpec((1, tm, n), lambda b, i: (b, i, 0))
    sds = lambda n, dt: jax.ShapeDtypeStruct((B, L, n), dt)
    return pl.pallas_call(
        _proj_kernel,
        grid=(B, L // tm),
        in_specs=[tok(D),
                  pl.BlockSpec((1, 1, D), lambda b, i: (b, 0, 0)),
                  pl.BlockSpec((1, 1, D), lambda b, i: (b, 0, 1)),
                  _const_spec((1, D)), _const_spec((D_IN, D)),
                  _const_spec((1, NA_W)), _const_spec((1, NA_W)), _const_spec((256, 256)),
                  pl.BlockSpec((rows_per, GQK), lambda b, i: (i, 0)),
                  pl.BlockSpec((rows_per, GQK), lambda b, i: (i, 0)),
                  _const_spec((GRID_W, GQK)), _const_spec((GRID_W, GQK)),
                  _const_spec((N_LR, 2 * GQK)), _const_spec((1, 2 * GQK)), _const_spec((256, 256))]
                 + [slab(w) for w in cast_ws],
        out_specs=[tok(W_NA), tok(W_GF), tok(W_GB), tok(W_MG),
                   pl.BlockSpec((1, tm // CHUNK, 2 * GQK), lambda b, i: (b, i, 0))]
                  + [slab(w) for w in cast_ws],
        out_shape=[sds(W_NA, bf16), sds(W_GF, bf16), sds(W_GB, bf16), sds(W_MG, bf16),
                   jax.ShapeDtypeStruct((B, L // CHUNK, 2 * GQK), f32)]
                  + [jax.ShapeDtypeStruct(w.shape, bf16) for w in cast_ws],
        compiler_params=pltpu.CompilerParams(
            dimension_semantics=("parallel", "parallel"), vmem_limit_bytes=VMEM_LIMIT),
        name="in_proj",
    )(x, mod3, mod3, norm1_g, w_t, gq_t, gk_t, bd, rc, rs, cc, cs, w_a, b_a, tl, *cast_ws)


def _lane_blocks(t):
    return [t[:, c:c + 128] for c in range(0, t.shape[1], 128)]


def _na_kernel(q_ref, k_ref, v_ref, kc_ref, vc_ref, t2_ref, o_ref, sl_ref, sc_ref, m_ref):
    i = pl.program_id(1)
    rows = k_ref.shape[1] // GRID_W
    lane_head = lax.broadcasted_iota(jnp.int32, (GRID_W, 256), 1) >> 6
    hmask = [lane_head == h for h in range(4)]

    def window_start(r):
        rs = jnp.clip(r - NA_WR // 2, 0, rows - NA_WR)
        return rs, pl.multiple_of(rs * GRID_W, GRID_W)

    def produce(rr, bank, j):
        r = i * NA_R + rr
        rs, start = window_start(r)
        d0 = rs - r + (NA_WR - 1)
        row0 = pl.multiple_of(rr * GRID_W, GRID_W)
        for g in range(2):
            ls = slice(256 * g, 256 * g + 256)
            q = q_ref[0, pl.ds(row0, GRID_W), ls]
            qs = jnp.concatenate([jnp.where(m, q, jnp.zeros_like(q)) for m in hmask], axis=0)
            kw = k_ref[0, pl.ds(start, NA_WR * GRID_W), ls]
            bias = jnp.concatenate(
                [jnp.concatenate([t2_ref[4 * g + h, d0 + 2 * jj] for jj in range(NA_WR // 2)], axis=1)
                 for h in range(4)], axis=0)
            s_loc = _dot_nt(qs, kw) + bias
            s_ctx = _dot_nt(qs, kc_ref[0, :, ls])
            mx = functools.reduce(jnp.maximum, _lane_blocks(s_loc) + _lane_blocks(s_ctx))
            sl_ref[bank, j, g] = s_loc
            sc_ref[bank, j, g] = s_ctx
            m_ref[bank, j, g] = jnp.broadcast_to(jnp.max(mx, axis=-1, keepdims=True), (4 * GRID_W, 128))

    def consume(rr, bank, j):
        _, start = window_start(i * NA_R + rr)
        row0 = pl.multiple_of(rr * GRID_W, GRID_W)
        outs = []
        for g in range(2):
            ls = slice(256 * g, 256 * g + 256)
            m = m_ref[bank, j, g]
            p_loc = [jnp.exp2(sl_ref[bank, j, g, :, c:c + 128] - m) for c in range(0, NA_WR * GRID_W, 128)]
            p_ctx = [jnp.exp2(sc_ref[bank, j, g, :, c:c + 128] - m) for c in range(0, CTX, 128)]
            den = jnp.sum(functools.reduce(jnp.add, p_loc + p_ctx), axis=-1, keepdims=True)
            vw = v_ref[0, pl.ds(start, NA_WR * GRID_W), ls]
            acc = (_dot(jnp.concatenate(p_loc, axis=1).astype(bf16), vw)
                   + _dot(jnp.concatenate(p_ctx, axis=1).astype(bf16), vc_ref[0, :, ls]))
            acc = acc * (1.0 / den)
            o = jnp.zeros((GRID_W, 256), f32)
            for h in range(4):
                o = o + jnp.where(hmask[h], acc[h * GRID_W:(h + 1) * GRID_W], 0.0)
            outs.append(o)
        o_ref[0, pl.ds(row0, GRID_W), :] = jnp.concatenate(outs, axis=1).astype(bf16)

    for j in range(NA_RI):
        produce(jnp.int32(j), 0, j)

    def step(t, carry):
        bank = t & 1
        for j in range(NA_RI):
            consume(t * NA_RI + j, bank, j)
        for j in range(NA_RI):
            produce(jnp.minimum((t + 1) * NA_RI + j, NA_R - 1), 1 - bank, j)
        return carry

    lax.fori_loop(0, NA_R // NA_RI, step, 0)


def _na_attn(qkv, kc, vc, t2):
    B, L, _ = qkv.shape
    nblk = L // (NA_R * GRID_W)
    return pl.pallas_call(
        _na_kernel,
        grid=(B, nblk),
        in_specs=[pl.BlockSpec((1, NA_R * GRID_W, NA_W), lambda b, i: (b, i, NA_Q // NA_W)),
                  pl.BlockSpec((1, L, NA_W), lambda b, i: (b, 0, NA_K // NA_W), pipeline_mode=pl.Buffered(1)),
                  pl.BlockSpec((1, L, NA_W), lambda b, i: (b, 0, NA_V // NA_W), pipeline_mode=pl.Buffered(1)),
                  pl.BlockSpec((1, CTX, NA_W), lambda b, i: (b, 0, 0)),
                  pl.BlockSpec((1, CTX, NA_W), lambda b, i: (b, 0, 0)),
                  _const_spec((NA_H, 2 * NA_WR - 1, GRID_W, 128))],
        out_specs=pl.BlockSpec((1, NA_R * GRID_W, NA_W), lambda b, i: (b, i, 0)),
        out_shape=jax.ShapeDtypeStruct((B, L, NA_W), bf16),
        scratch_shapes=[pltpu.VMEM((2, NA_RI, 2, 4 * GRID_W, NA_WR * GRID_W), f32),
                        pltpu.VMEM((2, NA_RI, 2, 4 * GRID_W, CTX), f32),
                        pltpu.VMEM((2, NA_RI, 2, 4 * GRID_W, 128), f32)],
        compiler_params=pltpu.CompilerParams(
            dimension_semantics=("parallel", "parallel"), vmem_limit_bytes=VMEM_LIMIT),
        name="na_attn",
    )(qkv, qkv, qkv, kc, vc, t2)


def _gla_chunk(q, ki, ke, v, dcol, s_ref, tri, hmask):
    qs = jnp.concatenate([jnp.where(m, q, jnp.zeros_like(q)) for m in hmask], axis=0)
    att = jnp.where(tri, _dot_nt(qs, ki), 0.0).astype(bf16)
    o = _dot(q, s_ref[...].astype(bf16))
    intra = [_dot(att[h * CHUNK:(h + 1) * CHUNK], v[:, h * GLA_DV:(h + 1) * GLA_DV]) for h in range(GLA_H)]
    o = o + jnp.concatenate(intra, axis=1)
    ket = ke.T
    for h in range(GLA_H):
        rs = slice(h * GLA_DK, (h + 1) * GLA_DK)
        cs = slice(h * GLA_DV, (h + 1) * GLA_DV)
        s_ref[rs, cs] = dcol[rs] * s_ref[rs, cs] + _dot(ket[rs], v[:, cs])
    return o


def _gla_kernel(gf_ref, decf_ref, gb_ref, vb_ref, decb_ref,
                s0f_ref, s0b_ref, of_ref, ob_ref, sf_ref, sb_ref):
    @pl.when(pl.program_id(1) == 0)
    def _():
        sf_ref[...] = s0f_ref[0]
        sb_ref[...] = s0b_ref[0]

    lane_head = lax.broadcasted_iota(jnp.int32, (CHUNK, GQK), 1) >> 6
    hmask = [lane_head == h for h in range(GLA_H)]
    ri = lax.broadcasted_iota(jnp.int32, (GLA_H * CHUNK, CHUNK), 0) & (CHUNK - 1)
    ci = lax.broadcasted_iota(jnp.int32, (GLA_H * CHUNK, CHUNK), 1)
    tri_f = ri >= ci
    tri_b = ri <= ci
    dect_f = decf_ref[0][:, :GQK].T
    dect_b = decb_ref[0][:, GQK:].T
    for c in range(GLA_CB):
        ts = slice(c * CHUNK, (c + 1) * CHUNK)
        of_ref[0, ts, :] = _gla_chunk(gf_ref[0, ts, G_Q:G_Q + GQK], gf_ref[0, ts, G_KI:G_KI + GQK],
                                      gf_ref[0, ts, G_KE:G_KE + GQK], gf_ref[0, ts, G_V:G_V + GV],
                                      dect_f[:, c:c + 1], sf_ref, tri_f, hmask)
        cb = GLA_CB - 1 - c
        tb = slice(cb * CHUNK, (cb + 1) * CHUNK)
        ob_ref[0, tb, :] = _gla_chunk(gb_ref[0, tb, G_Q - GV:G_Q - GV + GQK], gb_ref[0, tb, G_KI - GV:G_KI - GV + GQK],
                                      gb_ref[0, tb, G_KE - GV:G_KE - GV + GQK], vb_ref[0, tb, :],
                                      dect_b[:, cb:cb + 1], sb_ref, tri_b, hmask)


def _gla_scan(gf, gb, dec, s0f, s0b):
    B, L, _ = gf.shape
    tm = GLA_CB * CHUNK
    nb = L // tm
    fw = lambda n: pl.BlockSpec((1, tm, n), lambda b, i: (b, i, 0))
    bw = lambda n: pl.BlockSpec((1, tm, n), lambda b, i: (b, nb - 1 - i, 0))
    st = pl.BlockSpec((1, GQK, GV), lambda b, i: (b, 0, 0))
    return pl.pallas_call(
        _gla_kernel,
        grid=(B, nb),
        in_specs=[fw(W_GF),
                  pl.BlockSpec((1, GLA_CB, 2 * GQK), lambda b, i: (b, i, 0)),
                  bw(W_GB), bw(GV),
                  pl.BlockSpec((1, GLA_CB, 2 * GQK), lambda b, i: (b, nb - 1 - i, 0)),
                  st, st],
        out_specs=[fw(GV), bw(GV)],
        out_shape=[jax.ShapeDtypeStruct((B, L, GV), f32)] * 2,
        scratch_shapes=[pltpu.VMEM((GQK, GV), f32), pltpu.VMEM((GQK, GV), f32)],
        compiler_params=pltpu.CompilerParams(
            dimension_semantics=("parallel", "arbitrary"), vmem_limit_bytes=VMEM_LIMIT),
        name="gla_scan",
    )(gf, dec, gb, gf, dec, s0f, s0b)


def _merge_ffn_kernel(x_ref, ona_ref, of_ref, ob_ref, mg_ref,
                      g1_ref, sh2_ref, sc2_ref, g2_ref, n2_ref, gg_ref,
                      wna_ref, wgl_ref, wo_ref, wg_ref, wu_ref, wd_ref, o_ref):
    for hs in range(0, x_ref.shape[1], FFN_SUB):
        rows = slice(hs, hs + FFN_SUB)
        og = of_ref[0, rows] + ob_ref[0, rows]
        parts = []
        for h in range(GLA_H):
            seg = og[:, h * GLA_DV:(h + 1) * GLA_DV]
            ms = jnp.mean(seg * seg, axis=-1, keepdims=True)
            parts.append(seg * lax.rsqrt(ms + EPS) * gg_ref[...])
        ogl = (jnp.concatenate(parts, axis=1) * mg_ref[0, rows, MG_OG:MG_OG + GV].astype(f32)).astype(bf16)
        y = (mg_ref[0, rows, MG_A:MG_A + D].astype(f32) * _dot(ona_ref[0, rows], wna_ref[...])
             + mg_ref[0, rows, MG_B:MG_B + D].astype(f32) * _dot(ogl, wgl_ref[...]))
        x1 = x_ref[0, rows] + g1_ref[0] * _dot(y.astype(bf16), wo_ref[...])
        h2 = _norm_mod(x1, n2_ref[...], sh2_ref[0], sc2_ref[0]).astype(bf16)
        act = (_silu(_dot(h2, wg_ref[...])) * _dot(h2, wu_ref[...])).astype(bf16)
        o_ref[0, rows] = x1 + g2_ref[0] * _dot(act, wd_ref[...])


def _merge_ffn(x, ona, of, ob, mg, mod3, norm2_g, gg, wna, wgl, wo, wg, wu, wd):
    B, L, _ = x.shape
    tm = TM_FFN
    tok = lambda n: pl.BlockSpec((1, tm, n), lambda b, i: (b, i, 0))
    modv = lambda j: pl.BlockSpec((1, 1, D), lambda b, i: (b, 0, j))
    return pl.pallas_call(
        _merge_ffn_kernel,
        grid=(B, L // tm),
        in_specs=[tok(D), tok(NA_W), tok(GV), tok(GV), tok(W_MG),
                  modv(2), modv(3), modv(4), modv(5),
                  _const_spec((1, D)), _const_spec((1, GLA_DV)),
                  _const_spec((NA_W, D)), _const_spec((GV, D)), _const_spec((D, D)),
                  _const_spec((D, D_FF)), _const_spec((D, D_FF)), _const_spec((D_FF, D))],
        out_specs=tok(D),
        out_shape=jax.ShapeDtypeStruct((B, L, D), f32),
        compiler_params=pltpu.CompilerParams(
            dimension_semantics=("parallel", "parallel"), vmem_limit_bytes=VMEM_LIMIT),
        name="merge_ffn",
    )(x, ona, of, ob, mg, mod3, mod3, mod3, mod3, norm2_g, gg, wna, wgl, wo, wg, wu, wd)


def kernel(x, c, ctx, c_ctx, w_mod, b_mod, norm1_g, norm2_g, w_in, na_q_norm_g, na_k_norm_g, na_rpb,
           gla_w_alpha, gla_b_alpha, gla_norm_g, w_branch_na, w_branch_gla, w_out,
           w_ffn_gate, w_ffn_up, w_ffn_down):
    B, L, _ = x.shape
    assert w_mod.shape[0] == 1 and L % (NA_R * GRID_W) == 0 and L % TM_PROJ == 0

    w_t = _cast_bf16(jnp.transpose(w_in[0]), D_IN // 7)
    w_a = jnp.zeros((N_LR, 2 * GQK), f32)
    w_a = w_a.at[:GLA_RANK, :GQK].set(gla_w_alpha[0, 0]).at[GLA_RANK:, GQK:].set(gla_w_alpha[0, 1])
    w_a = w_a.astype(bf16)
    b_a = gla_b_alpha[0].reshape(1, 2 * GQK)
    gq_t = jnp.tile(na_q_norm_g[0] * (NA_DH ** -0.5 * LOG2E), NA_H).reshape(1, NA_W)
    gk_t = jnp.tile(na_k_norm_g[0], NA_H).reshape(1, NA_W)
    bd = jnp.asarray(np.kron(np.eye(4), np.full((NA_DH, NA_DH), 1.0 / NA_DH)), bf16)

    mod = _adaln_mod(c, c_ctx, w_mod[0], b_mod[0])
    mod3 = mod.reshape(8, 1, 6 * D)
    t2 = _rpb_expand(na_rpb[0])

    kc, vc, s0f, s0b = _ctx_side(ctx, mod3, norm1_g, w_t, w_a, b_a, gk_t, bd, B)
    later_ws = [w_branch_na[0], w_branch_gla[0], w_out[0], w_ffn_gate[0], w_ffn_up[0], w_ffn_down[0]]
    na_qkv, gla_f, gla_b, gates, dec, *later_bf = _in_proj(
        x, mod3, norm1_g, w_t, gq_t, gk_t, bd, w_a, b_a, later_ws)
    ona = _na_attn(na_qkv, kc, vc, t2)
    of, ob = _gla_scan(gla_f, gla_b, dec, s0f, s0b)
    return _merge_ffn(x, ona, of, ob, gates, mod3, norm2_g, gla_norm_g, *later_bf)
```

```python
import functools

import numpy as np
import jax
import jax.numpy as jnp
from jax import lax
from jax.experimental import pallas as pl
from jax.experimental.pallas import tpu as pltpu

D = 1024
GRID_W = 64
CTX = 256
NA_H = 8
NA_DH = 64
NA_WR = 8
NA_WC = 16
GLA_H = 4
GLA_DK = 64
GLA_DV = 128
GLA_RANK = 16
GLA_TAU = 16.0
GLA_LA_MIN = -1.0
CHUNK = 64
ROPE_THETA = 10000.0
D_FF = 2816
EPS = 1e-6
NEG_INF = -1e30
NA_W = NA_H * NA_DH
GQK = GLA_H * GLA_DK
GV = GLA_H * GLA_DV

OFF_QNA, OFF_KNA, OFF_VNA = 0, 512, 1024
OFF_QG, OFF_KG, OFF_VG, OFF_OG = 1536, 1792, 2048, 2560
OFF_LR, OFF_MA, OFF_MB = 3072, 3104, 4128
N_LR = 2 * GLA_RANK
N_PROJ_OUT = 5
NA_Q, NA_K, NA_V, W_NA = 0, 512, 1024, 1536
G_V, G_Q, G_KI, G_KE, W_GF = 0, 512, 768, 1024, 1280
W_GB = W_GF - GV
MG_A, MG_B, MG_OG, W_MG = 0, 1024, 2048, 2560
D_IN = 5152
LOG2E = 1.4426950408889634

BF16_SUBLANES = 16
VMEM_LIMIT = 56 * 1024 * 1024

TM_PROJ = 512
PROJ_SUB = 256
TM_FFN = 512
FFN_SUB = 512
NA_R = 32
NA_RI = 4
GLA_CB = 32

f32 = jnp.float32
bf16 = jnp.bfloat16


def _const_spec(shape):
    nd = len(shape)
    return pl.BlockSpec(shape, lambda *_: (0,) * nd, pipeline_mode=pl.Buffered(1))


def _dot(a, b):
    return jnp.dot(a, b, preferred_element_type=f32)


def _dot_nt(a, b):
    return lax.dot_general(a, b, (((1,), (1,)), ((), ())), preferred_element_type=f32)


def _split2(x):
    hi = x.astype(bf16)
    lo = (x - hi.astype(f32)).astype(bf16)
    return hi, lo


def _sigmoid(x):
    return 0.5 * jnp.tanh(0.5 * x) + 0.5


def _silu(x):
    return x * _sigmoid(x)


def _split3_dot(r, oh):
    h1 = r.astype(bf16)
    r2 = r - h1.astype(f32)
    h2 = r2.astype(bf16)
    h3 = (r2 - h2.astype(f32)).astype(bf16)
    return _dot(h1, oh) + _dot(h2, oh) + _dot(h3, oh)


def _rpb_expand_kernel(r0_ref, r1_ref, oh0_ref, oh1_ref, mask_ref, o_ref):
    gathered = _split3_dot(r0_ref[...], oh0_ref[...]) + _split3_dot(r1_ref[...], oh1_ref[...])
    o_ref[...] = gathered * LOG2E + mask_ref[...]


def _rpb_expand(rpb):
    nd = 2 * NA_WR - 1
    nrow = NA_H * nd
    ncol = 2 * NA_WC - 1
    nxt = jnp.concatenate([rpb[:, 1:], jnp.zeros_like(rpb[:, :1])], axis=1)
    r0 = jnp.zeros((128, 128), f32).at[:nrow, :ncol].set(rpb.reshape(nrow, ncol))
    r1 = jnp.zeros((128, 128), f32).at[:nrow, :ncol].set(nxt.reshape(nrow, ncol))
    q = np.arange(GRID_W)
    cs = np.clip(q - NA_WC // 2, 0, GRID_W - NA_WC)
    in_win = (q[None, :] >= cs[:, None]) & (q[None, :] < cs[:, None] + NA_WC)
    dc = np.clip(q[None, :] - q[:, None], -(NA_WC - 1), NA_WC - 1) + NA_WC - 1
    oh = np.zeros((2, 128, GRID_W, 128), np.float32)
    qq, kk = np.nonzero(in_win)
    oh[0, dc[qq, kk], qq, kk] = 1.0
    oh[1, dc[qq, kk], qq, kk + GRID_W] = 1.0
    oh = oh.reshape(2, 128, GRID_W * 128)
    mask = np.where(np.concatenate([in_win, in_win], axis=1), 0.0, NEG_INF).reshape(1, GRID_W * 128)
    out = pl.pallas_call(
        _rpb_expand_kernel,
        out_shape=jax.ShapeDtypeStruct((128, GRID_W * 128), f32),
        name="rpb_expand",
    )(r0, r1, jnp.asarray(oh[0], bf16), jnp.asarray(oh[1], bf16), jnp.asarray(mask, f32))
    return out[:nrow].reshape(NA_H, nd, GRID_W, 128)


def _cast_kernel(x_ref, o_ref):
    o_ref[...] = x_ref[...].astype(bf16)


def _cast_bf16(w, rows):
    R, C = w.shape
    return pl.pallas_call(
        _cast_kernel,
        grid=(R // rows,),
        in_specs=[pl.BlockSpec((rows, C), lambda j: (j, 0))],
        out_specs=pl.BlockSpec((rows, C), lambda j: (j, 0)),
        out_shape=jax.ShapeDtypeStruct((R, C), bf16),
        name="cast_bf16",
    )(w)


def _mod_kernel(a_ref, w_ref, b_ref, o_ref):
    a = _silu(a_ref[...]).astype(bf16)
    o_ref[...] = _dot(a, w_ref[...].astype(bf16)) + b_ref[...]


def _adaln_mod(c, c_ctx, w_mod, b_mod):
    B = c.shape[0]
    a = jnp.zeros((8, D), f32).at[:B].set(c).at[B].set(c_ctx)
    n = w_mod.shape[1]
    return pl.pallas_call(
        _mod_kernel,
        grid=(n // D,),
        in_specs=[pl.BlockSpec((8, D), lambda j: (0, 0)),
                  pl.BlockSpec((D, D), lambda j: (0, j)),
                  pl.BlockSpec((1, D), lambda j: (0, j))],
        out_specs=pl.BlockSpec((8, D), lambda j: (0, j)),
        out_shape=jax.ShapeDtypeStruct((8, n), f32),
        name="adaln_mod",
    )(a, w_mod, b_mod.reshape(1, n))


def _norm_mod(x, g, sh, sc):
    ms = jnp.mean(x * x, axis=-1, keepdims=True)
    return (x * lax.rsqrt(ms + EPS)) * (g * (1.0 + sc)) + sh


def _head_rms(t, bd, g):
    sq = (t * t).astype(bf16)
    ms = jnp.concatenate([_dot(sq[:, :256], bd), _dot(sq[:, 256:], bd)], axis=1)
    return t * lax.rsqrt(ms + EPS) * g


def _log_alpha(zl, wa_ref, ba_ref):
    z = _dot(zl.astype(bf16), wa_ref[...]) + ba_ref[...]
    soft = jnp.log2(1.0 + jnp.exp2(z * -LOG2E))
    return jnp.maximum(soft * (-1.0 / GLA_TAU), GLA_LA_MIN * LOG2E)


def _ctx_kernel(ctx_ref, sh_ref, sc_ref, g1_ref, wk_ref, wv_ref, wkg_ref, wvg_ref, wlr_ref,
                wa_ref, ba_ref, gk_ref, bd_ref, su_ref, sl_ref,
                kc_ref, vc_ref, sf_ref, sb_ref):
    h = _norm_mod(ctx_ref[0], g1_ref[...], sh_ref[0], sc_ref[0]).astype(bf16)
    k = _dot_nt(h, wk_ref[...])
    kc_ref[0] = _head_rms(k, bd_ref[...], gk_ref[...]).astype(bf16)
    vc_ref[0] = _dot_nt(h, wv_ref[...]).astype(bf16)
    kg = _dot_nt(h, wkg_ref[...])
    vg = _dot_nt(h, wvg_ref[...]).astype(bf16)
    la = _log_alpha(_dot_nt(h, wlr_ref[...]), wa_ref, ba_ref)
    hi, lo = _split2(la)
    ef = _dot(su_ref[...], hi[:, :GQK]) + _dot(su_ref[...], lo[:, :GQK])
    eb = _dot(sl_ref[...], hi[:, GQK:]) + _dot(sl_ref[...], lo[:, GQK:])
    r = lax.broadcasted_iota(jnp.int32, (GQK, GV), 0) >> 6
    cc = lax.broadcasted_iota(jnp.int32, (GQK, GV), 1) >> 7
    diag = r == cc
    for e, out in ((ef, sf_ref), (eb, sb_ref)):
        kw = (kg * jnp.exp2(e)).astype(bf16)
        u = _dot(kw.T, vg)
        out[0] = jnp.where(diag, u, 0.0)


def _ctx_side(ctx, mod3, norm1_g, w_t, w_a, b_a, gk_t, bd, n_b):
    su = jnp.asarray(np.triu(np.ones((CTX, CTX), np.float32), 1), bf16)
    sl = jnp.asarray(np.tril(np.ones((CTX, CTX), np.float32), -1), bf16)
    B = ctx.shape[0]
    cst = lambda shape, idx: pl.BlockSpec(shape, lambda b: idx)
    return pl.pallas_call(
        _ctx_kernel,
        grid=(B,),
        in_specs=[pl.BlockSpec((1, CTX, D), lambda b: (b, 0, 0)),
                  cst((1, 1, D), (n_b, 0, 0)), cst((1, 1, D), (n_b, 0, 1)),
                  cst((1, D), (0, 0)),
                  cst((NA_W, D), (OFF_KNA // NA_W, 0)), cst((NA_W, D), (OFF_VNA // NA_W, 0)),
                  cst((GQK, D), (OFF_KG // GQK, 0)), cst((GV, D), (OFF_VG // GV, 0)),
                  cst((N_LR, D), (OFF_LR // N_LR, 0)),
                  cst((N_LR, 2 * GQK), (0, 0)), cst((1, 2 * GQK), (0, 0)),
                  cst((1, NA_W), (0, 0)), cst((256, 256), (0, 0)),
                  cst((CTX, CTX), (0, 0)), cst((CTX, CTX), (0, 0))],
        out_specs=[pl.BlockSpec((1, CTX, NA_W), lambda b: (b, 0, 0)),
                   pl.BlockSpec((1, CTX, NA_W), lambda b: (b, 0, 0)),
                   pl.BlockSpec((1, GQK, GV), lambda b: (b, 0, 0)),
                   pl.BlockSpec((1, GQK, GV), lambda b: (b, 0, 0))],
        out_shape=[jax.ShapeDtypeStruct((B, CTX, NA_W), bf16),
                   jax.ShapeDtypeStruct((B, CTX, NA_W), bf16),
                   jax.ShapeDtypeStruct((B, GQK, GV), f32),
                   jax.ShapeDtypeStruct((B, GQK, GV), f32)],
        compiler_params=pltpu.CompilerParams(vmem_limit_bytes=VMEM_LIMIT),
        name="ctx_side",
    )(ctx, mod3, mod3, norm1_g, w_t, w_t, w_t, w_t, w_t, w_a, b_a, gk_t, bd, su, sl)


def _proj_kernel(x_ref, sh_ref, sc_ref, g1_ref, w_ref, gq_ref, gk_ref, bd_ref,
                 rc_ref, rs_ref, cc_ref, cs_ref, wa_ref, ba_ref, tl_ref, *refs):
    n_cast = (len(refs) - N_PROJ_OUT) // 2
    cast_in, refs = refs[:n_cast], refs[n_cast:]
    na_ref, gf_ref, gb_ref, mg_ref, dec_ref = refs[:N_PROJ_OUT]
    cast_out = refs[N_PROJ_OUT:]
    for src, dst in zip(cast_in, cast_out):
        dst[...] = src[...].astype(bf16)

    bd = bd_ref[...]
    tl = tl_ref[...]
    nrow = PROJ_SUB // GRID_W
    nchunk = PROJ_SUB // CHUNK
    lane = lax.broadcasted_iota(jnp.int32, (PROJ_SUB, GQK), 1)
    first_half = (lane & 31) < 16

    for hs in range(0, x_ref.shape[1], PROJ_SUB):
        rows = slice(hs, hs + PROJ_SUB)
        hb = _norm_mod(x_ref[0, rows], g1_ref[...], sh_ref[0], sc_ref[0]).astype(bf16)

        def proj(off, n):
            return _dot_nt(hb, w_ref[off:off + n, :])

        zl = proj(OFF_LR, N_LR)
        q_raw = proj(OFF_QNA, NA_W)
        la = _log_alpha(zl, wa_ref, ba_ref)
        k_raw = proj(OFF_KNA, NA_W)
        hi, lo = _split2(la)
        na_ref[0, rows, NA_Q:NA_Q + NA_W] = _head_rms(q_raw, bd, gq_ref[...]).astype(bf16)
        cum = jnp.concatenate(
            [_dot(tl, hi[s:s + 256]) + _dot(tl, lo[s:s + 256]) for s in range(0, PROJ_SUB, 256)], axis=0)
        na_ref[0, rows, NA_K:NA_K + NA_W] = _head_rms(k_raw, bd, gk_ref[...]).astype(bf16)
        qg_raw = proj(OFF_QG, GQK)
        kg_raw = proj(OFF_KG, GQK)
        na_ref[0, rows, NA_V:NA_V + NA_W] = proj(OFF_VNA, NA_W).astype(bf16)
        gf_ref[0, rows, G_V:G_V + GV] = proj(OFF_VG, GV).astype(bf16)

        gr = slice(hs // GRID_W, hs // GRID_W + nrow)
        cos_t = (rc_ref[gr][:, None, :] + cc_ref[...][None]).reshape(PROJ_SUB, GQK)
        sin_t = (rs_ref[gr][:, None, :] + cs_ref[...][None]).reshape(PROJ_SUB, GQK)

        def rope(t):
            partner = jnp.where(first_half, pltpu.roll(t, GQK - 16, 1), pltpu.roll(t, 16, 1))
            return t * cos_t + partner * sin_t

        qg = rope(qg_raw) * (GLA_DK ** -0.5)
        kg = rope(kg_raw)

        cum3 = cum.reshape(nchunk, CHUNK, 2 * GQK)
        tot3 = cum3[:, CHUNK - 1:CHUNK, :]
        la3 = la.reshape(nchunk, CHUNK, 2 * GQK)
        dec_ref[0, hs // CHUNK:hs // CHUNK + nchunk] = jnp.exp2(tot3).reshape(nchunk, 2 * GQK)
        b_f = cum3[:, :, :GQK].reshape(PROJ_SUB, GQK)
        e_f = (tot3 - cum3)[:, :, :GQK].reshape(PROJ_SUB, GQK)
        b_b = (tot3 - cum3 + la3)[:, :, GQK:].reshape(PROJ_SUB, GQK)
        e_b = (cum3 - la3)[:, :, GQK:].reshape(PROJ_SUB, GQK)
        mg_ref[0, rows, MG_OG:MG_OG + GV] = _silu(proj(OFF_OG, GV)).astype(bf16)
        gf_ref[0, rows, G_Q:G_Q + GQK] = (qg * jnp.exp2(b_f)).astype(bf16)
        gf_ref[0, rows, G_KI:G_KI + GQK] = (kg * jnp.exp2(-b_f)).astype(bf16)
        gf_ref[0, rows, G_KE:G_KE + GQK] = (kg * jnp.exp2(e_f)).astype(bf16)
        mg_ref[0, rows, MG_A:MG_A + D] = _sigmoid(proj(OFF_MA, D)).astype(bf16)
        gb_ref[0, rows, G_Q - GV:G_Q - GV + GQK] = (qg * jnp.exp2(b_b)).astype(bf16)
        gb_ref[0, rows, G_KI - GV:G_KI - GV + GQK] = (kg * jnp.exp2(-b_b)).astype(bf16)
        gb_ref[0, rows, G_KE - GV:G_KE - GV + GQK] = (kg * jnp.exp2(e_b)).astype(bf16)
        mg_ref[0, rows, MG_B:MG_B + D] = _sigmoid(proj(OFF_MB, D)).astype(bf16)


def _rope_tables(L):
    n = GLA_DK // 4
    freqs = ROPE_THETA ** (-np.arange(n, dtype=np.float64) / n)
    lane = np.arange(GQK)
    fl = freqs[lane % n]
    is_row = (lane % GLA_DK) < GLA_DK // 2
    sign = np.where((lane % 32) < 16, -1.0, 1.0)
    rows = np.arange(L // GRID_W)[:, None] * fl[None]
    cols = np.arange(GRID_W)[:, None] * fl[None]
    rc = np.where(is_row, np.cos(rows), 0.0)
    rs = np.where(is_row, np.sin(rows) * sign, 0.0)
    cc = np.where(~is_row, np.cos(cols), 0.0)
    cs = np.where(~is_row, np.sin(cols) * sign, 0.0)
    return [jnp.asarray(t, f32) for t in (rc, rs, cc, cs)]


def _in_proj(x, mod3, norm1_g, w_t, gq_t, gk_t, bd, w_a, b_a, cast_ws):
    B, L, _ = x.shape
    tm = TM_PROJ
    rows_per = tm // GRID_W
    nstep = L // tm
    total = B * nstep

    def slab(w):
        R, C = w.shape
        if R % (total * BF16_SUBLANES) == 0:
            return pl.BlockSpec((R // total, C), lambda b, i: (b * nstep + i, 0))
        assert (2 * R) % (total * BF16_SUBLANES) == 0 and C % 256 == 0
        return pl.BlockSpec((2 * R // total, C // 2), lambda b, i: ((b * nstep + i) // 2, (b * nstep + i) % 2))
    rc, rs, cc, cs = _rope_tables(L)
    blk = np.kron(np.eye(256 // CHUNK), np.tril(np.ones((CHUNK, CHUNK)))).astype(np.float32)
    tl = jnp.asarray(blk, bf16)
    tok = lambda n: pl.BlockSpec((1, tm, n), lambda b, i: (b, i, 0))
    sds = lambda n, dt: jax.ShapeDtypeStruct((B, L, n), dt)
    return pl.pallas_call(
        _proj_kernel,
        grid=(B, L // tm),
        in_specs=[tok(D),
                  pl.BlockSpec((1, 1, D), lambda b, i: (b, 0, 0)),
                  pl.BlockSpec((1, 1, D), lambda b, i: (b, 0, 1)),
                  _const_spec((1, D)), _const_spec((D_IN, D)),
                  _const_spec((1, NA_W)), _const_spec((1, NA_W)), _const_spec((256, 256)),
                  pl.BlockSpec((rows_per, GQK), lambda b, i: (i, 0)),
                  pl.BlockSpec((rows_per, GQK), lambda b, i: (i, 0)),
                  _const_spec((GRID_W, GQK)), _const_spec((GRID_W, GQK)),
                  _const_spec((N_LR, 2 * GQK)), _const_spec((1, 2 * GQK)), _const_spec((256, 256))]
                 + [slab(w) for w in cast_ws],
        out_specs=[tok(W_NA), tok(W_GF), tok(W_GB), tok(W_MG),
                   pl.BlockSpec((1, tm // CHUNK, 2 * GQK), lambda b, i: (b, i, 0))]
                  + [slab(w) for w in cast_ws],
        out_shape=[sds(W_NA, bf16), sds(W_GF, bf16), sds(W_GB, bf16), sds(W_MG, bf16),
                   jax.ShapeDtypeStruct((B, L // CHUNK, 2 * GQK), f32)]
                  + [jax.ShapeDtypeStruct(w.shape, bf16) for w in cast_ws],
        compiler_params=pltpu.CompilerParams(
            dimension_semantics=("parallel", "parallel"), vmem_limit_bytes=VMEM_LIMIT),
        name="in_proj",
    )(x, mod3, mod3, norm1_g, w_t, gq_t, gk_t, bd, rc, rs, cc, cs, w_a, b_a, tl, *cast_ws)


def _lane_blocks(t):
    return [t[:, c:c + 128] for c in range(0, t.shape[1], 128)]


def _na_kernel(q_ref, k_ref, v_ref, kc_ref, vc_ref, t2_ref, o_ref, sl_ref, sc_ref, m_ref):
    i = pl.program_id(1)
    rows = k_ref.shape[1] // GRID_W
    lane_head = lax.broadcasted_iota(jnp.int32, (GRID_W, 256), 1) >> 6
    hmask = [lane_head == h for h in range(4)]

    def window_start(r):
        rs = jnp.clip(r - NA_WR // 2, 0, rows - NA_WR)
        return rs, pl.multiple_of(rs * GRID_W, GRID_W)

    def produce(rr, bank, j):
        r = i * NA_R + rr
        rs, start = window_start(r)
        d0 = rs - r + (NA_WR - 1)
        row0 = pl.multiple_of(rr * GRID_W, GRID_W)
        for g in range(2):
            ls = slice(256 * g, 256 * g + 256)
            q = q_ref[0, pl.ds(row0, GRID_W), ls]
            qs = jnp.concatenate([jnp.where(m, q, jnp.zeros_like(q)) for m in hmask], axis=0)
            kw = k_ref[0, pl.ds(start, NA_WR * GRID_W), ls]
            bias = jnp.concatenate(
                [jnp.concatenate([t2_ref[4 * g + h, d0 + 2 * jj] for jj in range(NA_WR // 2)], axis=1)
                 for h in range(4)], axis=0)
            s_loc = _dot_nt(qs, kw) + bias
            s_ctx = _dot_nt(qs, kc_ref[0, :, ls])
            mx = functools.reduce(jnp.maximum, _lane_blocks(s_loc) + _lane_blocks(s_ctx))
            sl_ref[bank, j, g] = s_loc
            sc_ref[bank, j, g] = s_ctx
            m_ref[bank, j, g] = jnp.broadcast_to(jnp.max(mx, axis=-1, keepdims=True), (4 * GRID_W, 128))

    def consume(rr, bank, j):
        _, start = window_start(i * NA_R + rr)
        row0 = pl.multiple_of(rr * GRID_W, GRID_W)
        outs = []
        for g in range(2):
            ls = slice(256 * g, 256 * g + 256)
            m = m_ref[bank, j, g]
            p_loc = [jnp.exp2(sl_ref[bank, j, g, :, c:c + 128] - m) for c in range(0, NA_WR * GRID_W, 128)]
            p_ctx = [jnp.exp2(sc_ref[bank, j, g, :, c:c + 128] - m) for c in range(0, CTX, 128)]
            den = jnp.sum(functools.reduce(jnp.add, p_loc + p_ctx), axis=-1, keepdims=True)
            vw = v_ref[0, pl.ds(start, NA_WR * GRID_W), ls]
            acc = (_dot(jnp.concatenate(p_loc, axis=1).astype(bf16), vw)
                   + _dot(jnp.concatenate(p_ctx, axis=1).astype(bf16), vc_ref[0, :, ls]))
            acc = acc * (1.0 / den)
            o = jnp.zeros((GRID_W, 256), f32)
            for h in range(4):
                o = o + jnp.where(hmask[h], acc[h * GRID_W:(h + 1) * GRID_W], 0.0)
            outs.append(o)
        o_ref[0, pl.ds(row0, GRID_W), :] = jnp.concatenate(outs, axis=1).astype(bf16)

    for j in range(NA_RI):
        produce(jnp.int32(j), 0, j)

    def step(t, carry):
        bank = t & 1
        for j in range(NA_RI):
            consume(t * NA_RI + j, bank, j)
        for j in range(NA_RI):
            produce(jnp.minimum((t + 1) * NA_RI + j, NA_R - 1), 1 - bank, j)
        return carry

    lax.fori_loop(0, NA_R // NA_RI, step, 0)


def _na_attn(qkv, kc, vc, t2):
    B, L, _ = qkv.shape
    nblk = L // (NA_R * GRID_W)
    return pl.pallas_call(
        _na_kernel,
        grid=(B, nblk),
        in_specs=[pl.BlockSpec((1, NA_R * GRID_W, NA_W), lambda b, i: (b, i, NA_Q // NA_W)),
                  pl.BlockSpec((1, L, NA_W), lambda b, i: (b, 0, NA_K // NA_W), pipeline_mode=pl.Buffered(1)),
                  pl.BlockSpec((1, L, NA_W), lambda b, i: (b, 0, NA_V // NA_W), pipeline_mode=pl.Buffered(1)),
                  pl.BlockSpec((1, CTX, NA_W), lambda b, i: (b, 0, 0)),
                  pl.BlockSpec((1, CTX, NA_W), lambda b, i: (b, 0, 0)),
                  _const_spec((NA_H, 2 * NA_WR - 1, GRID_W, 128))],
        out_specs=pl.BlockSpec((1, NA_R * GRID_W, NA_W), lambda b, i: (b, i, 0)),
        out_shape=jax.ShapeDtypeStruct((B, L, NA_W), bf16),
        scratch_shapes=[pltpu.VMEM((2, NA_RI, 2, 4 * GRID_W, NA_WR * GRID_W), f32),
                        pltpu.VMEM((2, NA_RI, 2, 4 * GRID_W, CTX), f32),
                        pltpu.VMEM((2, NA_RI, 2, 4 * GRID_W, 128), f32)],
        compiler_params=pltpu.CompilerParams(
            dimension_semantics=("parallel", "parallel"), vmem_limit_bytes=VMEM_LIMIT),
        name="na_attn",
    )(qkv, qkv, qkv, kc, vc, t2)


def _gla_chunk(q, ki, ke, v, dcol, s_ref, tri, hmask):
    qs = jnp.concatenate([jnp.where(m, q, jnp.zeros_like(q)) for m in hmask], axis=0)
    att = jnp.where(tri, _dot_nt(qs, ki), 0.0).astype(bf16)
    o = _dot(q, s_ref[...].astype(bf16))
    intra = [_dot(att[h * CHUNK:(h + 1) * CHUNK], v[:, h * GLA_DV:(h + 1) * GLA_DV]) for h in range(GLA_H)]
    o = o + jnp.concatenate(intra, axis=1)
    ket = ke.T
    for h in range(GLA_H):
        rs = slice(h * GLA_DK, (h + 1) * GLA_DK)
        cs = slice(h * GLA_DV, (h + 1) * GLA_DV)
        s_ref[rs, cs] = dcol[rs] * s_ref[rs, cs] + _dot(ket[rs], v[:, cs])
    return o


def _gla_kernel(gf_ref, decf_ref, gb_ref, vb_ref, decb_ref,
                s0f_ref, s0b_ref, of_ref, ob_ref, sf_ref, sb_ref):
    @pl.when(pl.program_id(1) == 0)
    def _():
        sf_ref[...] = s0f_ref[0]
        sb_ref[...] = s0b_ref[0]

    lane_head = lax.broadcasted_iota(jnp.int32, (CHUNK, GQK), 1) >> 6
    hmask = [lane_head == h for h in range(GLA_H)]
    ri = lax.broadcasted_iota(jnp.int32, (GLA_H * CHUNK, CHUNK), 0) & (CHUNK - 1)
    ci = lax.broadcasted_iota(jnp.int32, (GLA_H * CHUNK, CHUNK), 1)
    tri_f = ri >= ci
    tri_b = ri <= ci
    dect_f = decf_ref[0][:, :GQK].T
    dect_b = decb_ref[0][:, GQK:].T
    for c in range(GLA_CB):
        ts = slice(c * CHUNK, (c + 1) * CHUNK)
        of_ref[0, ts, :] = _gla_chunk(gf_ref[0, ts, G_Q:G_Q + GQK], gf_ref[0, ts, G_KI:G_KI + GQK],
                                      gf_ref[0, ts, G_KE:G_KE + GQK], gf_ref[0, ts, G_V:G_V + GV],
                                      dect_f[:, c:c + 1], sf_ref, tri_f, hmask).astype(of_ref.dtype)
        cb = GLA_CB - 1 - c
        tb = slice(cb * CHUNK, (cb + 1) * CHUNK)
        ob_ref[0, tb, :] = _gla_chunk(gb_ref[0, tb, G_Q - GV:G_Q - GV + GQK], gb_ref[0, tb, G_KI - GV:G_KI - GV + GQK],
                                      gb_ref[0, tb, G_KE - GV:G_KE - GV + GQK], vb_ref[0, tb, :],
                                      dect_b[:, cb:cb + 1], sb_ref, tri_b, hmask).astype(ob_ref.dtype)


def _gla_scan(gf, gb, dec, s0f, s0b):
    B, L, _ = gf.shape
    tm = GLA_CB * CHUNK
    nb = L // tm
    fw = lambda n: pl.BlockSpec((1, tm, n), lambda b, i: (b, i, 0))
    bw = lambda n: pl.BlockSpec((1, tm, n), lambda b, i: (b, nb - 1 - i, 0))
    st = pl.BlockSpec((1, GQK, GV), lambda b, i: (b, 0, 0))
    return pl.pallas_call(
        _gla_kernel,
        grid=(B, nb),
        in_specs=[fw(W_GF),
                  pl.BlockSpec((1, GLA_CB, 2 * GQK), lambda b, i: (b, i, 0)),
                  bw(W_GB), bw(GV),
                  pl.BlockSpec((1, GLA_CB, 2 * GQK), lambda b, i: (b, nb - 1 - i, 0)),
                  st, st],
        out_specs=[fw(GV), bw(GV)],
        out_shape=[jax.ShapeDtypeStruct((B, L, GV), bf16)] * 2,
        scratch_shapes=[pltpu.VMEM((GQK, GV), f32), pltpu.VMEM((GQK, GV), f32)],
        compiler_params=pltpu.CompilerParams(
            dimension_semantics=("parallel", "arbitrary"), vmem_limit_bytes=VMEM_LIMIT),
        name="gla_scan",
    )(gf, dec, gb, gf, dec, s0f, s0b)


def _merge_ffn_kernel(x_ref, ona_ref, of_ref, ob_ref, mg_ref,
                      g1_ref, sh2_ref, sc2_ref, g2_ref, n2_ref, gg_ref,
                      wna_ref, wgl_ref, wo_ref, wg_ref, wu_ref, wd_ref, o_ref):
    for hs in range(0, x_ref.shape[1], FFN_SUB):
        rows = slice(hs, hs + FFN_SUB)
        og = of_ref[0, rows].astype(f32) + ob_ref[0, rows].astype(f32)
        parts = []
        for h in range(GLA_H):
            seg = og[:, h * GLA_DV:(h + 1) * GLA_DV]
            ms = jnp.mean(seg * seg, axis=-1, keepdims=True)
            parts.append(seg * lax.rsqrt(ms + EPS) * gg_ref[...])
        ogl = (jnp.concatenate(parts, axis=1) * mg_ref[0, rows, MG_OG:MG_OG + GV].astype(f32)).astype(bf16)
        y = (mg_ref[0, rows, MG_A:MG_A + D].astype(f32) * _dot(ona_ref[0, rows], wna_ref[...])
             + mg_ref[0, rows, MG_B:MG_B + D].astype(f32) * _dot(ogl, wgl_ref[...]))
        x1 = x_ref[0, rows] + g1_ref[0] * _dot(y.astype(bf16), wo_ref[...])
        h2 = _norm_mod(x1, n2_ref[...], sh2_ref[0], sc2_ref[0]).astype(bf16)
        act = (_silu(_dot(h2, wg_ref[...])) * _dot(h2, wu_ref[...])).astype(bf16)
        o_ref[0, rows] = x1 + g2_ref[0] * _dot(act, wd_ref[...])


def _merge_ffn(x, ona, of, ob, mg, mod3, norm2_g, gg, wna, wgl, wo, wg, wu, wd):
    B, L, _ = x.shape
    tm = TM_FFN
    tok = lambda n: pl.BlockSpec((1, tm, n), lambda b, i: (b, i, 0))
    modv = lambda j: pl.BlockSpec((1, 1, D), lambda b, i: (b, 0, j))
    return pl.pallas_call(
        _merge_ffn_kernel,
        grid=(B, L // tm),
        in_specs=[tok(D), tok(NA_W), tok(GV), tok(GV), tok(W_MG),
                  modv(2), modv(3), modv(4), modv(5),
                  _const_spec((1, D)), _const_spec((1, GLA_DV)),
                  _const_spec((NA_W, D)), _const_spec((GV, D)), _const_spec((D, D)),
                  _const_spec((D, D_FF)), _const_spec((D, D_FF)), _const_spec((D_FF, D))],
        out_specs=tok(D),
        out_shape=jax.ShapeDtypeStruct((B, L, D), f32),
        compiler_params=pltpu.CompilerParams(
            dimension_semantics=("parallel", "parallel"), vmem_limit_bytes=VMEM_LIMIT),
        name="merge_ffn",
    )(x, ona, of, ob, mg, mod3, mod3, mod3, mod3, norm2_g, gg, wna, wgl, wo, wg, wu, wd)


def kernel(x, c, ctx, c_ctx, w_mod, b_mod, norm1_g, norm2_g, w_in, na_q_norm_g, na_k_norm_g, na_rpb,
           gla_w_alpha, gla_b_alpha, gla_norm_g, w_branch_na, w_branch_gla, w_out,
           w_ffn_gate, w_ffn_up, w_ffn_down):
    B, L, _ = x.shape
    assert w_mod.shape[0] == 1 and L % (NA_R * GRID_W) == 0 and L % TM_PROJ == 0

    w_t = _cast_bf16(jnp.transpose(w_in[0]), D_IN // 7)
    w_a = jnp.zeros((N_LR, 2 * GQK), f32)
    w_a = w_a.at[:GLA_RANK, :GQK].set(gla_w_alpha[0, 0]).at[GLA_RANK:, GQK:].set(gla_w_alpha[0, 1])
    w_a = w_a.astype(bf16)
    b_a = gla_b_alpha[0].reshape(1, 2 * GQK)
    gq_t = jnp.tile(na_q_norm_g[0] * (NA_DH ** -0.5 * LOG2E), NA_H).reshape(1, NA_W)
    gk_t = jnp.tile(na_k_norm_g[0], NA_H).reshape(1, NA_W)
    bd = jnp.asarray(np.kron(np.eye(4), np.full((NA_DH, NA_DH), 1.0 / NA_DH)), bf16)

    mod = _adaln_mod(c, c_ctx, w_mod[0], b_mod[0])
    mod3 = mod.reshape(8, 1, 6 * D)
    t2 = _rpb_expand(na_rpb[0])

    kc, vc, s0f, s0b = _ctx_side(ctx, mod3, norm1_g, w_t, w_a, b_a, gk_t, bd, B)
    later_ws = [w_branch_na[0], w_branch_gla[0], w_out[0], w_ffn_gate[0], w_ffn_up[0], w_ffn_down[0]]
    na_qkv, gla_f, gla_b, gates, dec, *later_bf = _in_proj(
        x, mod3, norm1_g, w_t, gq_t, gk_t, bd, w_a, b_a, later_ws)
    ona = _na_attn(na_qkv, kc, vc, t2)
    of, ob = _gla_scan(gla_f, gla_b, dec, s0f, s0b)
    return _merge_ffn(x, ona, of, ob, gates, mod3, norm2_g, gla_norm_g, *later_bf)
```

```python
import functools

import numpy as np
import jax
import jax.numpy as jnp
from jax import lax
from jax.experimental import pallas as pl
from jax.experimental.pallas import tpu as pltpu

D = 1024
GRID_W = 64
CTX = 256
NA_H = 8
NA_DH = 64
NA_WR = 8
NA_WC = 16
GLA_H = 4
GLA_DK = 64
GLA_DV = 128
GLA_RANK = 16
GLA_TAU = 16.0
GLA_LA_MIN = -1.0
CHUNK = 64
ROPE_THETA = 10000.0
D_FF = 2816
EPS = 1e-6
NEG_INF = -1e30
NA_W = NA_H * NA_DH
GQK = GLA_H * GLA_DK
GV = GLA_H * GLA_DV

OFF_QNA, OFF_KNA, OFF_VNA = 0, 512, 1024
OFF_QG, OFF_KG, OFF_VG, OFF_OG = 1536, 1792, 2048, 2560
OFF_LR, OFF_MA, OFF_MB = 3072, 3104, 4128
N_LR = 2 * GLA_RANK
N_PROJ_OUT = 7
NA_Q, NA_K, NA_V, W_NA = 0, 512, 1024, 1536
G_V, G_Q, G_KI, W_GF = 0, 512, 768, 1024
W_GB = W_GF - GV
MG_A, MG_B, MG_OG, W_MG = 0, 1024, 2048, 2560
D_IN = 5152
LOG2E = 1.4426950408889634

BF16_SUBLANES = 16
VMEM_LIMIT = 56 * 1024 * 1024

TM_PROJ = 512
PROJ_SUB = 256
TM_FFN = 512
FFN_SUB = 512
NA_R = 32
NA_RI = 4
GLA_CB = 32

f32 = jnp.float32
bf16 = jnp.bfloat16


def _const_spec(shape):
    nd = len(shape)
    return pl.BlockSpec(shape, lambda *_: (0,) * nd, pipeline_mode=pl.Buffered(1))


def _dot(a, b):
    return jnp.dot(a, b, preferred_element_type=f32)


def _dot_nt(a, b):
    return lax.dot_general(a, b, (((1,), (1,)), ((), ())), preferred_element_type=f32)


def _split2(x):
    hi = x.astype(bf16)
    lo = (x - hi.astype(f32)).astype(bf16)
    return hi, lo


def _sigmoid(x):
    return 0.5 * jnp.tanh(0.5 * x) + 0.5


def _silu(x):
    return x * _sigmoid(x)


def _split3_dot(r, oh):
    h1 = r.astype(bf16)
    r2 = r - h1.astype(f32)
    h2 = r2.astype(bf16)
    h3 = (r2 - h2.astype(f32)).astype(bf16)
    return _dot(h1, oh) + _dot(h2, oh) + _dot(h3, oh)


def _rpb_expand_kernel(r0_ref, r1_ref, oh0_ref, oh1_ref, mask_ref, o_ref):
    gathered = _split3_dot(r0_ref[...], oh0_ref[...]) + _split3_dot(r1_ref[...], oh1_ref[...])
    o_ref[...] = gathered * LOG2E + mask_ref[...]


def _rpb_expand(rpb):
    nd = 2 * NA_WR - 1
    nrow = NA_H * nd
    ncol = 2 * NA_WC - 1
    nxt = jnp.concatenate([rpb[:, 1:], jnp.zeros_like(rpb[:, :1])], axis=1)
    r0 = jnp.zeros((128, 128), f32).at[:nrow, :ncol].set(rpb.reshape(nrow, ncol))
    r1 = jnp.zeros((128, 128), f32).at[:nrow, :ncol].set(nxt.reshape(nrow, ncol))
    q = np.arange(GRID_W)
    cs = np.clip(q - NA_WC // 2, 0, GRID_W - NA_WC)
    in_win = (q[None, :] >= cs[:, None]) & (q[None, :] < cs[:, None] + NA_WC)
    dc = np.clip(q[None, :] - q[:, None], -(NA_WC - 1), NA_WC - 1) + NA_WC - 1
    oh = np.zeros((2, 128, GRID_W, 128), np.float32)
    qq, kk = np.nonzero(in_win)
    oh[0, dc[qq, kk], qq, kk] = 1.0
    oh[1, dc[qq, kk], qq, kk + GRID_W] = 1.0
    oh = oh.reshape(2, 128, GRID_W * 128)
    mask = np.where(np.concatenate([in_win, in_win], axis=1), 0.0, NEG_INF).reshape(1, GRID_W * 128)
    out = pl.pallas_call(
        _rpb_expand_kernel,
        out_shape=jax.ShapeDtypeStruct((128, GRID_W * 128), f32),
        name="rpb_expand",
    )(r0, r1, jnp.asarray(oh[0], bf16), jnp.asarray(oh[1], bf16), jnp.asarray(mask, f32))
    return out[:nrow].reshape(NA_H, nd, GRID_W, 128)


def _cast_kernel(x_ref, o_ref):
    o_ref[...] = x_ref[...].astype(bf16)


def _cast_bf16(w, rows):
    R, C = w.shape
    return pl.pallas_call(
        _cast_kernel,
        grid=(R // rows,),
        in_specs=[pl.BlockSpec((rows, C), lambda j: (j, 0))],
        out_specs=pl.BlockSpec((rows, C), lambda j: (j, 0)),
        out_shape=jax.ShapeDtypeStruct((R, C), bf16),
        name="cast_bf16",
    )(w)


def _mod_kernel(a_ref, w_ref, b_ref, o_ref):
    a = _silu(a_ref[...]).astype(bf16)
    o_ref[...] = _dot(a, w_ref[...].astype(bf16)) + b_ref[...]


def _adaln_mod(c, c_ctx, w_mod, b_mod):
    B = c.shape[0]
    a = jnp.zeros((8, D), f32).at[:B].set(c).at[B].set(c_ctx)
    n = w_mod.shape[1]
    return pl.pallas_call(
        _mod_kernel,
        grid=(n // D,),
        in_specs=[pl.BlockSpec((8, D), lambda j: (0, 0)),
                  pl.BlockSpec((D, D), lambda j: (0, j)),
                  pl.BlockSpec((1, D), lambda j: (0, j))],
        out_specs=pl.BlockSpec((8, D), lambda j: (0, j)),
        out_shape=jax.ShapeDtypeStruct((8, n), f32),
        name="adaln_mod",
    )(a, w_mod, b_mod.reshape(1, n))


def _norm_mod(x, g, sh, sc):
    ms = jnp.mean(x * x, axis=-1, keepdims=True)
    return (x * lax.rsqrt(ms + EPS)) * (g * (1.0 + sc)) + sh


def _head_rms(t, bd, g):
    sq = (t * t).astype(bf16)
    ms = jnp.concatenate([_dot(sq[:, :256], bd), _dot(sq[:, 256:], bd)], axis=1)
    return t * lax.rsqrt(ms + EPS) * g


def _log_alpha(zl, wa_ref, ba_ref):
    z = _dot(zl.astype(bf16), wa_ref[...]) + ba_ref[...]
    soft = jnp.log2(1.0 + jnp.exp2(z * -LOG2E))
    return jnp.maximum(soft * (-1.0 / GLA_TAU), GLA_LA_MIN * LOG2E)


def _ctx_kernel(ctx_ref, sh_ref, sc_ref, g1_ref, wk_ref, wv_ref, wkg_ref, wvg_ref, wlr_ref,
                wa_ref, ba_ref, gk_ref, bd_ref, su_ref, sl_ref,
                kc_ref, vc_ref, sf_ref, sb_ref):
    h = _norm_mod(ctx_ref[0], g1_ref[...], sh_ref[0], sc_ref[0]).astype(bf16)
    k = _dot_nt(h, wk_ref[...])
    kc_ref[0] = _head_rms(k, bd_ref[...], gk_ref[...]).astype(bf16)
    vc_ref[0] = _dot_nt(h, wv_ref[...]).astype(bf16)
    kg = _dot_nt(h, wkg_ref[...])
    vg = _dot_nt(h, wvg_ref[...]).astype(bf16)
    la = _log_alpha(_dot_nt(h, wlr_ref[...]), wa_ref, ba_ref)
    hi, lo = _split2(la)
    ef = _dot(su_ref[...], hi[:, :GQK]) + _dot(su_ref[...], lo[:, :GQK])
    eb = _dot(sl_ref[...], hi[:, GQK:]) + _dot(sl_ref[...], lo[:, GQK:])
    r = lax.broadcasted_iota(jnp.int32, (GQK, GV), 0) >> 6
    cc = lax.broadcasted_iota(jnp.int32, (GQK, GV), 1) >> 7
    diag = r == cc
    for e, out in ((ef, sf_ref), (eb, sb_ref)):
        kw = (kg * jnp.exp2(e)).astype(bf16)
        u = _dot(kw.T, vg)
        out[0] = jnp.where(diag, u, 0.0)


def _ctx_side(ctx, mod3, norm1_g, w_t, w_a, b_a, gk_t, bd, n_b):
    su = jnp.asarray(np.triu(np.ones((CTX, CTX), np.float32), 1), bf16)
    sl = jnp.asarray(np.tril(np.ones((CTX, CTX), np.float32), -1), bf16)
    B = ctx.shape[0]
    cst = lambda shape, idx: pl.BlockSpec(shape, lambda b: idx)
    return pl.pallas_call(
        _ctx_kernel,
        grid=(B,),
        in_specs=[pl.BlockSpec((1, CTX, D), lambda b: (b, 0, 0)),
                  cst((1, 1, D), (n_b, 0, 0)), cst((1, 1, D), (n_b, 0, 1)),
                  cst((1, D), (0, 0)),
                  cst((NA_W, D), (OFF_KNA // NA_W, 0)), cst((NA_W, D), (OFF_VNA // NA_W, 0)),
                  cst((GQK, D), (OFF_KG // GQK, 0)), cst((GV, D), (OFF_VG // GV, 0)),
                  cst((N_LR, D), (OFF_LR // N_LR, 0)),
                  cst((N_LR, 2 * GQK), (0, 0)), cst((1, 2 * GQK), (0, 0)),
                  cst((1, NA_W), (0, 0)), cst((256, 256), (0, 0)),
                  cst((CTX, CTX), (0, 0)), cst((CTX, CTX), (0, 0))],
        out_specs=[pl.BlockSpec((1, CTX, NA_W), lambda b: (b, 0, 0)),
                   pl.BlockSpec((1, CTX, NA_W), lambda b: (b, 0, 0)),
                   pl.BlockSpec((1, GQK, GV), lambda b: (b, 0, 0)),
                   pl.BlockSpec((1, GQK, GV), lambda b: (b, 0, 0))],
        out_shape=[jax.ShapeDtypeStruct((B, CTX, NA_W), bf16),
                   jax.ShapeDtypeStruct((B, CTX, NA_W), bf16),
                   jax.ShapeDtypeStruct((B, GQK, GV), f32),
                   jax.ShapeDtypeStruct((B, GQK, GV), f32)],
        compiler_params=pltpu.CompilerParams(vmem_limit_bytes=VMEM_LIMIT),
        name="ctx_side",
    )(ctx, mod3, mod3, norm1_g, w_t, w_t, w_t, w_t, w_t, w_a, b_a, gk_t, bd, su, sl)


def _proj_kernel(x_ref, sh_ref, sc_ref, g1_ref, w_ref, gq_ref, gk_ref, bd_ref,
                 rc_ref, rs_ref, cc_ref, cs_ref, wa_ref, ba_ref, tl_ref, *refs):
    n_cast = (len(refs) - N_PROJ_OUT) // 2
    cast_in, refs = refs[:n_cast], refs[n_cast:]
    na_ref, gf_ref, gb_ref, mg_ref, dec_ref, ktf_ref, ktb_ref = refs[:N_PROJ_OUT]
    cast_out = refs[N_PROJ_OUT:]
    for src, dst in zip(cast_in, cast_out):
        dst[...] = src[...].astype(bf16)

    bd = bd_ref[...]
    tl = tl_ref[...]
    nrow = PROJ_SUB // GRID_W
    nchunk = PROJ_SUB // CHUNK
    lane = lax.broadcasted_iota(jnp.int32, (PROJ_SUB, GQK), 1)
    first_half = (lane & 31) < 16

    for hs in range(0, x_ref.shape[1], PROJ_SUB):
        rows = slice(hs, hs + PROJ_SUB)
        hb = _norm_mod(x_ref[0, rows], g1_ref[...], sh_ref[0], sc_ref[0]).astype(bf16)

        def proj(off, n):
            return _dot_nt(hb, w_ref[off:off + n, :])

        zl = proj(OFF_LR, N_LR)
        q_raw = proj(OFF_QNA, NA_W)
        la = _log_alpha(zl, wa_ref, ba_ref)
        k_raw = proj(OFF_KNA, NA_W)
        hi, lo = _split2(la)
        na_ref[0, rows, NA_Q:NA_Q + NA_W] = _head_rms(q_raw, bd, gq_ref[...]).astype(bf16)
        cum = jnp.concatenate(
            [_dot(tl, hi[s:s + 256]) + _dot(tl, lo[s:s + 256]) for s in range(0, PROJ_SUB, 256)], axis=0)
        na_ref[0, rows, NA_K:NA_K + NA_W] = _head_rms(k_raw, bd, gk_ref[...]).astype(bf16)
        qg_raw = proj(OFF_QG, GQK)
        kg_raw = proj(OFF_KG, GQK)
        na_ref[0, rows, NA_V:NA_V + NA_W] = proj(OFF_VNA, NA_W).astype(bf16)
        gf_ref[0, rows, G_V:G_V + GV] = proj(OFF_VG, GV).astype(bf16)

        gr = slice(hs // GRID_W, hs // GRID_W + nrow)
        cos_t = (rc_ref[gr][:, None, :] + cc_ref[...][None]).reshape(PROJ_SUB, GQK)
        sin_t = (rs_ref[gr][:, None, :] + cs_ref[...][None]).reshape(PROJ_SUB, GQK)

        def rope(t):
            partner = jnp.where(first_half, pltpu.roll(t, GQK - 16, 1), pltpu.roll(t, 16, 1))
            return t * cos_t + partner * sin_t

        qg = rope(qg_raw) * (GLA_DK ** -0.5)
        kg = rope(kg_raw)

        cum3 = cum.reshape(nchunk, CHUNK, 2 * GQK)
        tot3 = cum3[:, CHUNK - 1:CHUNK, :]
        la3 = la.reshape(nchunk, CHUNK, 2 * GQK)
        dec_ref[0, hs // CHUNK:hs // CHUNK + nchunk] = jnp.exp2(tot3).reshape(nchunk, 2 * GQK)
        b_f = cum3[:, :, :GQK].reshape(PROJ_SUB, GQK)
        e_f = (tot3 - cum3)[:, :, :GQK].reshape(PROJ_SUB, GQK)
        b_b = (tot3 - cum3 + la3)[:, :, GQK:].reshape(PROJ_SUB, GQK)
        e_b = (cum3 - la3)[:, :, GQK:].reshape(PROJ_SUB, GQK)
        mg_ref[0, rows, MG_OG:MG_OG + GV] = _silu(proj(OFF_OG, GV)).astype(bf16)
        gf_ref[0, rows, G_Q:G_Q + GQK] = (qg * jnp.exp2(b_f)).astype(bf16)
        gf_ref[0, rows, G_KI:G_KI + GQK] = (kg * jnp.exp2(-b_f)).astype(bf16)
        ktf_ref[0, :, rows] = (kg * jnp.exp2(e_f)).T.astype(bf16)
        mg_ref[0, rows, MG_A:MG_A + D] = _sigmoid(proj(OFF_MA, D)).astype(bf16)
        gb_ref[0, rows, G_Q - GV:G_Q - GV + GQK] = (qg * jnp.exp2(b_b)).astype(bf16)
        gb_ref[0, rows, G_KI - GV:G_KI - GV + GQK] = (kg * jnp.exp2(-b_b)).astype(bf16)
        ktb_ref[0, :, rows] = (kg * jnp.exp2(e_b)).T.astype(bf16)
        mg_ref[0, rows, MG_B:MG_B + D] = _sigmoid(proj(OFF_MB, D)).astype(bf16)


def _rope_tables(L):
    n = GLA_DK // 4
    freqs = ROPE_THETA ** (-np.arange(n, dtype=np.float64) / n)
    lane = np.arange(GQK)
    fl = freqs[lane % n]
    is_row = (lane % GLA_DK) < GLA_DK // 2
    sign = np.where((lane % 32) < 16, -1.0, 1.0)
    rows = np.arange(L // GRID_W)[:, None] * fl[None]
    cols = np.arange(GRID_W)[:, None] * fl[None]
    rc = np.where(is_row, np.cos(rows), 0.0)
    rs = np.where(is_row, np.sin(rows) * sign, 0.0)
    cc = np.where(~is_row, np.cos(cols), 0.0)
    cs = np.where(~is_row, np.sin(cols) * sign, 0.0)
    return [jnp.asarray(t, f32) for t in (rc, rs, cc, cs)]


def _in_proj(x, mod3, norm1_g, w_t, gq_t, gk_t, bd, w_a, b_a, cast_ws):
    B, L, _ = x.shape
    tm = TM_PROJ
    rows_per = tm // GRID_W
    nstep = L // tm
    total = B * nstep

    def slab(w):
        R, C = w.shape
        if R % (total * BF16_SUBLANES) == 0:
            return pl.BlockSpec((R // total, C), lambda b, i: (b * nstep + i, 0))
        assert (2 * R) % (total * BF16_SUBLANES) == 0 and C % 256 == 0
        return pl.BlockSpec((2 * R // total, C // 2), lambda b, i: ((b * nstep + i) // 2, (b * nstep + i) % 2))
    rc, rs, cc, cs = _rope_tables(L)
    blk = np.kron(np.eye(256 // CHUNK), np.tril(np.ones((CHUNK, CHUNK)))).astype(np.float32)
    tl = jnp.asarray(blk, bf16)
    tok = lambda n: pl.BlockSpec((1, tm, n), lambda b, i: (b, i, 0))
    sds = lambda n, dt: jax.ShapeDtypeStruct((B, L, n), dt)
    return pl.pallas_call(
        _proj_kernel,
        grid=(B, L // tm),
        in_specs=[tok(D),
                  pl.BlockSpec((1, 1, D), lambda b, i: (b, 0, 0)),
                  pl.BlockSpec((1, 1, D), lambda b, i: (b, 0, 1)),
                  _const_spec((1, D)), _const_spec((D_IN, D)),
                  _const_spec((1, NA_W)), _const_spec((1, NA_W)), _const_spec((256, 256)),
                  pl.BlockSpec((rows_per, GQK), lambda b, i: (i, 0)),
                  pl.BlockSpec((rows_per, GQK), lambda b, i: (i, 0)),
                  _const_spec((GRID_W, GQK)), _const_spec((GRID_W, GQK)),
                  _const_spec((N_LR, 2 * GQK)), _const_spec((1, 2 * GQK)), _const_spec((256, 256))]
                 + [slab(w) for w in cast_ws],
        out_specs=[tok(W_NA), tok(W_GF), tok(W_GB), tok(W_MG),
                   pl.BlockSpec((1, tm // CHUNK, 2 * GQK), lambda b, i: (b, i, 0)),
                   pl.BlockSpec((1, GQK, tm), lambda b, i: (b, 0, i)),
                   pl.BlockSpec((1, GQK, tm), lambda b, i: (b, 0, i))]
                  + [slab(w) for w in cast_ws],
        out_shape=[sds(W_NA, bf16), sds(W_GF, bf16), sds(W_GB, bf16), sds(W_MG, bf16),
                   jax.ShapeDtypeStruct((B, L // CHUNK, 2 * GQK), f32),
                   jax.ShapeDtypeStruct((B, GQK, L), bf16), jax.ShapeDtypeStruct((B, GQK, L), bf16)]
                  + [jax.ShapeDtypeStruct(w.shape, bf16) for w in cast_ws],
        compiler_params=pltpu.CompilerParams(
            dimension_semantics=("parallel", "parallel"), vmem_limit_bytes=VMEM_LIMIT),
        name="in_proj",
    )(x, mod3, mod3, norm1_g, w_t, gq_t, gk_t, bd, rc, rs, cc, cs, w_a, b_a, tl, *cast_ws)


def _lane_blocks(t):
    return [t[:, c:c + 128] for c in range(0, t.shape[1], 128)]


def _na_kernel(q_ref, k_ref, v_ref, kc_ref, vc_ref, t2_ref, o_ref, sl_ref, sc_ref, m_ref):
    i = pl.program_id(1)
    rows = k_ref.shape[1] // GRID_W
    lane_head = lax.broadcasted_iota(jnp.int32, (GRID_W, 256), 1) >> 6
    hmask = [lane_head == h for h in range(4)]

    def window_start(r):
        rs = jnp.clip(r - NA_WR // 2, 0, rows - NA_WR)
        return rs, pl.multiple_of(rs * GRID_W, GRID_W)

    def produce(rr, bank, j):
        r = i * NA_R + rr
        rs, start = window_start(r)
        d0 = rs - r + (NA_WR - 1)
        row0 = pl.multiple_of(rr * GRID_W, GRID_W)
        for g in range(2):
            ls = slice(256 * g, 256 * g + 256)
            q = q_ref[0, pl.ds(row0, GRID_W), ls]
            qs = jnp.concatenate([jnp.where(m, q, jnp.zeros_like(q)) for m in hmask], axis=0)
            kw = k_ref[0, pl.ds(start, NA_WR * GRID_W), ls]
            bias = jnp.concatenate(
                [jnp.concatenate([t2_ref[4 * g + h, d0 + 2 * jj] for jj in range(NA_WR // 2)], axis=1)
                 for h in range(4)], axis=0)
            s_loc = _dot_nt(qs, kw) + bias
            s_ctx = _dot_nt(qs, kc_ref[0, :, ls])
            mx = functools.reduce(jnp.maximum, _lane_blocks(s_loc) + _lane_blocks(s_ctx))
            sl_ref[bank, j, g] = s_loc
            sc_ref[bank, j, g] = s_ctx
            m_ref[bank, j, g] = jnp.broadcast_to(jnp.max(mx, axis=-1, keepdims=True), (4 * GRID_W, 128))

    def consume(rr, bank, j):
        _, start = window_start(i * NA_R + rr)
        row0 = pl.multiple_of(rr * GRID_W, GRID_W)
        outs = []
        for g in range(2):
            ls = slice(256 * g, 256 * g + 256)
            m = m_ref[bank, j, g]
            p_loc = [jnp.exp2(sl_ref[bank, j, g, :, c:c + 128] - m) for c in range(0, NA_WR * GRID_W, 128)]
            p_ctx = [jnp.exp2(sc_ref[bank, j, g, :, c:c + 128] - m) for c in range(0, CTX, 128)]
            den = jnp.sum(functools.reduce(jnp.add, p_loc + p_ctx), axis=-1, keepdims=True)
            vw = v_ref[0, pl.ds(start, NA_WR * GRID_W), ls]
            acc = (_dot(jnp.concatenate(p_loc, axis=1).astype(bf16), vw)
                   + _dot(jnp.concatenate(p_ctx, axis=1).astype(bf16), vc_ref[0, :, ls]))
            acc = acc * (1.0 / den)
            o = jnp.zeros((GRID_W, 256), f32)
            for h in range(4):
                o = o + jnp.where(hmask[h], acc[h * GRID_W:(h + 1) * GRID_W], 0.0)
            outs.append(o)
        o_ref[0, pl.ds(row0, GRID_W), :] = jnp.concatenate(outs, axis=1).astype(bf16)

    for j in range(NA_RI):
        produce(jnp.int32(j), 0, j)

    def step(t, carry):
        bank = t & 1
        for j in range(NA_RI):
            consume(t * NA_RI + j, bank, j)
        for j in range(NA_RI):
            produce(jnp.minimum((t + 1) * NA_RI + j, NA_R - 1), 1 - bank, j)
        return carry

    lax.fori_loop(0, NA_R // NA_RI, step, 0)


def _na_attn(qkv, kc, vc, t2):
    B, L, _ = qkv.shape
    nblk = L // (NA_R * GRID_W)
    return pl.pallas_call(
        _na_kernel,
        grid=(B, nblk),
        in_specs=[pl.BlockSpec((1, NA_R * GRID_W, NA_W), lambda b, i: (b, i, NA_Q // NA_W)),
                  pl.BlockSpec((1, L, NA_W), lambda b, i: (b, 0, NA_K // NA_W), pipeline_mode=pl.Buffered(1)),
                  pl.BlockSpec((1, L, NA_W), lambda b, i: (b, 0, NA_V // NA_W), pipeline_mode=pl.Buffered(1)),
                  pl.BlockSpec((1, CTX, NA_W), lambda b, i: (b, 0, 0)),
                  pl.BlockSpec((1, CTX, NA_W), lambda b, i: (b, 0, 0)),
                  _const_spec((NA_H, 2 * NA_WR - 1, GRID_W, 128))],
        out_specs=pl.BlockSpec((1, NA_R * GRID_W, NA_W), lambda b, i: (b, i, 0)),
        out_shape=jax.ShapeDtypeStruct((B, L, NA_W), bf16),
        scratch_shapes=[pltpu.VMEM((2, NA_RI, 2, 4 * GRID_W, NA_WR * GRID_W), f32),
                        pltpu.VMEM((2, NA_RI, 2, 4 * GRID_W, CTX), f32),
                        pltpu.VMEM((2, NA_RI, 2, 4 * GRID_W, 128), f32)],
        compiler_params=pltpu.CompilerParams(
            dimension_semantics=("parallel", "parallel"), vmem_limit_bytes=VMEM_LIMIT),
        name="na_attn",
    )(qkv, qkv, qkv, kc, vc, t2)


def _gla_chunk(q, ki, v, ket, dcol, s_ref, tri, hmask):
    qs = jnp.concatenate([jnp.where(m, q, jnp.zeros_like(q)) for m in hmask], axis=0)
    att = jnp.where(tri, _dot_nt(qs, ki), 0.0).astype(bf16)
    o = _dot(q, s_ref[...].astype(bf16))
    intra = [_dot(att[h * CHUNK:(h + 1) * CHUNK], v[:, h * GLA_DV:(h + 1) * GLA_DV]) for h in range(GLA_H)]
    o = o + jnp.concatenate(intra, axis=1)
    for h in range(GLA_H):
        rs = slice(h * GLA_DK, (h + 1) * GLA_DK)
        cs = slice(h * GLA_DV, (h + 1) * GLA_DV)
        s_ref[rs, cs] = dcol[rs] * s_ref[rs, cs] + _dot(ket[rs], v[:, cs])
    return o


def _gla_kernel(gf_ref, ktf_ref, decf_ref, gb_ref, vb_ref, ktb_ref, decb_ref,
                s0f_ref, s0b_ref, of_ref, ob_ref, sf_ref, sb_ref):
    @pl.when(pl.program_id(1) == 0)
    def _():
        sf_ref[...] = s0f_ref[0]
        sb_ref[...] = s0b_ref[0]

    lane_head = lax.broadcasted_iota(jnp.int32, (CHUNK, GQK), 1) >> 6
    hmask = [lane_head == h for h in range(GLA_H)]
    ri = lax.broadcasted_iota(jnp.int32, (GLA_H * CHUNK, CHUNK), 0) & (CHUNK - 1)
    ci = lax.broadcasted_iota(jnp.int32, (GLA_H * CHUNK, CHUNK), 1)
    tri_f = ri >= ci
    tri_b = ri <= ci
    dect_f = decf_ref[0][:, :GQK].T
    dect_b = decb_ref[0][:, GQK:].T

    def k_end(kt_ref, c):
        kt = kt_ref[0, :, (c // 2) * 2 * CHUNK:(c // 2 + 1) * 2 * CHUNK]
        if c % 2:
            kt = pltpu.roll(kt, CHUNK, 1)
        return kt[:, :CHUNK]

    for c in range(GLA_CB):
        ts = slice(c * CHUNK, (c + 1) * CHUNK)
        of_ref[0, ts, :] = _gla_chunk(gf_ref[0, ts, G_Q:G_Q + GQK], gf_ref[0, ts, G_KI:G_KI + GQK],
                                      gf_ref[0, ts, G_V:G_V + GV], k_end(ktf_ref, c),
                                      dect_f[:, c:c + 1], sf_ref, tri_f, hmask)
        cb = GLA_CB - 1 - c
        tb = slice(cb * CHUNK, (cb + 1) * CHUNK)
        ob_ref[0, tb, :] = _gla_chunk(gb_ref[0, tb, G_Q - GV:G_Q - GV + GQK], gb_ref[0, tb, G_KI - GV:G_KI - GV + GQK],
                                      vb_ref[0, tb, :], k_end(ktb_ref, cb),
                                      dect_b[:, cb:cb + 1], sb_ref, tri_b, hmask)


def _gla_scan(gf, gb, ktf, ktb, dec, s0f, s0b):
    B, L, _ = gf.shape
    tm = GLA_CB * CHUNK
    nb = L // tm
    fw = lambda n: pl.BlockSpec((1, tm, n), lambda b, i: (b, i, 0))
    bw = lambda n: pl.BlockSpec((1, tm, n), lambda b, i: (b, nb - 1 - i, 0))
    st = pl.BlockSpec((1, GQK, GV), lambda b, i: (b, 0, 0))
    return pl.pallas_call(
        _gla_kernel,
        grid=(B, nb),
        in_specs=[fw(W_GF),
                  pl.BlockSpec((1, GQK, tm), lambda b, i: (b, 0, i)),
                  pl.BlockSpec((1, GLA_CB, 2 * GQK), lambda b, i: (b, i, 0)),
                  bw(W_GB), bw(GV),
                  pl.BlockSpec((1, GQK, tm), lambda b, i: (b, 0, nb - 1 - i)),
                  pl.BlockSpec((1, GLA_CB, 2 * GQK), lambda b, i: (b, nb - 1 - i, 0)),
                  st, st],
        out_specs=[fw(GV), bw(GV)],
        out_shape=[jax.ShapeDtypeStruct((B, L, GV), f32)] * 2,
        scratch_shapes=[pltpu.VMEM((GQK, GV), f32), pltpu.VMEM((GQK, GV), f32)],
        compiler_params=pltpu.CompilerParams(
            dimension_semantics=("parallel", "arbitrary"), vmem_limit_bytes=VMEM_LIMIT),
        name="gla_scan",
    )(gf, ktf, dec, gb, gf, ktb, dec, s0f, s0b)


def _merge_ffn_kernel(x_ref, ona_ref, of_ref, ob_ref, mg_ref,
                      g1_ref, sh2_ref, sc2_ref, g2_ref, n2_ref, gg_ref,
                      wna_ref, wgl_ref, wo_ref, wg_ref, wu_ref, wd_ref, o_ref):
    for hs in range(0, x_ref.shape[1], FFN_SUB):
        rows = slice(hs, hs + FFN_SUB)
        og = of_ref[0, rows] + ob_ref[0, rows]
        parts = []
        for h in range(GLA_H):
            seg = og[:, h * GLA_DV:(h + 1) * GLA_DV]
            ms = jnp.mean(seg * seg, axis=-1, keepdims=True)
            parts.append(seg * lax.rsqrt(ms + EPS) * gg_ref[...])
        ogl = (jnp.concatenate(parts, axis=1) * mg_ref[0, rows, MG_OG:MG_OG + GV].astype(f32)).astype(bf16)
        y = (mg_ref[0, rows, MG_A:MG_A + D].astype(f32) * _dot(ona_ref[0, rows], wna_ref[...])
             + mg_ref[0, rows, MG_B:MG_B + D].astype(f32) * _dot(ogl, wgl_ref[...]))
        x1 = x_ref[0, rows] + g1_ref[0] * _dot(y.astype(bf16), wo_ref[...])
        h2 = _norm_mod(x1, n2_ref[...], sh2_ref[0], sc2_ref[0]).astype(bf16)
        act = (_silu(_dot(h2, wg_ref[...])) * _dot(h2, wu_ref[...])).astype(bf16)
        o_ref[0, rows] = x1 + g2_ref[0] * _dot(act, wd_ref[...])


def _merge_ffn(x, ona, of, ob, mg, mod3, norm2_g, gg, wna, wgl, wo, wg, wu, wd):
    B, L, _ = x.shape
    tm = TM_FFN
    tok = lambda n: pl.BlockSpec((1, tm, n), lambda b, i: (b, i, 0))
    modv = lambda j: pl.BlockSpec((1, 1, D), lambda b, i: (b, 0, j))
    return pl.pallas_call(
        _merge_ffn_kernel,
        grid=(B, L // tm),
        in_specs=[tok(D), tok(NA_W), tok(GV), tok(GV), tok(W_MG),
                  modv(2), modv(3), modv(4), modv(5),
                  _const_spec((1, D)), _const_spec((1, GLA_DV)),
                  _const_spec((NA_W, D)), _const_spec((GV, D)), _const_spec((D, D)),
                  _const_spec((D, D_FF)), _const_spec((D, D_FF)), _const_spec((D_FF, D))],
        out_specs=tok(D),
        out_shape=jax.ShapeDtypeStruct((B, L, D), f32),
        compiler_params=pltpu.CompilerParams(
            dimension_semantics=("parallel", "parallel"), vmem_limit_bytes=VMEM_LIMIT),
        name="merge_ffn",
    )(x, ona, of, ob, mg, mod3, mod3, mod3, mod3, norm2_g, gg, wna, wgl, wo, wg, wu, wd)


def kernel(x, c, ctx, c_ctx, w_mod, b_mod, norm1_g, norm2_g, w_in, na_q_norm_g, na_k_norm_g, na_rpb,
           gla_w_alpha, gla_b_alpha, gla_norm_g, w_branch_na, w_branch_gla, w_out,
           w_ffn_gate, w_ffn_up, w_ffn_down):
    B, L, _ = x.shape
    assert w_mod.shape[0] == 1 and L % (NA_R * GRID_W) == 0 and L % TM_PROJ == 0

    w_t = _cast_bf16(jnp.transpose(w_in[0]), D_IN // 7)
    w_a = jnp.zeros((N_LR, 2 * GQK), f32)
    w_a = w_a.at[:GLA_RANK, :GQK].set(gla_w_alpha[0, 0]).at[GLA_RANK:, GQK:].set(gla_w_alpha[0, 1])
    w_a = w_a.astype(bf16)
    b_a = gla_b_alpha[0].reshape(1, 2 * GQK)
    gq_t = jnp.tile(na_q_norm_g[0] * (NA_DH ** -0.5 * LOG2E), NA_H).reshape(1, NA_W)
    gk_t = jnp.tile(na_k_norm_g[0], NA_H).reshape(1, NA_W)
    bd = jnp.asarray(np.kron(np.eye(4), np.full((NA_DH, NA_DH), 1.0 / NA_DH)), bf16)

    mod = _adaln_mod(c, c_ctx, w_mod[0], b_mod[0])
    mod3 = mod.reshape(8, 1, 6 * D)
    t2 = _rpb_expand(na_rpb[0])

    kc, vc, s0f, s0b = _ctx_side(ctx, mod3, norm1_g, w_t, w_a, b_a, gk_t, bd, B)
    later_ws = [w_branch_na[0], w_branch_gla[0], w_out[0], w_ffn_gate[0], w_ffn_up[0], w_ffn_down[0]]
    na_qkv, gla_f, gla_b, gates, dec, ktf, ktb, *later_bf = _in_proj(
        x, mod3, norm1_g, w_t, gq_t, gk_t, bd, w_a, b_a, later_ws)
    ona = _na_attn(na_qkv, kc, vc, t2)
    of, ob = _gla_scan(gla_f, gla_b, ktf, ktb, dec, s0f, s0b)
    return _merge_ffn(x, ona, of, ob, gates, mod3, norm2_g, gla_norm_g, *later_bf)
```

```python
import functools

import numpy as np
import jax
import jax.numpy as jnp
from jax import lax
from jax.experimental import pallas as pl
from jax.experimental.pallas import tpu as pltpu

D = 1024
GRID_W = 64
CTX = 256
NA_H = 8
NA_DH = 64
NA_WR = 8
NA_WC = 16
GLA_H = 4
GLA_DK = 64
GLA_DV = 128
GLA_RANK = 16
GLA_TAU = 16.0
GLA_LA_MIN = -1.0
CHUNK = 64
ROPE_THETA = 10000.0
D_FF = 2816
EPS = 1e-6
NEG_INF = -1e30
NA_W = NA_H * NA_DH
GQK = GLA_H * GLA_DK
GV = GLA_H * GLA_DV

OFF_QNA, OFF_KNA, OFF_VNA = 0, 512, 1024
OFF_QG, OFF_KG, OFF_VG, OFF_OG = 1536, 1792, 2048, 2560
OFF_LR, OFF_MA, OFF_MB = 3072, 3104, 4128
N_LR = 2 * GLA_RANK
N_PROJ_OUT = 7
NA_Q, NA_K, NA_V, W_NA = 0, 512, 1024, 1536
G_V, G_Q, G_KI, W_GF = 0, 512, 768, 1024
W_GB = W_GF - GV
MG_A, MG_B, MG_OG, W_MG = 0, 1024, 2048, 2560
D_IN = 5152
LOG2E = 1.4426950408889634

BF16_SUBLANES = 16
VMEM_LIMIT = 56 * 1024 * 1024

TM_PROJ = 512
PROJ_SUB = 256
TM_FFN = 512
FFN_SUB = 256
NA_R = 32
NA_RI = 4
GLA_CB = 32

f32 = jnp.float32
bf16 = jnp.bfloat16


def _const_spec(shape):
    nd = len(shape)
    return pl.BlockSpec(shape, lambda *_: (0,) * nd, pipeline_mode=pl.Buffered(1))


def _dot(a, b):
    return jnp.dot(a, b, preferred_element_type=f32)


def _dot_nt(a, b):
    return lax.dot_general(a, b, (((1,), (1,)), ((), ())), preferred_element_type=f32)


def _split2(x):
    hi = x.astype(bf16)
    lo = (x - hi.astype(f32)).astype(bf16)
    return hi, lo


def _sigmoid(x):
    return 0.5 * jnp.tanh(0.5 * x) + 0.5


def _silu(x):
    return x * _sigmoid(x)


def _split3_dot(r, oh):
    h1 = r.astype(bf16)
    r2 = r - h1.astype(f32)
    h2 = r2.astype(bf16)
    h3 = (r2 - h2.astype(f32)).astype(bf16)
    return _dot(h1, oh) + _dot(h2, oh) + _dot(h3, oh)


def _rpb_expand_kernel(r0_ref, r1_ref, oh0_ref, oh1_ref, mask_ref, o_ref):
    gathered = _split3_dot(r0_ref[...], oh0_ref[...]) + _split3_dot(r1_ref[...], oh1_ref[...])
    o_ref[...] = gathered * LOG2E + mask_ref[...]


def _rpb_expand(rpb):
    nd = 2 * NA_WR - 1
    nrow = NA_H * nd
    ncol = 2 * NA_WC - 1
    nxt = jnp.concatenate([rpb[:, 1:], jnp.zeros_like(rpb[:, :1])], axis=1)
    r0 = jnp.zeros((128, 128), f32).at[:nrow, :ncol].set(rpb.reshape(nrow, ncol))
    r1 = jnp.zeros((128, 128), f32).at[:nrow, :ncol].set(nxt.reshape(nrow, ncol))
    q = np.arange(GRID_W)
    cs = np.clip(q - NA_WC // 2, 0, GRID_W - NA_WC)
    in_win = (q[None, :] >= cs[:, None]) & (q[None, :] < cs[:, None] + NA_WC)
    dc = np.clip(q[None, :] - q[:, None], -(NA_WC - 1), NA_WC - 1) + NA_WC - 1
    oh = np.zeros((2, 128, GRID_W, 128), np.float32)
    qq, kk = np.nonzero(in_win)
    oh[0, dc[qq, kk], qq, kk] = 1.0
    oh[1, dc[qq, kk], qq, kk + GRID_W] = 1.0
    oh = oh.reshape(2, 128, GRID_W * 128)
    mask = np.where(np.concatenate([in_win, in_win], axis=1), 0.0, NEG_INF).reshape(1, GRID_W * 128)
    out = pl.pallas_call(
        _rpb_expand_kernel,
        out_shape=jax.ShapeDtypeStruct((128, GRID_W * 128), f32),
        name="rpb_expand",
    )(r0, r1, jnp.asarray(oh[0], bf16), jnp.asarray(oh[1], bf16), jnp.asarray(mask, f32))
    return out[:nrow].reshape(NA_H, nd, GRID_W, 128)


def _cast_kernel(x_ref, o_ref):
    o_ref[...] = x_ref[...].astype(bf16)


def _cast_bf16(w, rows):
    R, C = w.shape
    return pl.pallas_call(
        _cast_kernel,
        grid=(R // rows,),
        in_specs=[pl.BlockSpec((rows, C), lambda j: (j, 0))],
        out_specs=pl.BlockSpec((rows, C), lambda j: (j, 0)),
        out_shape=jax.ShapeDtypeStruct((R, C), bf16),
        name="cast_bf16",
    )(w)


def _mod_kernel(a_ref, w_ref, b_ref, o_ref):
    a = _silu(a_ref[...]).astype(bf16)
    o_ref[...] = _dot(a, w_ref[...].astype(bf16)) + b_ref[...]


def _adaln_mod(c, c_ctx, w_mod, b_mod):
    B = c.shape[0]
    a = jnp.zeros((8, D), f32).at[:B].set(c).at[B].set(c_ctx)
    n = w_mod.shape[1]
    return pl.pallas_call(
        _mod_kernel,
        grid=(n // D,),
        in_specs=[pl.BlockSpec((8, D), lambda j: (0, 0)),
                  pl.BlockSpec((D, D), lambda j: (0, j)),
                  pl.BlockSpec((1, D), lambda j: (0, j))],
        out_specs=pl.BlockSpec((8, D), lambda j: (0, j)),
        out_shape=jax.ShapeDtypeStruct((8, n), f32),
        name="adaln_mod",
    )(a, w_mod, b_mod.reshape(1, n))


def _norm_mod(x, g, sh, sc):
    ms = jnp.mean(x * x, axis=-1, keepdims=True)
    return (x * lax.rsqrt(ms + EPS)) * (g * (1.0 + sc)) + sh


def _head_rms(t, bd, g):
    sq = (t * t).astype(bf16)
    ms = jnp.concatenate([_dot(sq[:, :256], bd), _dot(sq[:, 256:], bd)], axis=1)
    return t * lax.rsqrt(ms + EPS) * g


def _log_alpha(zl, wa_ref, ba_ref):
    z = _dot(zl.astype(bf16), wa_ref[...]) + ba_ref[...]
    soft = jnp.log2(1.0 + jnp.exp2(z * -LOG2E))
    return jnp.maximum(soft * (-1.0 / GLA_TAU), GLA_LA_MIN * LOG2E)


def _ctx_kernel(ctx_ref, sh_ref, sc_ref, g1_ref, wk_ref, wv_ref, wkg_ref, wvg_ref, wlr_ref,
                wa_ref, ba_ref, gk_ref, bd_ref, su_ref, sl_ref,
                kc_ref, vc_ref, sf_ref, sb_ref):
    h = _norm_mod(ctx_ref[0], g1_ref[...], sh_ref[0], sc_ref[0]).astype(bf16)
    k = _dot_nt(h, wk_ref[...])
    kc_ref[0] = _head_rms(k, bd_ref[...], gk_ref[...]).astype(bf16)
    vc_ref[0] = _dot_nt(h, wv_ref[...]).astype(bf16)
    kg = _dot_nt(h, wkg_ref[...])
    vg = _dot_nt(h, wvg_ref[...]).astype(bf16)
    la = _log_alpha(_dot_nt(h, wlr_ref[...]), wa_ref, ba_ref)
    hi, lo = _split2(la)
    ef = _dot(su_ref[...], hi[:, :GQK]) + _dot(su_ref[...], lo[:, :GQK])
    eb = _dot(sl_ref[...], hi[:, GQK:]) + _dot(sl_ref[...], lo[:, GQK:])
    r = lax.broadcasted_iota(jnp.int32, (GQK, GV), 0) >> 6
    cc = lax.broadcasted_iota(jnp.int32, (GQK, GV), 1) >> 7
    diag = r == cc
    for e, out in ((ef, sf_ref), (eb, sb_ref)):
        kw = (kg * jnp.exp2(e)).astype(bf16)
        u = _dot(kw.T, vg)
        out[0] = jnp.where(diag, u, 0.0)


def _ctx_side(ctx, mod3, norm1_g, w_t, w_a, b_a, gk_t, bd, n_b):
    su = jnp.asarray(np.triu(np.ones((CTX, CTX), np.float32), 1), bf16)
    sl = jnp.asarray(np.tril(np.ones((CTX, CTX), np.float32), -1), bf16)
    B = ctx.shape[0]
    cst = lambda shape, idx: pl.BlockSpec(shape, lambda b: idx)
    return pl.pallas_call(
        _ctx_kernel,
        grid=(B,),
        in_specs=[pl.BlockSpec((1, CTX, D), lambda b: (b, 0, 0)),
                  cst((1, 1, D), (n_b, 0, 0)), cst((1, 1, D), (n_b, 0, 1)),
                  cst((1, D), (0, 0)),
                  cst((NA_W, D), (OFF_KNA // NA_W, 0)), cst((NA_W, D), (OFF_VNA // NA_W, 0)),
                  cst((GQK, D), (OFF_KG // GQK, 0)), cst((GV, D), (OFF_VG // GV, 0)),
                  cst((N_LR, D), (OFF_LR // N_LR, 0)),
                  cst((N_LR, 2 * GQK), (0, 0)), cst((1, 2 * GQK), (0, 0)),
                  cst((1, NA_W), (0, 0)), cst((256, 256), (0, 0)),
                  cst((CTX, CTX), (0, 0)), cst((CTX, CTX), (0, 0))],
        out_specs=[pl.BlockSpec((1, CTX, NA_W), lambda b: (b, 0, 0)),
                   pl.BlockSpec((1, CTX, NA_W), lambda b: (b, 0, 0)),
                   pl.BlockSpec((1, GQK, GV), lambda b: (b, 0, 0)),
                   pl.BlockSpec((1, GQK, GV), lambda b: (b, 0, 0))],
        out_shape=[jax.ShapeDtypeStruct((B, CTX, NA_W), bf16),
                   jax.ShapeDtypeStruct((B, CTX, NA_W), bf16),
                   jax.ShapeDtypeStruct((B, GQK, GV), f32),
                   jax.ShapeDtypeStruct((B, GQK, GV), f32)],
        compiler_params=pltpu.CompilerParams(vmem_limit_bytes=VMEM_LIMIT),
        name="ctx_side",
    )(ctx, mod3, mod3, norm1_g, w_t, w_t, w_t, w_t, w_t, w_a, b_a, gk_t, bd, su, sl)


def _proj_kernel(x_ref, sh_ref, sc_ref, g1_ref, w_ref, gq_ref, gk_ref, bd_ref,
                 rc_ref, rs_ref, cc_ref, cs_ref, wa_ref, ba_ref, tl_ref, *refs):
    n_cast = (len(refs) - N_PROJ_OUT) // 2
    cast_in, refs = refs[:n_cast], refs[n_cast:]
    na_ref, gf_ref, gb_ref, mg_ref, dec_ref, ktf_ref, ktb_ref = refs[:N_PROJ_OUT]
    cast_out = refs[N_PROJ_OUT:]
    for src, dst in zip(cast_in, cast_out):
        dst[...] = src[...].astype(bf16)

    bd = bd_ref[...]
    tl = tl_ref[...]
    nrow = PROJ_SUB // GRID_W
    nchunk = PROJ_SUB // CHUNK
    lane = lax.broadcasted_iota(jnp.int32, (PROJ_SUB, GQK), 1)
    first_half = (lane & 31) < 16

    for hs in range(0, x_ref.shape[1], PROJ_SUB):
        rows = slice(hs, hs + PROJ_SUB)
        hb = _norm_mod(x_ref[0, rows], g1_ref[...], sh_ref[0], sc_ref[0]).astype(bf16)

        def proj(off, n):
            return _dot_nt(hb, w_ref[off:off + n, :])

        zl = proj(OFF_LR, N_LR)
        q_raw = proj(OFF_QNA, NA_W)
        la = _log_alpha(zl, wa_ref, ba_ref)
        k_raw = proj(OFF_KNA, NA_W)
        hi, lo = _split2(la)
        na_ref[0, rows, NA_Q:NA_Q + NA_W] = _head_rms(q_raw, bd, gq_ref[...]).astype(bf16)
        cum = jnp.concatenate(
            [_dot(tl, hi[s:s + 256]) + _dot(tl, lo[s:s + 256]) for s in range(0, PROJ_SUB, 256)], axis=0)
        na_ref[0, rows, NA_K:NA_K + NA_W] = _head_rms(k_raw, bd, gk_ref[...]).astype(bf16)
        qg_raw = proj(OFF_QG, GQK)
        kg_raw = proj(OFF_KG, GQK)
        na_ref[0, rows, NA_V:NA_V + NA_W] = proj(OFF_VNA, NA_W).astype(bf16)
        gf_ref[0, rows, G_V:G_V + GV] = proj(OFF_VG, GV).astype(bf16)

        gr = slice(hs // GRID_W, hs // GRID_W + nrow)
        cos_t = (rc_ref[gr][:, None, :] + cc_ref[...][None]).reshape(PROJ_SUB, GQK)
        sin_t = (rs_ref[gr][:, None, :] + cs_ref[...][None]).reshape(PROJ_SUB, GQK)

        def rope(t):
            partner = jnp.where(first_half, pltpu.roll(t, GQK - 16, 1), pltpu.roll(t, 16, 1))
            return t * cos_t + partner * sin_t

        qg = rope(qg_raw) * (GLA_DK ** -0.5)
        kg = rope(kg_raw)

        cum3 = cum.reshape(nchunk, CHUNK, 2 * GQK)
        tot3 = cum3[:, CHUNK - 1:CHUNK, :]
        la3 = la.reshape(nchunk, CHUNK, 2 * GQK)
        dec_ref[0, hs // CHUNK:hs // CHUNK + nchunk] = jnp.exp2(tot3).reshape(nchunk, 2 * GQK)
        b_f = cum3[:, :, :GQK].reshape(PROJ_SUB, GQK)
        e_f = (tot3 - cum3)[:, :, :GQK].reshape(PROJ_SUB, GQK)
        b_b = (tot3 - cum3 + la3)[:, :, GQK:].reshape(PROJ_SUB, GQK)
        e_b = (cum3 - la3)[:, :, GQK:].reshape(PROJ_SUB, GQK)
        mg_ref[0, rows, MG_OG:MG_OG + GV] = _silu(proj(OFF_OG, GV)).astype(bf16)
        gf_ref[0, rows, G_Q:G_Q + GQK] = (qg * jnp.exp2(b_f)).astype(bf16)
        gf_ref[0, rows, G_KI:G_KI + GQK] = (kg * jnp.exp2(-b_f)).astype(bf16)
        ktf_ref[0, :, rows] = (kg * jnp.exp2(e_f)).T.astype(bf16)
        mg_ref[0, rows, MG_A:MG_A + D] = _sigmoid(proj(OFF_MA, D)).astype(bf16)
        gb_ref[0, rows, G_Q - GV:G_Q - GV + GQK] = (qg * jnp.exp2(b_b)).astype(bf16)
        gb_ref[0, rows, G_KI - GV:G_KI - GV + GQK] = (kg * jnp.exp2(-b_b)).astype(bf16)
        ktb_ref[0, :, rows] = (kg * jnp.exp2(e_b)).T.astype(bf16)
        mg_ref[0, rows, MG_B:MG_B + D] = _sigmoid(proj(OFF_MB, D)).astype(bf16)


def _rope_tables(L):
    n = GLA_DK // 4
    freqs = ROPE_THETA ** (-np.arange(n, dtype=np.float64) / n)
    lane = np.arange(GQK)
    fl = freqs[lane % n]
    is_row = (lane % GLA_DK) < GLA_DK // 2
    sign = np.where((lane % 32) < 16, -1.0, 1.0)
    rows = np.arange(L // GRID_W)[:, None] * fl[None]
    cols = np.arange(GRID_W)[:, None] * fl[None]
    rc = np.where(is_row, np.cos(rows), 0.0)
    rs = np.where(is_row, np.sin(rows) * sign, 0.0)
    cc = np.where(~is_row, np.cos(cols), 0.0)
    cs = np.where(~is_row, np.sin(cols) * sign, 0.0)
    return [jnp.asarray(t, f32) for t in (rc, rs, cc, cs)]


def _in_proj(x, mod3, norm1_g, w_t, gq_t, gk_t, bd, w_a, b_a, cast_ws):
    B, L, _ = x.shape
    tm = TM_PROJ
    rows_per = tm // GRID_W
    nstep = L // tm
    total = B * nstep

    def slab(w):
        R, C = w.shape
        if R % (total * BF16_SUBLANES) == 0:
            return pl.BlockSpec((R // total, C), lambda b, i: (b * nstep + i, 0))
        assert (2 * R) % (total * BF16_SUBLANES) == 0 and C % 256 == 0
        return pl.BlockSpec((2 * R // total, C // 2), lambda b, i: ((b * nstep + i) // 2, (b * nstep + i) % 2))
    rc, rs, cc, cs = _rope_tables(L)
    blk = np.kron(np.eye(256 // CHUNK), np.tril(np.ones((CHUNK, CHUNK)))).astype(np.float32)
    tl = jnp.asarray(blk, bf16)
    tok = lambda n: pl.BlockSpec((1, tm, n), lambda b, i: (b, i, 0))
    sds = lambda n, dt: jax.ShapeDtypeStruct((B, L, n), dt)
    return pl.pallas_call(
        _proj_kernel,
        grid=(B, L // tm),
        in_specs=[tok(D),
                  pl.BlockSpec((1, 1, D), lambda b, i: (b, 0, 0)),
                  pl.BlockSpec((1, 1, D), lambda b, i: (b, 0, 1)),
                  _const_spec((1, D)), _const_spec((D_IN, D)),
                  _const_spec((1, NA_W)), _const_spec((1, NA_W)), _const_spec((256, 256)),
                  pl.BlockSpec((rows_per, GQK), lambda b, i: (i, 0)),
                  pl.BlockSpec((rows_per, GQK), lambda b, i: (i, 0)),
                  _const_spec((GRID_W, GQK)), _const_spec((GRID_W, GQK)),
                  _const_spec((N_LR, 2 * GQK)), _const_spec((1, 2 * GQK)), _const_spec((256, 256))]
                 + [slab(w) for w in cast_ws],
        out_specs=[tok(W_NA), tok(W_GF), tok(W_GB), tok(W_MG),
                   pl.BlockSpec((1, tm // CHUNK, 2 * GQK), lambda b, i: (b, i, 0)),
                   pl.BlockSpec((1, GQK, tm), lambda b, i: (b, 0, i)),
                   pl.BlockSpec((1, GQK, tm), lambda b, i: (b, 0, i))]
                  + [slab(w) for w in cast_ws],
        out_shape=[sds(W_NA, bf16), sds(W_GF, bf16), sds(W_GB, bf16), sds(W_MG, bf16),
                   jax.ShapeDtypeStruct((B, L // CHUNK, 2 * GQK), f32),
                   jax.ShapeDtypeStruct((B, GQK, L), bf16), jax.ShapeDtypeStruct((B, GQK, L), bf16)]
                  + [jax.ShapeDtypeStruct(w.shape, bf16) for w in cast_ws],
        compiler_params=pltpu.CompilerParams(
            dimension_semantics=("parallel", "parallel"), vmem_limit_bytes=VMEM_LIMIT),
        name="in_proj",
    )(x, mod3, mod3, norm1_g, w_t, gq_t, gk_t, bd, rc, rs, cc, cs, w_a, b_a, tl, *cast_ws)


def _lane_blocks(t):
    return [t[:, c:c + 128] for c in range(0, t.shape[1], 128)]


def _na_kernel(q_ref, k_ref, v_ref, kc_ref, vc_ref, t2_ref, o_ref, sl_ref, sc_ref, m_ref):
    i = pl.program_id(1)
    rows = k_ref.shape[1] // GRID_W
    lane_head = lax.broadcasted_iota(jnp.int32, (GRID_W, 256), 1) >> 6
    hmask = [lane_head == h for h in range(4)]

    def window_start(r):
        rs = jnp.clip(r - NA_WR // 2, 0, rows - NA_WR)
        return rs, pl.multiple_of(rs * GRID_W, GRID_W)

    def produce(rr, bank, j):
        r = i * NA_R + rr
        rs, start = window_start(r)
        d0 = rs - r + (NA_WR - 1)
        row0 = pl.multiple_of(rr * GRID_W, GRID_W)
        for g in range(2):
            ls = slice(256 * g, 256 * g + 256)
            q = q_ref[0, pl.ds(row0, GRID_W), ls]
            qs = jnp.concatenate([jnp.where(m, q, jnp.zeros_like(q)) for m in hmask], axis=0)
            kw = k_ref[0, pl.ds(start, NA_WR * GRID_W), ls]
            bias = jnp.concatenate(
                [jnp.concatenate([t2_ref[4 * g + h, d0 + 2 * jj] for jj in range(NA_WR // 2)], axis=1)
                 for h in range(4)], axis=0)
            s_loc = _dot_nt(qs, kw) + bias
            s_ctx = _dot_nt(qs, kc_ref[0, :, ls])
            mx = functools.reduce(jnp.maximum, _lane_blocks(s_loc) + _lane_blocks(s_ctx))
            sl_ref[bank, j, g] = s_loc
            sc_ref[bank, j, g] = s_ctx
            m_ref[bank, j, g] = jnp.broadcast_to(jnp.max(mx, axis=-1, keepdims=True), (4 * GRID_W, 128))

    def consume(rr, bank, j):
        _, start = window_start(i * NA_R + rr)
        row0 = pl.multiple_of(rr * GRID_W, GRID_W)
        outs = []
        for g in range(2):
            ls = slice(256 * g, 256 * g + 256)
            m = m_ref[bank, j, g]
            p_loc = [jnp.exp2(sl_ref[bank, j, g, :, c:c + 128] - m) for c in range(0, NA_WR * GRID_W, 128)]
            p_ctx = [jnp.exp2(sc_ref[bank, j, g, :, c:c + 128] - m) for c in range(0, CTX, 128)]
            den = jnp.sum(functools.reduce(jnp.add, p_loc + p_ctx), axis=-1, keepdims=True)
            vw = v_ref[0, pl.ds(start, NA_WR * GRID_W), ls]
            acc = (_dot(jnp.concatenate(p_loc, axis=1).astype(bf16), vw)
                   + _dot(jnp.concatenate(p_ctx, axis=1).astype(bf16), vc_ref[0, :, ls]))
            acc = acc * (1.0 / den)
            o = jnp.zeros((GRID_W, 256), f32)
            for h in range(4):
                o = o + jnp.where(hmask[h], acc[h * GRID_W:(h + 1) * GRID_W], 0.0)
            outs.append(o)
        o_ref[0, pl.ds(row0, GRID_W), :] = jnp.concatenate(outs, axis=1).astype(bf16)

    for j in range(NA_RI):
        produce(jnp.int32(j), 0, j)

    def step(t, carry):
        bank = t & 1
        for j in range(NA_RI):
            consume(t * NA_RI + j, bank, j)
        for j in range(NA_RI):
            produce(jnp.minimum((t + 1) * NA_RI + j, NA_R - 1), 1 - bank, j)
        return carry

    lax.fori_loop(0, NA_R // NA_RI, step, 0)


def _na_attn(qkv, kc, vc, t2):
    B, L, _ = qkv.shape
    nblk = L // (NA_R * GRID_W)
    return pl.pallas_call(
        _na_kernel,
        grid=(B, nblk),
        in_specs=[pl.BlockSpec((1, NA_R * GRID_W, NA_W), lambda b, i: (b, i, NA_Q // NA_W)),
                  pl.BlockSpec((1, L, NA_W), lambda b, i: (b, 0, NA_K // NA_W), pipeline_mode=pl.Buffered(1)),
                  pl.BlockSpec((1, L, NA_W), lambda b, i: (b, 0, NA_V // NA_W), pipeline_mode=pl.Buffered(1)),
                  pl.BlockSpec((1, CTX, NA_W), lambda b, i: (b, 0, 0)),
                  pl.BlockSpec((1, CTX, NA_W), lambda b, i: (b, 0, 0)),
                  _const_spec((NA_H, 2 * NA_WR - 1, GRID_W, 128))],
        out_specs=pl.BlockSpec((1, NA_R * GRID_W, NA_W), lambda b, i: (b, i, 0)),
        out_shape=jax.ShapeDtypeStruct((B, L, NA_W), bf16),
        scratch_shapes=[pltpu.VMEM((2, NA_RI, 2, 4 * GRID_W, NA_WR * GRID_W), f32),
                        pltpu.VMEM((2, NA_RI, 2, 4 * GRID_W, CTX), f32),
                        pltpu.VMEM((2, NA_RI, 2, 4 * GRID_W, 128), f32)],
        compiler_params=pltpu.CompilerParams(
            dimension_semantics=("parallel", "parallel"), vmem_limit_bytes=VMEM_LIMIT),
        name="na_attn",
    )(qkv, qkv, qkv, kc, vc, t2)


def _gla_chunk(q, ki, v, ket, dcol, s_ref, tri, hmask):
    qs = jnp.concatenate([jnp.where(m, q, jnp.zeros_like(q)) for m in hmask], axis=0)
    att = jnp.where(tri, _dot_nt(qs, ki), 0.0).astype(bf16)
    o = _dot(q, s_ref[...].astype(bf16))
    intra = [_dot(att[h * CHUNK:(h + 1) * CHUNK], v[:, h * GLA_DV:(h + 1) * GLA_DV]) for h in range(GLA_H)]
    o = o + jnp.concatenate(intra, axis=1)
    for h in range(GLA_H):
        rs = slice(h * GLA_DK, (h + 1) * GLA_DK)
        cs = slice(h * GLA_DV, (h + 1) * GLA_DV)
        s_ref[rs, cs] = dcol[rs] * s_ref[rs, cs] + _dot(ket[rs], v[:, cs])
    return o


def _gla_kernel(gf_ref, ktf_ref, decf_ref, gb_ref, vb_ref, ktb_ref, decb_ref,
                s0f_ref, s0b_ref, of_ref, ob_ref, sf_ref, sb_ref):
    @pl.when(pl.program_id(1) == 0)
    def _():
        sf_ref[...] = s0f_ref[0]
        sb_ref[...] = s0b_ref[0]

    lane_head = lax.broadcasted_iota(jnp.int32, (CHUNK, GQK), 1) >> 6
    hmask = [lane_head == h for h in range(GLA_H)]
    ri = lax.broadcasted_iota(jnp.int32, (GLA_H * CHUNK, CHUNK), 0) & (CHUNK - 1)
    ci = lax.broadcasted_iota(jnp.int32, (GLA_H * CHUNK, CHUNK), 1)
    tri_f = ri >= ci
    tri_b = ri <= ci
    dect_f = decf_ref[0][:, :GQK].T
    dect_b = decb_ref[0][:, GQK:].T

    def k_end(kt_ref, c):
        kt = kt_ref[0, :, (c // 2) * 2 * CHUNK:(c // 2 + 1) * 2 * CHUNK]
        if c % 2:
            kt = pltpu.roll(kt, CHUNK, 1)
        return kt[:, :CHUNK]

    for c in range(GLA_CB):
        ts = slice(c * CHUNK, (c + 1) * CHUNK)
        of_ref[0, ts, :] = _gla_chunk(gf_ref[0, ts, G_Q:G_Q + GQK], gf_ref[0, ts, G_KI:G_KI + GQK],
                                      gf_ref[0, ts, G_V:G_V + GV], k_end(ktf_ref, c),
                                      dect_f[:, c:c + 1], sf_ref, tri_f, hmask)
        cb = GLA_CB - 1 - c
        tb = slice(cb * CHUNK, (cb + 1) * CHUNK)
        ob_ref[0, tb, :] = _gla_chunk(gb_ref[0, tb, G_Q - GV:G_Q - GV + GQK], gb_ref[0, tb, G_KI - GV:G_KI - GV + GQK],
                                      vb_ref[0, tb, :], k_end(ktb_ref, cb),
                                      dect_b[:, cb:cb + 1], sb_ref, tri_b, hmask)


def _gla_scan(gf, gb, ktf, ktb, dec, s0f, s0b):
    B, L, _ = gf.shape
    tm = GLA_CB * CHUNK
    nb = L // tm
    fw = lambda n: pl.BlockSpec((1, tm, n), lambda b, i: (b, i, 0))
    bw = lambda n: pl.BlockSpec((1, tm, n), lambda b, i: (b, nb - 1 - i, 0))
    st = pl.BlockSpec((1, GQK, GV), lambda b, i: (b, 0, 0))
    return pl.pallas_call(
        _gla_kernel,
        grid=(B, nb),
        in_specs=[fw(W_GF),
                  pl.BlockSpec((1, GQK, tm), lambda b, i: (b, 0, i)),
                  pl.BlockSpec((1, GLA_CB, 2 * GQK), lambda b, i: (b, i, 0)),
                  bw(W_GB), bw(GV),
                  pl.BlockSpec((1, GQK, tm), lambda b, i: (b, 0, nb - 1 - i)),
                  pl.BlockSpec((1, GLA_CB, 2 * GQK), lambda b, i: (b, nb - 1 - i, 0)),
                  st, st],
        out_specs=[fw(GV), bw(GV)],
        out_shape=[jax.ShapeDtypeStruct((B, L, GV), f32)] * 2,
        scratch_shapes=[pltpu.VMEM((GQK, GV), f32), pltpu.VMEM((GQK, GV), f32)],
        compiler_params=pltpu.CompilerParams(
            dimension_semantics=("parallel", "arbitrary"), vmem_limit_bytes=VMEM_LIMIT),
        name="gla_scan",
    )(gf, ktf, dec, gb, gf, ktb, dec, s0f, s0b)


def _merge_ffn_kernel(x_ref, ona_ref, of_ref, ob_ref, mg_ref,
                      g1_ref, sh2_ref, sc2_ref, g2_ref, n2_ref, gg_ref,
                      wna_ref, wgl_ref, wo_ref, wg_ref, wu_ref, wd_ref, o_ref):
    x1s, h2s = [], []
    for hs in range(0, x_ref.shape[1], FFN_SUB):
        rows = slice(hs, hs + FFN_SUB)
        og = of_ref[0, rows] + ob_ref[0, rows]
        parts = []
        for h in range(GLA_H):
            seg = og[:, h * GLA_DV:(h + 1) * GLA_DV]
            ms = jnp.mean(seg * seg, axis=-1, keepdims=True)
            parts.append(seg * lax.rsqrt(ms + EPS) * gg_ref[...])
        ogl = (jnp.concatenate(parts, axis=1) * mg_ref[0, rows, MG_OG:MG_OG + GV].astype(f32)).astype(bf16)
        y = (mg_ref[0, rows, MG_A:MG_A + D].astype(f32) * _dot(ona_ref[0, rows], wna_ref[...])
             + mg_ref[0, rows, MG_B:MG_B + D].astype(f32) * _dot(ogl, wgl_ref[...]))
        x1 = x_ref[0, rows] + g1_ref[0] * _dot(y.astype(bf16), wo_ref[...])
        x1s.append(x1)
        h2s.append(_norm_mod(x1, n2_ref[...], sh2_ref[0], sc2_ref[0]).astype(bf16))
    x1 = jnp.concatenate(x1s, axis=0)
    h2 = jnp.concatenate(h2s, axis=0)
    act = (_silu(_dot(h2, wg_ref[...])) * _dot(h2, wu_ref[...])).astype(bf16)
    o_ref[0] = x1 + g2_ref[0] * _dot(act, wd_ref[...])


def _merge_ffn(x, ona, of, ob, mg, mod3, norm2_g, gg, wna, wgl, wo, wg, wu, wd):
    B, L, _ = x.shape
    tm = TM_FFN
    tok = lambda n: pl.BlockSpec((1, tm, n), lambda b, i: (b, i, 0))
    modv = lambda j: pl.BlockSpec((1, 1, D), lambda b, i: (b, 0, j))
    return pl.pallas_call(
        _merge_ffn_kernel,
        grid=(B, L // tm),
        in_specs=[tok(D), tok(NA_W), tok(GV), tok(GV), tok(W_MG),
                  modv(2), modv(3), modv(4), modv(5),
                  _const_spec((1, D)), _const_spec((1, GLA_DV)),
                  _const_spec((NA_W, D)), _const_spec((GV, D)), _const_spec((D, D)),
                  _const_spec((D, D_FF)), _const_spec((D, D_FF)), _const_spec((D_FF, D))],
        out_specs=tok(D),
        out_shape=jax.ShapeDtypeStruct((B, L, D), f32),
        compiler_params=pltpu.CompilerParams(
            dimension_semantics=("parallel", "parallel"), vmem_limit_bytes=VMEM_LIMIT),
        name="merge_ffn",
    )(x, ona, of, ob, mg, mod3, mod3, mod3, mod3, norm2_g, gg, wna, wgl, wo, wg, wu, wd)


def kernel(x, c, ctx, c_ctx, w_mod, b_mod, norm1_g, norm2_g, w_in, na_q_norm_g, na_k_norm_g, na_rpb,
           gla_w_alpha, gla_b_alpha, gla_norm_g, w_branch_na, w_branch_gla, w_out,
           w_ffn_gate, w_ffn_up, w_ffn_down):
    B, L, _ = x.shape
    assert w_mod.shape[0] == 1 and L % (NA_R * GRID_W) == 0 and L % TM_PROJ == 0

    w_t = _cast_bf16(jnp.transpose(w_in[0]), D_IN // 7)
    w_a = jnp.zeros((N_LR, 2 * GQK), f32)
    w_a = w_a.at[:GLA_RANK, :GQK].set(gla_w_alpha[0, 0]).at[GLA_RANK:, GQK:].set(gla_w_alpha[0, 1])
    w_a = w_a.astype(bf16)
    b_a = gla_b_alpha[0].reshape(1, 2 * GQK)
    gq_t = jnp.tile(na_q_norm_g[0] * (NA_DH ** -0.5 * LOG2E), NA_H).reshape(1, NA_W)
    gk_t = jnp.tile(na_k_norm_g[0], NA_H).reshape(1, NA_W)
    bd = jnp.asarray(np.kron(np.eye(4), np.full((NA_DH, NA_DH), 1.0 / NA_DH)), bf16)

    mod = _adaln_mod(c, c_ctx, w_mod[0], b_mod[0])
    mod3 = mod.reshape(8, 1, 6 * D)
    t2 = _rpb_expand(na_rpb[0])

    kc, vc, s0f, s0b = _ctx_side(ctx, mod3, norm1_g, w_t, w_a, b_a, gk_t, bd, B)
    later_ws = [w_branch_na[0], w_branch_gla[0], w_out[0], w_ffn_gate[0], w_ffn_up[0], w_ffn_down[0]]
    na_qkv, gla_f, gla_b, gates, dec, ktf, ktb, *later_bf = _in_proj(
        x, mod3, norm1_g, w_t, gq_t, gk_t, bd, w_a, b_a, later_ws)
    ona = _na_attn(na_qkv, kc, vc, t2)
    of, ob = _gla_scan(gla_f, gla_b, ktf, ktb, dec, s0f, s0b)
    return _merge_ffn(x, ona, of, ob, gates, mod3, norm2_g, gla_norm_g, *later_bf)
```

```python
import functools

import numpy as np
import jax
import jax.numpy as jnp
from jax import lax
from jax.experimental import pallas as pl
from jax.experimental.pallas import tpu as pltpu

D = 1024
GRID_W = 64
CTX = 256
NA_H = 8
NA_DH = 64
NA_WR = 8
NA_WC = 16
GLA_H = 4
GLA_DK = 64
GLA_DV = 128
GLA_RANK = 16
GLA_TAU = 16.0
GLA_LA_MIN = -1.0
CHUNK = 64
ROPE_THETA = 10000.0
D_FF = 2816
EPS = 1e-6
NEG_INF = -1e30
NA_W = NA_H * NA_DH
GQK = GLA_H * GLA_DK
GV = GLA_H * GLA_DV

OFF_QNA, OFF_KNA, OFF_VNA = 0, 512, 1024
OFF_QG, OFF_KG, OFF_VG, OFF_OG = 1536, 1792, 2048, 2560
OFF_LR, OFF_MA, OFF_MB = 3072, 3104, 4128
N_LR = 2 * GLA_RANK
N_PROJ_OUT = 7
NA_Q, NA_K, NA_V, W_NA = 0, 512, 1024, 1536
G_V, G_Q, G_KI, W_GF = 0, 512, 768, 1024
W_GB = W_GF - GV
MG_A, MG_B, MG_OG, W_MG = 0, 1024, 2048, 2560
D_IN = 5152
LOG2E = 1.4426950408889634

BF16_SUBLANES = 16
VMEM_LIMIT = 56 * 1024 * 1024

TM_PROJ = 512
PROJ_SUB = 256
TM_FFN = 512
FFN_SUB = 256
NA_R = 32
NA_RI = 4
GLA_CB = 32

f32 = jnp.float32
bf16 = jnp.bfloat16


def _const_spec(shape):
    nd = len(shape)
    return pl.BlockSpec(shape, lambda *_: (0,) * nd, pipeline_mode=pl.Buffered(1))


def _dot(a, b):
    return jnp.dot(a, b, preferred_element_type=f32)


def _dot_nt(a, b):
    return lax.dot_general(a, b, (((1,), (1,)), ((), ())), preferred_element_type=f32)


def _split2(x):
    hi = x.astype(bf16)
    lo = (x - hi.astype(f32)).astype(bf16)
    return hi, lo


def _sigmoid(x):
    return 0.5 * jnp.tanh(0.5 * x) + 0.5


def _silu(x):
    return x * _sigmoid(x)


def _sigmoid_of_half(xh):
    return 0.5 * jnp.tanh(xh) + 0.5


def _silu_of_half(xh):
    return xh * (jnp.tanh(xh) + 1.0)


def _split3_dot(r, oh):
    h1 = r.astype(bf16)
    r2 = r - h1.astype(f32)
    h2 = r2.astype(bf16)
    h3 = (r2 - h2.astype(f32)).astype(bf16)
    return _dot(h1, oh) + _dot(h2, oh) + _dot(h3, oh)


def _rpb_expand_kernel(r0_ref, r1_ref, oh0_ref, oh1_ref, mask_ref, o_ref):
    gathered = _split3_dot(r0_ref[...], oh0_ref[...]) + _split3_dot(r1_ref[...], oh1_ref[...])
    o_ref[...] = gathered * LOG2E + mask_ref[...]


def _rpb_expand(rpb):
    nd = 2 * NA_WR - 1
    nrow = NA_H * nd
    ncol = 2 * NA_WC - 1
    nxt = jnp.concatenate([rpb[:, 1:], jnp.zeros_like(rpb[:, :1])], axis=1)
    r0 = jnp.zeros((128, 128), f32).at[:nrow, :ncol].set(rpb.reshape(nrow, ncol))
    r1 = jnp.zeros((128, 128), f32).at[:nrow, :ncol].set(nxt.reshape(nrow, ncol))
    q = np.arange(GRID_W)
    cs = np.clip(q - NA_WC // 2, 0, GRID_W - NA_WC)
    in_win = (q[None, :] >= cs[:, None]) & (q[None, :] < cs[:, None] + NA_WC)
    dc = np.clip(q[None, :] - q[:, None], -(NA_WC - 1), NA_WC - 1) + NA_WC - 1
    oh = np.zeros((2, 128, GRID_W, 128), np.float32)
    qq, kk = np.nonzero(in_win)
    oh[0, dc[qq, kk], qq, kk] = 1.0
    oh[1, dc[qq, kk], qq, kk + GRID_W] = 1.0
    oh = oh.reshape(2, 128, GRID_W * 128)
    mask = np.where(np.concatenate([in_win, in_win], axis=1), 0.0, NEG_INF).reshape(1, GRID_W * 128)
    out = pl.pallas_call(
        _rpb_expand_kernel,
        out_shape=jax.ShapeDtypeStruct((128, GRID_W * 128), f32),
        name="rpb_expand",
    )(r0, r1, jnp.asarray(oh[0], bf16), jnp.asarray(oh[1], bf16), jnp.asarray(mask, f32))
    return out[:nrow].reshape(NA_H, nd, GRID_W, 128)


def _cast_w_in_kernel(x_ref, o_ref):
    rows = x_ref.shape[0]
    r = pl.program_id(0) * rows + lax.broadcasted_iota(jnp.int32, (rows, 1), 0)
    gate_row = ((r >= OFF_OG) & (r < OFF_LR)) | (r >= OFF_MA)
    o_ref[...] = (x_ref[...] * jnp.where(gate_row, 0.5, 1.0)).astype(bf16)


def _cast_w_in(w, rows):
    R, C = w.shape
    return pl.pallas_call(
        _cast_w_in_kernel,
        grid=(R // rows,),
        in_specs=[pl.BlockSpec((rows, C), lambda j: (j, 0))],
        out_specs=pl.BlockSpec((rows, C), lambda j: (j, 0)),
        out_shape=jax.ShapeDtypeStruct((R, C), bf16),
        name="cast_w_in",
    )(w)


def _mod_kernel(a_ref, w_ref, b_ref, o_ref):
    a = _silu(a_ref[...]).astype(bf16)
    o_ref[...] = _dot(a, w_ref[...].astype(bf16)) + b_ref[...]


def _adaln_mod(c, c_ctx, w_mod, b_mod):
    B = c.shape[0]
    a = jnp.zeros((8, D), f32).at[:B].set(c).at[B].set(c_ctx)
    n = w_mod.shape[1]
    return pl.pallas_call(
        _mod_kernel,
        grid=(n // D,),
        in_specs=[pl.BlockSpec((8, D), lambda j: (0, 0)),
                  pl.BlockSpec((D, D), lambda j: (0, j)),
                  pl.BlockSpec((1, D), lambda j: (0, j))],
        out_specs=pl.BlockSpec((8, D), lambda j: (0, j)),
        out_shape=jax.ShapeDtypeStruct((8, n), f32),
        name="adaln_mod",
    )(a, w_mod, b_mod.reshape(1, n))


def _norm_mod(x, g, sh, sc):
    ms = jnp.mean(x * x, axis=-1, keepdims=True)
    return (x * lax.rsqrt(ms + EPS)) * (g * (1.0 + sc)) + sh


def _head_rms(t, bd, g):
    sq = (t * t).astype(bf16)
    ms = jnp.concatenate([_dot(sq[:, :256], bd), _dot(sq[:, 256:], bd)], axis=1)
    return t * lax.rsqrt(ms + EPS) * g


def _log_alpha(zl, wa_ref, ba_ref):
    z = _dot(zl.astype(bf16), wa_ref[...]) + ba_ref[...]
    soft = jnp.log2(1.0 + jnp.exp2(z * -LOG2E))
    return jnp.maximum(soft * (-1.0 / GLA_TAU), GLA_LA_MIN * LOG2E)


def _ctx_kernel(ctx_ref, sh_ref, sc_ref, g1_ref, wk_ref, wv_ref, wkg_ref, wvg_ref, wlr_ref,
                wa_ref, ba_ref, gk_ref, bd_ref, su_ref, sl_ref,
                kc_ref, vc_ref, sf_ref, sb_ref):
    h = _norm_mod(ctx_ref[0], g1_ref[...], sh_ref[0], sc_ref[0]).astype(bf16)
    k = _dot_nt(h, wk_ref[...])
    kc_ref[0] = _head_rms(k, bd_ref[...], gk_ref[...]).astype(bf16)
    vc_ref[0] = _dot_nt(h, wv_ref[...]).astype(bf16)
    kg = _dot_nt(h, wkg_ref[...])
    vg = _dot_nt(h, wvg_ref[...]).astype(bf16)
    la = _log_alpha(_dot_nt(h, wlr_ref[...]), wa_ref, ba_ref)
    hi, lo = _split2(la)
    ef = _dot(su_ref[...], hi[:, :GQK]) + _dot(su_ref[...], lo[:, :GQK])
    eb = _dot(sl_ref[...], hi[:, GQK:]) + _dot(sl_ref[...], lo[:, GQK:])
    r = lax.broadcasted_iota(jnp.int32, (GQK, GV), 0) >> 6
    cc = lax.broadcasted_iota(jnp.int32, (GQK, GV), 1) >> 7
    diag = r == cc
    for e, out in ((ef, sf_ref), (eb, sb_ref)):
        kw = (kg * jnp.exp2(e)).astype(bf16)
        u = _dot(kw.T, vg)
        out[0] = jnp.where(diag, u, 0.0)


def _ctx_side(ctx, mod3, norm1_g, w_t, w_a, b_a, gk_t, bd, n_b):
    su = jnp.asarray(np.triu(np.ones((CTX, CTX), np.float32), 1), bf16)
    sl = jnp.asarray(np.tril(np.ones((CTX, CTX), np.float32), -1), bf16)
    B = ctx.shape[0]
    cst = lambda shape, idx: pl.BlockSpec(shape, lambda b: idx)
    return pl.pallas_call(
        _ctx_kernel,
        grid=(B,),
        in_specs=[pl.BlockSpec((1, CTX, D), lambda b: (b, 0, 0)),
                  cst((1, 1, D), (n_b, 0, 0)), cst((1, 1, D), (n_b, 0, 1)),
                  cst((1, D), (0, 0)),
                  cst((NA_W, D), (OFF_KNA // NA_W, 0)), cst((NA_W, D), (OFF_VNA // NA_W, 0)),
                  cst((GQK, D), (OFF_KG // GQK, 0)), cst((GV, D), (OFF_VG // GV, 0)),
                  cst((N_LR, D), (OFF_LR // N_LR, 0)),
                  cst((N_LR, 2 * GQK), (0, 0)), cst((1, 2 * GQK), (0, 0)),
                  cst((1, NA_W), (0, 0)), cst((256, 256), (0, 0)),
                  cst((CTX, CTX), (0, 0)), cst((CTX, CTX), (0, 0))],
        out_specs=[pl.BlockSpec((1, CTX, NA_W), lambda b: (b, 0, 0)),
                   pl.BlockSpec((1, CTX, NA_W), lambda b: (b, 0, 0)),
                   pl.BlockSpec((1, GQK, GV), lambda b: (b, 0, 0)),
                   pl.BlockSpec((1, GQK, GV), lambda b: (b, 0, 0))],
        out_shape=[jax.ShapeDtypeStruct((B, CTX, NA_W), bf16),
                   jax.ShapeDtypeStruct((B, CTX, NA_W), bf16),
                   jax.ShapeDtypeStruct((B, GQK, GV), f32),
                   jax.ShapeDtypeStruct((B, GQK, GV), f32)],
        compiler_params=pltpu.CompilerParams(vmem_limit_bytes=VMEM_LIMIT),
        name="ctx_side",
    )(ctx, mod3, mod3, norm1_g, w_t, w_t, w_t, w_t, w_t, w_a, b_a, gk_t, bd, su, sl)


def _proj_kernel(x_ref, sh_ref, sc_ref, g1_ref, w_ref, gq_ref, gk_ref, bd_ref,
                 rc_ref, rs_ref, cc_ref, cs_ref, wa_ref, ba_ref, tl_ref, *refs):
    n_cast = (len(refs) - N_PROJ_OUT) // 2
    cast_in, refs = refs[:n_cast], refs[n_cast:]
    na_ref, gf_ref, gb_ref, mg_ref, dec_ref, ktf_ref, ktb_ref = refs[:N_PROJ_OUT]
    cast_out = refs[N_PROJ_OUT:]
    for src, dst in zip(cast_in, cast_out):
        dst[...] = src[...].astype(bf16)

    bd = bd_ref[...]
    tl = tl_ref[...]
    nrow = PROJ_SUB // GRID_W
    nchunk = PROJ_SUB // CHUNK
    lane = lax.broadcasted_iota(jnp.int32, (PROJ_SUB, GQK), 1)
    first_half = (lane & 31) < 16

    for hs in range(0, x_ref.shape[1], PROJ_SUB):
        rows = slice(hs, hs + PROJ_SUB)
        hb = _norm_mod(x_ref[0, rows], g1_ref[...], sh_ref[0], sc_ref[0]).astype(bf16)

        def proj(off, n):
            return _dot_nt(hb, w_ref[off:off + n, :])

        zl = proj(OFF_LR, N_LR)
        q_raw = proj(OFF_QNA, NA_W)
        la = _log_alpha(zl, wa_ref, ba_ref)
        k_raw = proj(OFF_KNA, NA_W)
        hi, lo = _split2(la)
        na_ref[0, rows, NA_Q:NA_Q + NA_W] = _head_rms(q_raw, bd, gq_ref[...]).astype(bf16)
        cum = jnp.concatenate(
            [_dot(tl, hi[s:s + 256]) + _dot(tl, lo[s:s + 256]) for s in range(0, PROJ_SUB, 256)], axis=0)
        na_ref[0, rows, NA_K:NA_K + NA_W] = _head_rms(k_raw, bd, gk_ref[...]).astype(bf16)
        qg_raw = proj(OFF_QG, GQK)
        kg_raw = proj(OFF_KG, GQK)
        na_ref[0, rows, NA_V:NA_V + NA_W] = proj(OFF_VNA, NA_W).astype(bf16)
        gf_ref[0, rows, G_V:G_V + GV] = proj(OFF_VG, GV).astype(bf16)

        gr = slice(hs // GRID_W, hs // GRID_W + nrow)
        cos_t = (rc_ref[gr][:, None, :] + cc_ref[...][None]).reshape(PROJ_SUB, GQK)
        sin_t = (rs_ref[gr][:, None, :] + cs_ref[...][None]).reshape(PROJ_SUB, GQK)

        def rope(t):
            partner = jnp.where(first_half, pltpu.roll(t, GQK - 16, 1), pltpu.roll(t, 16, 1))
            return t * cos_t + partner * sin_t

        qg = rope(qg_raw) * (GLA_DK ** -0.5)
        kg = rope(kg_raw)

        cum3 = cum.reshape(nchunk, CHUNK, 2 * GQK)
        tot3 = cum3[:, CHUNK - 1:CHUNK, :]
        la3 = la.reshape(nchunk, CHUNK, 2 * GQK)
        dec_ref[0, hs // CHUNK:hs // CHUNK + nchunk] = jnp.exp2(tot3).reshape(nchunk, 2 * GQK)
        b_f = cum3[:, :, :GQK].reshape(PROJ_SUB, GQK)
        e_f = (tot3 - cum3)[:, :, :GQK].reshape(PROJ_SUB, GQK)
        b_b = (tot3 - cum3 + la3)[:, :, GQK:].reshape(PROJ_SUB, GQK)
        e_b = (cum3 - la3)[:, :, GQK:].reshape(PROJ_SUB, GQK)
        mg_ref[0, rows, MG_OG:MG_OG + GV] = _silu_of_half(proj(OFF_OG, GV)).astype(bf16)
        gf_ref[0, rows, G_Q:G_Q + GQK] = (qg * jnp.exp2(b_f)).astype(bf16)
        gf_ref[0, rows, G_KI:G_KI + GQK] = (kg * jnp.exp2(-b_f)).astype(bf16)
        ktf_ref[0, :, rows] = (kg * jnp.exp2(e_f)).T.astype(bf16)
        mg_ref[0, rows, MG_A:MG_A + D] = _sigmoid_of_half(proj(OFF_MA, D)).astype(bf16)
        gb_ref[0, rows, G_Q - GV:G_Q - GV + GQK] = (qg * jnp.exp2(b_b)).astype(bf16)
        gb_ref[0, rows, G_KI - GV:G_KI - GV + GQK] = (kg * jnp.exp2(-b_b)).astype(bf16)
        ktb_ref[0, :, rows] = (kg * jnp.exp2(e_b)).T.astype(bf16)
        mg_ref[0, rows, MG_B:MG_B + D] = _sigmoid_of_half(proj(OFF_MB, D)).astype(bf16)


def _rope_tables(L):
    n = GLA_DK // 4
    freqs = ROPE_THETA ** (-np.arange(n, dtype=np.float64) / n)
    lane = np.arange(GQK)
    fl = freqs[lane % n]
    is_row = (lane % GLA_DK) < GLA_DK // 2
    sign = np.where((lane % 32) < 16, -1.0, 1.0)
    rows = np.arange(L // GRID_W)[:, None] * fl[None]
    cols = np.arange(GRID_W)[:, None] * fl[None]
    rc = np.where(is_row, np.cos(rows), 0.0)
    rs = np.where(is_row, np.sin(rows) * sign, 0.0)
    cc = np.where(~is_row, np.cos(cols), 0.0)
    cs = np.where(~is_row, np.sin(cols) * sign, 0.0)
    return [jnp.asarray(t, f32) for t in (rc, rs, cc, cs)]


def _in_proj(x, mod3, norm1_g, w_t, gq_t, gk_t, bd, w_a, b_a, cast_ws):
    B, L, _ = x.shape
    tm = TM_PROJ
    rows_per = tm // GRID_W
    nstep = L // tm
    total = B * nstep

    def slab(w):
        R, C = w.shape
        if R % (total * BF16_SUBLANES) == 0:
            return pl.BlockSpec((R // total, C), lambda b, i: (b * nstep + i, 0))
        assert (2 * R) % (total * BF16_SUBLANES) == 0 and C % 256 == 0
        return pl.BlockSpec((2 * R // total, C // 2), lambda b, i: ((b * nstep + i) // 2, (b * nstep + i) % 2))
    rc, rs, cc, cs = _rope_tables(L)
    blk = np.kron(np.eye(256 // CHUNK), np.tril(np.ones((CHUNK, CHUNK)))).astype(np.float32)
    tl = jnp.asarray(blk, bf16)
    tok = lambda n: pl.BlockSpec((1, tm, n), lambda b, i: (b, i, 0))
    sds = lambda n, dt: jax.ShapeDtypeStruct((B, L, n), dt)
    return pl.pallas_call(
        _proj_kernel,
        grid=(B, L // tm),
        in_specs=[tok(D),
                  pl.BlockSpec((1, 1, D), lambda b, i: (b, 0, 0)),
                  pl.BlockSpec((1, 1, D), lambda b, i: (b, 0, 1)),
                  _const_spec((1, D)), _const_spec((D_IN, D)),
                  _const_spec((1, NA_W)), _const_spec((1, NA_W)), _const_spec((256, 256)),
                  pl.BlockSpec((rows_per, GQK), lambda b, i: (i, 0)),
                  pl.BlockSpec((rows_per, GQK), lambda b, i: (i, 0)),
                  _const_spec((GRID_W, GQK)), _const_spec((GRID_W, GQK)),
                  _const_spec((N_LR, 2 * GQK)), _const_spec((1, 2 * GQK)), _const_spec((256, 256))]
                 + [slab(w) for w in cast_ws],
        out_specs=[tok(W_NA), tok(W_GF), tok(W_GB), tok(W_MG),
                   pl.BlockSpec((1, tm // CHUNK, 2 * GQK), lambda b, i: (b, i, 0)),
                   pl.BlockSpec((1, GQK, tm), lambda b, i: (b, 0, i)),
                   pl.BlockSpec((1, GQK, tm), lambda b, i: (b, 0, i))]
                  + [slab(w) for w in cast_ws],
        out_shape=[sds(W_NA, bf16), sds(W_GF, bf16), sds(W_GB, bf16), sds(W_MG, bf16),
                   jax.ShapeDtypeStruct((B, L // CHUNK, 2 * GQK), f32),
                   jax.ShapeDtypeStruct((B, GQK, L), bf16), jax.ShapeDtypeStruct((B, GQK, L), bf16)]
                  + [jax.ShapeDtypeStruct(w.shape, bf16) for w in cast_ws],
        compiler_params=pltpu.CompilerParams(
            dimension_semantics=("parallel", "parallel"), vmem_limit_bytes=VMEM_LIMIT),
        name="in_proj",
    )(x, mod3, mod3, norm1_g, w_t, gq_t, gk_t, bd, rc, rs, cc, cs, w_a, b_a, tl, *cast_ws)


def _lane_blocks(t):
    return [t[:, c:c + 128] for c in range(0, t.shape[1], 128)]


def _na_kernel(q_ref, k_ref, v_ref, kc_ref, vc_ref, t2_ref, o_ref, sl_ref, sc_ref, m_ref):
    i = pl.program_id(1)
    rows = k_ref.shape[1] // GRID_W
    lane_head = lax.broadcasted_iota(jnp.int32, (GRID_W, 256), 1) >> 6
    hmask = [lane_head == h for h in range(4)]

    def window_start(r):
        rs = jnp.clip(r - NA_WR // 2, 0, rows - NA_WR)
        return rs, pl.multiple_of(rs * GRID_W, GRID_W)

    def produce(rr, bank, j):
        r = i * NA_R + rr
        rs, start = window_start(r)
        d0 = rs - r + (NA_WR - 1)
        row0 = pl.multiple_of(rr * GRID_W, GRID_W)
        for g in range(2):
            ls = slice(256 * g, 256 * g + 256)
            q = q_ref[0, pl.ds(row0, GRID_W), ls]
            qs = jnp.concatenate([jnp.where(m, q, jnp.zeros_like(q)) for m in hmask], axis=0)
            kw = k_ref[0, pl.ds(start, NA_WR * GRID_W), ls]
            bias = jnp.concatenate(
                [jnp.concatenate([t2_ref[4 * g + h, d0 + 2 * jj] for jj in range(NA_WR // 2)], axis=1)
                 for h in range(4)], axis=0)
            s_loc = _dot_nt(qs, kw) + bias
            s_ctx = _dot_nt(qs, kc_ref[0, :, ls])
            mx = functools.reduce(jnp.maximum, _lane_blocks(s_loc) + _lane_blocks(s_ctx))
            sl_ref[bank, j, g] = s_loc
            sc_ref[bank, j, g] = s_ctx
            m_ref[bank, j, g] = jnp.broadcast_to(jnp.max(mx, axis=-1, keepdims=True), (4 * GRID_W, 128))

    def consume(rr, bank, j):
        _, start = window_start(i * NA_R + rr)
        row0 = pl.multiple_of(rr * GRID_W, GRID_W)
        outs = []
        for g in range(2):
            ls = slice(256 * g, 256 * g + 256)
            m = m_ref[bank, j, g]
            p_loc = [jnp.exp2(sl_ref[bank, j, g, :, c:c + 128] - m) for c in range(0, NA_WR * GRID_W, 128)]
            p_ctx = [jnp.exp2(sc_ref[bank, j, g, :, c:c + 128] - m) for c in range(0, CTX, 128)]
            den = jnp.sum(functools.reduce(jnp.add, p_loc + p_ctx), axis=-1, keepdims=True)
            vw = v_ref[0, pl.ds(start, NA_WR * GRID_W), ls]
            acc = (_dot(jnp.concatenate(p_loc, axis=1).astype(bf16), vw)
                   + _dot(jnp.concatenate(p_ctx, axis=1).astype(bf16), vc_ref[0, :, ls]))
            acc = acc * (1.0 / den)
            o = jnp.zeros((GRID_W, 256), f32)
            for h in range(4):
                o = o + jnp.where(hmask[h], acc[h * GRID_W:(h + 1) * GRID_W], 0.0)
            outs.append(o)
        o_ref[0, pl.ds(row0, GRID_W), :] = jnp.concatenate(outs, axis=1).astype(bf16)

    for j in range(NA_RI):
        produce(jnp.int32(j), 0, j)

    def step(t, carry):
        bank = t & 1
        for j in range(NA_RI):
            consume(t * NA_RI + j, bank, j)
        for j in range(NA_RI):
            produce(jnp.minimum((t + 1) * NA_RI + j, NA_R - 1), 1 - bank, j)
        return carry

    lax.fori_loop(0, NA_R // NA_RI, step, 0)


def _na_attn(qkv, kc, vc, t2):
    B, L, _ = qkv.shape
    nblk = L // (NA_R * GRID_W)
    return pl.pallas_call(
        _na_kernel,
        grid=(B, nblk),
        in_specs=[pl.BlockSpec((1, NA_R * GRID_W, NA_W), lambda b, i: (b, i, NA_Q // NA_W)),
                  pl.BlockSpec((1, L, NA_W), lambda b, i: (b, 0, NA_K // NA_W), pipeline_mode=pl.Buffered(1)),
                  pl.BlockSpec((1, L, NA_W), lambda b, i: (b, 0, NA_V // NA_W), pipeline_mode=pl.Buffered(1)),
                  pl.BlockSpec((1, CTX, NA_W), lambda b, i: (b, 0, 0)),
                  pl.BlockSpec((1, CTX, NA_W), lambda b, i: (b, 0, 0)),
                  _const_spec((NA_H, 2 * NA_WR - 1, GRID_W, 128))],
        out_specs=pl.BlockSpec((1, NA_R * GRID_W, NA_W), lambda b, i: (b, i, 0)),
        out_shape=jax.ShapeDtypeStruct((B, L, NA_W), bf16),
        scratch_shapes=[pltpu.VMEM((2, NA_RI, 2, 4 * GRID_W, NA_WR * GRID_W), f32),
                        pltpu.VMEM((2, NA_RI, 2, 4 * GRID_W, CTX), f32),
                        pltpu.VMEM((2, NA_RI, 2, 4 * GRID_W, 128), f32)],
        compiler_params=pltpu.CompilerParams(
            dimension_semantics=("parallel", "parallel"), vmem_limit_bytes=VMEM_LIMIT),
        name="na_attn",
    )(qkv, qkv, qkv, kc, vc, t2)


def _gla_chunk(q, ki, v, ket, dcol, s_ref, tri, hmask):
    qs = jnp.concatenate([jnp.where(m, q, jnp.zeros_like(q)) for m in hmask], axis=0)
    att = jnp.where(tri, _dot_nt(qs, ki), 0.0).astype(bf16)
    o = _dot(q, s_ref[...].astype(bf16))
    intra = [_dot(att[h * CHUNK:(h + 1) * CHUNK], v[:, h * GLA_DV:(h + 1) * GLA_DV]) for h in range(GLA_H)]
    o = o + jnp.concatenate(intra, axis=1)
    for h in range(GLA_H):
        rs = slice(h * GLA_DK, (h + 1) * GLA_DK)
        cs = slice(h * GLA_DV, (h + 1) * GLA_DV)
        s_ref[rs, cs] = dcol[rs] * s_ref[rs, cs] + _dot(ket[rs], v[:, cs])
    return o


def _gla_kernel(gf_ref, ktf_ref, decf_ref, gb_ref, vb_ref, ktb_ref, decb_ref,
                s0f_ref, s0b_ref, of_ref, ob_ref, sf_ref, sb_ref):
    @pl.when(pl.program_id(1) == 0)
    def _():
        sf_ref[...] = s0f_ref[0]
        sb_ref[...] = s0b_ref[0]

    lane_head = lax.broadcasted_iota(jnp.int32, (CHUNK, GQK), 1) >> 6
    hmask = [lane_head == h for h in range(GLA_H)]
    ri = lax.broadcasted_iota(jnp.int32, (GLA_H * CHUNK, CHUNK), 0) & (CHUNK - 1)
    ci = lax.broadcasted_iota(jnp.int32, (GLA_H * CHUNK, CHUNK), 1)
    tri_f = ri >= ci
    tri_b = ri <= ci
    dect_f = decf_ref[0][:, :GQK].T
    dect_b = decb_ref[0][:, GQK:].T

    def k_end(kt_ref, c):
        kt = kt_ref[0, :, (c // 2) * 2 * CHUNK:(c // 2 + 1) * 2 * CHUNK]
        if c % 2:
            kt = pltpu.roll(kt, CHUNK, 1)
        return kt[:, :CHUNK]

    for c in range(GLA_CB):
        ts = slice(c * CHUNK, (c + 1) * CHUNK)
        of_ref[0, ts, :] = _gla_chunk(gf_ref[0, ts, G_Q:G_Q + GQK], gf_ref[0, ts, G_KI:G_KI + GQK],
                                      gf_ref[0, ts, G_V:G_V + GV], k_end(ktf_ref, c),
                                      dect_f[:, c:c + 1], sf_ref, tri_f, hmask)
        cb = GLA_CB - 1 - c
        tb = slice(cb * CHUNK, (cb + 1) * CHUNK)
        ob_ref[0, tb, :] = _gla_chunk(gb_ref[0, tb, G_Q - GV:G_Q - GV + GQK], gb_ref[0, tb, G_KI - GV:G_KI - GV + GQK],
                                      vb_ref[0, tb, :], k_end(ktb_ref, cb),
                                      dect_b[:, cb:cb + 1], sb_ref, tri_b, hmask)


def _gla_scan(gf, gb, ktf, ktb, dec, s0f, s0b):
    B, L, _ = gf.shape
    tm = GLA_CB * CHUNK
    nb = L // tm
    fw = lambda n: pl.BlockSpec((1, tm, n), lambda b, i: (b, i, 0))
    bw = lambda n: pl.BlockSpec((1, tm, n), lambda b, i: (b, nb - 1 - i, 0))
    st = pl.BlockSpec((1, GQK, GV), lambda b, i: (b, 0, 0))
    return pl.pallas_call(
        _gla_kernel,
        grid=(B, nb),
        in_specs=[fw(W_GF),
                  pl.BlockSpec((1, GQK, tm), lambda b, i: (b, 0, i)),
                  pl.BlockSpec((1, GLA_CB, 2 * GQK), lambda b, i: (b, i, 0)),
                  bw(W_GB), bw(GV),
                  pl.BlockSpec((1, GQK, tm), lambda b, i: (b, 0, nb - 1 - i)),
                  pl.BlockSpec((1, GLA_CB, 2 * GQK), lambda b, i: (b, nb - 1 - i, 0)),
                  st, st],
        out_specs=[fw(GV), bw(GV)],
        out_shape=[jax.ShapeDtypeStruct((B, L, GV), f32)] * 2,
        scratch_shapes=[pltpu.VMEM((GQK, GV), f32), pltpu.VMEM((GQK, GV), f32)],
        compiler_params=pltpu.CompilerParams(
            dimension_semantics=("parallel", "arbitrary"), vmem_limit_bytes=VMEM_LIMIT),
        name="gla_scan",
    )(gf, ktf, dec, gb, gf, ktb, dec, s0f, s0b)


def _merge_ffn_kernel(x_ref, ona_ref, of_ref, ob_ref, mg_ref,
                      g1_ref, sh2_ref, sc2_ref, g2_ref, n2_ref, gg_ref,
                      wna_ref, wgl_ref, wo_ref, wg_ref, wu_ref, wd_ref, o_ref):
    x1s, h2s = [], []
    for hs in range(0, x_ref.shape[1], FFN_SUB):
        rows = slice(hs, hs + FFN_SUB)
        og = of_ref[0, rows] + ob_ref[0, rows]
        parts = []
        for h in range(GLA_H):
            seg = og[:, h * GLA_DV:(h + 1) * GLA_DV]
            ms = jnp.mean(seg * seg, axis=-1, keepdims=True)
            parts.append(seg * lax.rsqrt(ms + EPS) * gg_ref[...])
        ogl = (jnp.concatenate(parts, axis=1) * mg_ref[0, rows, MG_OG:MG_OG + GV].astype(f32)).astype(bf16)
        y = (mg_ref[0, rows, MG_A:MG_A + D].astype(f32) * _dot(ona_ref[0, rows], wna_ref[...])
             + mg_ref[0, rows, MG_B:MG_B + D].astype(f32) * _dot(ogl, wgl_ref[...]))
        x1 = x_ref[0, rows] + g1_ref[0] * _dot(y.astype(bf16), wo_ref[...])
        x1s.append(x1)
        h2s.append(_norm_mod(x1, n2_ref[...], sh2_ref[0], sc2_ref[0]).astype(bf16))
    x1 = jnp.concatenate(x1s, axis=0)
    h2 = jnp.concatenate(h2s, axis=0)
    act = (_silu(_dot(h2, wg_ref[...])) * _dot(h2, wu_ref[...])).astype(bf16)
    o_ref[0] = x1 + g2_ref[0] * _dot(act, wd_ref[...])


def _merge_ffn(x, ona, of, ob, mg, mod3, norm2_g, gg, wna, wgl, wo, wg, wu, wd):
    B, L, _ = x.shape
    tm = TM_FFN
    tok = lambda n: pl.BlockSpec((1, tm, n), lambda b, i: (b, i, 0))
    modv = lambda j: pl.BlockSpec((1, 1, D), lambda b, i: (b, 0, j))
    return pl.pallas_call(
        _merge_ffn_kernel,
        grid=(B, L // tm),
        in_specs=[tok(D), tok(NA_W), tok(GV), tok(GV), tok(W_MG),
                  modv(2), modv(3), modv(4), modv(5),
                  _const_spec((1, D)), _const_spec((1, GLA_DV)),
                  _const_spec((NA_W, D)), _const_spec((GV, D)), _const_spec((D, D)),
                  _const_spec((D, D_FF)), _const_spec((D, D_FF)), _const_spec((D_FF, D))],
        out_specs=tok(D),
        out_shape=jax.ShapeDtypeStruct((B, L, D), f32),
        compiler_params=pltpu.CompilerParams(
            dimension_semantics=("parallel", "parallel"), vmem_limit_bytes=VMEM_LIMIT),
        name="merge_ffn",
    )(x, ona, of, ob, mg, mod3, mod3, mod3, mod3, norm2_g, gg, wna, wgl, wo, wg, wu, wd)


def kernel(x, c, ctx, c_ctx, w_mod, b_mod, norm1_g, norm2_g, w_in, na_q_norm_g, na_k_norm_g, na_rpb,
           gla_w_alpha, gla_b_alpha, gla_norm_g, w_branch_na, w_branch_gla, w_out,
           w_ffn_gate, w_ffn_up, w_ffn_down):
    B, L, _ = x.shape
    assert w_mod.shape[0] == 1 and L % (NA_R * GRID_W) == 0 and L % TM_PROJ == 0

    w_t = _cast_w_in(jnp.transpose(w_in[0]), D_IN // 7)
    w_a = jnp.zeros((N_LR, 2 * GQK), f32)
    w_a = w_a.at[:GLA_RANK, :GQK].set(gla_w_alpha[0, 0]).at[GLA_RANK:, GQK:].set(gla_w_alpha[0, 1])
    w_a = w_a.astype(bf16)
    b_a = gla_b_alpha[0].reshape(1, 2 * GQK)
    gq_t = jnp.tile(na_q_norm_g[0] * (NA_DH ** -0.5 * LOG2E), NA_H).reshape(1, NA_W)
    gk_t = jnp.tile(na_k_norm_g[0], NA_H).reshape(1, NA_W)
    bd = jnp.asarray(np.kron(np.eye(4), np.full((NA_DH, NA_DH), 1.0 / NA_DH)), bf16)

    mod = _adaln_mod(c, c_ctx, w_mod[0], b_mod[0])
    mod3 = mod.reshape(8, 1, 6 * D)
    t2 = _rpb_expand(na_rpb[0])

    kc, vc, s0f, s0b = _ctx_side(ctx, mod3, norm1_g, w_t, w_a, b_a, gk_t, bd, B)
    later_ws = [w_branch_na[0], w_branch_gla[0], w_out[0], w_ffn_gate[0], w_ffn_up[0], w_ffn_down[0]]
    na_qkv, gla_f, gla_b, gates, dec, ktf, ktb, *later_bf = _in_proj(
        x, mod3, norm1_g, w_t, gq_t, gk_t, bd, w_a, b_a, later_ws)
    ona = _na_attn(na_qkv, kc, vc, t2)
    of, ob = _gla_scan(gla_f, gla_b, ktf, ktb, dec, s0f, s0b)
    return _merge_ffn(x, ona, of, ob, gates, mod3, norm2_g, gla_norm_g, *later_bf)
```

```python
import functools

import numpy as np
import jax
import jax.numpy as jnp
from jax import lax
from jax.experimental import pallas as pl
from jax.experimental.pallas import tpu as pltpu

D = 1024
GRID_W = 64
CTX = 256
NA_H = 8
NA_DH = 64
NA_WR = 8
NA_WC = 16
GLA_H = 4
GLA_DK = 64
GLA_DV = 128
GLA_RANK = 16
GLA_TAU = 16.0
GLA_LA_MIN = -1.0
CHUNK = 64
ROPE_THETA = 10000.0
D_FF = 2816
EPS = 1e-6
NEG_INF = -1e30
NA_W = NA_H * NA_DH
GQK = GLA_H * GLA_DK
GV = GLA_H * GLA_DV

OFF_QNA, OFF_KNA, OFF_VNA = 0, 512, 1024
OFF_QG, OFF_KG, OFF_VG, OFF_OG = 1536, 1792, 2048, 2560
OFF_LR, OFF_MA, OFF_MB = 3072, 3104, 4128
N_LR = 2 * GLA_RANK
N_PROJ_OUT = 7
NA_Q, NA_K, NA_V, W_NA = 0, 512, 1024, 1536
G_V, G_Q, G_KI, W_GF = 0, 512, 768, 1024
W_GB = W_GF - GV
MG_A, MG_B, MG_OG, W_MG = 0, 1024, 2048, 2560
D_IN = 5152
LOG2E = 1.4426950408889634

BF16_SUBLANES = 16
VMEM_LIMIT = 56 * 1024 * 1024

TM_PROJ = 512
PROJ_SUB = 256
TM_FFN = 512
FFN_SUB = 256
NA_R = 64
NA_RI = 4
GLA_CB = 32

f32 = jnp.float32
bf16 = jnp.bfloat16


def _const_spec(shape):
    nd = len(shape)
    return pl.BlockSpec(shape, lambda *_: (0,) * nd, pipeline_mode=pl.Buffered(1))


def _dot(a, b):
    return jnp.dot(a, b, preferred_element_type=f32)


def _dot_nt(a, b):
    return lax.dot_general(a, b, (((1,), (1,)), ((), ())), preferred_element_type=f32)


def _split2(x):
    hi = x.astype(bf16)
    lo = (x - hi.astype(f32)).astype(bf16)
    return hi, lo


def _sigmoid(x):
    return 0.5 * jnp.tanh(0.5 * x) + 0.5


def _silu(x):
    return x * _sigmoid(x)


def _split3_dot(r, oh):
    h1 = r.astype(bf16)
    r2 = r - h1.astype(f32)
    h2 = r2.astype(bf16)
    h3 = (r2 - h2.astype(f32)).astype(bf16)
    return _dot(h1, oh) + _dot(h2, oh) + _dot(h3, oh)


def _rpb_expand_kernel(r0_ref, r1_ref, oh0_ref, oh1_ref, mask_ref, o_ref):
    gathered = _split3_dot(r0_ref[...], oh0_ref[...]) + _split3_dot(r1_ref[...], oh1_ref[...])
    o_ref[...] = gathered * LOG2E + mask_ref[...]


def _rpb_expand(rpb):
    nd = 2 * NA_WR - 1
    nrow = NA_H * nd
    ncol = 2 * NA_WC - 1
    nxt = jnp.concatenate([rpb[:, 1:], jnp.zeros_like(rpb[:, :1])], axis=1)
    r0 = jnp.zeros((128, 128), f32).at[:nrow, :ncol].set(rpb.reshape(nrow, ncol))
    r1 = jnp.zeros((128, 128), f32).at[:nrow, :ncol].set(nxt.reshape(nrow, ncol))
    q = np.arange(GRID_W)
    cs = np.clip(q - NA_WC // 2, 0, GRID_W - NA_WC)
    in_win = (q[None, :] >= cs[:, None]) & (q[None, :] < cs[:, None] + NA_WC)
    dc = np.clip(q[None, :] - q[:, None], -(NA_WC - 1), NA_WC - 1) + NA_WC - 1
    oh = np.zeros((2, 128, GRID_W, 128), np.float32)
    qq, kk = np.nonzero(in_win)
    oh[0, dc[qq, kk], qq, kk] = 1.0
    oh[1, dc[qq, kk], qq, kk + GRID_W] = 1.0
    oh = oh.reshape(2, 128, GRID_W * 128)
    mask = np.where(np.concatenate([in_win, in_win], axis=1), 0.0, NEG_INF).reshape(1, GRID_W * 128)
    out = pl.pallas_call(
        _rpb_expand_kernel,
        out_shape=jax.ShapeDtypeStruct((128, GRID_W * 128), f32),
        name="rpb_expand",
    )(r0, r1, jnp.asarray(oh[0], bf16), jnp.asarray(oh[1], bf16), jnp.asarray(mask, f32))
    return out[:nrow].reshape(NA_H, nd, GRID_W, 128)


def _cast_kernel(x_ref, o_ref):
    o_ref[...] = x_ref[...].astype(bf16)


def _cast_bf16(w, rows):
    R, C = w.shape
    return pl.pallas_call(
        _cast_kernel,
        grid=(R // rows,),
        in_specs=[pl.BlockSpec((rows, C), lambda j: (j, 0))],
        out_specs=pl.BlockSpec((rows, C), lambda j: (j, 0)),
        out_shape=jax.ShapeDtypeStruct((R, C), bf16),
        name="cast_bf16",
    )(w)


def _mod_kernel(a_ref, w_ref, b_ref, o_ref):
    a = _silu(a_ref[...]).astype(bf16)
    o_ref[...] = _dot(a, w_ref[...].astype(bf16)) + b_ref[...]


def _adaln_mod(c, c_ctx, w_mod, b_mod):
    B = c.shape[0]
    a = jnp.zeros((8, D), f32).at[:B].set(c).at[B].set(c_ctx)
    n = w_mod.shape[1]
    return pl.pallas_call(
        _mod_kernel,
        grid=(n // D,),
        in_specs=[pl.BlockSpec((8, D), lambda j: (0, 0)),
                  pl.BlockSpec((D, D), lambda j: (0, j)),
                  pl.BlockSpec((1, D), lambda j: (0, j))],
        out_specs=pl.BlockSpec((8, D), lambda j: (0, j)),
        out_shape=jax.ShapeDtypeStruct((8, n), f32),
        name="adaln_mod",
    )(a, w_mod, b_mod.reshape(1, n))


def _norm_mod(x, g, sh, sc):
    ms = jnp.mean(x * x, axis=-1, keepdims=True)
    return (x * lax.rsqrt(ms + EPS)) * (g * (1.0 + sc)) + sh


def _head_rms(t, bd, g):
    sq = (t * t).astype(bf16)
    ms = jnp.concatenate([_dot(sq[:, :256], bd), _dot(sq[:, 256:], bd)], axis=1)
    return t * lax.rsqrt(ms + EPS) * g


def _log_alpha(zl, wa_ref, ba_ref):
    z = _dot(zl.astype(bf16), wa_ref[...]) + ba_ref[...]
    soft = jnp.log2(1.0 + jnp.exp2(z * -LOG2E))
    return jnp.maximum(soft * (-1.0 / GLA_TAU), GLA_LA_MIN * LOG2E)


def _ctx_kernel(ctx_ref, sh_ref, sc_ref, g1_ref, wk_ref, wv_ref, wkg_ref, wvg_ref, wlr_ref,
                wa_ref, ba_ref, gk_ref, bd_ref, su_ref, sl_ref,
                kc_ref, vc_ref, sf_ref, sb_ref):
    h = _norm_mod(ctx_ref[0], g1_ref[...], sh_ref[0], sc_ref[0]).astype(bf16)
    k = _dot_nt(h, wk_ref[...])
    kc_ref[0] = _head_rms(k, bd_ref[...], gk_ref[...]).astype(bf16)
    vc_ref[0] = _dot_nt(h, wv_ref[...]).astype(bf16)
    kg = _dot_nt(h, wkg_ref[...])
    vg = _dot_nt(h, wvg_ref[...]).astype(bf16)
    la = _log_alpha(_dot_nt(h, wlr_ref[...]), wa_ref, ba_ref)
    hi, lo = _split2(la)
    ef = _dot(su_ref[...], hi[:, :GQK]) + _dot(su_ref[...], lo[:, :GQK])
    eb = _dot(sl_ref[...], hi[:, GQK:]) + _dot(sl_ref[...], lo[:, GQK:])
    r = lax.broadcasted_iota(jnp.int32, (GQK, GV), 0) >> 6
    cc = lax.broadcasted_iota(jnp.int32, (GQK, GV), 1) >> 7
    diag = r == cc
    for e, out in ((ef, sf_ref), (eb, sb_ref)):
        kw = (kg * jnp.exp2(e)).astype(bf16)
        u = _dot(kw.T, vg)
        out[0] = jnp.where(diag, u, 0.0)


def _ctx_side(ctx, mod3, norm1_g, w_t, w_a, b_a, gk_t, bd, n_b):
    su = jnp.asarray(np.triu(np.ones((CTX, CTX), np.float32), 1), bf16)
    sl = jnp.asarray(np.tril(np.ones((CTX, CTX), np.float32), -1), bf16)
    B = ctx.shape[0]
    cst = lambda shape, idx: pl.BlockSpec(shape, lambda b: idx)
    return pl.pallas_call(
        _ctx_kernel,
        grid=(B,),
        in_specs=[pl.BlockSpec((1, CTX, D), lambda b: (b, 0, 0)),
                  cst((1, 1, D), (n_b, 0, 0)), cst((1, 1, D), (n_b, 0, 1)),
                  cst((1, D), (0, 0)),
                  cst((NA_W, D), (OFF_KNA // NA_W, 0)), cst((NA_W, D), (OFF_VNA // NA_W, 0)),
                  cst((GQK, D), (OFF_KG // GQK, 0)), cst((GV, D), (OFF_VG // GV, 0)),
                  cst((N_LR, D), (OFF_LR // N_LR, 0)),
                  cst((N_LR, 2 * GQK), (0, 0)), cst((1, 2 * GQK), (0, 0)),
                  cst((1, NA_W), (0, 0)), cst((256, 256), (0, 0)),
                  cst((CTX, CTX), (0, 0)), cst((CTX, CTX), (0, 0))],
        out_specs=[pl.BlockSpec((1, CTX, NA_W), lambda b: (b, 0, 0)),
                   pl.BlockSpec((1, CTX, NA_W), lambda b: (b, 0, 0)),
                   pl.BlockSpec((1, GQK, GV), lambda b: (b, 0, 0)),
                   pl.BlockSpec((1, GQK, GV), lambda b: (b, 0, 0))],
        out_shape=[jax.ShapeDtypeStruct((B, CTX, NA_W), bf16),
                   jax.ShapeDtypeStruct((B, CTX, NA_W), bf16),
                   jax.ShapeDtypeStruct((B, GQK, GV), f32),
                   jax.ShapeDtypeStruct((B, GQK, GV), f32)],
        compiler_params=pltpu.CompilerParams(vmem_limit_bytes=VMEM_LIMIT),
        name="ctx_side",
    )(ctx, mod3, mod3, norm1_g, w_t, w_t, w_t, w_t, w_t, w_a, b_a, gk_t, bd, su, sl)


def _proj_kernel(x_ref, sh_ref, sc_ref, g1_ref, w_ref, gq_ref, gk_ref, bd_ref,
                 rc_ref, rs_ref, cc_ref, cs_ref, wa_ref, ba_ref, tl_ref, *refs):
    n_cast = (len(refs) - N_PROJ_OUT) // 2
    cast_in, refs = refs[:n_cast], refs[n_cast:]
    na_ref, gf_ref, gb_ref, mg_ref, dec_ref, ktf_ref, ktb_ref = refs[:N_PROJ_OUT]
    cast_out = refs[N_PROJ_OUT:]
    for src, dst in zip(cast_in, cast_out):
        dst[...] = src[...].astype(bf16)

    bd = bd_ref[...]
    tl = tl_ref[...]
    nrow = PROJ_SUB // GRID_W
    nchunk = PROJ_SUB // CHUNK
    lane = lax.broadcasted_iota(jnp.int32, (PROJ_SUB, GQK), 1)
    first_half = (lane & 31) < 16

    for hs in range(0, x_ref.shape[1], PROJ_SUB):
        rows = slice(hs, hs + PROJ_SUB)
        hb = _norm_mod(x_ref[0, rows], g1_ref[...], sh_ref[0], sc_ref[0]).astype(bf16)

        def proj(off, n):
            return _dot_nt(hb, w_ref[off:off + n, :])

        zl = proj(OFF_LR, N_LR)
        q_raw = proj(OFF_QNA, NA_W)
        la = _log_alpha(zl, wa_ref, ba_ref)
        k_raw = proj(OFF_KNA, NA_W)
        hi, lo = _split2(la)
        na_ref[0, rows, NA_Q:NA_Q + NA_W] = _head_rms(q_raw, bd, gq_ref[...]).astype(bf16)
        cum = jnp.concatenate(
            [_dot(tl, hi[s:s + 256]) + _dot(tl, lo[s:s + 256]) for s in range(0, PROJ_SUB, 256)], axis=0)
        na_ref[0, rows, NA_K:NA_K + NA_W] = _head_rms(k_raw, bd, gk_ref[...]).astype(bf16)
        qg_raw = proj(OFF_QG, GQK)
        kg_raw = proj(OFF_KG, GQK)
        na_ref[0, rows, NA_V:NA_V + NA_W] = proj(OFF_VNA, NA_W).astype(bf16)
        gf_ref[0, rows, G_V:G_V + GV] = proj(OFF_VG, GV).astype(bf16)

        gr = slice(hs // GRID_W, hs // GRID_W + nrow)
        cos_t = (rc_ref[gr][:, None, :] + cc_ref[...][None]).reshape(PROJ_SUB, GQK)
        sin_t = (rs_ref[gr][:, None, :] + cs_ref[...][None]).reshape(PROJ_SUB, GQK)

        def rope(t):
            partner = jnp.where(first_half, pltpu.roll(t, GQK - 16, 1), pltpu.roll(t, 16, 1))
            return t * cos_t + partner * sin_t

        qg = rope(qg_raw) * (GLA_DK ** -0.5)
        kg = rope(kg_raw)

        cum3 = cum.reshape(nchunk, CHUNK, 2 * GQK)
        tot3 = cum3[:, CHUNK - 1:CHUNK, :]
        la3 = la.reshape(nchunk, CHUNK, 2 * GQK)
        dec_ref[0, hs // CHUNK:hs // CHUNK + nchunk] = jnp.exp2(tot3).reshape(nchunk, 2 * GQK)
        b_f = cum3[:, :, :GQK].reshape(PROJ_SUB, GQK)
        e_f = (tot3 - cum3)[:, :, :GQK].reshape(PROJ_SUB, GQK)
        b_b = (tot3 - cum3 + la3)[:, :, GQK:].reshape(PROJ_SUB, GQK)
        e_b = (cum3 - la3)[:, :, GQK:].reshape(PROJ_SUB, GQK)
        mg_ref[0, rows, MG_OG:MG_OG + GV] = _silu(proj(OFF_OG, GV)).astype(bf16)
        gf_ref[0, rows, G_Q:G_Q + GQK] = (qg * jnp.exp2(b_f)).astype(bf16)
        gf_ref[0, rows, G_KI:G_KI + GQK] = (kg * jnp.exp2(-b_f)).astype(bf16)
        ktf_ref[0, :, rows] = (kg * jnp.exp2(e_f)).T.astype(bf16)
        mg_ref[0, rows, MG_A:MG_A + D] = _sigmoid(proj(OFF_MA, D)).astype(bf16)
        gb_ref[0, rows, G_Q - GV:G_Q - GV + GQK] = (qg * jnp.exp2(b_b)).astype(bf16)
        gb_ref[0, rows, G_KI - GV:G_KI - GV + GQK] = (kg * jnp.exp2(-b_b)).astype(bf16)
        ktb_ref[0, :, rows] = (kg * jnp.exp2(e_b)).T.astype(bf16)
        mg_ref[0, rows, MG_B:MG_B + D] = _sigmoid(proj(OFF_MB, D)).astype(bf16)


def _rope_tables(L):
    n = GLA_DK // 4
    freqs = ROPE_THETA ** (-np.arange(n, dtype=np.float64) / n)
    lane = np.arange(GQK)
    fl = freqs[lane % n]
    is_row = (lane % GLA_DK) < GLA_DK // 2
    sign = np.where((lane % 32) < 16, -1.0, 1.0)
    rows = np.arange(L // GRID_W)[:, None] * fl[None]
    cols = np.arange(GRID_W)[:, None] * fl[None]
    rc = np.where(is_row, np.cos(rows), 0.0)
    rs = np.where(is_row, np.sin(rows) * sign, 0.0)
    cc = np.where(~is_row, np.cos(cols), 0.0)
    cs = np.where(~is_row, np.sin(cols) * sign, 0.0)
    return [jnp.asarray(t, f32) for t in (rc, rs, cc, cs)]


def _in_proj(x, mod3, norm1_g, w_t, gq_t, gk_t, bd, w_a, b_a, cast_ws):
    B, L, _ = x.shape
    tm = TM_PROJ
    rows_per = tm // GRID_W
    nstep = L // tm
    total = B * nstep

    def slab(w):
        R, C = w.shape
        if R % (total * BF16_SUBLANES) == 0:
            return pl.BlockSpec((R // total, C), lambda b, i: (b * nstep + i, 0))
        assert (2 * R) % (total * BF16_SUBLANES) == 0 and C % 256 == 0
        return pl.BlockSpec((2 * R // total, C // 2), lambda b, i: ((b * nstep + i) // 2, (b * nstep + i) % 2))
    rc, rs, cc, cs = _rope_tables(L)
    blk = np.kron(np.eye(256 // CHUNK), np.tril(np.ones((CHUNK, CHUNK)))).astype(np.float32)
    tl = jnp.asarray(blk, bf16)
    tok = lambda n: pl.BlockSpec((1, tm, n), lambda b, i: (b, i, 0))
    sds = lambda n, dt: jax.ShapeDtypeStruct((B, L, n), dt)
    return pl.pallas_call(
        _proj_kernel,
        grid=(B, L // tm),
        in_specs=[tok(D),
                  pl.BlockSpec((1, 1, D), lambda b, i: (b, 0, 0)),
                  pl.BlockSpec((1, 1, D), lambda b, i: (b, 0, 1)),
                  _const_spec((1, D)), _const_spec((D_IN, D)),
                  _const_spec((1, NA_W)), _const_spec((1, NA_W)), _const_spec((256, 256)),
                  pl.BlockSpec((rows_per, GQK), lambda b, i: (i, 0)),
                  pl.BlockSpec((rows_per, GQK), lambda b, i: (i, 0)),
                  _const_spec((GRID_W, GQK)), _const_spec((GRID_W, GQK)),
                  _const_spec((N_LR, 2 * GQK)), _const_spec((1, 2 * GQK)), _const_spec((256, 256))]
                 + [slab(w) for w in cast_ws],
        out_specs=[tok(W_NA), tok(W_GF), tok(W_GB), tok(W_MG),
                   pl.BlockSpec((1, tm // CHUNK, 2 * GQK), lambda b, i: (b, i, 0)),
                   pl.BlockSpec((1, GQK, tm), lambda b, i: (b, 0, i)),
                   pl.BlockSpec((1, GQK, tm), lambda b, i: (b, 0, i))]
                  + [slab(w) for w in cast_ws],
        out_shape=[sds(W_NA, bf16), sds(W_GF, bf16), sds(W_GB, bf16), sds(W_MG, bf16),
                   jax.ShapeDtypeStruct((B, L // CHUNK, 2 * GQK), f32),
                   jax.ShapeDtypeStruct((B, GQK, L), bf16), jax.ShapeDtypeStruct((B, GQK, L), bf16)]
                  + [jax.ShapeDtypeStruct(w.shape, bf16) for w in cast_ws],
        compiler_params=pltpu.CompilerParams(
            dimension_semantics=("parallel", "parallel"), vmem_limit_bytes=VMEM_LIMIT),
        name="in_proj",
    )(x, mod3, mod3, norm1_g, w_t, gq_t, gk_t, bd, rc, rs, cc, cs, w_a, b_a, tl, *cast_ws)


def _lane_blocks(t):
    return [t[:, c:c + 128] for c in range(0, t.shape[1], 128)]


def _na_kernel(q_ref, k_ref, v_ref, kc_ref, vc_ref, t2_ref, o_ref, sl_ref, sc_ref, m_ref):
    i = pl.program_id(1)
    rows = k_ref.shape[1] // GRID_W
    lane_head = lax.broadcasted_iota(jnp.int32, (GRID_W, 256), 1) >> 6
    hmask = [lane_head == h for h in range(4)]

    def window_start(r):
        rs = jnp.clip(r - NA_WR // 2, 0, rows - NA_WR)
        return rs, pl.multiple_of(rs * GRID_W, GRID_W)

    def produce(rr, bank, j):
        r = i * NA_R + rr
        rs, start = window_start(r)
        d0 = rs - r + (NA_WR - 1)
        row0 = pl.multiple_of(rr * GRID_W, GRID_W)
        for g in range(2):
            ls = slice(256 * g, 256 * g + 256)
            q = q_ref[0, pl.ds(row0, GRID_W), ls]
            qs = jnp.concatenate([jnp.where(m, q, jnp.zeros_like(q)) for m in hmask], axis=0)
            kw = k_ref[0, pl.ds(start, NA_WR * GRID_W), ls]
            bias = jnp.concatenate(
                [jnp.concatenate([t2_ref[4 * g + h, d0 + 2 * jj] for jj in range(NA_WR // 2)], axis=1)
                 for h in range(4)], axis=0)
            s_loc = _dot_nt(qs, kw) + bias
            s_ctx = _dot_nt(qs, kc_ref[0, :, ls])
            mx = functools.reduce(jnp.maximum, _lane_blocks(s_loc) + _lane_blocks(s_ctx))
            sl_ref[bank, j, g] = s_loc
            sc_ref[bank, j, g] = s_ctx
            m_ref[bank, j, g] = jnp.broadcast_to(jnp.max(mx, axis=-1, keepdims=True), (4 * GRID_W, 128))

    def consume(rr, bank, j):
        _, start = window_start(i * NA_R + rr)
        row0 = pl.multiple_of(rr * GRID_W, GRID_W)
        outs = []
        for g in range(2):
            ls = slice(256 * g, 256 * g + 256)
            m = m_ref[bank, j, g]
            p_loc = [jnp.exp2(sl_ref[bank, j, g, :, c:c + 128] - m) for c in range(0, NA_WR * GRID_W, 128)]
            p_ctx = [jnp.exp2(sc_ref[bank, j, g, :, c:c + 128] - m) for c in range(0, CTX, 128)]
            den = jnp.sum(functools.reduce(jnp.add, p_loc + p_ctx), axis=-1, keepdims=True)
            vw = v_ref[0, pl.ds(start, NA_WR * GRID_W), ls]
            acc = (_dot(jnp.concatenate(p_loc, axis=1).astype(bf16), vw)
                   + _dot(jnp.concatenate(p_ctx, axis=1).astype(bf16), vc_ref[0, :, ls]))
            acc = acc * (1.0 / den)
            o = jnp.zeros((GRID_W, 256), f32)
            for h in range(4):
                o = o + jnp.where(hmask[h], acc[h * GRID_W:(h + 1) * GRID_W], 0.0)
            outs.append(o)
        o_ref[0, pl.ds(row0, GRID_W), :] = jnp.concatenate(outs, axis=1).astype(bf16)

    for j in range(NA_RI):
        produce(jnp.int32(j), 0, j)

    def step(t, carry):
        bank = t & 1
        for j in range(NA_RI):
            consume(t * NA_RI + j, bank, j)
        for j in range(NA_RI):
            produce(jnp.minimum((t + 1) * NA_RI + j, NA_R - 1), 1 - bank, j)
        return carry

    lax.fori_loop(0, NA_R // NA_RI, step, 0)


def _na_attn(qkv, kc, vc, t2):
    B, L, _ = qkv.shape
    nblk = L // (NA_R * GRID_W)
    return pl.pallas_call(
        _na_kernel,
        grid=(B, nblk),
        in_specs=[pl.BlockSpec((1, NA_R * GRID_W, NA_W), lambda b, i: (b, i, NA_Q // NA_W)),
                  pl.BlockSpec((1, L, NA_W), lambda b, i: (b, 0, NA_K // NA_W), pipeline_mode=pl.Buffered(1)),
                  pl.BlockSpec((1, L, NA_W), lambda b, i: (b, 0, NA_V // NA_W), pipeline_mode=pl.Buffered(1)),
                  pl.BlockSpec((1, CTX, NA_W), lambda b, i: (b, 0, 0)),
                  pl.BlockSpec((1, CTX, NA_W), lambda b, i: (b, 0, 0)),
                  _const_spec((NA_H, 2 * NA_WR - 1, GRID_W, 128))],
        out_specs=pl.BlockSpec((1, NA_R * GRID_W, NA_W), lambda b, i: (b, i, 0)),
        out_shape=jax.ShapeDtypeStruct((B, L, NA_W), bf16),
        scratch_shapes=[pltpu.VMEM((2, NA_RI, 2, 4 * GRID_W, NA_WR * GRID_W), f32),
                        pltpu.VMEM((2, NA_RI, 2, 4 * GRID_W, CTX), f32),
                        pltpu.VMEM((2, NA_RI, 2, 4 * GRID_W, 128), f32)],
        compiler_params=pltpu.CompilerParams(
            dimension_semantics=("parallel", "parallel"), vmem_limit_bytes=VMEM_LIMIT),
        name="na_attn",
    )(qkv, qkv, qkv, kc, vc, t2)


def _gla_chunk(q, ki, v, ket, dcol, s_ref, tri, hmask):
    qs = jnp.concatenate([jnp.where(m, q, jnp.zeros_like(q)) for m in hmask], axis=0)
    att = jnp.where(tri, _dot_nt(qs, ki), 0.0).astype(bf16)
    o = _dot(q, s_ref[...].astype(bf16))
    intra = [_dot(att[h * CHUNK:(h + 1) * CHUNK], v[:, h * GLA_DV:(h + 1) * GLA_DV]) for h in range(GLA_H)]
    o = o + jnp.concatenate(intra, axis=1)
    for h in range(GLA_H):
        rs = slice(h * GLA_DK, (h + 1) * GLA_DK)
        cs = slice(h * GLA_DV, (h + 1) * GLA_DV)
        s_ref[rs, cs] = dcol[rs] * s_ref[rs, cs] + _dot(ket[rs], v[:, cs])
    return o


def _gla_kernel(gf_ref, ktf_ref, decf_ref, gb_ref, vb_ref, ktb_ref, decb_ref,
                s0f_ref, s0b_ref, of_ref, ob_ref, sf_ref, sb_ref):
    @pl.when(pl.program_id(1) == 0)
    def _():
        sf_ref[...] = s0f_ref[0]
        sb_ref[...] = s0b_ref[0]

    lane_head = lax.broadcasted_iota(jnp.int32, (CHUNK, GQK), 1) >> 6
    hmask = [lane_head == h for h in range(GLA_H)]
    ri = lax.broadcasted_iota(jnp.int32, (GLA_H * CHUNK, CHUNK), 0) & (CHUNK - 1)
    ci = lax.broadcasted_iota(jnp.int32, (GLA_H * CHUNK, CHUNK), 1)
    tri_f = ri >= ci
    tri_b = ri <= ci
    dect_f = decf_ref[0][:, :GQK].T
    dect_b = decb_ref[0][:, GQK:].T

    def k_end(kt_ref, c):
        kt = kt_ref[0, :, (c // 2) * 2 * CHUNK:(c // 2 + 1) * 2 * CHUNK]
        if c % 2:
            kt = pltpu.roll(kt, CHUNK, 1)
        return kt[:, :CHUNK]

    for c in range(GLA_CB):
        ts = slice(c * CHUNK, (c + 1) * CHUNK)
        of_ref[0, ts, :] = _gla_chunk(gf_ref[0, ts, G_Q:G_Q + GQK], gf_ref[0, ts, G_KI:G_KI + GQK],
                                      gf_ref[0, ts, G_V:G_V + GV], k_end(ktf_ref, c),
                                      dect_f[:, c:c + 1], sf_ref, tri_f, hmask)
        cb = GLA_CB - 1 - c
        tb = slice(cb * CHUNK, (cb + 1) * CHUNK)
        ob_ref[0, tb, :] = _gla_chunk(gb_ref[0, tb, G_Q - GV:G_Q - GV + GQK], gb_ref[0, tb, G_KI - GV:G_KI - GV + GQK],
                                      vb_ref[0, tb, :], k_end(ktb_ref, cb),
                                      dect_b[:, cb:cb + 1], sb_ref, tri_b, hmask)


def _gla_scan(gf, gb, ktf, ktb, dec, s0f, s0b):
    B, L, _ = gf.shape
    tm = GLA_CB * CHUNK
    nb = L // tm
    fw = lambda n: pl.BlockSpec((1, tm, n), lambda b, i: (b, i, 0))
    bw = lambda n: pl.BlockSpec((1, tm, n), lambda b, i: (b, nb - 1 - i, 0))
    st = pl.BlockSpec((1, GQK, GV), lambda b, i: (b, 0, 0))
    return pl.pallas_call(
        _gla_kernel,
        grid=(B, nb),
        in_specs=[fw(W_GF),
                  pl.BlockSpec((1, GQK, tm), lambda b, i: (b, 0, i)),
                  pl.BlockSpec((1, GLA_CB, 2 * GQK), lambda b, i: (b, i, 0)),
                  bw(W_GB), bw(GV),
                  pl.BlockSpec((1, GQK, tm), lambda b, i: (b, 0, nb - 1 - i)),
                  pl.BlockSpec((1, GLA_CB, 2 * GQK), lambda b, i: (b, nb - 1 - i, 0)),
                  st, st],
        out_specs=[fw(GV), bw(GV)],
        out_shape=[jax.ShapeDtypeStruct((B, L, GV), f32)] * 2,
        scratch_shapes=[pltpu.VMEM((GQK, GV), f32), pltpu.VMEM((GQK, GV), f32)],
        compiler_params=pltpu.CompilerParams(
            dimension_semantics=("parallel", "arbitrary"), vmem_limit_bytes=VMEM_LIMIT),
        name="gla_scan",
    )(gf, ktf, dec, gb, gf, ktb, dec, s0f, s0b)


def _merge_ffn_kernel(x_ref, ona_ref, of_ref, ob_ref, mg_ref,
                      g1_ref, sh2_ref, sc2_ref, g2_ref, n2_ref, gg_ref,
                      wna_ref, wgl_ref, wo_ref, wg_ref, wu_ref, wd_ref, o_ref):
    x1s, h2s = [], []
    for hs in range(0, x_ref.shape[1], FFN_SUB):
        rows = slice(hs, hs + FFN_SUB)
        og = of_ref[0, rows] + ob_ref[0, rows]
        parts = []
        for h in range(GLA_H):
            seg = og[:, h * GLA_DV:(h + 1) * GLA_DV]
            ms = jnp.mean(seg * seg, axis=-1, keepdims=True)
            parts.append(seg * lax.rsqrt(ms + EPS) * gg_ref[...])
        ogl = (jnp.concatenate(parts, axis=1) * mg_ref[0, rows, MG_OG:MG_OG + GV].astype(f32)).astype(bf16)
        y = (mg_ref[0, rows, MG_A:MG_A + D].astype(f32) * _dot(ona_ref[0, rows], wna_ref[...])
             + mg_ref[0, rows, MG_B:MG_B + D].astype(f32) * _dot(ogl, wgl_ref[...]))
        x1 = x_ref[0, rows] + g1_ref[0] * _dot(y.astype(bf16), wo_ref[...])
        x1s.append(x1)
        h2s.append(_norm_mod(x1, n2_ref[...], sh2_ref[0], sc2_ref[0]).astype(bf16))
    x1 = jnp.concatenate(x1s, axis=0)
    h2 = jnp.concatenate(h2s, axis=0)
    act = (_silu(_dot(h2, wg_ref[...])) * _dot(h2, wu_ref[...])).astype(bf16)
    o_ref[0] = x1 + g2_ref[0] * _dot(act, wd_ref[...])


def _merge_ffn(x, ona, of, ob, mg, mod3, norm2_g, gg, wna, wgl, wo, wg, wu, wd):
    B, L, _ = x.shape
    tm = TM_FFN
    tok = lambda n: pl.BlockSpec((1, tm, n), lambda b, i: (b, i, 0))
    modv = lambda j: pl.BlockSpec((1, 1, D), lambda b, i: (b, 0, j))
    return pl.pallas_call(
        _merge_ffn_kernel,
        grid=(B, L // tm),
        in_specs=[tok(D), tok(NA_W), tok(GV), tok(GV), tok(W_MG),
                  modv(2), modv(3), modv(4), modv(5),
                  _const_spec((1, D)), _const_spec((1, GLA_DV)),
                  _const_spec((NA_W, D)), _const_spec((GV, D)), _const_spec((D, D)),
                  _const_spec((D, D_FF)), _const_spec((D, D_FF)), _const_spec((D_FF, D))],
        out_specs=tok(D),
        out_shape=jax.ShapeDtypeStruct((B, L, D), f32),
        compiler_params=pltpu.CompilerParams(
            dimension_semantics=("parallel", "parallel"), vmem_limit_bytes=VMEM_LIMIT),
        name="merge_ffn",
    )(x, ona, of, ob, mg, mod3, mod3, mod3, mod3, norm2_g, gg, wna, wgl, wo, wg, wu, wd)


def kernel(x, c, ctx, c_ctx, w_mod, b_mod, norm1_g, norm2_g, w_in, na_q_norm_g, na_k_norm_g, na_rpb,
           gla_w_alpha, gla_b_alpha, gla_norm_g, w_branch_na, w_branch_gla, w_out,
           w_ffn_gate, w_ffn_up, w_ffn_down):
    B, L, _ = x.shape
    assert w_mod.shape[0] == 1 and L % (NA_R * GRID_W) == 0 and L % TM_PROJ == 0

    w_t = _cast_bf16(jnp.transpose(w_in[0]), D_IN // 7)
    w_a = jnp.zeros((N_LR, 2 * GQK), f32)
    w_a = w_a.at[:GLA_RANK, :GQK].set(gla_w_alpha[0, 0]).at[GLA_RANK:, GQK:].set(gla_w_alpha[0, 1])
    w_a = w_a.astype(bf16)
    b_a = gla_b_alpha[0].reshape(1, 2 * GQK)
    gq_t = jnp.tile(na_q_norm_g[0] * (NA_DH ** -0.5 * LOG2E), NA_H).reshape(1, NA_W)
    gk_t = jnp.tile(na_k_norm_g[0], NA_H).reshape(1, NA_W)
    bd = jnp.asarray(np.kron(np.eye(4), np.full((NA_DH, NA_DH), 1.0 / NA_DH)), bf16)

    mod = _adaln_mod(c, c_ctx, w_mod[0], b_mod[0])
    mod3 = mod.reshape(8, 1, 6 * D)
    t2 = _rpb_expand(na_rpb[0])

    kc, vc, s0f, s0b = _ctx_side(ctx, mod3, norm1_g, w_t, w_a, b_a, gk_t, bd, B)
    later_ws = [w_branch_na[0], w_branch_gla[0], w_out[0], w_ffn_gate[0], w_ffn_up[0], w_ffn_down[0]]
    na_qkv, gla_f, gla_b, gates, dec, ktf, ktb, *later_bf = _in_proj(
        x, mod3, norm1_g, w_t, gq_t, gk_t, bd, w_a, b_a, later_ws)
    ona = _na_attn(na_qkv, kc, vc, t2)
    of, ob = _gla_scan(gla_f, gla_b, ktf, ktb, dec, s0f, s0b)
    return _merge_ffn(x, ona, of, ob, gates, mod3, norm2_g, gla_norm_g, *later_bf)
```

```python
import functools

import numpy as np
import jax
import jax.numpy as jnp
from jax import lax
from jax.experimental import pallas as pl
from jax.experimental.pallas import tpu as pltpu

D = 1024
GRID_W = 64
CTX = 256
NA_H = 8
NA_DH = 64
NA_WR = 8
NA_WC = 16
GLA_H = 4
GLA_DK = 64
GLA_DV = 128
GLA_RANK = 16
GLA_TAU = 16.0
GLA_LA_MIN = -1.0
CHUNK = 64
ROPE_THETA = 10000.0
D_FF = 2816
EPS = 1e-6
NEG_INF = -1e30
NA_W = NA_H * NA_DH
GQK = GLA_H * GLA_DK
GV = GLA_H * GLA_DV

OFF_QNA, OFF_KNA, OFF_VNA = 0, 512, 1024
OFF_QG, OFF_KG, OFF_VG, OFF_OG = 1536, 1792, 2048, 2560
OFF_LR, OFF_MA, OFF_MB = 3072, 3104, 4128
N_LR = 2 * GLA_RANK
N_PROJ_OUT = 7
NA_Q, NA_K, NA_V, W_NA = 0, 512, 1024, 1536
G_V, G_Q, G_KI, W_GF = 0, 512, 768, 1024
W_GB = W_GF - GV
MG_A, MG_B, MG_OG, W_MG = 0, 1024, 2048, 2560
D_IN = 5152
LOG2E = 1.4426950408889634

BF16_SUBLANES = 16
VMEM_LIMIT = 56 * 1024 * 1024

TM_PROJ = 512
PROJ_SUB = 256
TM_FFN = 512
FFN_SUB = 256
NA_R = 64
NA_RI = 4
GLA_CB = 32

f32 = jnp.float32
bf16 = jnp.bfloat16


def _const_spec(shape):
    nd = len(shape)
    return pl.BlockSpec(shape, lambda *_: (0,) * nd, pipeline_mode=pl.Buffered(1))


def _dot(a, b):
    return jnp.dot(a, b, preferred_element_type=f32)


def _dot_nt(a, b):
    return lax.dot_general(a, b, (((1,), (1,)), ((), ())), preferred_element_type=f32)


def _split2(x):
    hi = x.astype(bf16)
    lo = (x - hi.astype(f32)).astype(bf16)
    return hi, lo


def _sigmoid(x):
    return 0.5 * jnp.tanh(0.5 * x) + 0.5


def _silu(x):
    return x * _sigmoid(x)


def _split3_dot(r, oh):
    h1 = r.astype(bf16)
    r2 = r - h1.astype(f32)
    h2 = r2.astype(bf16)
    h3 = (r2 - h2.astype(f32)).astype(bf16)
    return _dot(h1, oh) + _dot(h2, oh) + _dot(h3, oh)


def _rpb_expand_kernel(r0_ref, r1_ref, oh0_ref, oh1_ref, mask_ref, o_ref):
    gathered = _split3_dot(r0_ref[...], oh0_ref[...]) + _split3_dot(r1_ref[...], oh1_ref[...])
    tiles = gathered * LOG2E + mask_ref[...]
    for q in range(GRID_W):
        o_ref[:, q, :] = tiles[:o_ref.shape[0], q * 128:(q + 1) * 128]


def _rpb_expand(rpb):
    nd = 2 * NA_WR - 1
    nrow = NA_H * nd
    ncol = 2 * NA_WC - 1
    nxt = jnp.concatenate([rpb[:, 1:], jnp.zeros_like(rpb[:, :1])], axis=1)
    pad = ((0, 128 - nrow), (0, 128 - ncol))
    r0 = jnp.pad(rpb.reshape(nrow, ncol), pad)
    r1 = jnp.pad(nxt.reshape(nrow, ncol), pad)
    q = np.arange(GRID_W)
    cs = np.clip(q - NA_WC // 2, 0, GRID_W - NA_WC)
    in_win = (q[None, :] >= cs[:, None]) & (q[None, :] < cs[:, None] + NA_WC)
    dc = np.clip(q[None, :] - q[:, None], -(NA_WC - 1), NA_WC - 1) + NA_WC - 1
    oh = np.zeros((2, 128, GRID_W, 128), np.float32)
    qq, kk = np.nonzero(in_win)
    oh[0, dc[qq, kk], qq, kk] = 1.0
    oh[1, dc[qq, kk], qq, kk + GRID_W] = 1.0
    oh = oh.reshape(2, 128, GRID_W * 128)
    mask = np.where(np.concatenate([in_win, in_win], axis=1), 0.0, NEG_INF).reshape(1, GRID_W * 128)
    out = pl.pallas_call(
        _rpb_expand_kernel,
        out_shape=jax.ShapeDtypeStruct((nrow, GRID_W, 128), f32),
        name="rpb_expand",
    )(r0, r1, jnp.asarray(oh[0], bf16), jnp.asarray(oh[1], bf16), jnp.asarray(mask, f32))
    return out.reshape(NA_H, nd, GRID_W, 128)


def _cast_kernel(x_ref, o_ref):
    o_ref[...] = x_ref[...].astype(bf16)


def _cast_bf16(w, rows):
    R, C = w.shape
    return pl.pallas_call(
        _cast_kernel,
        grid=(R // rows,),
        in_specs=[pl.BlockSpec((rows, C), lambda j: (j, 0))],
        out_specs=pl.BlockSpec((rows, C), lambda j: (j, 0)),
        out_shape=jax.ShapeDtypeStruct((R, C), bf16),
        name="cast_bf16",
    )(w)


def _mod_kernel(a_ref, w_ref, b_ref, o_ref):
    a = _silu(a_ref[...]).astype(bf16)
    o_ref[...] = (_dot(a, w_ref[...].astype(bf16)) + b_ref[...])[:, None, :]


def _adaln_mod(c, c_ctx, w_mod, b_mod):
    B = c.shape[0]
    a = jnp.concatenate([c, c_ctx[None], jnp.zeros((8 - B - 1, D), f32)], axis=0)
    n = w_mod.shape[1]
    return pl.pallas_call(
        _mod_kernel,
        grid=(n // D,),
        in_specs=[pl.BlockSpec((8, D), lambda j: (0, 0)),
                  pl.BlockSpec((D, D), lambda j: (0, j)),
                  pl.BlockSpec((1, D), lambda j: (0, j))],
        out_specs=pl.BlockSpec((8, 1, D), lambda j: (0, 0, j)),
        out_shape=jax.ShapeDtypeStruct((8, 1, n), f32),
        name="adaln_mod",
    )(a, w_mod, b_mod.reshape(1, n))


def _norm_mod(x, g, sh, sc):
    ms = jnp.mean(x * x, axis=-1, keepdims=True)
    return (x * lax.rsqrt(ms + EPS)) * (g * (1.0 + sc)) + sh


def _head_rms(t, bd, g):
    sq = (t * t).astype(bf16)
    ms = jnp.concatenate([_dot(sq[:, :256], bd), _dot(sq[:, 256:], bd)], axis=1)
    return t * lax.rsqrt(ms + EPS) * g


def _log_alpha(zl, wa_ref, ba_ref):
    z = _dot(zl.astype(bf16), wa_ref[...]) + ba_ref[...]
    soft = jnp.log2(1.0 + jnp.exp2(z * -LOG2E))
    return jnp.maximum(soft * (-1.0 / GLA_TAU), GLA_LA_MIN * LOG2E)


def _ctx_kernel(ctx_ref, sh_ref, sc_ref, g1_ref, wk_ref, wv_ref, wkg_ref, wvg_ref, wlr_ref,
                wa_ref, ba_ref, gk_ref, bd_ref, su_ref, sl_ref,
                kc_ref, vc_ref, sf_ref, sb_ref):
    h = _norm_mod(ctx_ref[0], g1_ref[...], sh_ref[0], sc_ref[0]).astype(bf16)
    k = _dot_nt(h, wk_ref[...])
    kc_ref[0] = _head_rms(k, bd_ref[...], gk_ref[...]).astype(bf16)
    vc_ref[0] = _dot_nt(h, wv_ref[...]).astype(bf16)
    kg = _dot_nt(h, wkg_ref[...])
    vg = _dot_nt(h, wvg_ref[...]).astype(bf16)
    la = _log_alpha(_dot_nt(h, wlr_ref[...]), wa_ref, ba_ref)
    hi, lo = _split2(la)
    ef = _dot(su_ref[...], hi[:, :GQK]) + _dot(su_ref[...], lo[:, :GQK])
    eb = _dot(sl_ref[...], hi[:, GQK:]) + _dot(sl_ref[...], lo[:, GQK:])
    r = lax.broadcasted_iota(jnp.int32, (GQK, GV), 0) >> 6
    cc = lax.broadcasted_iota(jnp.int32, (GQK, GV), 1) >> 7
    diag = r == cc
    for e, out in ((ef, sf_ref), (eb, sb_ref)):
        kw = (kg * jnp.exp2(e)).astype(bf16)
        u = _dot(kw.T, vg)
        out[0] = jnp.where(diag, u, 0.0)


def _ctx_side(ctx, mod3, norm1_g, w_t, w_a, b_a, gk_t, bd, n_b):
    su = jnp.asarray(np.triu(np.ones((CTX, CTX), np.float32), 1), bf16)
    sl = jnp.asarray(np.tril(np.ones((CTX, CTX), np.float32), -1), bf16)
    B = ctx.shape[0]
    cst = lambda shape, idx: pl.BlockSpec(shape, lambda b: idx)
    return pl.pallas_call(
        _ctx_kernel,
        grid=(B,),
        in_specs=[pl.BlockSpec((1, CTX, D), lambda b: (b, 0, 0)),
                  cst((1, 1, D), (n_b, 0, 0)), cst((1, 1, D), (n_b, 0, 1)),
                  cst((1, D), (0, 0)),
                  cst((NA_W, D), (OFF_KNA // NA_W, 0)), cst((NA_W, D), (OFF_VNA // NA_W, 0)),
                  cst((GQK, D), (OFF_KG // GQK, 0)), cst((GV, D), (OFF_VG // GV, 0)),
                  cst((N_LR, D), (OFF_LR // N_LR, 0)),
                  cst((N_LR, 2 * GQK), (0, 0)), cst((1, 2 * GQK), (0, 0)),
                  cst((1, NA_W), (0, 0)), cst((256, 256), (0, 0)),
                  cst((CTX, CTX), (0, 0)), cst((CTX, CTX), (0, 0))],
        out_specs=[pl.BlockSpec((1, CTX, NA_W), lambda b: (b, 0, 0)),
                   pl.BlockSpec((1, CTX, NA_W), lambda b: (b, 0, 0)),
                   pl.BlockSpec((1, GQK, GV), lambda b: (b, 0, 0)),
                   pl.BlockSpec((1, GQK, GV), lambda b: (b, 0, 0))],
        out_shape=[jax.ShapeDtypeStruct((B, CTX, NA_W), bf16),
                   jax.ShapeDtypeStruct((B, CTX, NA_W), bf16),
                   jax.ShapeDtypeStruct((B, GQK, GV), f32),
                   jax.ShapeDtypeStruct((B, GQK, GV), f32)],
        compiler_params=pltpu.CompilerParams(vmem_limit_bytes=VMEM_LIMIT),
        name="ctx_side",
    )(ctx, mod3, mod3, norm1_g, w_t, w_t, w_t, w_t, w_t, w_a, b_a, gk_t, bd, su, sl)


def _proj_kernel(x_ref, sh_ref, sc_ref, g1_ref, w_ref, gq_ref, gk_ref, bd_ref,
                 rc_ref, rs_ref, cc_ref, cs_ref, wa_ref, ba_ref, tl_ref, *refs):
    n_cast = (len(refs) - N_PROJ_OUT) // 2
    cast_in, refs = refs[:n_cast], refs[n_cast:]
    na_ref, gf_ref, gb_ref, mg_ref, dec_ref, ktf_ref, ktb_ref = refs[:N_PROJ_OUT]
    cast_out = refs[N_PROJ_OUT:]
    for src, dst in zip(cast_in, cast_out):
        dst[...] = src[...].astype(bf16)

    bd = bd_ref[...]
    tl = tl_ref[...]
    nrow = PROJ_SUB // GRID_W
    nchunk = PROJ_SUB // CHUNK
    lane = lax.broadcasted_iota(jnp.int32, (PROJ_SUB, GQK), 1)
    first_half = (lane & 31) < 16

    for hs in range(0, x_ref.shape[1], PROJ_SUB):
        rows = slice(hs, hs + PROJ_SUB)
        hb = _norm_mod(x_ref[0, rows], g1_ref[...], sh_ref[0], sc_ref[0]).astype(bf16)

        def proj(off, n):
            return _dot_nt(hb, w_ref[off:off + n, :])

        zl = proj(OFF_LR, N_LR)
        q_raw = proj(OFF_QNA, NA_W)
        la = _log_alpha(zl, wa_ref, ba_ref)
        k_raw = proj(OFF_KNA, NA_W)
        hi, lo = _split2(la)
        na_ref[0, rows, NA_Q:NA_Q + NA_W] = _head_rms(q_raw, bd, gq_ref[...]).astype(bf16)
        cum = jnp.concatenate(
            [_dot(tl, hi[s:s + 256]) + _dot(tl, lo[s:s + 256]) for s in range(0, PROJ_SUB, 256)], axis=0)
        na_ref[0, rows, NA_K:NA_K + NA_W] = _head_rms(k_raw, bd, gk_ref[...]).astype(bf16)
        qg_raw = proj(OFF_QG, GQK)
        kg_raw = proj(OFF_KG, GQK)
        na_ref[0, rows, NA_V:NA_V + NA_W] = proj(OFF_VNA, NA_W).astype(bf16)
        gf_ref[0, rows, G_V:G_V + GV] = proj(OFF_VG, GV).astype(bf16)

        gr = slice(hs // GRID_W, hs // GRID_W + nrow)
        cos_t = (rc_ref[gr][:, None, :] + cc_ref[...][None]).reshape(PROJ_SUB, GQK)
        sin_t = (rs_ref[gr][:, None, :] + cs_ref[...][None]).reshape(PROJ_SUB, GQK)

        def rope(t):
            partner = jnp.where(first_half, pltpu.roll(t, GQK - 16, 1), pltpu.roll(t, 16, 1))
            return t * cos_t + partner * sin_t

        qg = rope(qg_raw) * (GLA_DK ** -0.5)
        kg = rope(kg_raw)

        cum3 = cum.reshape(nchunk, CHUNK, 2 * GQK)
        tot3 = cum3[:, CHUNK - 1:CHUNK, :]
        la3 = la.reshape(nchunk, CHUNK, 2 * GQK)
        dec_ref[0, hs // CHUNK:hs // CHUNK + nchunk] = jnp.exp2(tot3).reshape(nchunk, 2 * GQK)
        b_f = cum3[:, :, :GQK].reshape(PROJ_SUB, GQK)
        e_f = (tot3 - cum3)[:, :, :GQK].reshape(PROJ_SUB, GQK)
        b_b = (tot3 - cum3 + la3)[:, :, GQK:].reshape(PROJ_SUB, GQK)
        e_b = (cum3 - la3)[:, :, GQK:].reshape(PROJ_SUB, GQK)
        mg_ref[0, rows, MG_OG:MG_OG + GV] = _silu(proj(OFF_OG, GV)).astype(bf16)
        gf_ref[0, rows, G_Q:G_Q + GQK] = (qg * jnp.exp2(b_f)).astype(bf16)
        gf_ref[0, rows, G_KI:G_KI + GQK] = (kg * jnp.exp2(-b_f)).astype(bf16)
        ktf_ref[0, :, rows] = (kg * jnp.exp2(e_f)).T.astype(bf16)
        mg_ref[0, rows, MG_A:MG_A + D] = _sigmoid(proj(OFF_MA, D)).astype(bf16)
        gb_ref[0, rows, G_Q - GV:G_Q - GV + GQK] = (qg * jnp.exp2(b_b)).astype(bf16)
        gb_ref[0, rows, G_KI - GV:G_KI - GV + GQK] = (kg * jnp.exp2(-b_b)).astype(bf16)
        ktb_ref[0, :, rows] = (kg * jnp.exp2(e_b)).T.astype(bf16)
        mg_ref[0, rows, MG_B:MG_B + D] = _sigmoid(proj(OFF_MB, D)).astype(bf16)


def _rope_tables(L):
    n = GLA_DK // 4
    freqs = ROPE_THETA ** (-np.arange(n, dtype=np.float64) / n)
    lane = np.arange(GQK)
    fl = freqs[lane % n]
    is_row = (lane % GLA_DK) < GLA_DK // 2
    sign = np.where((lane % 32) < 16, -1.0, 1.0)
    rows = np.arange(L // GRID_W)[:, None] * fl[None]
    cols = np.arange(GRID_W)[:, None] * fl[None]
    rc = np.where(is_row, np.cos(rows), 0.0)
    rs = np.where(is_row, np.sin(rows) * sign, 0.0)
    cc = np.where(~is_row, np.cos(cols), 0.0)
    cs = np.where(~is_row, np.sin(cols) * sign, 0.0)
    return [jnp.asarray(t, f32) for t in (rc, rs, cc, cs)]


def _in_proj(x, mod3, norm1_g, w_t, gq_t, gk_t, bd, w_a, b_a, cast_ws):
    B, L, _ = x.shape
    tm = TM_PROJ
    rows_per = tm // GRID_W
    nstep = L // tm
    total = B * nstep

    def slab(w):
        R, C = w.shape
        if R % (total * BF16_SUBLANES) == 0:
            return pl.BlockSpec((R // total, C), lambda b, i: (b * nstep + i, 0))
        assert (2 * R) % (total * BF16_SUBLANES) == 0 and C % 256 == 0
        return pl.BlockSpec((2 * R // total, C // 2), lambda b, i: ((b * nstep + i) // 2, (b * nstep + i) % 2))
    rc, rs, cc, cs = _rope_tables(L)
    blk = np.kron(np.eye(256 // CHUNK), np.tril(np.ones((CHUNK, CHUNK)))).astype(np.float32)
    tl = jnp.asarray(blk, bf16)
    tok = lambda n: pl.BlockSpec((1, tm, n), lambda b, i: (b, i, 0))
    sds = lambda n, dt: jax.ShapeDtypeStruct((B, L, n), dt)
    return pl.pallas_call(
        _proj_kernel,
        grid=(B, L // tm),
        in_specs=[tok(D),
                  pl.BlockSpec((1, 1, D), lambda b, i: (b, 0, 0)),
                  pl.BlockSpec((1, 1, D), lambda b, i: (b, 0, 1)),
                  _const_spec((1, D)), _const_spec((D_IN, D)),
                  _const_spec((1, NA_W)), _const_spec((1, NA_W)), _const_spec((256, 256)),
                  pl.BlockSpec((rows_per, GQK), lambda b, i: (i, 0)),
                  pl.BlockSpec((rows_per, GQK), lambda b, i: (i, 0)),
                  _const_spec((GRID_W, GQK)), _const_spec((GRID_W, GQK)),
                  _const_spec((N_LR, 2 * GQK)), _const_spec((1, 2 * GQK)), _const_spec((256, 256))]
                 + [slab(w) for w in cast_ws],
        out_specs=[tok(W_NA), tok(W_GF), tok(W_GB), tok(W_MG),
                   pl.BlockSpec((1, tm // CHUNK, 2 * GQK), lambda b, i: (b, i, 0)),
                   pl.BlockSpec((1, GQK, tm), lambda b, i: (b, 0, i)),
                   pl.BlockSpec((1, GQK, tm), lambda b, i: (b, 0, i))]
                  + [slab(w) for w in cast_ws],
        out_shape=[sds(W_NA, bf16), sds(W_GF, bf16), sds(W_GB, bf16), sds(W_MG, bf16),
                   jax.ShapeDtypeStruct((B, L // CHUNK, 2 * GQK), f32),
                   jax.ShapeDtypeStruct((B, GQK, L), bf16), jax.ShapeDtypeStruct((B, GQK, L), bf16)]
                  + [jax.ShapeDtypeStruct(w.shape, bf16) for w in cast_ws],
        compiler_params=pltpu.CompilerParams(
            dimension_semantics=("parallel", "parallel"), vmem_limit_bytes=VMEM_LIMIT),
        name="in_proj",
    )(x, mod3, mod3, norm1_g, w_t, gq_t, gk_t, bd, rc, rs, cc, cs, w_a, b_a, tl, *cast_ws)


def _lane_blocks(t):
    return [t[:, c:c + 128] for c in range(0, t.shape[1], 128)]


def _na_kernel(q_ref, k_ref, v_ref, kc_ref, vc_ref, t2_ref, o_ref, sl_ref, sc_ref, m_ref):
    i = pl.program_id(1)
    rows = k_ref.shape[1] // GRID_W
    lane_head = lax.broadcasted_iota(jnp.int32, (GRID_W, 256), 1) >> 6
    hmask = [lane_head == h for h in range(4)]

    def window_start(r):
        rs = jnp.clip(r - NA_WR // 2, 0, rows - NA_WR)
        return rs, pl.multiple_of(rs * GRID_W, GRID_W)

    def produce(rr, bank, j):
        r = i * NA_R + rr
        rs, start = window_start(r)
        d0 = rs - r + (NA_WR - 1)
        row0 = pl.multiple_of(rr * GRID_W, GRID_W)
        for g in range(2):
            ls = slice(256 * g, 256 * g + 256)
            q = q_ref[0, pl.ds(row0, GRID_W), ls]
            qs = jnp.concatenate([jnp.where(m, q, jnp.zeros_like(q)) for m in hmask], axis=0)
            kw = k_ref[0, pl.ds(start, NA_WR * GRID_W), ls]
            bias = jnp.concatenate(
                [jnp.concatenate([t2_ref[4 * g + h, d0 + 2 * jj] for jj in range(NA_WR // 2)], axis=1)
                 for h in range(4)], axis=0)
            s_loc = _dot_nt(qs, kw) + bias
            s_ctx = _dot_nt(qs, kc_ref[0, :, ls])
            mx = functools.reduce(jnp.maximum, _lane_blocks(s_loc) + _lane_blocks(s_ctx))
            sl_ref[bank, j, g] = s_loc
            sc_ref[bank, j, g] = s_ctx
            m_ref[bank, j, g] = jnp.broadcast_to(jnp.max(mx, axis=-1, keepdims=True), (4 * GRID_W, 128))

    def consume(rr, bank, j):
        _, start = window_start(i * NA_R + rr)
        row0 = pl.multiple_of(rr * GRID_W, GRID_W)
        outs = []
        for g in range(2):
            ls = slice(256 * g, 256 * g + 256)
            m = m_ref[bank, j, g]
            p_loc = [jnp.exp2(sl_ref[bank, j, g, :, c:c + 128] - m) for c in range(0, NA_WR * GRID_W, 128)]
            p_ctx = [jnp.exp2(sc_ref[bank, j, g, :, c:c + 128] - m) for c in range(0, CTX, 128)]
            den = jnp.sum(functools.reduce(jnp.add, p_loc + p_ctx), axis=-1, keepdims=True)
            vw = v_ref[0, pl.ds(start, NA_WR * GRID_W), ls]
            acc = (_dot(jnp.concatenate(p_loc, axis=1).astype(bf16), vw)
                   + _dot(jnp.concatenate(p_ctx, axis=1).astype(bf16), vc_ref[0, :, ls]))
            acc = acc * (1.0 / den)
            o = jnp.zeros((GRID_W, 256), f32)
            for h in range(4):
                o = o + jnp.where(hmask[h], acc[h * GRID_W:(h + 1) * GRID_W], 0.0)
            outs.append(o)
        o_ref[0, pl.ds(row0, GRID_W), :] = jnp.concatenate(outs, axis=1).astype(bf16)

    for j in range(NA_RI):
        produce(jnp.int32(j), 0, j)

    def step(t, carry):
        bank = t & 1
        for j in range(NA_RI):
            consume(t * NA_RI + j, bank, j)
        for j in range(NA_RI):
            produce(jnp.minimum((t + 1) * NA_RI + j, NA_R - 1), 1 - bank, j)
        return carry

    lax.fori_loop(0, NA_R // NA_RI, step, 0)


def _na_attn(qkv, kc, vc, t2):
    B, L, _ = qkv.shape
    nblk = L // (NA_R * GRID_W)
    return pl.pallas_call(
        _na_kernel,
        grid=(B, nblk),
        in_specs=[pl.BlockSpec((1, NA_R * GRID_W, NA_W), lambda b, i: (b, i, NA_Q // NA_W)),
                  pl.BlockSpec((1, L, NA_W), lambda b, i: (b, 0, NA_K // NA_W), pipeline_mode=pl.Buffered(1)),
                  pl.BlockSpec((1, L, NA_W), lambda b, i: (b, 0, NA_V // NA_W), pipeline_mode=pl.Buffered(1)),
                  pl.BlockSpec((1, CTX, NA_W), lambda b, i: (b, 0, 0)),
                  pl.BlockSpec((1, CTX, NA_W), lambda b, i: (b, 0, 0)),
                  _const_spec((NA_H, 2 * NA_WR - 1, GRID_W, 128))],
        out_specs=pl.BlockSpec((1, NA_R * GRID_W, NA_W), lambda b, i: (b, i, 0)),
        out_shape=jax.ShapeDtypeStruct((B, L, NA_W), bf16),
        scratch_shapes=[pltpu.VMEM((2, NA_RI, 2, 4 * GRID_W, NA_WR * GRID_W), f32),
                        pltpu.VMEM((2, NA_RI, 2, 4 * GRID_W, CTX), f32),
                        pltpu.VMEM((2, NA_RI, 2, 4 * GRID_W, 128), f32)],
        compiler_params=pltpu.CompilerParams(
            dimension_semantics=("parallel", "parallel"), vmem_limit_bytes=VMEM_LIMIT),
        name="na_attn",
    )(qkv, qkv, qkv, kc, vc, t2)


def _gla_chunk(q, ki, v, ket, dcol, s_ref, tri, hmask):
    qs = jnp.concatenate([jnp.where(m, q, jnp.zeros_like(q)) for m in hmask], axis=0)
    att = jnp.where(tri, _dot_nt(qs, ki), 0.0).astype(bf16)
    o = _dot(q, s_ref[...].astype(bf16))
    intra = [_dot(att[h * CHUNK:(h + 1) * CHUNK], v[:, h * GLA_DV:(h + 1) * GLA_DV]) for h in range(GLA_H)]
    o = o + jnp.concatenate(intra, axis=1)
    for h in range(GLA_H):
        rs = slice(h * GLA_DK, (h + 1) * GLA_DK)
        cs = slice(h * GLA_DV, (h + 1) * GLA_DV)
        s_ref[rs, cs] = dcol[rs] * s_ref[rs, cs] + _dot(ket[rs], v[:, cs])
    return o


def _gla_kernel(gf_ref, ktf_ref, decf_ref, gb_ref, vb_ref, ktb_ref, decb_ref,
                s0f_ref, s0b_ref, of_ref, ob_ref, sf_ref, sb_ref):
    @pl.when(pl.program_id(1) == 0)
    def _():
        sf_ref[...] = s0f_ref[0]
        sb_ref[...] = s0b_ref[0]

    lane_head = lax.broadcasted_iota(jnp.int32, (CHUNK, GQK), 1) >> 6
    hmask = [lane_head == h for h in range(GLA_H)]
    ri = lax.broadcasted_iota(jnp.int32, (GLA_H * CHUNK, CHUNK), 0) & (CHUNK - 1)
    ci = lax.broadcasted_iota(jnp.int32, (GLA_H * CHUNK, CHUNK), 1)
    tri_f = ri >= ci
    tri_b = ri <= ci
    dect_f = decf_ref[0][:, :GQK].T
    dect_b = decb_ref[0][:, GQK:].T

    def k_end(kt_ref, c):
        kt = kt_ref[0, :, (c // 2) * 2 * CHUNK:(c // 2 + 1) * 2 * CHUNK]
        if c % 2:
            kt = pltpu.roll(kt, CHUNK, 1)
        return kt[:, :CHUNK]

    for c in range(GLA_CB):
        ts = slice(c * CHUNK, (c + 1) * CHUNK)
        of_ref[0, ts, :] = _gla_chunk(gf_ref[0, ts, G_Q:G_Q + GQK], gf_ref[0, ts, G_KI:G_KI + GQK],
                                      gf_ref[0, ts, G_V:G_V + GV], k_end(ktf_ref, c),
                                      dect_f[:, c:c + 1], sf_ref, tri_f, hmask)
        cb = GLA_CB - 1 - c
        tb = slice(cb * CHUNK, (cb + 1) * CHUNK)
        ob_ref[0, tb, :] = _gla_chunk(gb_ref[0, tb, G_Q - GV:G_Q - GV + GQK], gb_ref[0, tb, G_KI - GV:G_KI - GV + GQK],
                                      vb_ref[0, tb, :], k_end(ktb_ref, cb),
                                      dect_b[:, cb:cb + 1], sb_ref, tri_b, hmask)


def _gla_scan(gf, gb, ktf, ktb, dec, s0f, s0b):
    B, L, _ = gf.shape
    tm = GLA_CB * CHUNK
    nb = L // tm
    fw = lambda n: pl.BlockSpec((1, tm, n), lambda b, i: (b, i, 0))
    bw = lambda n: pl.BlockSpec((1, tm, n), lambda b, i: (b, nb - 1 - i, 0))
    st = pl.BlockSpec((1, GQK, GV), lambda b, i: (b, 0, 0))
    return pl.pallas_call(
        _gla_kernel,
        grid=(B, nb),
        in_specs=[fw(W_GF),
                  pl.BlockSpec((1, GQK, tm), lambda b, i: (b, 0, i)),
                  pl.BlockSpec((1, GLA_CB, 2 * GQK), lambda b, i: (b, i, 0)),
                  bw(W_GB), bw(GV),
                  pl.BlockSpec((1, GQK, tm), lambda b, i: (b, 0, nb - 1 - i)),
                  pl.BlockSpec((1, GLA_CB, 2 * GQK), lambda b, i: (b, nb - 1 - i, 0)),
                  st, st],
        out_specs=[fw(GV), bw(GV)],
        out_shape=[jax.ShapeDtypeStruct((B, L, GV), f32)] * 2,
        scratch_shapes=[pltpu.VMEM((GQK, GV), f32), pltpu.VMEM((GQK, GV), f32)],
        compiler_params=pltpu.CompilerParams(
            dimension_semantics=("parallel", "arbitrary"), vmem_limit_bytes=VMEM_LIMIT),
        name="gla_scan",
    )(gf, ktf, dec, gb, gf, ktb, dec, s0f, s0b)


def _merge_ffn_kernel(x_ref, ona_ref, of_ref, ob_ref, mg_ref,
                      g1_ref, sh2_ref, sc2_ref, g2_ref, n2_ref, gg_ref,
                      wna_ref, wgl_ref, wo_ref, wg_ref, wu_ref, wd_ref, o_ref):
    x1s, h2s = [], []
    for hs in range(0, x_ref.shape[1], FFN_SUB):
        rows = slice(hs, hs + FFN_SUB)
        og = of_ref[0, rows] + ob_ref[0, rows]
        parts = []
        for h in range(GLA_H):
            seg = og[:, h * GLA_DV:(h + 1) * GLA_DV]
            ms = jnp.mean(seg * seg, axis=-1, keepdims=True)
            parts.append(seg * lax.rsqrt(ms + EPS) * gg_ref[...])
        ogl = (jnp.concatenate(parts, axis=1) * mg_ref[0, rows, MG_OG:MG_OG + GV].astype(f32)).astype(bf16)
        y = (mg_ref[0, rows, MG_A:MG_A + D].astype(f32) * _dot(ona_ref[0, rows], wna_ref[...])
             + mg_ref[0, rows, MG_B:MG_B + D].astype(f32) * _dot(ogl, wgl_ref[...]))
        x1 = x_ref[0, rows] + g1_ref[0] * _dot(y.astype(bf16), wo_ref[...])
        x1s.append(x1)
        h2s.append(_norm_mod(x1, n2_ref[...], sh2_ref[0], sc2_ref[0]).astype(bf16))
    x1 = jnp.concatenate(x1s, axis=0)
    h2 = jnp.concatenate(h2s, axis=0)
    act = (_silu(_dot(h2, wg_ref[...])) * _dot(h2, wu_ref[...])).astype(bf16)
    o_ref[0] = x1 + g2_ref[0] * _dot(act, wd_ref[...])


def _merge_ffn(x, ona, of, ob, mg, mod3, norm2_g, gg, wna, wgl, wo, wg, wu, wd):
    B, L, _ = x.shape
    tm = TM_FFN
    tok = lambda n: pl.BlockSpec((1, tm, n), lambda b, i: (b, i, 0))
    modv = lambda j: pl.BlockSpec((1, 1, D), lambda b, i: (b, 0, j))
    return pl.pallas_call(
        _merge_ffn_kernel,
        grid=(B, L // tm),
        in_specs=[tok(D), tok(NA_W), tok(GV), tok(GV), tok(W_MG),
                  modv(2), modv(3), modv(4), modv(5),
                  _const_spec((1, D)), _const_spec((1, GLA_DV)),
                  _const_spec((NA_W, D)), _const_spec((GV, D)), _const_spec((D, D)),
                  _const_spec((D, D_FF)), _const_spec((D, D_FF)), _const_spec((D_FF, D))],
        out_specs=tok(D),
        out_shape=jax.ShapeDtypeStruct((B, L, D), f32),
        compiler_params=pltpu.CompilerParams(
            dimension_semantics=("parallel", "parallel"), vmem_limit_bytes=VMEM_LIMIT),
        name="merge_ffn",
    )(x, ona, of, ob, mg, mod3, mod3, mod3, mod3, norm2_g, gg, wna, wgl, wo, wg, wu, wd)


def kernel(x, c, ctx, c_ctx, w_mod, b_mod, norm1_g, norm2_g, w_in, na_q_norm_g, na_k_norm_g, na_rpb,
           gla_w_alpha, gla_b_alpha, gla_norm_g, w_branch_na, w_branch_gla, w_out,
           w_ffn_gate, w_ffn_up, w_ffn_down):
    B, L, _ = x.shape
    assert w_mod.shape[0] == 1 and L % (NA_R * GRID_W) == 0 and L % TM_PROJ == 0

    w_t = _cast_bf16(jnp.transpose(w_in[0]), D_IN // 7)
    zero = jnp.zeros((GLA_RANK, GQK), f32)
    w_a = jnp.concatenate([jnp.concatenate([gla_w_alpha[0, 0], zero], axis=1),
                           jnp.concatenate([zero, gla_w_alpha[0, 1]], axis=1)], axis=0).astype(bf16)
    b_a = gla_b_alpha[0].reshape(1, 2 * GQK)
    gq_t = jnp.tile(na_q_norm_g[0] * (NA_DH ** -0.5 * LOG2E), NA_H).reshape(1, NA_W)
    gk_t = jnp.tile(na_k_norm_g[0], NA_H).reshape(1, NA_W)
    bd = jnp.asarray(np.kron(np.eye(4), np.full((NA_DH, NA_DH), 1.0 / NA_DH)), bf16)

    mod3 = _adaln_mod(c, c_ctx, w_mod[0], b_mod[0])
    t2 = _rpb_expand(na_rpb[0])

    kc, vc, s0f, s0b = _ctx_side(ctx, mod3, norm1_g, w_t, w_a, b_a, gk_t, bd, B)
    later_ws = [w_branch_na[0], w_branch_gla[0], w_out[0], w_ffn_gate[0], w_ffn_up[0], w_ffn_down[0]]
    na_qkv, gla_f, gla_b, gates, dec, ktf, ktb, *later_bf = _in_proj(
        x, mod3, norm1_g, w_t, gq_t, gk_t, bd, w_a, b_a, later_ws)
    ona = _na_attn(na_qkv, kc, vc, t2)
    of, ob = _gla_scan(gla_f, gla_b, ktf, ktb, dec, s0f, s0b)
    return _merge_ffn(x, ona, of, ob, gates, mod3, norm2_g, gla_norm_g, *later_bf)
```

```python
import functools

import numpy as np
import jax
import jax.numpy as jnp
from jax import lax
from jax.experimental import pallas as pl
from jax.experimental.pallas import tpu as pltpu

D = 1024
GRID_W = 64
CTX = 256
NA_H = 8
NA_DH = 64
NA_WR = 8
NA_WC = 16
GLA_H = 4
GLA_DK = 64
GLA_DV = 128
GLA_RANK = 16
GLA_TAU = 16.0
GLA_LA_MIN = -1.0
CHUNK = 64
ROPE_THETA = 10000.0
D_FF = 2816
EPS = 1e-6
NEG_INF = -1e30
NA_W = NA_H * NA_DH
GQK = GLA_H * GLA_DK
GV = GLA_H * GLA_DV

OFF_QNA, OFF_KNA, OFF_VNA = 0, 512, 1024
OFF_QG, OFF_KG, OFF_VG, OFF_OG = 1536, 1792, 2048, 2560
OFF_LR, OFF_MA, OFF_MB = 3072, 3104, 4128
N_LR = 2 * GLA_RANK
N_PROJ_OUT = 7
NA_Q, NA_K, NA_V, W_NA = 0, 512, 1024, 1536
G_V, G_Q, G_KI, W_GF = 0, 512, 768, 1024
W_GB = W_GF - GV
MG_A, MG_B, MG_OG, W_MG = 0, 1024, 2048, 2560
D_IN = 5152
LOG2E = 1.4426950408889634

LANES = 128
SUBLANES = 8
BF16_SUBLANES = 16
MXU_DIM = 256
NA_GH = MXU_DIM // NA_DH
NA_NG = NA_H // NA_GH
VMEM_LIMIT = 56 * 1024 * 1024

W_IN_CAST_ROWS = 46 * BF16_SUBLANES
TM_PROJ = 512
PROJ_SUB = 256
TM_FFN = 512
FFN_SUB = 256
NA_R = 64
NA_RI = 4
GLA_CB = 32

f32 = jnp.float32
bf16 = jnp.bfloat16


def _const_spec(shape):
    nd = len(shape)
    return pl.BlockSpec(shape, lambda *_: (0,) * nd, pipeline_mode=pl.Buffered(1))


def _dot(a, b):
    return jnp.dot(a, b, preferred_element_type=f32)


def _dot_nt(a, b):
    return lax.dot_general(a, b, (((1,), (1,)), ((), ())), preferred_element_type=f32)


def _split2(x):
    hi = x.astype(bf16)
    lo = (x - hi.astype(f32)).astype(bf16)
    return hi, lo


def _sigmoid(x):
    return 0.5 * jnp.tanh(0.5 * x) + 0.5


def _silu(x):
    return x * _sigmoid(x)


def _split3_dot(r, oh):
    h1 = r.astype(bf16)
    r2 = r - h1.astype(f32)
    h2 = r2.astype(bf16)
    h3 = (r2 - h2.astype(f32)).astype(bf16)
    return _dot(h1, oh) + _dot(h2, oh) + _dot(h3, oh)


def _rpb_expand_kernel(r0_ref, r1_ref, oh0_ref, oh1_ref, mask_ref, o_ref):
    gathered = _split3_dot(r0_ref[...], oh0_ref[...]) + _split3_dot(r1_ref[...], oh1_ref[...])
    tiles = gathered * LOG2E + mask_ref[...]
    for q in range(GRID_W):
        o_ref[:, q, :] = tiles[:o_ref.shape[0], q * LANES:(q + 1) * LANES]


def _rpb_expand(rpb):
    nd = 2 * NA_WR - 1
    nrow = NA_H * nd
    ncol = 2 * NA_WC - 1
    nxt = jnp.concatenate([rpb[:, 1:], jnp.zeros_like(rpb[:, :1])], axis=1)
    pad = ((0, LANES - nrow), (0, LANES - ncol))
    r0 = jnp.pad(rpb.reshape(nrow, ncol), pad)
    r1 = jnp.pad(nxt.reshape(nrow, ncol), pad)
    q = np.arange(GRID_W)
    cs = np.clip(q - NA_WC // 2, 0, GRID_W - NA_WC)
    in_win = (q[None, :] >= cs[:, None]) & (q[None, :] < cs[:, None] + NA_WC)
    dc = np.clip(q[None, :] - q[:, None], -(NA_WC - 1), NA_WC - 1) + NA_WC - 1
    oh = np.zeros((2, LANES, GRID_W, LANES), np.float32)
    qq, kk = np.nonzero(in_win)
    oh[0, dc[qq, kk], qq, kk] = 1.0
    oh[1, dc[qq, kk], qq, kk + GRID_W] = 1.0
    oh = oh.reshape(2, LANES, GRID_W * LANES)
    mask = np.where(np.concatenate([in_win, in_win], axis=1), 0.0, NEG_INF).reshape(1, GRID_W * LANES)
    out = pl.pallas_call(
        _rpb_expand_kernel,
        out_shape=jax.ShapeDtypeStruct((nrow, GRID_W, LANES), f32),
        name="rpb_expand",
    )(r0, r1, jnp.asarray(oh[0], bf16), jnp.asarray(oh[1], bf16), jnp.asarray(mask, f32))
    return out.reshape(NA_H, nd, GRID_W, LANES)


def _cast_kernel(x_ref, o_ref):
    o_ref[...] = x_ref[...].astype(bf16)


def _cast_bf16(w, rows):
    R, C = w.shape
    return pl.pallas_call(
        _cast_kernel,
        grid=(R // rows,),
        in_specs=[pl.BlockSpec((rows, C), lambda j: (j, 0))],
        out_specs=pl.BlockSpec((rows, C), lambda j: (j, 0)),
        out_shape=jax.ShapeDtypeStruct((R, C), bf16),
        name="cast_bf16",
    )(w)


def _mod_kernel(a_ref, w_ref, b_ref, o_ref):
    a = _silu(a_ref[...]).astype(bf16)
    o_ref[...] = (_dot(a, w_ref[...].astype(bf16)) + b_ref[...])[:, None, :]


def _adaln_mod(c, c_ctx, w_mod, b_mod):
    B = c.shape[0]
    assert B < SUBLANES
    a = jnp.concatenate([c, c_ctx[None], jnp.zeros((SUBLANES - B - 1, D), f32)], axis=0)
    n = w_mod.shape[1]
    return pl.pallas_call(
        _mod_kernel,
        grid=(n // D,),
        in_specs=[pl.BlockSpec((SUBLANES, D), lambda j: (0, 0)),
                  pl.BlockSpec((D, D), lambda j: (0, j)),
                  pl.BlockSpec((1, D), lambda j: (0, j))],
        out_specs=pl.BlockSpec((SUBLANES, 1, D), lambda j: (0, 0, j)),
        out_shape=jax.ShapeDtypeStruct((SUBLANES, 1, n), f32),
        name="adaln_mod",
    )(a, w_mod, b_mod.reshape(1, n))


def _norm_mod(x, g, sh, sc):
    ms = jnp.mean(x * x, axis=-1, keepdims=True)
    return (x * lax.rsqrt(ms + EPS)) * (g * (1.0 + sc)) + sh


def _head_rms(t, bd, g):
    sq = (t * t).astype(bf16)
    ms = jnp.concatenate([_dot(sq[:, :MXU_DIM], bd), _dot(sq[:, MXU_DIM:], bd)], axis=1)
    return t * lax.rsqrt(ms + EPS) * g


def _log_alpha(zl, wa_ref, ba_ref):
    z = _dot(zl.astype(bf16), wa_ref[...]) + ba_ref[...]
    soft = jnp.log2(1.0 + jnp.exp2(z * -LOG2E))
    return jnp.maximum(soft * (-1.0 / GLA_TAU), GLA_LA_MIN * LOG2E)


def _ctx_kernel(ctx_ref, sh_ref, sc_ref, g1_ref, wk_ref, wv_ref, wkg_ref, wvg_ref, wlr_ref,
                wa_ref, ba_ref, gk_ref, bd_ref, su_ref, sl_ref,
                kc_ref, vc_ref, sf_ref, sb_ref):
    h = _norm_mod(ctx_ref[0], g1_ref[...], sh_ref[0], sc_ref[0]).astype(bf16)
    k = _dot_nt(h, wk_ref[...])
    kc_ref[0] = _head_rms(k, bd_ref[...], gk_ref[...]).astype(bf16)
    vc_ref[0] = _dot_nt(h, wv_ref[...]).astype(bf16)
    kg = _dot_nt(h, wkg_ref[...])
    vg = _dot_nt(h, wvg_ref[...]).astype(bf16)
    la = _log_alpha(_dot_nt(h, wlr_ref[...]), wa_ref, ba_ref)
    hi, lo = _split2(la)
    ef = _dot(su_ref[...], hi[:, :GQK]) + _dot(su_ref[...], lo[:, :GQK])
    eb = _dot(sl_ref[...], hi[:, GQK:]) + _dot(sl_ref[...], lo[:, GQK:])
    r = lax.broadcasted_iota(jnp.int32, (GQK, GV), 0) // GLA_DK
    cc = lax.broadcasted_iota(jnp.int32, (GQK, GV), 1) // GLA_DV
    diag = r == cc
    for e, out in ((ef, sf_ref), (eb, sb_ref)):
        kw = (kg * jnp.exp2(e)).astype(bf16)
        u = _dot(kw.T, vg)
        out[0] = jnp.where(diag, u, 0.0)


def _ctx_side(ctx, mod3, norm1_g, w_t, w_a, b_a, gk_t, bd, n_b):
    su = jnp.asarray(np.triu(np.ones((CTX, CTX), np.float32), 1), bf16)
    sl = jnp.asarray(np.tril(np.ones((CTX, CTX), np.float32), -1), bf16)
    B = ctx.shape[0]
    cst = lambda shape, idx: pl.BlockSpec(shape, lambda b: idx)
    return pl.pallas_call(
        _ctx_kernel,
        grid=(B,),
        in_specs=[pl.BlockSpec((1, CTX, D), lambda b: (b, 0, 0)),
                  cst((1, 1, D), (n_b, 0, 0)), cst((1, 1, D), (n_b, 0, 1)),
                  cst((1, D), (0, 0)),
                  cst((NA_W, D), (OFF_KNA // NA_W, 0)), cst((NA_W, D), (OFF_VNA // NA_W, 0)),
                  cst((GQK, D), (OFF_KG // GQK, 0)), cst((GV, D), (OFF_VG // GV, 0)),
                  cst((N_LR, D), (OFF_LR // N_LR, 0)),
                  cst((N_LR, 2 * GQK), (0, 0)), cst((1, 2 * GQK), (0, 0)),
                  cst((1, NA_W), (0, 0)), cst((MXU_DIM, MXU_DIM), (0, 0)),
                  cst((CTX, CTX), (0, 0)), cst((CTX, CTX), (0, 0))],
        out_specs=[pl.BlockSpec((1, CTX, NA_W), lambda b: (b, 0, 0)),
                   pl.BlockSpec((1, CTX, NA_W), lambda b: (b, 0, 0)),
                   pl.BlockSpec((1, GQK, GV), lambda b: (b, 0, 0)),
                   pl.BlockSpec((1, GQK, GV), lambda b: (b, 0, 0))],
        out_shape=[jax.ShapeDtypeStruct((B, CTX, NA_W), bf16),
                   jax.ShapeDtypeStruct((B, CTX, NA_W), bf16),
                   jax.ShapeDtypeStruct((B, GQK, GV), f32),
                   jax.ShapeDtypeStruct((B, GQK, GV), f32)],
        compiler_params=pltpu.CompilerParams(vmem_limit_bytes=VMEM_LIMIT),
        name="ctx_side",
    )(ctx, mod3, mod3, norm1_g, w_t, w_t, w_t, w_t, w_t, w_a, b_a, gk_t, bd, su, sl)


def _proj_kernel(x_ref, sh_ref, sc_ref, g1_ref, w_ref, gq_ref, gk_ref, bd_ref,
                 rc_ref, rs_ref, cc_ref, cs_ref, wa_ref, ba_ref, tl_ref, *refs):
    n_cast = (len(refs) - N_PROJ_OUT) // 2
    cast_in, refs = refs[:n_cast], refs[n_cast:]
    na_ref, gf_ref, gb_ref, mg_ref, dec_ref, ktf_ref, ktb_ref = refs[:N_PROJ_OUT]
    cast_out = refs[N_PROJ_OUT:]
    for src, dst in zip(cast_in, cast_out):
        dst[...] = src[...].astype(bf16)

    bd = bd_ref[...]
    tl = tl_ref[...]
    nrow = PROJ_SUB // GRID_W
    nchunk = PROJ_SUB // CHUNK
    lane = lax.broadcasted_iota(jnp.int32, (PROJ_SUB, GQK), 1)
    first_half = (lane & 31) < 16

    for hs in range(0, x_ref.shape[1], PROJ_SUB):
        rows = slice(hs, hs + PROJ_SUB)
        hb = _norm_mod(x_ref[0, rows], g1_ref[...], sh_ref[0], sc_ref[0]).astype(bf16)

        def proj(off, n):
            return _dot_nt(hb, w_ref[off:off + n, :])

        zl = proj(OFF_LR, N_LR)
        q_raw = proj(OFF_QNA, NA_W)
        la = _log_alpha(zl, wa_ref, ba_ref)
        k_raw = proj(OFF_KNA, NA_W)
        hi, lo = _split2(la)
        na_ref[0, rows, NA_Q:NA_Q + NA_W] = _head_rms(q_raw, bd, gq_ref[...]).astype(bf16)
        cum = jnp.concatenate(
            [_dot(tl, hi[s:s + MXU_DIM]) + _dot(tl, lo[s:s + MXU_DIM]) for s in range(0, PROJ_SUB, MXU_DIM)], axis=0)
        na_ref[0, rows, NA_K:NA_K + NA_W] = _head_rms(k_raw, bd, gk_ref[...]).astype(bf16)
        qg_raw = proj(OFF_QG, GQK)
        kg_raw = proj(OFF_KG, GQK)
        na_ref[0, rows, NA_V:NA_V + NA_W] = proj(OFF_VNA, NA_W).astype(bf16)
        gf_ref[0, rows, G_V:G_V + GV] = proj(OFF_VG, GV).astype(bf16)

        gr = slice(hs // GRID_W, hs // GRID_W + nrow)
        cos_t = (rc_ref[gr][:, None, :] + cc_ref[...][None]).reshape(PROJ_SUB, GQK)
        sin_t = (rs_ref[gr][:, None, :] + cs_ref[...][None]).reshape(PROJ_SUB, GQK)

        def rope(t):
            partner = jnp.where(first_half, pltpu.roll(t, GQK - 16, 1), pltpu.roll(t, 16, 1))
            return t * cos_t + partner * sin_t

        qg = rope(qg_raw) * (GLA_DK ** -0.5)
        kg = rope(kg_raw)

        cum3 = cum.reshape(nchunk, CHUNK, 2 * GQK)
        tot3 = cum3[:, CHUNK - 1:CHUNK, :]
        la3 = la.reshape(nchunk, CHUNK, 2 * GQK)
        dec_ref[0, hs // CHUNK:hs // CHUNK + nchunk] = jnp.exp2(tot3).reshape(nchunk, 2 * GQK)
        b_f = cum3[:, :, :GQK].reshape(PROJ_SUB, GQK)
        e_f = (tot3 - cum3)[:, :, :GQK].reshape(PROJ_SUB, GQK)
        b_b = (tot3 - cum3 + la3)[:, :, GQK:].reshape(PROJ_SUB, GQK)
        e_b = (cum3 - la3)[:, :, GQK:].reshape(PROJ_SUB, GQK)
        mg_ref[0, rows, MG_OG:MG_OG + GV] = _silu(proj(OFF_OG, GV)).astype(bf16)
        gf_ref[0, rows, G_Q:G_Q + GQK] = (qg * jnp.exp2(b_f)).astype(bf16)
        gf_ref[0, rows, G_KI:G_KI + GQK] = (kg * jnp.exp2(-b_f)).astype(bf16)
        ktf_ref[0, :, rows] = (kg * jnp.exp2(e_f)).T.astype(bf16)
        mg_ref[0, rows, MG_A:MG_A + D] = _sigmoid(proj(OFF_MA, D)).astype(bf16)
        gb_ref[0, rows, G_Q - GV:G_Q - GV + GQK] = (qg * jnp.exp2(b_b)).astype(bf16)
        gb_ref[0, rows, G_KI - GV:G_KI - GV + GQK] = (kg * jnp.exp2(-b_b)).astype(bf16)
        ktb_ref[0, :, rows] = (kg * jnp.exp2(e_b)).T.astype(bf16)
        mg_ref[0, rows, MG_B:MG_B + D] = _sigmoid(proj(OFF_MB, D)).astype(bf16)


def _rope_tables(L):
    n = GLA_DK // 4
    freqs = ROPE_THETA ** (-np.arange(n, dtype=np.float64) / n)
    lane = np.arange(GQK)
    fl = freqs[lane % n]
    is_row = (lane % GLA_DK) < GLA_DK // 2
    sign = np.where((lane % 32) < 16, -1.0, 1.0)
    rows = np.arange(L // GRID_W)[:, None] * fl[None]
    cols = np.arange(GRID_W)[:, None] * fl[None]
    rc = np.where(is_row, np.cos(rows), 0.0)
    rs = np.where(is_row, np.sin(rows) * sign, 0.0)
    cc = np.where(~is_row, np.cos(cols), 0.0)
    cs = np.where(~is_row, np.sin(cols) * sign, 0.0)
    return [jnp.asarray(t, f32) for t in (rc, rs, cc, cs)]


def _in_proj(x, mod3, norm1_g, w_t, gq_t, gk_t, bd, w_a, b_a, cast_ws):
    B, L, _ = x.shape
    tm = TM_PROJ
    rows_per = tm // GRID_W
    nstep = L // tm
    total = B * nstep

    def slab(w):
        R, C = w.shape
        if R % (total * BF16_SUBLANES) == 0:
            return pl.BlockSpec((R // total, C), lambda b, i: (b * nstep + i, 0))
        assert (2 * R) % (total * BF16_SUBLANES) == 0 and C % (2 * LANES) == 0
        return pl.BlockSpec((2 * R // total, C // 2), lambda b, i: ((b * nstep + i) // 2, (b * nstep + i) % 2))
    rc, rs, cc, cs = _rope_tables(L)
    blk = np.kron(np.eye(MXU_DIM // CHUNK), np.tril(np.ones((CHUNK, CHUNK)))).astype(np.float32)
    tl = jnp.asarray(blk, bf16)
    tok = lambda n: pl.BlockSpec((1, tm, n), lambda b, i: (b, i, 0))
    sds = lambda n, dt: jax.ShapeDtypeStruct((B, L, n), dt)
    return pl.pallas_call(
        _proj_kernel,
        grid=(B, L // tm),
        in_specs=[tok(D),
                  pl.BlockSpec((1, 1, D), lambda b, i: (b, 0, 0)),
                  pl.BlockSpec((1, 1, D), lambda b, i: (b, 0, 1)),
                  _const_spec((1, D)), _const_spec((D_IN, D)),
                  _const_spec((1, NA_W)), _const_spec((1, NA_W)), _const_spec((MXU_DIM, MXU_DIM)),
                  pl.BlockSpec((rows_per, GQK), lambda b, i: (i, 0)),
                  pl.BlockSpec((rows_per, GQK), lambda b, i: (i, 0)),
                  _const_spec((GRID_W, GQK)), _const_spec((GRID_W, GQK)),
                  _const_spec((N_LR, 2 * GQK)), _const_spec((1, 2 * GQK)), _const_spec((MXU_DIM, MXU_DIM))]
                 + [slab(w) for w in cast_ws],
        out_specs=[tok(W_NA), tok(W_GF), tok(W_GB), tok(W_MG),
                   pl.BlockSpec((1, tm // CHUNK, 2 * GQK), lambda b, i: (b, i, 0)),
                   pl.BlockSpec((1, GQK, tm), lambda b, i: (b, 0, i)),
                   pl.BlockSpec((1, GQK, tm), lambda b, i: (b, 0, i))]
                  + [slab(w) for w in cast_ws],
        out_shape=[sds(W_NA, bf16), sds(W_GF, bf16), sds(W_GB, bf16), sds(W_MG, bf16),
                   jax.ShapeDtypeStruct((B, L // CHUNK, 2 * GQK), f32),
                   jax.ShapeDtypeStruct((B, GQK, L), bf16), jax.ShapeDtypeStruct((B, GQK, L), bf16)]
                  + [jax.ShapeDtypeStruct(w.shape, bf16) for w in cast_ws],
        compiler_params=pltpu.CompilerParams(
            dimension_semantics=("parallel", "parallel"), vmem_limit_bytes=VMEM_LIMIT),
        name="in_proj",
    )(x, mod3, mod3, norm1_g, w_t, gq_t, gk_t, bd, rc, rs, cc, cs, w_a, b_a, tl, *cast_ws)


def _lane_blocks(t):
    return [t[:, c:c + LANES] for c in range(0, t.shape[1], LANES)]


def _na_kernel(q_ref, k_ref, v_ref, kc_ref, vc_ref, t2_ref, o_ref, sl_ref, sc_ref, m_ref):
    i = pl.program_id(1)
    rows = k_ref.shape[1] // GRID_W
    lane_head = lax.broadcasted_iota(jnp.int32, (GRID_W, MXU_DIM), 1) // NA_DH
    hmask = [lane_head == h for h in range(NA_GH)]

    def window_start(r):
        rs = jnp.clip(r - NA_WR // 2, 0, rows - NA_WR)
        return rs, pl.multiple_of(rs * GRID_W, GRID_W)

    def produce(rr, bank, j):
        r = i * NA_R + rr
        rs, start = window_start(r)
        d0 = rs - r + (NA_WR - 1)
        row0 = pl.multiple_of(rr * GRID_W, GRID_W)
        for g in range(NA_NG):
            ls = slice(MXU_DIM * g, MXU_DIM * (g + 1))
            q = q_ref[0, pl.ds(row0, GRID_W), ls]
            qs = jnp.concatenate([jnp.where(m, q, jnp.zeros_like(q)) for m in hmask], axis=0)
            kw = k_ref[0, pl.ds(start, NA_WR * GRID_W), ls]
            bias = jnp.concatenate(
                [jnp.concatenate([t2_ref[NA_GH * g + h, d0 + 2 * jj] for jj in range(NA_WR // 2)], axis=1)
                 for h in range(NA_GH)], axis=0)
            s_loc = _dot_nt(qs, kw) + bias
            s_ctx = _dot_nt(qs, kc_ref[0, :, ls])
            mx = functools.reduce(jnp.maximum, _lane_blocks(s_loc) + _lane_blocks(s_ctx))
            sl_ref[bank, j, g] = s_loc
            sc_ref[bank, j, g] = s_ctx
            m_ref[bank, j, g] = jnp.broadcast_to(jnp.max(mx, axis=-1, keepdims=True), (NA_GH * GRID_W, LANES))

    def consume(rr, bank, j):
        _, start = window_start(i * NA_R + rr)
        row0 = pl.multiple_of(rr * GRID_W, GRID_W)
        outs = []
        for g in range(NA_NG):
            ls = slice(MXU_DIM * g, MXU_DIM * (g + 1))
            m = m_ref[bank, j, g]
            p_loc = [jnp.exp2(sl_ref[bank, j, g, :, c:c + LANES] - m) for c in range(0, NA_WR * GRID_W, LANES)]
            p_ctx = [jnp.exp2(sc_ref[bank, j, g, :, c:c + LANES] - m) for c in range(0, CTX, LANES)]
            den = jnp.sum(functools.reduce(jnp.add, p_loc + p_ctx), axis=-1, keepdims=True)
            vw = v_ref[0, pl.ds(start, NA_WR * GRID_W), ls]
            acc = (_dot(jnp.concatenate(p_loc, axis=1).astype(bf16), vw)
                   + _dot(jnp.concatenate(p_ctx, axis=1).astype(bf16), vc_ref[0, :, ls]))
            acc = acc * (1.0 / den)
            o = jnp.zeros((GRID_W, MXU_DIM), f32)
            for h in range(NA_GH):
                o = o + jnp.where(hmask[h], acc[h * GRID_W:(h + 1) * GRID_W], 0.0)
            outs.append(o)
        o_ref[0, pl.ds(row0, GRID_W), :] = jnp.concatenate(outs, axis=1).astype(bf16)

    for j in range(NA_RI):
        produce(jnp.int32(j), 0, j)

    def step(t, carry):
        bank = t & 1
        for j in range(NA_RI):
            consume(t * NA_RI + j, bank, j)
        for j in range(NA_RI):
            produce(jnp.minimum((t + 1) * NA_RI + j, NA_R - 1), 1 - bank, j)
        return carry

    lax.fori_loop(0, NA_R // NA_RI, step, 0)


def _na_attn(qkv, kc, vc, t2):
    B, L, _ = qkv.shape
    nblk = L // (NA_R * GRID_W)
    return pl.pallas_call(
        _na_kernel,
        grid=(B, nblk),
        in_specs=[pl.BlockSpec((1, NA_R * GRID_W, NA_W), lambda b, i: (b, i, NA_Q // NA_W)),
                  pl.BlockSpec((1, L, NA_W), lambda b, i: (b, 0, NA_K // NA_W), pipeline_mode=pl.Buffered(1)),
                  pl.BlockSpec((1, L, NA_W), lambda b, i: (b, 0, NA_V // NA_W), pipeline_mode=pl.Buffered(1)),
                  pl.BlockSpec((1, CTX, NA_W), lambda b, i: (b, 0, 0)),
                  pl.BlockSpec((1, CTX, NA_W), lambda b, i: (b, 0, 0)),
                  _const_spec((NA_H, 2 * NA_WR - 1, GRID_W, LANES))],
        out_specs=pl.BlockSpec((1, NA_R * GRID_W, NA_W), lambda b, i: (b, i, 0)),
        out_shape=jax.ShapeDtypeStruct((B, L, NA_W), bf16),
        scratch_shapes=[pltpu.VMEM((2, NA_RI, NA_NG, NA_GH * GRID_W, NA_WR * GRID_W), f32),
                        pltpu.VMEM((2, NA_RI, NA_NG, NA_GH * GRID_W, CTX), f32),
                        pltpu.VMEM((2, NA_RI, NA_NG, NA_GH * GRID_W, LANES), f32)],
        compiler_params=pltpu.CompilerParams(
            dimension_semantics=("parallel", "parallel"), vmem_limit_bytes=VMEM_LIMIT),
        name="na_attn",
    )(qkv, qkv, qkv, kc, vc, t2)


def _gla_chunk(q, ki, v, ket, dcol, s_ref, tri, hmask):
    qs = jnp.concatenate([jnp.where(m, q, jnp.zeros_like(q)) for m in hmask], axis=0)
    att = jnp.where(tri, _dot_nt(qs, ki), 0.0).astype(bf16)
    o = _dot(q, s_ref[...].astype(bf16))
    intra = [_dot(att[h * CHUNK:(h + 1) * CHUNK], v[:, h * GLA_DV:(h + 1) * GLA_DV]) for h in range(GLA_H)]
    o = o + jnp.concatenate(intra, axis=1)
    for h in range(GLA_H):
        rs = slice(h * GLA_DK, (h + 1) * GLA_DK)
        cs = slice(h * GLA_DV, (h + 1) * GLA_DV)
        s_ref[rs, cs] = dcol[rs] * s_ref[rs, cs] + _dot(ket[rs], v[:, cs])
    return o


def _gla_kernel(gf_ref, ktf_ref, decf_ref, gb_ref, vb_ref, ktb_ref, decb_ref,
                s0f_ref, s0b_ref, of_ref, ob_ref, sf_ref, sb_ref):
    @pl.when(pl.program_id(1) == 0)
    def _():
        sf_ref[...] = s0f_ref[0]
        sb_ref[...] = s0b_ref[0]

    lane_head = lax.broadcasted_iota(jnp.int32, (CHUNK, GQK), 1) // GLA_DK
    hmask = [lane_head == h for h in range(GLA_H)]
    ri = lax.broadcasted_iota(jnp.int32, (GLA_H * CHUNK, CHUNK), 0) & (CHUNK - 1)
    ci = lax.broadcasted_iota(jnp.int32, (GLA_H * CHUNK, CHUNK), 1)
    tri_f = ri >= ci
    tri_b = ri <= ci
    dect_f = decf_ref[0][:, :GQK].T
    dect_b = decb_ref[0][:, GQK:].T

    def k_end(kt_ref, c):
        kt = kt_ref[0, :, (c // 2) * 2 * CHUNK:(c // 2 + 1) * 2 * CHUNK]
        if c % 2:
            kt = pltpu.roll(kt, CHUNK, 1)
        return kt[:, :CHUNK]

    for c in range(GLA_CB):
        ts = slice(c * CHUNK, (c + 1) * CHUNK)
        of_ref[0, ts, :] = _gla_chunk(gf_ref[0, ts, G_Q:G_Q + GQK], gf_ref[0, ts, G_KI:G_KI + GQK],
                                      gf_ref[0, ts, G_V:G_V + GV], k_end(ktf_ref, c),
                                      dect_f[:, c:c + 1], sf_ref, tri_f, hmask)
        cb = GLA_CB - 1 - c
        tb = slice(cb * CHUNK, (cb + 1) * CHUNK)
        ob_ref[0, tb, :] = _gla_chunk(gb_ref[0, tb, G_Q - GV:G_Q - GV + GQK], gb_ref[0, tb, G_KI - GV:G_KI - GV + GQK],
                                      vb_ref[0, tb, :], k_end(ktb_ref, cb),
                                      dect_b[:, cb:cb + 1], sb_ref, tri_b, hmask)


def _gla_scan(gf, gb, ktf, ktb, dec, s0f, s0b):
    B, L, _ = gf.shape
    tm = GLA_CB * CHUNK
    nb = L // tm
    fw = lambda n: pl.BlockSpec((1, tm, n), lambda b, i: (b, i, 0))
    bw = lambda n: pl.BlockSpec((1, tm, n), lambda b, i: (b, nb - 1 - i, 0))
    st = pl.BlockSpec((1, GQK, GV), lambda b, i: (b, 0, 0))
    return pl.pallas_call(
        _gla_kernel,
        grid=(B, nb),
        in_specs=[fw(W_GF),
                  pl.BlockSpec((1, GQK, tm), lambda b, i: (b, 0, i)),
                  pl.BlockSpec((1, GLA_CB, 2 * GQK), lambda b, i: (b, i, 0)),
                  bw(W_GB), bw(GV),
                  pl.BlockSpec((1, GQK, tm), lambda b, i: (b, 0, nb - 1 - i)),
                  pl.BlockSpec((1, GLA_CB, 2 * GQK), lambda b, i: (b, nb - 1 - i, 0)),
                  st, st],
        out_specs=[fw(GV), bw(GV)],
        out_shape=[jax.ShapeDtypeStruct((B, L, GV), f32)] * 2,
        scratch_shapes=[pltpu.VMEM((GQK, GV), f32), pltpu.VMEM((GQK, GV), f32)],
        compiler_params=pltpu.CompilerParams(
            dimension_semantics=("parallel", "arbitrary"), vmem_limit_bytes=VMEM_LIMIT),
        name="gla_scan",
    )(gf, ktf, dec, gb, gf, ktb, dec, s0f, s0b)


def _merge_ffn_kernel(x_ref, ona_ref, of_ref, ob_ref, mg_ref,
                      g1_ref, sh2_ref, sc2_ref, g2_ref, n2_ref, gg_ref,
                      wna_ref, wgl_ref, wo_ref, wg_ref, wu_ref, wd_ref, o_ref):
    x1s, h2s = [], []
    for hs in range(0, x_ref.shape[1], FFN_SUB):
        rows = slice(hs, hs + FFN_SUB)
        og = of_ref[0, rows] + ob_ref[0, rows]
        parts = []
        for h in range(GLA_H):
            seg = og[:, h * GLA_DV:(h + 1) * GLA_DV]
            ms = jnp.mean(seg * seg, axis=-1, keepdims=True)
            parts.append(seg * lax.rsqrt(ms + EPS) * gg_ref[...])
        ogl = (jnp.concatenate(parts, axis=1) * mg_ref[0, rows, MG_OG:MG_OG + GV].astype(f32)).astype(bf16)
        y = (mg_ref[0, rows, MG_A:MG_A + D].astype(f32) * _dot(ona_ref[0, rows], wna_ref[...])
             + mg_ref[0, rows, MG_B:MG_B + D].astype(f32) * _dot(ogl, wgl_ref[...]))
        x1 = x_ref[0, rows] + g1_ref[0] * _dot(y.astype(bf16), wo_ref[...])
        x1s.append(x1)
        h2s.append(_norm_mod(x1, n2_ref[...], sh2_ref[0], sc2_ref[0]).astype(bf16))
    x1 = jnp.concatenate(x1s, axis=0)
    h2 = jnp.concatenate(h2s, axis=0)
    act = (_silu(_dot(h2, wg_ref[...])) * _dot(h2, wu_ref[...])).astype(bf16)
    o_ref[0] = x1 + g2_ref[0] * _dot(act, wd_ref[...])


def _merge_ffn(x, ona, of, ob, mg, mod3, norm2_g, gg, wna, wgl, wo, wg, wu, wd):
    B, L, _ = x.shape
    tm = TM_FFN
    tok = lambda n: pl.BlockSpec((1, tm, n), lambda b, i: (b, i, 0))
    modv = lambda j: pl.BlockSpec((1, 1, D), lambda b, i: (b, 0, j))
    return pl.pallas_call(
        _merge_ffn_kernel,
        grid=(B, L // tm),
        in_specs=[tok(D), tok(NA_W), tok(GV), tok(GV), tok(W_MG),
                  modv(2), modv(3), modv(4), modv(5),
                  _const_spec((1, D)), _const_spec((1, GLA_DV)),
                  _const_spec((NA_W, D)), _const_spec((GV, D)), _const_spec((D, D)),
                  _const_spec((D, D_FF)), _const_spec((D, D_FF)), _const_spec((D_FF, D))],
        out_specs=tok(D),
        out_shape=jax.ShapeDtypeStruct((B, L, D), f32),
        compiler_params=pltpu.CompilerParams(
            dimension_semantics=("parallel", "parallel"), vmem_limit_bytes=VMEM_LIMIT),
        name="merge_ffn",
    )(x, ona, of, ob, mg, mod3, mod3, mod3, mod3, norm2_g, gg, wna, wgl, wo, wg, wu, wd)


def kernel(x, c, ctx, c_ctx, w_mod, b_mod, norm1_g, norm2_g, w_in, na_q_norm_g, na_k_norm_g, na_rpb,
           gla_w_alpha, gla_b_alpha, gla_norm_g, w_branch_na, w_branch_gla, w_out,
           w_ffn_gate, w_ffn_up, w_ffn_down):
    B, L, _ = x.shape
    assert w_mod.shape[0] == 1 and L % (NA_R * GRID_W) == 0 and L % TM_PROJ == 0 and D_IN % W_IN_CAST_ROWS == 0

    w_t = _cast_bf16(jnp.transpose(w_in[0]), W_IN_CAST_ROWS)
    zero = jnp.zeros((GLA_RANK, GQK), f32)
    w_a = jnp.concatenate([jnp.concatenate([gla_w_alpha[0, 0], zero], axis=1),
                           jnp.concatenate([zero, gla_w_alpha[0, 1]], axis=1)], axis=0).astype(bf16)
    b_a = gla_b_alpha[0].reshape(1, 2 * GQK)
    gq_t = jnp.tile(na_q_norm_g[0] * (NA_DH ** -0.5 * LOG2E), NA_H).reshape(1, NA_W)
    gk_t = jnp.tile(na_k_norm_g[0], NA_H).reshape(1, NA_W)
    bd = jnp.asarray(np.kron(np.eye(4), np.full((NA_DH, NA_DH), 1.0 / NA_DH)), bf16)

    mod3 = _adaln_mod(c, c_ctx, w_mod[0], b_mod[0])
    t2 = _rpb_expand(na_rpb[0])

    kc, vc, s0f, s0b = _ctx_side(ctx, mod3, norm1_g, w_t, w_a, b_a, gk_t, bd, B)
    later_ws = [w_branch_na[0], w_branch_gla[0], w_out[0], w_ffn_gate[0], w_ffn_up[0], w_ffn_down[0]]
    na_qkv, gla_f, gla_b, gates, dec, ktf, ktb, *later_bf = _in_proj(
        x, mod3, norm1_g, w_t, gq_t, gk_t, bd, w_a, b_a, later_ws)
    ona = _na_attn(na_qkv, kc, vc, t2)
    of, ob = _gla_scan(gla_f, gla_b, ktf, ktb, dec, s0f, s0b)
    return _merge_ffn(x, ona, of, ob, gates, mod3, norm2_g, gla_norm_g, *later_bf)
```

```python
import functools

import numpy as np
import jax
import jax.numpy as jnp
from jax import lax
from jax.experimental import pallas as pl
from jax.experimental.pallas import tpu as pltpu

D = 1024
GRID_W = 64
CTX = 256
NA_H = 8
NA_DH = 64
NA_WR = 8
NA_WC = 16
GLA_H = 4
GLA_DK = 64
GLA_DV = 128
GLA_RANK = 16
GLA_TAU = 16.0
GLA_LA_MIN = -1.0
CHUNK = 64
ROPE_THETA = 10000.0
D_FF = 2816
EPS = 1e-6
NEG_INF = -1e30
NA_W = NA_H * NA_DH
GQK = GLA_H * GLA_DK
GV = GLA_H * GLA_DV

OFF_QNA, OFF_KNA, OFF_VNA = 0, 512, 1024
OFF_QG, OFF_KG, OFF_VG, OFF_OG = 1536, 1792, 2048, 2560
OFF_LR, OFF_MA, OFF_MB = 3072, 3104, 4128
N_LR = 2 * GLA_RANK
N_PROJ_OUT = 7
NA_Q, NA_K, NA_V, W_NA = 0, 512, 1024, 1536
G_V, G_Q, G_KI, W_GF = 0, 512, 768, 1024
W_GB = W_GF - GV
MG_A, MG_B, MG_OG, W_MG = 0, 1024, 2048, 2560
D_IN = 5152
LOG2E = 1.4426950408889634

LANES = 128
SUBLANES = 8
BF16_SUBLANES = 16
MXU_DIM = 256
NA_GH = MXU_DIM // NA_DH
NA_NG = NA_H // NA_GH
VMEM_LIMIT = 56 * 1024 * 1024

W_IN_CAST_ROWS = 46 * BF16_SUBLANES
TM_PROJ = 512
PROJ_SUB = 256
TM_FFN = 512
FFN_SUB = 256
NA_R = 64
NA_RI = 4
GLA_CB = 32

f32 = jnp.float32
bf16 = jnp.bfloat16


def _const_spec(shape):
    nd = len(shape)
    return pl.BlockSpec(shape, lambda *_: (0,) * nd, pipeline_mode=pl.Buffered(1))


def _dot(a, b):
    return jnp.dot(a, b, preferred_element_type=f32)


def _dot_nt(a, b):
    return lax.dot_general(a, b, (((1,), (1,)), ((), ())), preferred_element_type=f32)


def _split2(x):
    hi = x.astype(bf16)
    lo = (x - hi.astype(f32)).astype(bf16)
    return hi, lo


def _sigmoid(x):
    return 0.5 * jnp.tanh(0.5 * x) + 0.5


def _silu(x):
    return x * _sigmoid(x)


def _split3_dot(r, oh):
    h1 = r.astype(bf16)
    r2 = r - h1.astype(f32)
    h2 = r2.astype(bf16)
    h3 = (r2 - h2.astype(f32)).astype(bf16)
    return _dot(h1, oh) + _dot(h2, oh) + _dot(h3, oh)


def _rpb_expand_kernel(r_ref, oh_ref, mask_ref, o_ref):
    tiles = _split3_dot(r_ref[...], oh_ref[...]) * LOG2E + mask_ref[...]
    for q in range(GRID_W):
        o_ref[:, q, :] = tiles[:o_ref.shape[0], q * LANES:(q + 1) * LANES]


def _rpb_expand(rpb):
    nd = 2 * NA_WR - 1
    nrow = NA_H * nd
    ncol = 2 * NA_WC - 1
    nxt = jnp.concatenate([rpb[:, 1:], jnp.zeros_like(rpb[:, :1])], axis=1)
    pad = ((0, LANES - nrow), (0, LANES - ncol))
    r01 = jnp.concatenate([jnp.pad(rpb.reshape(nrow, ncol), pad), jnp.pad(nxt.reshape(nrow, ncol), pad)], axis=1)
    q = np.arange(GRID_W)
    cs = np.clip(q - NA_WC // 2, 0, GRID_W - NA_WC)
    in_win = (q[None, :] >= cs[:, None]) & (q[None, :] < cs[:, None] + NA_WC)
    dc = np.clip(q[None, :] - q[:, None], -(NA_WC - 1), NA_WC - 1) + NA_WC - 1
    oh = np.zeros((2, LANES, GRID_W, LANES), np.float32)
    qq, kk = np.nonzero(in_win)
    oh[0, dc[qq, kk], qq, kk] = 1.0
    oh[1, dc[qq, kk], qq, kk + GRID_W] = 1.0
    oh = oh.reshape(2 * LANES, GRID_W * LANES)
    mask = np.where(np.concatenate([in_win, in_win], axis=1), 0.0, NEG_INF).reshape(1, GRID_W * LANES)
    out = pl.pallas_call(
        _rpb_expand_kernel,
        out_shape=jax.ShapeDtypeStruct((nrow, GRID_W, LANES), f32),
        name="rpb_expand",
    )(r01, jnp.asarray(oh, bf16), jnp.asarray(mask, f32))
    return out.reshape(NA_H, nd, GRID_W, LANES)


def _cast_kernel(x_ref, o_ref):
    o_ref[...] = x_ref[...].astype(bf16)


def _cast_bf16(w, rows):
    R, C = w.shape
    return pl.pallas_call(
        _cast_kernel,
        grid=(R // rows,),
        in_specs=[pl.BlockSpec((rows, C), lambda j: (j, 0))],
        out_specs=pl.BlockSpec((rows, C), lambda j: (j, 0)),
        out_shape=jax.ShapeDtypeStruct((R, C), bf16),
        name="cast_bf16",
    )(w)


def _mod_kernel(a_ref, w_ref, b_ref, o_ref):
    a = _silu(a_ref[...]).astype(bf16)
    o_ref[...] = (_dot(a, w_ref[...].astype(bf16)) + b_ref[...])[:, None, :]


def _adaln_mod(c, c_ctx, w_mod, b_mod):
    B = c.shape[0]
    assert B < SUBLANES
    a = jnp.concatenate([c, c_ctx[None], jnp.zeros((SUBLANES - B - 1, D), f32)], axis=0)
    n = w_mod.shape[1]
    return pl.pallas_call(
        _mod_kernel,
        grid=(n // D,),
        in_specs=[pl.BlockSpec((SUBLANES, D), lambda j: (0, 0)),
                  pl.BlockSpec((D, D), lambda j: (0, j)),
                  pl.BlockSpec((1, D), lambda j: (0, j))],
        out_specs=pl.BlockSpec((SUBLANES, 1, D), lambda j: (0, 0, j)),
        out_shape=jax.ShapeDtypeStruct((SUBLANES, 1, n), f32),
        name="adaln_mod",
    )(a, w_mod, b_mod.reshape(1, n))


def _norm_mod(x, g, sh, sc):
    ms = jnp.mean(x * x, axis=-1, keepdims=True)
    return (x * lax.rsqrt(ms + EPS)) * (g * (1.0 + sc)) + sh


def _head_rms(t, bd, g):
    sq = (t * t).astype(bf16)
    ms = jnp.concatenate([_dot(sq[:, :MXU_DIM], bd), _dot(sq[:, MXU_DIM:], bd)], axis=1)
    return t * lax.rsqrt(ms + EPS) * g


def _log_alpha(zl, wa_ref, ba_ref):
    z = _dot(zl.astype(bf16), wa_ref[...]) + ba_ref[...]
    soft = jnp.log2(1.0 + jnp.exp2(z * -LOG2E))
    return jnp.maximum(soft * (-1.0 / GLA_TAU), GLA_LA_MIN * LOG2E)


def _ctx_kernel(ctx_ref, sh_ref, sc_ref, g1_ref, wk_ref, wv_ref, wkg_ref, wvg_ref, wlr_ref,
                wa_ref, ba_ref, gk_ref, bd_ref, su_ref, sl_ref,
                kc_ref, vc_ref, sf_ref, sb_ref):
    h = _norm_mod(ctx_ref[0], g1_ref[...], sh_ref[0], sc_ref[0]).astype(bf16)
    k = _dot_nt(h, wk_ref[...])
    kc_ref[0] = _head_rms(k, bd_ref[...], gk_ref[...]).astype(bf16)
    vc_ref[0] = _dot_nt(h, wv_ref[...]).astype(bf16)
    kg = _dot_nt(h, wkg_ref[...])
    vg = _dot_nt(h, wvg_ref[...]).astype(bf16)
    la = _log_alpha(_dot_nt(h, wlr_ref[...]), wa_ref, ba_ref)
    hi, lo = _split2(la)
    ef = _dot(su_ref[...], hi[:, :GQK]) + _dot(su_ref[...], lo[:, :GQK])
    eb = _dot(sl_ref[...], hi[:, GQK:]) + _dot(sl_ref[...], lo[:, GQK:])
    r = lax.broadcasted_iota(jnp.int32, (GQK, GV), 0) // GLA_DK
    cc = lax.broadcasted_iota(jnp.int32, (GQK, GV), 1) // GLA_DV
    diag = r == cc
    for e, out in ((ef, sf_ref), (eb, sb_ref)):
        kw = (kg * jnp.exp2(e)).astype(bf16)
        u = _dot(kw.T, vg)
        out[0] = jnp.where(diag, u, 0.0)


def _ctx_side(ctx, mod3, norm1_g, w_t, w_a, b_a, gk_t, bd, n_b):
    su = jnp.asarray(np.triu(np.ones((CTX, CTX), np.float32), 1), bf16)
    sl = jnp.asarray(np.tril(np.ones((CTX, CTX), np.float32), -1), bf16)
    B = ctx.shape[0]
    cst = lambda shape, idx: pl.BlockSpec(shape, lambda b: idx)
    return pl.pallas_call(
        _ctx_kernel,
        grid=(B,),
        in_specs=[pl.BlockSpec((1, CTX, D), lambda b: (b, 0, 0)),
                  cst((1, 1, D), (n_b, 0, 0)), cst((1, 1, D), (n_b, 0, 1)),
                  cst((1, D), (0, 0)),
                  cst((NA_W, D), (OFF_KNA // NA_W, 0)), cst((NA_W, D), (OFF_VNA // NA_W, 0)),
                  cst((GQK, D), (OFF_KG // GQK, 0)), cst((GV, D), (OFF_VG // GV, 0)),
                  cst((N_LR, D), (OFF_LR // N_LR, 0)),
                  cst((N_LR, 2 * GQK), (0, 0)), cst((1, 2 * GQK), (0, 0)),
                  cst((1, NA_W), (0, 0)), cst((MXU_DIM, MXU_DIM), (0, 0)),
                  cst((CTX, CTX), (0, 0)), cst((CTX, CTX), (0, 0))],
        out_specs=[pl.BlockSpec((1, CTX, NA_W), lambda b: (b, 0, 0)),
                   pl.BlockSpec((1, CTX, NA_W), lambda b: (b, 0, 0)),
                   pl.BlockSpec((1, GQK, GV), lambda b: (b, 0, 0)),
                   pl.BlockSpec((1, GQK, GV), lambda b: (b, 0, 0))],
        out_shape=[jax.ShapeDtypeStruct((B, CTX, NA_W), bf16),
                   jax.ShapeDtypeStruct((B, CTX, NA_W), bf16),
                   jax.ShapeDtypeStruct((B, GQK, GV), f32),
                   jax.ShapeDtypeStruct((B, GQK, GV), f32)],
        compiler_params=pltpu.CompilerParams(vmem_limit_bytes=VMEM_LIMIT),
        name="ctx_side",
    )(ctx, mod3, mod3, norm1_g, w_t, w_t, w_t, w_t, w_t, w_a, b_a, gk_t, bd, su, sl)


def _proj_kernel(x_ref, sh_ref, sc_ref, g1_ref, w_ref, gq_ref, gk_ref, bd_ref,
                 rc_ref, rs_ref, cc_ref, cs_ref, wa_ref, ba_ref, tl_ref, *refs):
    n_cast = (len(refs) - N_PROJ_OUT) // 2
    cast_in, refs = refs[:n_cast], refs[n_cast:]
    na_ref, gf_ref, gb_ref, mg_ref, dec_ref, ktf_ref, ktb_ref = refs[:N_PROJ_OUT]
    cast_out = refs[N_PROJ_OUT:]
    for src, dst in zip(cast_in, cast_out):
        dst[...] = src[...].astype(bf16)

    bd = bd_ref[...]
    tl = tl_ref[...]
    nrow = PROJ_SUB // GRID_W
    nchunk = PROJ_SUB // CHUNK
    lane = lax.broadcasted_iota(jnp.int32, (PROJ_SUB, GQK), 1)
    first_half = (lane & 31) < 16

    for hs in range(0, x_ref.shape[1], PROJ_SUB):
        rows = slice(hs, hs + PROJ_SUB)
        hb = _norm_mod(x_ref[0, rows], g1_ref[...], sh_ref[0], sc_ref[0]).astype(bf16)

        def proj(off, n):
            return _dot_nt(hb, w_ref[off:off + n, :])

        zl = proj(OFF_LR, N_LR)
        q_raw = proj(OFF_QNA, NA_W)
        la = _log_alpha(zl, wa_ref, ba_ref)
        k_raw = proj(OFF_KNA, NA_W)
        hi, lo = _split2(la)
        na_ref[0, rows, NA_Q:NA_Q + NA_W] = _head_rms(q_raw, bd, gq_ref[...]).astype(bf16)
        cum = jnp.concatenate(
            [_dot(tl, hi[s:s + MXU_DIM]) + _dot(tl, lo[s:s + MXU_DIM]) for s in range(0, PROJ_SUB, MXU_DIM)], axis=0)
        na_ref[0, rows, NA_K:NA_K + NA_W] = _head_rms(k_raw, bd, gk_ref[...]).astype(bf16)
        qg_raw = proj(OFF_QG, GQK)
        kg_raw = proj(OFF_KG, GQK)
        na_ref[0, rows, NA_V:NA_V + NA_W] = proj(OFF_VNA, NA_W).astype(bf16)
        gf_ref[0, rows, G_V:G_V + GV] = proj(OFF_VG, GV).astype(bf16)

        gr = slice(hs // GRID_W, hs // GRID_W + nrow)
        cos_t = (rc_ref[gr][:, None, :] + cc_ref[...][None]).reshape(PROJ_SUB, GQK)
        sin_t = (rs_ref[gr][:, None, :] + cs_ref[...][None]).reshape(PROJ_SUB, GQK)

        def rope(t):
            partner = jnp.where(first_half, pltpu.roll(t, GQK - 16, 1), pltpu.roll(t, 16, 1))
            return t * cos_t + partner * sin_t

        qg = rope(qg_raw) * (GLA_DK ** -0.5)
        kg = rope(kg_raw)

        cum3 = cum.reshape(nchunk, CHUNK, 2 * GQK)
        tot3 = cum3[:, CHUNK - 1:CHUNK, :]
        la3 = la.reshape(nchunk, CHUNK, 2 * GQK)
        dec_ref[0, hs // CHUNK:hs // CHUNK + nchunk] = jnp.exp2(tot3).reshape(nchunk, 2 * GQK)
        b_f = cum3[:, :, :GQK].reshape(PROJ_SUB, GQK)
        e_f = (tot3 - cum3)[:, :, :GQK].reshape(PROJ_SUB, GQK)
        b_b = (tot3 - cum3 + la3)[:, :, GQK:].reshape(PROJ_SUB, GQK)
        e_b = (cum3 - la3)[:, :, GQK:].reshape(PROJ_SUB, GQK)
        mg_ref[0, rows, MG_OG:MG_OG + GV] = _silu(proj(OFF_OG, GV)).astype(bf16)
        gf_ref[0, rows, G_Q:G_Q + GQK] = (qg * jnp.exp2(b_f)).astype(bf16)
        gf_ref[0, rows, G_KI:G_KI + GQK] = (kg * jnp.exp2(-b_f)).astype(bf16)
        ktf_ref[0, :, rows] = (kg * jnp.exp2(e_f)).T.astype(bf16)
        mg_ref[0, rows, MG_A:MG_A + D] = _sigmoid(proj(OFF_MA, D)).astype(bf16)
        gb_ref[0, rows, G_Q - GV:G_Q - GV + GQK] = (qg * jnp.exp2(b_b)).astype(bf16)
        gb_ref[0, rows, G_KI - GV:G_KI - GV + GQK] = (kg * jnp.exp2(-b_b)).astype(bf16)
        ktb_ref[0, :, rows] = (kg * jnp.exp2(e_b)).T.astype(bf16)
        mg_ref[0, rows, MG_B:MG_B + D] = _sigmoid(proj(OFF_MB, D)).astype(bf16)


def _rope_tables(L):
    n = GLA_DK // 4
    freqs = ROPE_THETA ** (-np.arange(n, dtype=np.float64) / n)
    lane = np.arange(GQK)
    fl = freqs[lane % n]
    is_row = (lane % GLA_DK) < GLA_DK // 2
    sign = np.where((lane % 32) < 16, -1.0, 1.0)
    rows = np.arange(L // GRID_W)[:, None] * fl[None]
    cols = np.arange(GRID_W)[:, None] * fl[None]
    rc = np.where(is_row, np.cos(rows), 0.0)
    rs = np.where(is_row, np.sin(rows) * sign, 0.0)
    cc = np.where(~is_row, np.cos(cols), 0.0)
    cs = np.where(~is_row, np.sin(cols) * sign, 0.0)
    return [jnp.asarray(t, f32) for t in (rc, rs, cc, cs)]


def _in_proj(x, mod3, norm1_g, w_t, gq_t, gk_t, bd, w_a, b_a, cast_ws):
    B, L, _ = x.shape
    tm = TM_PROJ
    rows_per = tm // GRID_W
    nstep = L // tm
    total = B * nstep

    def slab(w):
        R, C = w.shape
        if R % (total * BF16_SUBLANES) == 0:
            return pl.BlockSpec((R // total, C), lambda b, i: (b * nstep + i, 0))
        assert (2 * R) % (total * BF16_SUBLANES) == 0 and C % (2 * LANES) == 0
        return pl.BlockSpec((2 * R // total, C // 2), lambda b, i: ((b * nstep + i) // 2, (b * nstep + i) % 2))
    rc, rs, cc, cs = _rope_tables(L)
    blk = np.kron(np.eye(MXU_DIM // CHUNK), np.tril(np.ones((CHUNK, CHUNK)))).astype(np.float32)
    tl = jnp.asarray(blk, bf16)
    tok = lambda n: pl.BlockSpec((1, tm, n), lambda b, i: (b, i, 0))
    sds = lambda n, dt: jax.ShapeDtypeStruct((B, L, n), dt)
    return pl.pallas_call(
        _proj_kernel,
        grid=(B, L // tm),
        in_specs=[tok(D),
                  pl.BlockSpec((1, 1, D), lambda b, i: (b, 0, 0)),
                  pl.BlockSpec((1, 1, D), lambda b, i: (b, 0, 1)),
                  _const_spec((1, D)), _const_spec((D_IN, D)),
                  _const_spec((1, NA_W)), _const_spec((1, NA_W)), _const_spec((MXU_DIM, MXU_DIM)),
                  pl.BlockSpec((rows_per, GQK), lambda b, i: (i, 0)),
                  pl.BlockSpec((rows_per, GQK), lambda b, i: (i, 0)),
                  _const_spec((GRID_W, GQK)), _const_spec((GRID_W, GQK)),
                  _const_spec((N_LR, 2 * GQK)), _const_spec((1, 2 * GQK)), _const_spec((MXU_DIM, MXU_DIM))]
                 + [slab(w) for w in cast_ws],
        out_specs=[tok(W_NA), tok(W_GF), tok(W_GB), tok(W_MG),
                   pl.BlockSpec((1, tm // CHUNK, 2 * GQK), lambda b, i: (b, i, 0)),
                   pl.BlockSpec((1, GQK, tm), lambda b, i: (b, 0, i)),
                   pl.BlockSpec((1, GQK, tm), lambda b, i: (b, 0, i))]
                  + [slab(w) for w in cast_ws],
        out_shape=[sds(W_NA, bf16), sds(W_GF, bf16), sds(W_GB, bf16), sds(W_MG, bf16),
                   jax.ShapeDtypeStruct((B, L // CHUNK, 2 * GQK), f32),
                   jax.ShapeDtypeStruct((B, GQK, L), bf16), jax.ShapeDtypeStruct((B, GQK, L), bf16)]
                  + [jax.ShapeDtypeStruct(w.shape, bf16) for w in cast_ws],
        compiler_params=pltpu.CompilerParams(
            dimension_semantics=("parallel", "parallel"), vmem_limit_bytes=VMEM_LIMIT),
        name="in_proj",
    )(x, mod3, mod3, norm1_g, w_t, gq_t, gk_t, bd, rc, rs, cc, cs, w_a, b_a, tl, *cast_ws)


def _lane_blocks(t):
    return [t[:, c:c + LANES] for c in range(0, t.shape[1], LANES)]


def _na_kernel(q_ref, k_ref, v_ref, kc_ref, vc_ref, t2_ref, o_ref, sl_ref, sc_ref, m_ref):
    i = pl.program_id(1)
    rows = k_ref.shape[1] // GRID_W
    lane_head = lax.broadcasted_iota(jnp.int32, (GRID_W, MXU_DIM), 1) // NA_DH
    hmask = [lane_head == h for h in range(NA_GH)]

    def window_start(r):
        rs = jnp.clip(r - NA_WR // 2, 0, rows - NA_WR)
        return rs, pl.multiple_of(rs * GRID_W, GRID_W)

    def produce(rr, bank, j):
        r = i * NA_R + rr
        rs, start = window_start(r)
        d0 = rs - r + (NA_WR - 1)
        row0 = pl.multiple_of(rr * GRID_W, GRID_W)
        for g in range(NA_NG):
            ls = slice(MXU_DIM * g, MXU_DIM * (g + 1))
            q = q_ref[0, pl.ds(row0, GRID_W), ls]
            qs = jnp.concatenate([jnp.where(m, q, jnp.zeros_like(q)) for m in hmask], axis=0)
            kw = k_ref[0, pl.ds(start, NA_WR * GRID_W), ls]
            bias = jnp.concatenate(
                [jnp.concatenate([t2_ref[NA_GH * g + h, d0 + 2 * jj] for jj in range(NA_WR // 2)], axis=1)
                 for h in range(NA_GH)], axis=0)
            s_loc = _dot_nt(qs, kw) + bias
            s_ctx = _dot_nt(qs, kc_ref[0, :, ls])
            mx = functools.reduce(jnp.maximum, _lane_blocks(s_loc) + _lane_blocks(s_ctx))
            sl_ref[bank, j, g] = s_loc
            sc_ref[bank, j, g] = s_ctx
            m_ref[bank, j, g] = jnp.broadcast_to(jnp.max(mx, axis=-1, keepdims=True), (NA_GH * GRID_W, LANES))

    def consume(rr, bank, j):
        _, start = window_start(i * NA_R + rr)
        row0 = pl.multiple_of(rr * GRID_W, GRID_W)
        outs = []
        for g in range(NA_NG):
            ls = slice(MXU_DIM * g, MXU_DIM * (g + 1))
            m = m_ref[bank, j, g]
            p_loc = [jnp.exp2(sl_ref[bank, j, g, :, c:c + LANES] - m) for c in range(0, NA_WR * GRID_W, LANES)]
            p_ctx = [jnp.exp2(sc_ref[bank, j, g, :, c:c + LANES] - m) for c in range(0, CTX, LANES)]
            den = jnp.sum(functools.reduce(jnp.add, p_loc + p_ctx), axis=-1, keepdims=True)
            vw = v_ref[0, pl.ds(start, NA_WR * GRID_W), ls]
            acc = (_dot(jnp.concatenate(p_loc, axis=1).astype(bf16), vw)
                   + _dot(jnp.concatenate(p_ctx, axis=1).astype(bf16), vc_ref[0, :, ls]))
            acc = acc * (1.0 / den)
            o = jnp.zeros((GRID_W, MXU_DIM), f32)
            for h in range(NA_GH):
                o = o + jnp.where(hmask[h], acc[h * GRID_W:(h + 1) * GRID_W], 0.0)
            outs.append(o)
        o_ref[0, pl.ds(row0, GRID_W), :] = jnp.concatenate(outs, axis=1).astype(bf16)

    for j in range(NA_RI):
        produce(jnp.int32(j), 0, j)

    def step(t, carry):
        bank = t & 1
        for j in range(NA_RI):
            consume(t * NA_RI + j, bank, j)
        for j in range(NA_RI):
            produce(jnp.minimum((t + 1) * NA_RI + j, NA_R - 1), 1 - bank, j)
        return carry

    lax.fori_loop(0, NA_R // NA_RI, step, 0)


def _na_attn(qkv, kc, vc, t2):
    B, L, _ = qkv.shape
    nblk = L // (NA_R * GRID_W)
    return pl.pallas_call(
        _na_kernel,
        grid=(B, nblk),
        in_specs=[pl.BlockSpec((1, NA_R * GRID_W, NA_W), lambda b, i: (b, i, NA_Q // NA_W)),
                  pl.BlockSpec((1, L, NA_W), lambda b, i: (b, 0, NA_K // NA_W), pipeline_mode=pl.Buffered(1)),
                  pl.BlockSpec((1, L, NA_W), lambda b, i: (b, 0, NA_V // NA_W), pipeline_mode=pl.Buffered(1)),
                  pl.BlockSpec((1, CTX, NA_W), lambda b, i: (b, 0, 0)),
                  pl.BlockSpec((1, CTX, NA_W), lambda b, i: (b, 0, 0)),
                  _const_spec((NA_H, 2 * NA_WR - 1, GRID_W, LANES))],
        out_specs=pl.BlockSpec((1, NA_R * GRID_W, NA_W), lambda b, i: (b, i, 0)),
        out_shape=jax.ShapeDtypeStruct((B, L, NA_W), bf16),
        scratch_shapes=[pltpu.VMEM((2, NA_RI, NA_NG, NA_GH * GRID_W, NA_WR * GRID_W), f32),
                        pltpu.VMEM((2, NA_RI, NA_NG, NA_GH * GRID_W, CTX), f32),
                        pltpu.VMEM((2, NA_RI, NA_NG, NA_GH * GRID_W, LANES), f32)],
        compiler_params=pltpu.CompilerParams(
            dimension_semantics=("parallel", "parallel"), vmem_limit_bytes=VMEM_LIMIT),
        name="na_attn",
    )(qkv, qkv, qkv, kc, vc, t2)


def _gla_chunk(q, ki, v, ket, dcol, s_ref, tri, hmask):
    qs = jnp.concatenate([jnp.where(m, q, jnp.zeros_like(q)) for m in hmask], axis=0)
    att = jnp.where(tri, _dot_nt(qs, ki), 0.0).astype(bf16)
    o = _dot(q, s_ref[...].astype(bf16))
    intra = [_dot(att[h * CHUNK:(h + 1) * CHUNK], v[:, h * GLA_DV:(h + 1) * GLA_DV]) for h in range(GLA_H)]
    o = o + jnp.concatenate(intra, axis=1)
    for h in range(GLA_H):
        rs = slice(h * GLA_DK, (h + 1) * GLA_DK)
        cs = slice(h * GLA_DV, (h + 1) * GLA_DV)
        s_ref[rs, cs] = dcol[rs] * s_ref[rs, cs] + _dot(ket[rs], v[:, cs])
    return o


def _gla_kernel(gf_ref, ktf_ref, decf_ref, gb_ref, vb_ref, ktb_ref, decb_ref,
                s0f_ref, s0b_ref, of_ref, ob_ref, sf_ref, sb_ref):
    @pl.when(pl.program_id(1) == 0)
    def _():
        sf_ref[...] = s0f_ref[0]
        sb_ref[...] = s0b_ref[0]

    lane_head = lax.broadcasted_iota(jnp.int32, (CHUNK, GQK), 1) // GLA_DK
    hmask = [lane_head == h for h in range(GLA_H)]
    ri = lax.broadcasted_iota(jnp.int32, (GLA_H * CHUNK, CHUNK), 0) & (CHUNK - 1)
    ci = lax.broadcasted_iota(jnp.int32, (GLA_H * CHUNK, CHUNK), 1)
    tri_f = ri >= ci
    tri_b = ri <= ci
    dect_f = decf_ref[0][:, :GQK].T
    dect_b = decb_ref[0][:, GQK:].T

    def k_end(kt_ref, c):
        kt = kt_ref[0, :, (c // 2) * 2 * CHUNK:(c // 2 + 1) * 2 * CHUNK]
        if c % 2:
            kt = pltpu.roll(kt, CHUNK, 1)
        return kt[:, :CHUNK]

    for c in range(GLA_CB):
        ts = slice(c * CHUNK, (c + 1) * CHUNK)
        of_ref[0, ts, :] = _gla_chunk(gf_ref[0, ts, G_Q:G_Q + GQK], gf_ref[0, ts, G_KI:G_KI + GQK],
                                      gf_ref[0, ts, G_V:G_V + GV], k_end(ktf_ref, c),
                                      dect_f[:, c:c + 1], sf_ref, tri_f, hmask)
        cb = GLA_CB - 1 - c
        tb = slice(cb * CHUNK, (cb + 1) * CHUNK)
        ob_ref[0, tb, :] = _gla_chunk(gb_ref[0, tb, G_Q - GV:G_Q - GV + GQK], gb_ref[0, tb, G_KI - GV:G_KI - GV + GQK],
                                      vb_ref[0, tb, :], k_end(ktb_ref, cb),
                                      dect_b[:, cb:cb + 1], sb_ref, tri_b, hmask)


def _gla_scan(gf, gb, ktf, ktb, dec, s0f, s0b):
    B, L, _ = gf.shape
    tm = GLA_CB * CHUNK
    nb = L // tm
    fw = lambda n: pl.BlockSpec((1, tm, n), lambda b, i: (b, i, 0))
    bw = lambda n: pl.BlockSpec((1, tm, n), lambda b, i: (b, nb - 1 - i, 0))
    st = pl.BlockSpec((1, GQK, GV), lambda b, i: (b, 0, 0))
    return pl.pallas_call(
        _gla_kernel,
        grid=(B, nb),
        in_specs=[fw(W_GF),
                  pl.BlockSpec((1, GQK, tm), lambda b, i: (b, 0, i)),
                  pl.BlockSpec((1, GLA_CB, 2 * GQK), lambda b, i: (b, i, 0)),
                  bw(W_GB), bw(GV),
                  pl.BlockSpec((1, GQK, tm), lambda b, i: (b, 0, nb - 1 - i)),
                  pl.BlockSpec((1, GLA_CB, 2 * GQK), lambda b, i: (b, nb - 1 - i, 0)),
                  st, st],
        out_specs=[fw(GV), bw(GV)],
        out_shape=[jax.ShapeDtypeStruct((B, L, GV), f32)] * 2,
        scratch_shapes=[pltpu.VMEM((GQK, GV), f32), pltpu.VMEM((GQK, GV), f32)],
        compiler_params=pltpu.CompilerParams(
            dimension_semantics=("parallel", "arbitrary"), vmem_limit_bytes=VMEM_LIMIT),
        name="gla_scan",
    )(gf, ktf, dec, gb, gf, ktb, dec, s0f, s0b)


def _merge_ffn_kernel(x_ref, ona_ref, of_ref, ob_ref, mg_ref,
                      g1_ref, sh2_ref, sc2_ref, g2_ref, n2_ref, gg_ref,
                      wna_ref, wgl_ref, wo_ref, wg_ref, wu_ref, wd_ref, o_ref):
    x1s, h2s = [], []
    for hs in range(0, x_ref.shape[1], FFN_SUB):
        rows = slice(hs, hs + FFN_SUB)
        og = of_ref[0, rows] + ob_ref[0, rows]
        parts = []
        for h in range(GLA_H):
            seg = og[:, h * GLA_DV:(h + 1) * GLA_DV]
            ms = jnp.mean(seg * seg, axis=-1, keepdims=True)
            parts.append(seg * lax.rsqrt(ms + EPS) * gg_ref[...])
        ogl = (jnp.concatenate(parts, axis=1) * mg_ref[0, rows, MG_OG:MG_OG + GV].astype(f32)).astype(bf16)
        y = (mg_ref[0, rows, MG_A:MG_A + D].astype(f32) * _dot(ona_ref[0, rows], wna_ref[...])
             + mg_ref[0, rows, MG_B:MG_B + D].astype(f32) * _dot(ogl, wgl_ref[...]))
        x1 = x_ref[0, rows] + g1_ref[0] * _dot(y.astype(bf16), wo_ref[...])
        x1s.append(x1)
        h2s.append(_norm_mod(x1, n2_ref[...], sh2_ref[0], sc2_ref[0]).astype(bf16))
    x1 = jnp.concatenate(x1s, axis=0)
    h2 = jnp.concatenate(h2s, axis=0)
    act = (_silu(_dot(h2, wg_ref[...])) * _dot(h2, wu_ref[...])).astype(bf16)
    o_ref[0] = x1 + g2_ref[0] * _dot(act, wd_ref[...])


def _merge_ffn(x, ona, of, ob, mg, mod3, norm2_g, gg, wna, wgl, wo, wg, wu, wd):
    B, L, _ = x.shape
    tm = TM_FFN
    tok = lambda n: pl.BlockSpec((1, tm, n), lambda b, i: (b, i, 0))
    modv = lambda j: pl.BlockSpec((1, 1, D), lambda b, i: (b, 0, j))
    return pl.pallas_call(
        _merge_ffn_kernel,
        grid=(B, L // tm),
        in_specs=[tok(D), tok(NA_W), tok(GV), tok(GV), tok(W_MG),
                  modv(2), modv(3), modv(4), modv(5),
                  _const_spec((1, D)), _const_spec((1, GLA_DV)),
                  _const_spec((NA_W, D)), _const_spec((GV, D)), _const_spec((D, D)),
                  _const_spec((D, D_FF)), _const_spec((D, D_FF)), _const_spec((D_FF, D))],
        out_specs=tok(D),
        out_shape=jax.ShapeDtypeStruct((B, L, D), f32),
        compiler_params=pltpu.CompilerParams(
            dimension_semantics=("parallel", "parallel"), vmem_limit_bytes=VMEM_LIMIT),
        name="merge_ffn",
    )(x, ona, of, ob, mg, mod3, mod3, mod3, mod3, norm2_g, gg, wna, wgl, wo, wg, wu, wd)


def kernel(x, c, ctx, c_ctx, w_mod, b_mod, norm1_g, norm2_g, w_in, na_q_norm_g, na_k_norm_g, na_rpb,
           gla_w_alpha, gla_b_alpha, gla_norm_g, w_branch_na, w_branch_gla, w_out,
           w_ffn_gate, w_ffn_up, w_ffn_down):
    B, L, _ = x.shape
    assert w_mod.shape[0] == 1 and L % (NA_R * GRID_W) == 0 and L % TM_PROJ == 0 and D_IN % W_IN_CAST_ROWS == 0

    w_t = _cast_bf16(jnp.transpose(w_in[0]), W_IN_CAST_ROWS)
    zero = jnp.zeros((GLA_RANK, GQK), f32)
    w_a = jnp.concatenate([jnp.concatenate([gla_w_alpha[0, 0], zero], axis=1),
                           jnp.concatenate([zero, gla_w_alpha[0, 1]], axis=1)], axis=0).astype(bf16)
    b_a = gla_b_alpha[0].reshape(1, 2 * GQK)
    gq_t = jnp.tile(na_q_norm_g[0] * (NA_DH ** -0.5 * LOG2E), NA_H).reshape(1, NA_W)
    gk_t = jnp.tile(na_k_norm_g[0], NA_H).reshape(1, NA_W)
    bd = jnp.asarray(np.kron(np.eye(4), np.full((NA_DH, NA_DH), 1.0 / NA_DH)), bf16)

    mod3 = _adaln_mod(c, c_ctx, w_mod[0], b_mod[0])
    t2 = _rpb_expand(na_rpb[0])

    kc, vc, s0f, s0b = _ctx_side(ctx, mod3, norm1_g, w_t, w_a, b_a, gk_t, bd, B)
    later_ws = [w_branch_na[0], w_branch_gla[0], w_out[0], w_ffn_gate[0], w_ffn_up[0], w_ffn_down[0]]
    na_qkv, gla_f, gla_b, gates, dec, ktf, ktb, *later_bf = _in_proj(
        x, mod3, norm1_g, w_t, gq_t, gk_t, bd, w_a, b_a, later_ws)
    ona = _na_attn(na_qkv, kc, vc, t2)
    of, ob = _gla_scan(gla_f, gla_b, ktf, ktb, dec, s0f, s0b)
    return _merge_ffn(x, ona, of, ob, gates, mod3, norm2_g, gla_norm_g, *later_bf)
```

```python
import functools

import numpy as np
import jax
import jax.numpy as jnp
from jax import lax
from jax.experimental import pallas as pl
from jax.experimental.pallas import tpu as pltpu

D = 1024
GRID_W = 64
CTX = 256
NA_H = 8
NA_DH = 64
NA_WR = 8
NA_WC = 16
GLA_H = 4
GLA_DK = 64
GLA_DV = 128
GLA_RANK = 16
GLA_TAU = 16.0
GLA_LA_MIN = -1.0
CHUNK = 64
ROPE_THETA = 10000.0
D_FF = 2816
EPS = 1e-6
NEG_INF = -1e30
NA_W = NA_H * NA_DH
GQK = GLA_H * GLA_DK
GV = GLA_H * GLA_DV

OFF_QNA, OFF_KNA, OFF_VNA = 0, 512, 1024
OFF_QG, OFF_KG, OFF_VG, OFF_OG = 1536, 1792, 2048, 2560
OFF_LR, OFF_MA, OFF_MB = 3072, 3104, 4128
N_LR = 2 * GLA_RANK
N_PROJ_OUT = 7
NA_Q, NA_K, NA_V, W_NA = 0, 512, 1024, 1536
G_V, G_Q, G_KI, W_GF = 0, 512, 768, 1024
W_GB = W_GF - GV
MG_A, MG_B, MG_OG, W_MG = 0, 1024, 2048, 2560
D_IN = 5152
LOG2E = 1.4426950408889634

LANES = 128
SUBLANES = 8
BF16_SUBLANES = 16
MXU_DIM = 256
NA_GH = MXU_DIM // NA_DH
NA_NG = NA_H // NA_GH
VMEM_LIMIT = 56 * 1024 * 1024

W_IN_CAST_ROWS = 46 * BF16_SUBLANES
TM_PROJ = 512
PROJ_SUB = 256
TM_FFN = 512
FFN_SUB = 256
NA_R = 64
NA_RI = 4
GLA_CB = 16

f32 = jnp.float32
bf16 = jnp.bfloat16


def _const_spec(shape):
    nd = len(shape)
    return pl.BlockSpec(shape, lambda *_: (0,) * nd, pipeline_mode=pl.Buffered(1))


def _dot(a, b):
    return jnp.dot(a, b, preferred_element_type=f32)


def _dot_nt(a, b):
    return lax.dot_general(a, b, (((1,), (1,)), ((), ())), preferred_element_type=f32)


def _split2(x):
    hi = x.astype(bf16)
    lo = (x - hi.astype(f32)).astype(bf16)
    return hi, lo


def _sigmoid(x):
    return 0.5 * jnp.tanh(0.5 * x) + 0.5


def _silu(x):
    return x * _sigmoid(x)


def _split3_dot(r, oh):
    h1 = r.astype(bf16)
    r2 = r - h1.astype(f32)
    h2 = r2.astype(bf16)
    h3 = (r2 - h2.astype(f32)).astype(bf16)
    return _dot(h1, oh) + _dot(h2, oh) + _dot(h3, oh)


def _rpb_expand_kernel(r_ref, oh_ref, mask_ref, o_ref):
    tiles = _split3_dot(r_ref[...], oh_ref[...]) * LOG2E + mask_ref[...]
    for q in range(GRID_W):
        o_ref[:, q, :] = tiles[:o_ref.shape[0], q * LANES:(q + 1) * LANES]


def _rpb_expand(rpb):
    nd = 2 * NA_WR - 1
    nrow = NA_H * nd
    ncol = 2 * NA_WC - 1
    nxt = jnp.concatenate([rpb[:, 1:], jnp.zeros_like(rpb[:, :1])], axis=1)
    pad = ((0, LANES - nrow), (0, LANES - ncol))
    r01 = jnp.concatenate([jnp.pad(rpb.reshape(nrow, ncol), pad), jnp.pad(nxt.reshape(nrow, ncol), pad)], axis=1)
    q = np.arange(GRID_W)
    cs = np.clip(q - NA_WC // 2, 0, GRID_W - NA_WC)
    in_win = (q[None, :] >= cs[:, None]) & (q[None, :] < cs[:, None] + NA_WC)
    dc = np.clip(q[None, :] - q[:, None], -(NA_WC - 1), NA_WC - 1) + NA_WC - 1
    oh = np.zeros((2, LANES, GRID_W, LANES), np.float32)
    qq, kk = np.nonzero(in_win)
    oh[0, dc[qq, kk], qq, kk] = 1.0
    oh[1, dc[qq, kk], qq, kk + GRID_W] = 1.0
    oh = oh.reshape(2 * LANES, GRID_W * LANES)
    mask = np.where(np.concatenate([in_win, in_win], axis=1), 0.0, NEG_INF).reshape(1, GRID_W * LANES)
    out = pl.pallas_call(
        _rpb_expand_kernel,
        out_shape=jax.ShapeDtypeStruct((nrow, GRID_W, LANES), f32),
        name="rpb_expand",
    )(r01, jnp.asarray(oh, bf16), jnp.asarray(mask, f32))
    return out.reshape(NA_H, nd, GRID_W, LANES)


def _cast_kernel(x_ref, o_ref):
    o_ref[...] = x_ref[...].astype(bf16)


def _cast_bf16(w, rows):
    R, C = w.shape
    return pl.pallas_call(
        _cast_kernel,
        grid=(R // rows,),
        in_specs=[pl.BlockSpec((rows, C), lambda j: (j, 0))],
        out_specs=pl.BlockSpec((rows, C), lambda j: (j, 0)),
        out_shape=jax.ShapeDtypeStruct((R, C), bf16),
        name="cast_bf16",
    )(w)


def _mod_kernel(a_ref, w_ref, b_ref, o_ref):
    a = _silu(a_ref[...]).astype(bf16)
    o_ref[...] = (_dot(a, w_ref[...].astype(bf16)) + b_ref[...])[:, None, :]


def _adaln_mod(c, c_ctx, w_mod, b_mod):
    B = c.shape[0]
    assert B < SUBLANES
    a = jnp.concatenate([c, c_ctx[None], jnp.zeros((SUBLANES - B - 1, D), f32)], axis=0)
    n = w_mod.shape[1]
    return pl.pallas_call(
        _mod_kernel,
        grid=(n // D,),
        in_specs=[pl.BlockSpec((SUBLANES, D), lambda j: (0, 0)),
                  pl.BlockSpec((D, D), lambda j: (0, j)),
                  pl.BlockSpec((1, D), lambda j: (0, j))],
        out_specs=pl.BlockSpec((SUBLANES, 1, D), lambda j: (0, 0, j)),
        out_shape=jax.ShapeDtypeStruct((SUBLANES, 1, n), f32),
        name="adaln_mod",
    )(a, w_mod, b_mod.reshape(1, n))


def _norm_mod(x, g, sh, sc):
    ms = jnp.mean(x * x, axis=-1, keepdims=True)
    return (x * lax.rsqrt(ms + EPS)) * (g * (1.0 + sc)) + sh


def _head_rms(t, bd, g):
    sq = (t * t).astype(bf16)
    ms = jnp.concatenate([_dot(sq[:, :MXU_DIM], bd), _dot(sq[:, MXU_DIM:], bd)], axis=1)
    return t * lax.rsqrt(ms + EPS) * g


def _log_alpha(zl, wa_ref, ba_ref):
    z = _dot(zl.astype(bf16), wa_ref[...]) + ba_ref[...]
    soft = jnp.log2(1.0 + jnp.exp2(z * -LOG2E))
    return jnp.maximum(soft * (-1.0 / GLA_TAU), GLA_LA_MIN * LOG2E)


def _ctx_kernel(ctx_ref, sh_ref, sc_ref, g1_ref, wk_ref, wv_ref, wkg_ref, wvg_ref, wlr_ref,
                wa_ref, ba_ref, gk_ref, bd_ref, su_ref, sl_ref,
                kc_ref, vc_ref, sf_ref, sb_ref):
    h = _norm_mod(ctx_ref[0], g1_ref[...], sh_ref[0], sc_ref[0]).astype(bf16)
    k = _dot_nt(h, wk_ref[...])
    kc_ref[0] = _head_rms(k, bd_ref[...], gk_ref[...]).astype(bf16)
    vc_ref[0] = _dot_nt(h, wv_ref[...]).astype(bf16)
    kg = _dot_nt(h, wkg_ref[...])
    vg = _dot_nt(h, wvg_ref[...]).astype(bf16)
    la = _log_alpha(_dot_nt(h, wlr_ref[...]), wa_ref, ba_ref)
    hi, lo = _split2(la)
    ef = _dot(su_ref[...], hi[:, :GQK]) + _dot(su_ref[...], lo[:, :GQK])
    eb = _dot(sl_ref[...], hi[:, GQK:]) + _dot(sl_ref[...], lo[:, GQK:])
    r = lax.broadcasted_iota(jnp.int32, (GQK, GV), 0) // GLA_DK
    cc = lax.broadcasted_iota(jnp.int32, (GQK, GV), 1) // GLA_DV
    diag = r == cc
    for e, out in ((ef, sf_ref), (eb, sb_ref)):
        kw = (kg * jnp.exp2(e)).astype(bf16)
        u = _dot(kw.T, vg)
        out[0] = jnp.where(diag, u, 0.0)


def _ctx_side(ctx, mod3, norm1_g, w_t, w_a, b_a, gk_t, bd, n_b):
    su = jnp.asarray(np.triu(np.ones((CTX, CTX), np.float32), 1), bf16)
    sl = jnp.asarray(np.tril(np.ones((CTX, CTX), np.float32), -1), bf16)
    B = ctx.shape[0]
    cst = lambda shape, idx: pl.BlockSpec(shape, lambda b: idx)
    return pl.pallas_call(
        _ctx_kernel,
        grid=(B,),
        in_specs=[pl.BlockSpec((1, CTX, D), lambda b: (b, 0, 0)),
                  cst((1, 1, D), (n_b, 0, 0)), cst((1, 1, D), (n_b, 0, 1)),
                  cst((1, D), (0, 0)),
                  cst((NA_W, D), (OFF_KNA // NA_W, 0)), cst((NA_W, D), (OFF_VNA // NA_W, 0)),
                  cst((GQK, D), (OFF_KG // GQK, 0)), cst((GV, D), (OFF_VG // GV, 0)),
                  cst((N_LR, D), (OFF_LR // N_LR, 0)),
                  cst((N_LR, 2 * GQK), (0, 0)), cst((1, 2 * GQK), (0, 0)),
                  cst((1, NA_W), (0, 0)), cst((MXU_DIM, MXU_DIM), (0, 0)),
                  cst((CTX, CTX), (0, 0)), cst((CTX, CTX), (0, 0))],
        out_specs=[pl.BlockSpec((1, CTX, NA_W), lambda b: (b, 0, 0)),
                   pl.BlockSpec((1, CTX, NA_W), lambda b: (b, 0, 0)),
                   pl.BlockSpec((1, GQK, GV), lambda b: (b, 0, 0)),
                   pl.BlockSpec((1, GQK, GV), lambda b: (b, 0, 0))],
        out_shape=[jax.ShapeDtypeStruct((B, CTX, NA_W), bf16),
                   jax.ShapeDtypeStruct((B, CTX, NA_W), bf16),
                   jax.ShapeDtypeStruct((B, GQK, GV), f32),
                   jax.ShapeDtypeStruct((B, GQK, GV), f32)],
        compiler_params=pltpu.CompilerParams(vmem_limit_bytes=VMEM_LIMIT),
        name="ctx_side",
    )(ctx, mod3, mod3, norm1_g, w_t, w_t, w_t, w_t, w_t, w_a, b_a, gk_t, bd, su, sl)


def _proj_kernel(x_ref, sh_ref, sc_ref, g1_ref, w_ref, gq_ref, gk_ref, bd_ref,
                 rc_ref, rs_ref, cc_ref, cs_ref, wa_ref, ba_ref, tl_ref, *refs):
    n_cast = (len(refs) - N_PROJ_OUT) // 2
    cast_in, refs = refs[:n_cast], refs[n_cast:]
    na_ref, gf_ref, gb_ref, mg_ref, dec_ref, ktf_ref, ktb_ref = refs[:N_PROJ_OUT]
    cast_out = refs[N_PROJ_OUT:]
    for src, dst in zip(cast_in, cast_out):
        dst[...] = src[...].astype(bf16)

    bd = bd_ref[...]
    tl = tl_ref[...]
    nrow = PROJ_SUB // GRID_W
    nchunk = PROJ_SUB // CHUNK
    lane = lax.broadcasted_iota(jnp.int32, (PROJ_SUB, GQK), 1)
    first_half = (lane & 31) < 16

    for hs in range(0, x_ref.shape[1], PROJ_SUB):
        rows = slice(hs, hs + PROJ_SUB)
        hb = _norm_mod(x_ref[0, rows], g1_ref[...], sh_ref[0], sc_ref[0]).astype(bf16)

        def proj(off, n):
            return _dot_nt(hb, w_ref[off:off + n, :])

        zl = proj(OFF_LR, N_LR)
        q_raw = proj(OFF_QNA, NA_W)
        la = _log_alpha(zl, wa_ref, ba_ref)
        k_raw = proj(OFF_KNA, NA_W)
        hi, lo = _split2(la)
        na_ref[0, rows, NA_Q:NA_Q + NA_W] = _head_rms(q_raw, bd, gq_ref[...]).astype(bf16)
        cum = jnp.concatenate(
            [_dot(tl, hi[s:s + MXU_DIM]) + _dot(tl, lo[s:s + MXU_DIM]) for s in range(0, PROJ_SUB, MXU_DIM)], axis=0)
        na_ref[0, rows, NA_K:NA_K + NA_W] = _head_rms(k_raw, bd, gk_ref[...]).astype(bf16)
        qg_raw = proj(OFF_QG, GQK)
        kg_raw = proj(OFF_KG, GQK)
        na_ref[0, rows, NA_V:NA_V + NA_W] = proj(OFF_VNA, NA_W).astype(bf16)
        gf_ref[0, rows, G_V:G_V + GV] = proj(OFF_VG, GV).astype(bf16)

        gr = slice(hs // GRID_W, hs // GRID_W + nrow)
        cos_t = (rc_ref[gr][:, None, :] + cc_ref[...][None]).reshape(PROJ_SUB, GQK)
        sin_t = (rs_ref[gr][:, None, :] + cs_ref[...][None]).reshape(PROJ_SUB, GQK)

        def rope(t):
            partner = jnp.where(first_half, pltpu.roll(t, GQK - 16, 1), pltpu.roll(t, 16, 1))
            return t * cos_t + partner * sin_t

        qg = rope(qg_raw) * (GLA_DK ** -0.5)
        kg = rope(kg_raw)

        cum3 = cum.reshape(nchunk, CHUNK, 2 * GQK)
        tot3 = cum3[:, CHUNK - 1:CHUNK, :]
        la3 = la.reshape(nchunk, CHUNK, 2 * GQK)
        dec_ref[0, hs // CHUNK:hs // CHUNK + nchunk] = jnp.exp2(tot3).reshape(nchunk, 2 * GQK)
        b_f = cum3[:, :, :GQK].reshape(PROJ_SUB, GQK)
        e_f = (tot3 - cum3)[:, :, :GQK].reshape(PROJ_SUB, GQK)
        b_b = (tot3 - cum3 + la3)[:, :, GQK:].reshape(PROJ_SUB, GQK)
        e_b = (cum3 - la3)[:, :, GQK:].reshape(PROJ_SUB, GQK)
        mg_ref[0, rows, MG_OG:MG_OG + GV] = _silu(proj(OFF_OG, GV)).astype(bf16)
        gf_ref[0, rows, G_Q:G_Q + GQK] = (qg * jnp.exp2(b_f)).astype(bf16)
        gf_ref[0, rows, G_KI:G_KI + GQK] = (kg * jnp.exp2(-b_f)).astype(bf16)
        ktf_ref[0, :, rows] = (kg * jnp.exp2(e_f)).T.astype(bf16)
        mg_ref[0, rows, MG_A:MG_A + D] = _sigmoid(proj(OFF_MA, D)).astype(bf16)
        gb_ref[0, rows, G_Q - GV:G_Q - GV + GQK] = (qg * jnp.exp2(b_b)).astype(bf16)
        gb_ref[0, rows, G_KI - GV:G_KI - GV + GQK] = (kg * jnp.exp2(-b_b)).astype(bf16)
        ktb_ref[0, :, rows] = (kg * jnp.exp2(e_b)).T.astype(bf16)
        mg_ref[0, rows, MG_B:MG_B + D] = _sigmoid(proj(OFF_MB, D)).astype(bf16)


def _rope_tables(L):
    n = GLA_DK // 4
    freqs = ROPE_THETA ** (-np.arange(n, dtype=np.float64) / n)
    lane = np.arange(GQK)
    fl = freqs[lane % n]
    is_row = (lane % GLA_DK) < GLA_DK // 2
    sign = np.where((lane % 32) < 16, -1.0, 1.0)
    rows = np.arange(L // GRID_W)[:, None] * fl[None]
    cols = np.arange(GRID_W)[:, None] * fl[None]
    rc = np.where(is_row, np.cos(rows), 0.0)
    rs = np.where(is_row, np.sin(rows) * sign, 0.0)
    cc = np.where(~is_row, np.cos(cols), 0.0)
    cs = np.where(~is_row, np.sin(cols) * sign, 0.0)
    return [jnp.asarray(t, f32) for t in (rc, rs, cc, cs)]


def _in_proj(x, mod3, norm1_g, w_t, gq_t, gk_t, bd, w_a, b_a, cast_ws):
    B, L, _ = x.shape
    tm = TM_PROJ
    rows_per = tm // GRID_W
    nstep = L // tm
    total = B * nstep

    def slab(w):
        R, C = w.shape
        if R % (total * BF16_SUBLANES) == 0:
            return pl.BlockSpec((R // total, C), lambda b, i: (b * nstep + i, 0))
        assert (2 * R) % (total * BF16_SUBLANES) == 0 and C % (2 * LANES) == 0
        return pl.BlockSpec((2 * R // total, C // 2), lambda b, i: ((b * nstep + i) // 2, (b * nstep + i) % 2))
    rc, rs, cc, cs = _rope_tables(L)
    blk = np.kron(np.eye(MXU_DIM // CHUNK), np.tril(np.ones((CHUNK, CHUNK)))).astype(np.float32)
    tl = jnp.asarray(blk, bf16)
    tok = lambda n: pl.BlockSpec((1, tm, n), lambda b, i: (b, i, 0))
    sds = lambda n, dt: jax.ShapeDtypeStruct((B, L, n), dt)
    return pl.pallas_call(
        _proj_kernel,
        grid=(B, L // tm),
        in_specs=[tok(D),
                  pl.BlockSpec((1, 1, D), lambda b, i: (b, 0, 0)),
                  pl.BlockSpec((1, 1, D), lambda b, i: (b, 0, 1)),
                  _const_spec((1, D)), _const_spec((D_IN, D)),
                  _const_spec((1, NA_W)), _const_spec((1, NA_W)), _const_spec((MXU_DIM, MXU_DIM)),
                  pl.BlockSpec((rows_per, GQK), lambda b, i: (i, 0)),
                  pl.BlockSpec((rows_per, GQK), lambda b, i: (i, 0)),
                  _const_spec((GRID_W, GQK)), _const_spec((GRID_W, GQK)),
                  _const_spec((N_LR, 2 * GQK)), _const_spec((1, 2 * GQK)), _const_spec((MXU_DIM, MXU_DIM))]
                 + [slab(w) for w in cast_ws],
        out_specs=[tok(W_NA), tok(W_GF), tok(W_GB), tok(W_MG),
                   pl.BlockSpec((1, tm // CHUNK, 2 * GQK), lambda b, i: (b, i, 0)),
                   pl.BlockSpec((1, GQK, tm), lambda b, i: (b, 0, i)),
                   pl.BlockSpec((1, GQK, tm), lambda b, i: (b, 0, i))]
                  + [slab(w) for w in cast_ws],
        out_shape=[sds(W_NA, bf16), sds(W_GF, bf16), sds(W_GB, bf16), sds(W_MG, bf16),
                   jax.ShapeDtypeStruct((B, L // CHUNK, 2 * GQK), f32),
                   jax.ShapeDtypeStruct((B, GQK, L), bf16), jax.ShapeDtypeStruct((B, GQK, L), bf16)]
                  + [jax.ShapeDtypeStruct(w.shape, bf16) for w in cast_ws],
        compiler_params=pltpu.CompilerParams(
            dimension_semantics=("parallel", "parallel"), vmem_limit_bytes=VMEM_LIMIT),
        name="in_proj",
    )(x, mod3, mod3, norm1_g, w_t, gq_t, gk_t, bd, rc, rs, cc, cs, w_a, b_a, tl, *cast_ws)


def _lane_blocks(t):
    return [t[:, c:c + LANES] for c in range(0, t.shape[1], LANES)]


def _na_kernel(q_ref, k_ref, v_ref, kc_ref, vc_ref, t2_ref, o_ref, sl_ref, sc_ref, m_ref):
    i = pl.program_id(1)
    rows = k_ref.shape[1] // GRID_W
    lane_head = lax.broadcasted_iota(jnp.int32, (GRID_W, MXU_DIM), 1) // NA_DH
    hmask = [lane_head == h for h in range(NA_GH)]

    def window_start(r):
        rs = jnp.clip(r - NA_WR // 2, 0, rows - NA_WR)
        return rs, pl.multiple_of(rs * GRID_W, GRID_W)

    def produce(rr, bank, j):
        r = i * NA_R + rr
        rs, start = window_start(r)
        d0 = rs - r + (NA_WR - 1)
        row0 = pl.multiple_of(rr * GRID_W, GRID_W)
        for g in range(NA_NG):
            ls = slice(MXU_DIM * g, MXU_DIM * (g + 1))
            q = q_ref[0, pl.ds(row0, GRID_W), ls]
            qs = jnp.concatenate([jnp.where(m, q, jnp.zeros_like(q)) for m in hmask], axis=0)
            kw = k_ref[0, pl.ds(start, NA_WR * GRID_W), ls]
            bias = jnp.concatenate(
                [jnp.concatenate([t2_ref[NA_GH * g + h, d0 + 2 * jj] for jj in range(NA_WR // 2)], axis=1)
                 for h in range(NA_GH)], axis=0)
            s_loc = _dot_nt(qs, kw) + bias
            s_ctx = _dot_nt(qs, kc_ref[0, :, ls])
            mx = functools.reduce(jnp.maximum, _lane_blocks(s_loc) + _lane_blocks(s_ctx))
            sl_ref[bank, j, g] = s_loc
            sc_ref[bank, j, g] = s_ctx
            m_ref[bank, j, g] = jnp.broadcast_to(jnp.max(mx, axis=-1, keepdims=True), (NA_GH * GRID_W, LANES))

    def consume(rr, bank, j):
        _, start = window_start(i * NA_R + rr)
        row0 = pl.multiple_of(rr * GRID_W, GRID_W)
        outs = []
        for g in range(NA_NG):
            ls = slice(MXU_DIM * g, MXU_DIM * (g + 1))
            m = m_ref[bank, j, g]
            p_loc = [jnp.exp2(sl_ref[bank, j, g, :, c:c + LANES] - m) for c in range(0, NA_WR * GRID_W, LANES)]
            p_ctx = [jnp.exp2(sc_ref[bank, j, g, :, c:c + LANES] - m) for c in range(0, CTX, LANES)]
            den = jnp.sum(functools.reduce(jnp.add, p_loc + p_ctx), axis=-1, keepdims=True)
            vw = v_ref[0, pl.ds(start, NA_WR * GRID_W), ls]
            acc = (_dot(jnp.concatenate(p_loc, axis=1).astype(bf16), vw)
                   + _dot(jnp.concatenate(p_ctx, axis=1).astype(bf16), vc_ref[0, :, ls]))
            acc = acc * (1.0 / den)
            o = jnp.zeros((GRID_W, MXU_DIM), f32)
            for h in range(NA_GH):
                o = o + jnp.where(hmask[h], acc[h * GRID_W:(h + 1) * GRID_W], 0.0)
            outs.append(o)
        o_ref[0, pl.ds(row0, GRID_W), :] = jnp.concatenate(outs, axis=1).astype(bf16)

    for j in range(NA_RI):
        produce(jnp.int32(j), 0, j)

    def step(t, carry):
        bank = t & 1
        for j in range(NA_RI):
            consume(t * NA_RI + j, bank, j)
        for j in range(NA_RI):
            produce(jnp.minimum((t + 1) * NA_RI + j, NA_R - 1), 1 - bank, j)
        return carry

    lax.fori_loop(0, NA_R // NA_RI, step, 0)


def _na_attn(qkv, kc, vc, t2):
    B, L, _ = qkv.shape
    nblk = L // (NA_R * GRID_W)
    return pl.pallas_call(
        _na_kernel,
        grid=(B, nblk),
        in_specs=[pl.BlockSpec((1, NA_R * GRID_W, NA_W), lambda b, i: (b, i, NA_Q // NA_W)),
                  pl.BlockSpec((1, L, NA_W), lambda b, i: (b, 0, NA_K // NA_W), pipeline_mode=pl.Buffered(1)),
                  pl.BlockSpec((1, L, NA_W), lambda b, i: (b, 0, NA_V // NA_W), pipeline_mode=pl.Buffered(1)),
                  pl.BlockSpec((1, CTX, NA_W), lambda b, i: (b, 0, 0)),
                  pl.BlockSpec((1, CTX, NA_W), lambda b, i: (b, 0, 0)),
                  _const_spec((NA_H, 2 * NA_WR - 1, GRID_W, LANES))],
        out_specs=pl.BlockSpec((1, NA_R * GRID_W, NA_W), lambda b, i: (b, i, 0)),
        out_shape=jax.ShapeDtypeStruct((B, L, NA_W), bf16),
        scratch_shapes=[pltpu.VMEM((2, NA_RI, NA_NG, NA_GH * GRID_W, NA_WR * GRID_W), f32),
                        pltpu.VMEM((2, NA_RI, NA_NG, NA_GH * GRID_W, CTX), f32),
                        pltpu.VMEM((2, NA_RI, NA_NG, NA_GH * GRID_W, LANES), f32)],
        compiler_params=pltpu.CompilerParams(
            dimension_semantics=("parallel", "parallel"), vmem_limit_bytes=VMEM_LIMIT),
        name="na_attn",
    )(qkv, qkv, qkv, kc, vc, t2)


def _gla_chunk(q, ki, v, ket, dcol, s_ref, tri, hmask):
    qs = jnp.concatenate([jnp.where(m, q, jnp.zeros_like(q)) for m in hmask], axis=0)
    att = jnp.where(tri, _dot_nt(qs, ki), 0.0).astype(bf16)
    o = _dot(q, s_ref[...].astype(bf16))
    intra = [_dot(att[h * CHUNK:(h + 1) * CHUNK], v[:, h * GLA_DV:(h + 1) * GLA_DV]) for h in range(GLA_H)]
    o = o + jnp.concatenate(intra, axis=1)
    for h in range(GLA_H):
        rs = slice(h * GLA_DK, (h + 1) * GLA_DK)
        cs = slice(h * GLA_DV, (h + 1) * GLA_DV)
        s_ref[rs, cs] = dcol[rs] * s_ref[rs, cs] + _dot(ket[rs], v[:, cs])
    return o


def _gla_kernel(gf_ref, ktf_ref, decf_ref, gb_ref, vb_ref, ktb_ref, decb_ref,
                s0f_ref, s0b_ref, of_ref, ob_ref, sf_ref, sb_ref):
    @pl.when(pl.program_id(0) == 0)
    def _():
        sf_ref[...] = s0f_ref[...]
        sb_ref[...] = s0b_ref[...]

    lane_head = lax.broadcasted_iota(jnp.int32, (CHUNK, GQK), 1) // GLA_DK
    hmask = [lane_head == h for h in range(GLA_H)]
    ri = lax.broadcasted_iota(jnp.int32, (GLA_H * CHUNK, CHUNK), 0) & (CHUNK - 1)
    ci = lax.broadcasted_iota(jnp.int32, (GLA_H * CHUNK, CHUNK), 1)
    tri_f = ri >= ci
    tri_b = ri <= ci
    nbatch = gf_ref.shape[0]
    dect_f = [decf_ref[b][:, :GQK].T for b in range(nbatch)]
    dect_b = [decb_ref[b][:, GQK:].T for b in range(nbatch)]

    def k_end(kt_ref, b, c):
        kt = kt_ref[b, :, (c // 2) * 2 * CHUNK:(c // 2 + 1) * 2 * CHUNK]
        if c % 2:
            kt = pltpu.roll(kt, CHUNK, 1)
        return kt[:, :CHUNK]

    for c in range(GLA_CB):
        ts = slice(c * CHUNK, (c + 1) * CHUNK)
        cb = GLA_CB - 1 - c
        tb = slice(cb * CHUNK, (cb + 1) * CHUNK)
        for b in range(nbatch):
            of_ref[b, ts, :] = _gla_chunk(gf_ref[b, ts, G_Q:G_Q + GQK], gf_ref[b, ts, G_KI:G_KI + GQK],
                                          gf_ref[b, ts, G_V:G_V + GV], k_end(ktf_ref, b, c),
                                          dect_f[b][:, c:c + 1], sf_ref.at[b], tri_f, hmask)
            ob_ref[b, tb, :] = _gla_chunk(gb_ref[b, tb, G_Q - GV:G_Q - GV + GQK],
                                          gb_ref[b, tb, G_KI - GV:G_KI - GV + GQK],
                                          vb_ref[b, tb, :], k_end(ktb_ref, b, cb),
                                          dect_b[b][:, cb:cb + 1], sb_ref.at[b], tri_b, hmask)


def _gla_scan(gf, gb, ktf, ktb, dec, s0f, s0b):
    B, L, _ = gf.shape
    tm = GLA_CB * CHUNK
    nb = L // tm
    fw = lambda n: pl.BlockSpec((B, tm, n), lambda i: (0, i, 0))
    bw = lambda n: pl.BlockSpec((B, tm, n), lambda i: (0, nb - 1 - i, 0))
    st = pl.BlockSpec((B, GQK, GV), lambda i: (0, 0, 0))
    return pl.pallas_call(
        _gla_kernel,
        grid=(nb,),
        in_specs=[fw(W_GF),
                  pl.BlockSpec((B, GQK, tm), lambda i: (0, 0, i)),
                  pl.BlockSpec((B, GLA_CB, 2 * GQK), lambda i: (0, i, 0)),
                  bw(W_GB), bw(GV),
                  pl.BlockSpec((B, GQK, tm), lambda i: (0, 0, nb - 1 - i)),
                  pl.BlockSpec((B, GLA_CB, 2 * GQK), lambda i: (0, nb - 1 - i, 0)),
                  st, st],
        out_specs=[fw(GV), bw(GV)],
        out_shape=[jax.ShapeDtypeStruct((B, L, GV), f32)] * 2,
        scratch_shapes=[pltpu.VMEM((B, GQK, GV), f32), pltpu.VMEM((B, GQK, GV), f32)],
        compiler_params=pltpu.CompilerParams(
            dimension_semantics=("arbitrary",), vmem_limit_bytes=VMEM_LIMIT),
        name="gla_scan",
    )(gf, ktf, dec, gb, gf, ktb, dec, s0f, s0b)


def _merge_ffn_kernel(x_ref, ona_ref, of_ref, ob_ref, mg_ref,
                      g1_ref, sh2_ref, sc2_ref, g2_ref, n2_ref, gg_ref,
                      wna_ref, wgl_ref, wo_ref, wg_ref, wu_ref, wd_ref, o_ref):
    x1s, h2s = [], []
    for hs in range(0, x_ref.shape[1], FFN_SUB):
        rows = slice(hs, hs + FFN_SUB)
        og = of_ref[0, rows] + ob_ref[0, rows]
        parts = []
        for h in range(GLA_H):
            seg = og[:, h * GLA_DV:(h + 1) * GLA_DV]
            ms = jnp.mean(seg * seg, axis=-1, keepdims=True)
            parts.append(seg * lax.rsqrt(ms + EPS) * gg_ref[...])
        ogl = (jnp.concatenate(parts, axis=1) * mg_ref[0, rows, MG_OG:MG_OG + GV].astype(f32)).astype(bf16)
        y = (mg_ref[0, rows, MG_A:MG_A + D].astype(f32) * _dot(ona_ref[0, rows], wna_ref[...])
             + mg_ref[0, rows, MG_B:MG_B + D].astype(f32) * _dot(ogl, wgl_ref[...]))
        x1 = x_ref[0, rows] + g1_ref[0] * _dot(y.astype(bf16), wo_ref[...])
        x1s.append(x1)
        h2s.append(_norm_mod(x1, n2_ref[...], sh2_ref[0], sc2_ref[0]).astype(bf16))
    x1 = jnp.concatenate(x1s, axis=0)
    h2 = jnp.concatenate(h2s, axis=0)
    act = (_silu(_dot(h2, wg_ref[...])) * _dot(h2, wu_ref[...])).astype(bf16)
    o_ref[0] = x1 + g2_ref[0] * _dot(act, wd_ref[...])


def _merge_ffn(x, ona, of, ob, mg, mod3, norm2_g, gg, wna, wgl, wo, wg, wu, wd):
    B, L, _ = x.shape
    tm = TM_FFN
    tok = lambda n: pl.BlockSpec((1, tm, n), lambda b, i: (b, i, 0))
    modv = lambda j: pl.BlockSpec((1, 1, D), lambda b, i: (b, 0, j))
    return pl.pallas_call(
        _merge_ffn_kernel,
        grid=(B, L // tm),
        in_specs=[tok(D), tok(NA_W), tok(GV), tok(GV), tok(W_MG),
                  modv(2), modv(3), modv(4), modv(5),
                  _const_spec((1, D)), _const_spec((1, GLA_DV)),
                  _const_spec((NA_W, D)), _const_spec((GV, D)), _const_spec((D, D)),
                  _const_spec((D, D_FF)), _const_spec((D, D_FF)), _const_spec((D_FF, D))],
        out_specs=tok(D),
        out_shape=jax.ShapeDtypeStruct((B, L, D), f32),
        compiler_params=pltpu.CompilerParams(
            dimension_semantics=("parallel", "parallel"), vmem_limit_bytes=VMEM_LIMIT),
        name="merge_ffn",
    )(x, ona, of, ob, mg, mod3, mod3, mod3, mod3, norm2_g, gg, wna, wgl, wo, wg, wu, wd)


def kernel(x, c, ctx, c_ctx, w_mod, b_mod, norm1_g, norm2_g, w_in, na_q_norm_g, na_k_norm_g, na_rpb,
           gla_w_alpha, gla_b_alpha, gla_norm_g, w_branch_na, w_branch_gla, w_out,
           w_ffn_gate, w_ffn_up, w_ffn_down):
    B, L, _ = x.shape
    assert w_mod.shape[0] == 1 and L % (NA_R * GRID_W) == 0 and L % TM_PROJ == 0 and D_IN % W_IN_CAST_ROWS == 0

    w_t = _cast_bf16(jnp.transpose(w_in[0]), W_IN_CAST_ROWS)
    zero = jnp.zeros((GLA_RANK, GQK), f32)
    w_a = jnp.concatenate([jnp.concatenate([gla_w_alpha[0, 0], zero], axis=1),
                           jnp.concatenate([zero, gla_w_alpha[0, 1]], axis=1)], axis=0).astype(bf16)
    b_a = gla_b_alpha[0].reshape(1, 2 * GQK)
    gq_t = jnp.tile(na_q_norm_g[0] * (NA_DH ** -0.5 * LOG2E), NA_H).reshape(1, NA_W)
    gk_t = jnp.tile(na_k_norm_g[0], NA_H).reshape(1, NA_W)
    bd = jnp.asarray(np.kron(np.eye(4), np.full((NA_DH, NA_DH), 1.0 / NA_DH)), bf16)

    mod3 = _adaln_mod(c, c_ctx, w_mod[0], b_mod[0])
    t2 = _rpb_expand(na_rpb[0])

    kc, vc, s0f, s0b = _ctx_side(ctx, mod3, norm1_g, w_t, w_a, b_a, gk_t, bd, B)
    later_ws = [w_branch_na[0], w_branch_gla[0], w_out[0], w_ffn_gate[0], w_ffn_up[0], w_ffn_down[0]]
    na_qkv, gla_f, gla_b, gates, dec, ktf, ktb, *later_bf = _in_proj(
        x, mod3, norm1_g, w_t, gq_t, gk_t, bd, w_a, b_a, later_ws)
    ona = _na_attn(na_qkv, kc, vc, t2)
    of, ob = _gla_scan(gla_f, gla_b, ktf, ktb, dec, s0f, s0b)
    return _merge_ffn(x, ona, of, ob, gates, mod3, norm2_g, gla_norm_g, *later_bf)
```

```python
import functools

import numpy as np
import jax
import jax.numpy as jnp
from jax import lax
from jax.experimental import pallas as pl
from jax.experimental.pallas import tpu as pltpu

D = 1024
GRID_W = 64
CTX = 256
NA_H = 8
NA_DH = 64
NA_WR = 8
NA_WC = 16
GLA_H = 4
GLA_DK = 64
GLA_DV = 128
GLA_RANK = 16
GLA_TAU = 16.0
GLA_LA_MIN = -1.0
CHUNK = 64
ROPE_THETA = 10000.0
D_FF = 2816
EPS = 1e-6
NEG_INF = -1e30
NA_W = NA_H * NA_DH
GQK = GLA_H * GLA_DK
GV = GLA_H * GLA_DV

OFF_QNA, OFF_KNA, OFF_VNA = 0, 512, 1024
OFF_QG, OFF_KG, OFF_VG, OFF_OG = 1536, 1792, 2048, 2560
OFF_LR, OFF_MA, OFF_MB = 3072, 3104, 4128
N_LR = 2 * GLA_RANK
N_PROJ_OUT = 7
NA_Q, NA_K, NA_V, W_NA = 0, 512, 1024, 1536
G_V, G_Q, G_KI, W_GF = 0, 512, 768, 1024
W_GB = W_GF - GV
MG_A, MG_B, MG_OG, W_MG = 0, 1024, 2048, 2560
D_IN = 5152
LOG2E = 1.4426950408889634

LANES = 128
SUBLANES = 8
BF16_SUBLANES = 16
MXU_DIM = 256
NA_GH = MXU_DIM // NA_DH
NA_NG = NA_H // NA_GH
VMEM_LIMIT = 56 * 1024 * 1024

W_IN_CAST_ROWS = 46 * BF16_SUBLANES
TM_PROJ = 1024
PROJ_SUB = 256
TM_FFN = 512
FFN_SUB = 256
NA_R = 64
NA_RI = 4
GLA_CB = 16

f32 = jnp.float32
bf16 = jnp.bfloat16


def _const_spec(shape):
    nd = len(shape)
    return pl.BlockSpec(shape, lambda *_: (0,) * nd, pipeline_mode=pl.Buffered(1))


def _dot(a, b):
    return jnp.dot(a, b, preferred_element_type=f32)


def _dot_nt(a, b):
    return lax.dot_general(a, b, (((1,), (1,)), ((), ())), preferred_element_type=f32)


def _split2(x):
    hi = x.astype(bf16)
    lo = (x - hi.astype(f32)).astype(bf16)
    return hi, lo


def _sigmoid(x):
    return 0.5 * jnp.tanh(0.5 * x) + 0.5


def _silu(x):
    return x * _sigmoid(x)


def _split3_dot(r, oh):
    h1 = r.astype(bf16)
    r2 = r - h1.astype(f32)
    h2 = r2.astype(bf16)
    h3 = (r2 - h2.astype(f32)).astype(bf16)
    return _dot(h1, oh) + _dot(h2, oh) + _dot(h3, oh)


def _rpb_expand_kernel(r_ref, oh_ref, mask_ref, o_ref):
    tiles = _split3_dot(r_ref[...], oh_ref[...]) * LOG2E + mask_ref[...]
    for q in range(GRID_W):
        o_ref[:, q, :] = tiles[:o_ref.shape[0], q * LANES:(q + 1) * LANES]


def _rpb_expand(rpb):
    nd = 2 * NA_WR - 1
    nrow = NA_H * nd
    ncol = 2 * NA_WC - 1
    nxt = jnp.concatenate([rpb[:, 1:], jnp.zeros_like(rpb[:, :1])], axis=1)
    pad = ((0, LANES - nrow), (0, LANES - ncol))
    r01 = jnp.concatenate([jnp.pad(rpb.reshape(nrow, ncol), pad), jnp.pad(nxt.reshape(nrow, ncol), pad)], axis=1)
    q = np.arange(GRID_W)
    cs = np.clip(q - NA_WC // 2, 0, GRID_W - NA_WC)
    in_win = (q[None, :] >= cs[:, None]) & (q[None, :] < cs[:, None] + NA_WC)
    dc = np.clip(q[None, :] - q[:, None], -(NA_WC - 1), NA_WC - 1) + NA_WC - 1
    oh = np.zeros((2, LANES, GRID_W, LANES), np.float32)
    qq, kk = np.nonzero(in_win)
    oh[0, dc[qq, kk], qq, kk] = 1.0
    oh[1, dc[qq, kk], qq, kk + GRID_W] = 1.0
    oh = oh.reshape(2 * LANES, GRID_W * LANES)
    mask = np.where(np.concatenate([in_win, in_win], axis=1), 0.0, NEG_INF).reshape(1, GRID_W * LANES)
    out = pl.pallas_call(
        _rpb_expand_kernel,
        out_shape=jax.ShapeDtypeStruct((nrow, GRID_W, LANES), f32),
        name="rpb_expand",
    )(r01, jnp.asarray(oh, bf16), jnp.asarray(mask, f32))
    return out.reshape(NA_H, nd, GRID_W, LANES)


def _cast_kernel(x_ref, o_ref):
    o_ref[...] = x_ref[...].astype(bf16)


def _cast_bf16(w, rows):
    R, C = w.shape
    return pl.pallas_call(
        _cast_kernel,
        grid=(R // rows,),
        in_specs=[pl.BlockSpec((rows, C), lambda j: (j, 0))],
        out_specs=pl.BlockSpec((rows, C), lambda j: (j, 0)),
        out_shape=jax.ShapeDtypeStruct((R, C), bf16),
        name="cast_bf16",
    )(w)


def _mod_kernel(a_ref, w_ref, b_ref, o_ref):
    a = _silu(a_ref[...]).astype(bf16)
    o_ref[...] = (_dot(a, w_ref[...].astype(bf16)) + b_ref[...])[:, None, :]


def _adaln_mod(c, c_ctx, w_mod, b_mod):
    B = c.shape[0]
    assert B < SUBLANES
    a = jnp.concatenate([c, c_ctx[None], jnp.zeros((SUBLANES - B - 1, D), f32)], axis=0)
    n = w_mod.shape[1]
    return pl.pallas_call(
        _mod_kernel,
        grid=(n // D,),
        in_specs=[pl.BlockSpec((SUBLANES, D), lambda j: (0, 0)),
                  pl.BlockSpec((D, D), lambda j: (0, j)),
                  pl.BlockSpec((1, D), lambda j: (0, j))],
        out_specs=pl.BlockSpec((SUBLANES, 1, D), lambda j: (0, 0, j)),
        out_shape=jax.ShapeDtypeStruct((SUBLANES, 1, n), f32),
        name="adaln_mod",
    )(a, w_mod, b_mod.reshape(1, n))


def _norm_mod(x, g, sh, sc):
    ms = jnp.mean(x * x, axis=-1, keepdims=True)
    return (x * lax.rsqrt(ms + EPS)) * (g * (1.0 + sc)) + sh


def _head_rms(t, bd, g):
    sq = (t * t).astype(bf16)
    ms = jnp.concatenate([_dot(sq[:, :MXU_DIM], bd), _dot(sq[:, MXU_DIM:], bd)], axis=1)
    return t * lax.rsqrt(ms + EPS) * g


def _log_alpha(zl, wa_ref, ba_ref):
    z = _dot(zl.astype(bf16), wa_ref[...]) + ba_ref[...]
    soft = jnp.log2(1.0 + jnp.exp2(z * -LOG2E))
    return jnp.maximum(soft * (-1.0 / GLA_TAU), GLA_LA_MIN * LOG2E)


def _ctx_kernel(ctx_ref, sh_ref, sc_ref, g1_ref, wk_ref, wv_ref, wkg_ref, wvg_ref, wlr_ref,
                wa_ref, ba_ref, gk_ref, bd_ref, su_ref, sl_ref,
                kc_ref, vc_ref, sf_ref, sb_ref):
    h = _norm_mod(ctx_ref[0], g1_ref[...], sh_ref[0], sc_ref[0]).astype(bf16)
    k = _dot_nt(h, wk_ref[...])
    kc_ref[0] = _head_rms(k, bd_ref[...], gk_ref[...]).astype(bf16)
    vc_ref[0] = _dot_nt(h, wv_ref[...]).astype(bf16)
    kg = _dot_nt(h, wkg_ref[...])
    vg = _dot_nt(h, wvg_ref[...]).astype(bf16)
    la = _log_alpha(_dot_nt(h, wlr_ref[...]), wa_ref, ba_ref)
    hi, lo = _split2(la)
    ef = _dot(su_ref[...], hi[:, :GQK]) + _dot(su_ref[...], lo[:, :GQK])
    eb = _dot(sl_ref[...], hi[:, GQK:]) + _dot(sl_ref[...], lo[:, GQK:])
    r = lax.broadcasted_iota(jnp.int32, (GQK, GV), 0) // GLA_DK
    cc = lax.broadcasted_iota(jnp.int32, (GQK, GV), 1) // GLA_DV
    diag = r == cc
    for e, out in ((ef, sf_ref), (eb, sb_ref)):
        kw = (kg * jnp.exp2(e)).astype(bf16)
        u = _dot(kw.T, vg)
        out[0] = jnp.where(diag, u, 0.0)


def _ctx_side(ctx, mod3, norm1_g, w_t, w_a, b_a, gk_t, bd, n_b):
    su = jnp.asarray(np.triu(np.ones((CTX, CTX), np.float32), 1), bf16)
    sl = jnp.asarray(np.tril(np.ones((CTX, CTX), np.float32), -1), bf16)
    B = ctx.shape[0]
    cst = lambda shape, idx: pl.BlockSpec(shape, lambda b: idx)
    return pl.pallas_call(
        _ctx_kernel,
        grid=(B,),
        in_specs=[pl.BlockSpec((1, CTX, D), lambda b: (b, 0, 0)),
                  cst((1, 1, D), (n_b, 0, 0)), cst((1, 1, D), (n_b, 0, 1)),
                  cst((1, D), (0, 0)),
                  cst((NA_W, D), (OFF_KNA // NA_W, 0)), cst((NA_W, D), (OFF_VNA // NA_W, 0)),
                  cst((GQK, D), (OFF_KG // GQK, 0)), cst((GV, D), (OFF_VG // GV, 0)),
                  cst((N_LR, D), (OFF_LR // N_LR, 0)),
                  cst((N_LR, 2 * GQK), (0, 0)), cst((1, 2 * GQK), (0, 0)),
                  cst((1, NA_W), (0, 0)), cst((MXU_DIM, MXU_DIM), (0, 0)),
                  cst((CTX, CTX), (0, 0)), cst((CTX, CTX), (0, 0))],
        out_specs=[pl.BlockSpec((1, CTX, NA_W), lambda b: (b, 0, 0)),
                   pl.BlockSpec((1, CTX, NA_W), lambda b: (b, 0, 0)),
                   pl.BlockSpec((1, GQK, GV), lambda b: (b, 0, 0)),
                   pl.BlockSpec((1, GQK, GV), lambda b: (b, 0, 0))],
        out_shape=[jax.ShapeDtypeStruct((B, CTX, NA_W), bf16),
                   jax.ShapeDtypeStruct((B, CTX, NA_W), bf16),
                   jax.ShapeDtypeStruct((B, GQK, GV), f32),
                   jax.ShapeDtypeStruct((B, GQK, GV), f32)],
        compiler_params=pltpu.CompilerParams(vmem_limit_bytes=VMEM_LIMIT),
        name="ctx_side",
    )(ctx, mod3, mod3, norm1_g, w_t, w_t, w_t, w_t, w_t, w_a, b_a, gk_t, bd, su, sl)


def _proj_kernel(x_ref, sh_ref, sc_ref, g1_ref, w_ref, gq_ref, gk_ref, bd_ref,
                 rc_ref, rs_ref, cc_ref, cs_ref, wa_ref, ba_ref, tl_ref, *refs):
    n_cast = (len(refs) - N_PROJ_OUT) // 2
    cast_in, refs = refs[:n_cast], refs[n_cast:]
    na_ref, gf_ref, gb_ref, mg_ref, dec_ref, ktf_ref, ktb_ref = refs[:N_PROJ_OUT]
    cast_out = refs[N_PROJ_OUT:]
    for src, dst in zip(cast_in, cast_out):
        dst[...] = src[...].astype(bf16)

    bd = bd_ref[...]
    tl = tl_ref[...]
    nrow = PROJ_SUB // GRID_W
    nchunk = PROJ_SUB // CHUNK
    lane = lax.broadcasted_iota(jnp.int32, (PROJ_SUB, GQK), 1)
    first_half = (lane & 31) < 16

    for hs in range(0, x_ref.shape[1], PROJ_SUB):
        rows = slice(hs, hs + PROJ_SUB)
        hb = _norm_mod(x_ref[0, rows], g1_ref[...], sh_ref[0], sc_ref[0]).astype(bf16)

        def proj(off, n):
            return _dot_nt(hb, w_ref[off:off + n, :])

        zl = proj(OFF_LR, N_LR)
        q_raw = proj(OFF_QNA, NA_W)
        la = _log_alpha(zl, wa_ref, ba_ref)
        k_raw = proj(OFF_KNA, NA_W)
        hi, lo = _split2(la)
        na_ref[0, rows, NA_Q:NA_Q + NA_W] = _head_rms(q_raw, bd, gq_ref[...]).astype(bf16)
        cum = jnp.concatenate(
            [_dot(tl, hi[s:s + MXU_DIM]) + _dot(tl, lo[s:s + MXU_DIM]) for s in range(0, PROJ_SUB, MXU_DIM)], axis=0)
        na_ref[0, rows, NA_K:NA_K + NA_W] = _head_rms(k_raw, bd, gk_ref[...]).astype(bf16)
        qg_raw = proj(OFF_QG, GQK)
        kg_raw = proj(OFF_KG, GQK)
        na_ref[0, rows, NA_V:NA_V + NA_W] = proj(OFF_VNA, NA_W).astype(bf16)
        gf_ref[0, rows, G_V:G_V + GV] = proj(OFF_VG, GV).astype(bf16)

        gr = slice(hs // GRID_W, hs // GRID_W + nrow)
        cos_t = (rc_ref[gr][:, None, :] + cc_ref[...][None]).reshape(PROJ_SUB, GQK)
        sin_t = (rs_ref[gr][:, None, :] + cs_ref[...][None]).reshape(PROJ_SUB, GQK)

        def rope(t):
            partner = jnp.where(first_half, pltpu.roll(t, GQK - 16, 1), pltpu.roll(t, 16, 1))
            return t * cos_t + partner * sin_t

        qg = rope(qg_raw) * (GLA_DK ** -0.5)
        kg = rope(kg_raw)

        cum3 = cum.reshape(nchunk, CHUNK, 2 * GQK)
        tot3 = cum3[:, CHUNK - 1:CHUNK, :]
        la3 = la.reshape(nchunk, CHUNK, 2 * GQK)
        dec_ref[0, hs // CHUNK:hs // CHUNK + nchunk] = jnp.exp2(tot3).reshape(nchunk, 2 * GQK)
        b_f = cum3[:, :, :GQK].reshape(PROJ_SUB, GQK)
        e_f = (tot3 - cum3)[:, :, :GQK].reshape(PROJ_SUB, GQK)
        b_b = (tot3 - cum3 + la3)[:, :, GQK:].reshape(PROJ_SUB, GQK)
        e_b = (cum3 - la3)[:, :, GQK:].reshape(PROJ_SUB, GQK)
        mg_ref[0, rows, MG_OG:MG_OG + GV] = _silu(proj(OFF_OG, GV)).astype(bf16)
        gf_ref[0, rows, G_Q:G_Q + GQK] = (qg * jnp.exp2(b_f)).astype(bf16)
        gf_ref[0, rows, G_KI:G_KI + GQK] = (kg * jnp.exp2(-b_f)).astype(bf16)
        ktf_ref[0, :, rows] = (kg * jnp.exp2(e_f)).T.astype(bf16)
        mg_ref[0, rows, MG_A:MG_A + D] = _sigmoid(proj(OFF_MA, D)).astype(bf16)
        gb_ref[0, rows, G_Q - GV:G_Q - GV + GQK] = (qg * jnp.exp2(b_b)).astype(bf16)
        gb_ref[0, rows, G_KI - GV:G_KI - GV + GQK] = (kg * jnp.exp2(-b_b)).astype(bf16)
        ktb_ref[0, :, rows] = (kg * jnp.exp2(e_b)).T.astype(bf16)
        mg_ref[0, rows, MG_B:MG_B + D] = _sigmoid(proj(OFF_MB, D)).astype(bf16)


def _rope_tables(L):
    n = GLA_DK // 4
    freqs = ROPE_THETA ** (-np.arange(n, dtype=np.float64) / n)
    lane = np.arange(GQK)
    fl = freqs[lane % n]
    is_row = (lane % GLA_DK) < GLA_DK // 2
    sign = np.where((lane % 32) < 16, -1.0, 1.0)
    rows = np.arange(L // GRID_W)[:, None] * fl[None]
    cols = np.arange(GRID_W)[:, None] * fl[None]
    rc = np.where(is_row, np.cos(rows), 0.0)
    rs = np.where(is_row, np.sin(rows) * sign, 0.0)
    cc = np.where(~is_row, np.cos(cols), 0.0)
    cs = np.where(~is_row, np.sin(cols) * sign, 0.0)
    return [jnp.asarray(t, f32) for t in (rc, rs, cc, cs)]


def _in_proj(x, mod3, norm1_g, w_t, gq_t, gk_t, bd, w_a, b_a, cast_ws):
    B, L, _ = x.shape
    tm = TM_PROJ
    rows_per = tm // GRID_W
    nstep = L // tm
    total = B * nstep

    def slab(w):
        R, C = w.shape
        if R % (total * BF16_SUBLANES) == 0:
            return pl.BlockSpec((R // total, C), lambda b, i: (b * nstep + i, 0))
        assert (2 * R) % (total * BF16_SUBLANES) == 0 and C % (2 * LANES) == 0
        return pl.BlockSpec((2 * R // total, C // 2), lambda b, i: ((b * nstep + i) // 2, (b * nstep + i) % 2))
    rc, rs, cc, cs = _rope_tables(L)
    blk = np.kron(np.eye(MXU_DIM // CHUNK), np.tril(np.ones((CHUNK, CHUNK)))).astype(np.float32)
    tl = jnp.asarray(blk, bf16)
    tok = lambda n: pl.BlockSpec((1, tm, n), lambda b, i: (b, i, 0))
    sds = lambda n, dt: jax.ShapeDtypeStruct((B, L, n), dt)
    return pl.pallas_call(
        _proj_kernel,
        grid=(B, L // tm),
        in_specs=[tok(D),
                  pl.BlockSpec((1, 1, D), lambda b, i: (b, 0, 0)),
                  pl.BlockSpec((1, 1, D), lambda b, i: (b, 0, 1)),
                  _const_spec((1, D)), _const_spec((D_IN, D)),
                  _const_spec((1, NA_W)), _const_spec((1, NA_W)), _const_spec((MXU_DIM, MXU_DIM)),
                  pl.BlockSpec((rows_per, GQK), lambda b, i: (i, 0)),
                  pl.BlockSpec((rows_per, GQK), lambda b, i: (i, 0)),
                  _const_spec((GRID_W, GQK)), _const_spec((GRID_W, GQK)),
                  _const_spec((N_LR, 2 * GQK)), _const_spec((1, 2 * GQK)), _const_spec((MXU_DIM, MXU_DIM))]
                 + [slab(w) for w in cast_ws],
        out_specs=[tok(W_NA), tok(W_GF), tok(W_GB), tok(W_MG),
                   pl.BlockSpec((1, tm // CHUNK, 2 * GQK), lambda b, i: (b, i, 0)),
                   pl.BlockSpec((1, GQK, tm), lambda b, i: (b, 0, i)),
                   pl.BlockSpec((1, GQK, tm), lambda b, i: (b, 0, i))]
                  + [slab(w) for w in cast_ws],
        out_shape=[sds(W_NA, bf16), sds(W_GF, bf16), sds(W_GB, bf16), sds(W_MG, bf16),
                   jax.ShapeDtypeStruct((B, L // CHUNK, 2 * GQK), f32),
                   jax.ShapeDtypeStruct((B, GQK, L), bf16), jax.ShapeDtypeStruct((B, GQK, L), bf16)]
                  + [jax.ShapeDtypeStruct(w.shape, bf16) for w in cast_ws],
        compiler_params=pltpu.CompilerParams(
            dimension_semantics=("parallel", "parallel"), vmem_limit_bytes=VMEM_LIMIT),
        name="in_proj",
    )(x, mod3, mod3, norm1_g, w_t, gq_t, gk_t, bd, rc, rs, cc, cs, w_a, b_a, tl, *cast_ws)


def _lane_blocks(t):
    return [t[:, c:c + LANES] for c in range(0, t.shape[1], LANES)]


def _na_kernel(q_ref, k_ref, v_ref, kc_ref, vc_ref, t2_ref, o_ref, sl_ref, sc_ref, m_ref):
    i = pl.program_id(1)
    rows = k_ref.shape[1] // GRID_W
    lane_head = lax.broadcasted_iota(jnp.int32, (GRID_W, MXU_DIM), 1) // NA_DH
    hmask = [lane_head == h for h in range(NA_GH)]

    def window_start(r):
        rs = jnp.clip(r - NA_WR // 2, 0, rows - NA_WR)
        return rs, pl.multiple_of(rs * GRID_W, GRID_W)

    def produce(rr, bank, j):
        r = i * NA_R + rr
        rs, start = window_start(r)
        d0 = rs - r + (NA_WR - 1)
        row0 = pl.multiple_of(rr * GRID_W, GRID_W)
        for g in range(NA_NG):
            ls = slice(MXU_DIM * g, MXU_DIM * (g + 1))
            q = q_ref[0, pl.ds(row0, GRID_W), ls]
            qs = jnp.concatenate([jnp.where(m, q, jnp.zeros_like(q)) for m in hmask], axis=0)
            kw = k_ref[0, pl.ds(start, NA_WR * GRID_W), ls]
            bias = jnp.concatenate(
                [jnp.concatenate([t2_ref[NA_GH * g + h, d0 + 2 * jj] for jj in range(NA_WR // 2)], axis=1)
                 for h in range(NA_GH)], axis=0)
            s_loc = _dot_nt(qs, kw) + bias
            s_ctx = _dot_nt(qs, kc_ref[0, :, ls])
            mx = functools.reduce(jnp.maximum, _lane_blocks(s_loc) + _lane_blocks(s_ctx))
            sl_ref[bank, j, g] = s_loc
            sc_ref[bank, j, g] = s_ctx
            m_ref[bank, j, g] = jnp.broadcast_to(jnp.max(mx, axis=-1, keepdims=True), (NA_GH * GRID_W, LANES))

    def consume(rr, bank, j):
        _, start = window_start(i * NA_R + rr)
        row0 = pl.multiple_of(rr * GRID_W, GRID_W)
        outs = []
        for g in range(NA_NG):
            ls = slice(MXU_DIM * g, MXU_DIM * (g + 1))
            m = m_ref[bank, j, g]
            p_loc = [jnp.exp2(sl_ref[bank, j, g, :, c:c + LANES] - m) for c in range(0, NA_WR * GRID_W, LANES)]
            p_ctx = [jnp.exp2(sc_ref[bank, j, g, :, c:c + LANES] - m) for c in range(0, CTX, LANES)]
            den = jnp.sum(functools.reduce(jnp.add, p_loc + p_ctx), axis=-1, keepdims=True)
            vw = v_ref[0, pl.ds(start, NA_WR * GRID_W), ls]
            acc = (_dot(jnp.concatenate(p_loc, axis=1).astype(bf16), vw)
                   + _dot(jnp.concatenate(p_ctx, axis=1).astype(bf16), vc_ref[0, :, ls]))
            acc = acc * (1.0 / den)
            o = jnp.zeros((GRID_W, MXU_DIM), f32)
            for h in range(NA_GH):
                o = o + jnp.where(hmask[h], acc[h * GRID_W:(h + 1) * GRID_W], 0.0)
            outs.append(o)
        o_ref[0, pl.ds(row0, GRID_W), :] = jnp.concatenate(outs, axis=1).astype(bf16)

    for j in range(NA_RI):
        produce(jnp.int32(j), 0, j)

    def step(t, carry):
        bank = t & 1
        for j in range(NA_RI):
            consume(t * NA_RI + j, bank, j)
        for j in range(NA_RI):
            produce(jnp.minimum((t + 1) * NA_RI + j, NA_R - 1), 1 - bank, j)
        return carry

    lax.fori_loop(0, NA_R // NA_RI, step, 0)


def _na_attn(qkv, kc, vc, t2):
    B, L, _ = qkv.shape
    nblk = L // (NA_R * GRID_W)
    return pl.pallas_call(
        _na_kernel,
        grid=(B, nblk),
        in_specs=[pl.BlockSpec((1, NA_R * GRID_W, NA_W), lambda b, i: (b, i, NA_Q // NA_W)),
                  pl.BlockSpec((1, L, NA_W), lambda b, i: (b, 0, NA_K // NA_W), pipeline_mode=pl.Buffered(1)),
                  pl.BlockSpec((1, L, NA_W), lambda b, i: (b, 0, NA_V // NA_W), pipeline_mode=pl.Buffered(1)),
                  pl.BlockSpec((1, CTX, NA_W), lambda b, i: (b, 0, 0)),
                  pl.BlockSpec((1, CTX, NA_W), lambda b, i: (b, 0, 0)),
                  _const_spec((NA_H, 2 * NA_WR - 1, GRID_W, LANES))],
        out_specs=pl.BlockSpec((1, NA_R * GRID_W, NA_W), lambda b, i: (b, i, 0)),
        out_shape=jax.ShapeDtypeStruct((B, L, NA_W), bf16),
        scratch_shapes=[pltpu.VMEM((2, NA_RI, NA_NG, NA_GH * GRID_W, NA_WR * GRID_W), f32),
                        pltpu.VMEM((2, NA_RI, NA_NG, NA_GH * GRID_W, CTX), f32),
                        pltpu.VMEM((2, NA_RI, NA_NG, NA_GH * GRID_W, LANES), f32)],
        compiler_params=pltpu.CompilerParams(
            dimension_semantics=("parallel", "parallel"), vmem_limit_bytes=VMEM_LIMIT),
        name="na_attn",
    )(qkv, qkv, qkv, kc, vc, t2)


def _gla_chunk(q, ki, v, ket, dcol, s_ref, tri, hmask):
    qs = jnp.concatenate([jnp.where(m, q, jnp.zeros_like(q)) for m in hmask], axis=0)
    att = jnp.where(tri, _dot_nt(qs, ki), 0.0).astype(bf16)
    o = _dot(q, s_ref[...].astype(bf16))
    intra = [_dot(att[h * CHUNK:(h + 1) * CHUNK], v[:, h * GLA_DV:(h + 1) * GLA_DV]) for h in range(GLA_H)]
    o = o + jnp.concatenate(intra, axis=1)
    for h in range(GLA_H):
        rs = slice(h * GLA_DK, (h + 1) * GLA_DK)
        cs = slice(h * GLA_DV, (h + 1) * GLA_DV)
        s_ref[rs, cs] = dcol[rs] * s_ref[rs, cs] + _dot(ket[rs], v[:, cs])
    return o


def _gla_kernel(gf_ref, ktf_ref, decf_ref, gb_ref, vb_ref, ktb_ref, decb_ref,
                s0f_ref, s0b_ref, of_ref, ob_ref, sf_ref, sb_ref):
    @pl.when(pl.program_id(0) == 0)
    def _():
        sf_ref[...] = s0f_ref[...]
        sb_ref[...] = s0b_ref[...]

    lane_head = lax.broadcasted_iota(jnp.int32, (CHUNK, GQK), 1) // GLA_DK
    hmask = [lane_head == h for h in range(GLA_H)]
    ri = lax.broadcasted_iota(jnp.int32, (GLA_H * CHUNK, CHUNK), 0) & (CHUNK - 1)
    ci = lax.broadcasted_iota(jnp.int32, (GLA_H * CHUNK, CHUNK), 1)
    tri_f = ri >= ci
    tri_b = ri <= ci
    nbatch = gf_ref.shape[0]
    dect_f = [decf_ref[b][:, :GQK].T for b in range(nbatch)]
    dect_b = [decb_ref[b][:, GQK:].T for b in range(nbatch)]

    def k_end(kt_ref, b, c):
        kt = kt_ref[b, :, (c // 2) * 2 * CHUNK:(c // 2 + 1) * 2 * CHUNK]
        if c % 2:
            kt = pltpu.roll(kt, CHUNK, 1)
        return kt[:, :CHUNK]

    for c in range(GLA_CB):
        ts = slice(c * CHUNK, (c + 1) * CHUNK)
        cb = GLA_CB - 1 - c
        tb = slice(cb * CHUNK, (cb + 1) * CHUNK)
        for b in range(nbatch):
            of_ref[b, ts, :] = _gla_chunk(gf_ref[b, ts, G_Q:G_Q + GQK], gf_ref[b, ts, G_KI:G_KI + GQK],
                                          gf_ref[b, ts, G_V:G_V + GV], k_end(ktf_ref, b, c),
                                          dect_f[b][:, c:c + 1], sf_ref.at[b], tri_f, hmask)
            ob_ref[b, tb, :] = _gla_chunk(gb_ref[b, tb, G_Q - GV:G_Q - GV + GQK],
                                          gb_ref[b, tb, G_KI - GV:G_KI - GV + GQK],
                                          vb_ref[b, tb, :], k_end(ktb_ref, b, cb),
                                          dect_b[b][:, cb:cb + 1], sb_ref.at[b], tri_b, hmask)


def _gla_scan(gf, gb, ktf, ktb, dec, s0f, s0b):
    B, L, _ = gf.shape
    tm = GLA_CB * CHUNK
    nb = L // tm
    fw = lambda n: pl.BlockSpec((B, tm, n), lambda i: (0, i, 0))
    bw = lambda n: pl.BlockSpec((B, tm, n), lambda i: (0, nb - 1 - i, 0))
    st = pl.BlockSpec((B, GQK, GV), lambda i: (0, 0, 0))
    return pl.pallas_call(
        _gla_kernel,
        grid=(nb,),
        in_specs=[fw(W_GF),
                  pl.BlockSpec((B, GQK, tm), lambda i: (0, 0, i)),
                  pl.BlockSpec((B, GLA_CB, 2 * GQK), lambda i: (0, i, 0)),
                  bw(W_GB), bw(GV),
                  pl.BlockSpec((B, GQK, tm), lambda i: (0, 0, nb - 1 - i)),
                  pl.BlockSpec((B, GLA_CB, 2 * GQK), lambda i: (0, nb - 1 - i, 0)),
                  st, st],
        out_specs=[fw(GV), bw(GV)],
        out_shape=[jax.ShapeDtypeStruct((B, L, GV), f32)] * 2,
        scratch_shapes=[pltpu.VMEM((B, GQK, GV), f32), pltpu.VMEM((B, GQK, GV), f32)],
        compiler_params=pltpu.CompilerParams(
            dimension_semantics=("arbitrary",), vmem_limit_bytes=VMEM_LIMIT),
        name="gla_scan",
    )(gf, ktf, dec, gb, gf, ktb, dec, s0f, s0b)


def _merge_ffn_kernel(x_ref, ona_ref, of_ref, ob_ref, mg_ref,
                      g1_ref, sh2_ref, sc2_ref, g2_ref, n2_ref, gg_ref,
                      wna_ref, wgl_ref, wo_ref, wg_ref, wu_ref, wd_ref, o_ref):
    x1s, h2s = [], []
    for hs in range(0, x_ref.shape[1], FFN_SUB):
        rows = slice(hs, hs + FFN_SUB)
        og = of_ref[0, rows] + ob_ref[0, rows]
        parts = []
        for h in range(GLA_H):
            seg = og[:, h * GLA_DV:(h + 1) * GLA_DV]
            ms = jnp.mean(seg * seg, axis=-1, keepdims=True)
            parts.append(seg * lax.rsqrt(ms + EPS) * gg_ref[...])
        ogl = (jnp.concatenate(parts, axis=1) * mg_ref[0, rows, MG_OG:MG_OG + GV].astype(f32)).astype(bf16)
        y = (mg_ref[0, rows, MG_A:MG_A + D].astype(f32) * _dot(ona_ref[0, rows], wna_ref[...])
             + mg_ref[0, rows, MG_B:MG_B + D].astype(f32) * _dot(ogl, wgl_ref[...]))
        x1 = x_ref[0, rows] + g1_ref[0] * _dot(y.astype(bf16), wo_ref[...])
        x1s.append(x1)
        h2s.append(_norm_mod(x1, n2_ref[...], sh2_ref[0], sc2_ref[0]).astype(bf16))
    x1 = jnp.concatenate(x1s, axis=0)
    h2 = jnp.concatenate(h2s, axis=0)
    act = (_silu(_dot(h2, wg_ref[...])) * _dot(h2, wu_ref[...])).astype(bf16)
    o_ref[0] = x1 + g2_ref[0] * _dot(act, wd_ref[...])


def _merge_ffn(x, ona, of, ob, mg, mod3, norm2_g, gg, wna, wgl, wo, wg, wu, wd):
    B, L, _ = x.shape
    tm = TM_FFN
    tok = lambda n: pl.BlockSpec((1, tm, n), lambda b, i: (b, i, 0))
    modv = lambda j: pl.BlockSpec((1, 1, D), lambda b, i: (b, 0, j))
    return pl.pallas_call(
        _merge_ffn_kernel,
        grid=(B, L // tm),
        in_specs=[tok(D), tok(NA_W), tok(GV), tok(GV), tok(W_MG),
                  modv(2), modv(3), modv(4), modv(5),
                  _const_spec((1, D)), _const_spec((1, GLA_DV)),
                  _const_spec((NA_W, D)), _const_spec((GV, D)), _const_spec((D, D)),
                  _const_spec((D, D_FF)), _const_spec((D, D_FF)), _const_spec((D_FF, D))],
        out_specs=tok(D),
        out_shape=jax.ShapeDtypeStruct((B, L, D), f32),
        compiler_params=pltpu.CompilerParams(
            dimension_semantics=("parallel", "parallel"), vmem_limit_bytes=VMEM_LIMIT),
        name="merge_ffn",
    )(x, ona, of, ob, mg, mod3, mod3, mod3, mod3, norm2_g, gg, wna, wgl, wo, wg, wu, wd)


def kernel(x, c, ctx, c_ctx, w_mod, b_mod, norm1_g, norm2_g, w_in, na_q_norm_g, na_k_norm_g, na_rpb,
           gla_w_alpha, gla_b_alpha, gla_norm_g, w_branch_na, w_branch_gla, w_out,
           w_ffn_gate, w_ffn_up, w_ffn_down):
    B, L, _ = x.shape
    assert w_mod.shape[0] == 1 and L % (NA_R * GRID_W) == 0 and L % TM_PROJ == 0 and D_IN % W_IN_CAST_ROWS == 0

    w_t = _cast_bf16(jnp.transpose(w_in[0]), W_IN_CAST_ROWS)
    zero = jnp.zeros((GLA_RANK, GQK), f32)
    w_a = jnp.concatenate([jnp.concatenate([gla_w_alpha[0, 0], zero], axis=1),
                           jnp.concatenate([zero, gla_w_alpha[0, 1]], axis=1)], axis=0).astype(bf16)
    b_a = gla_b_alpha[0].reshape(1, 2 * GQK)
    gq_t = jnp.tile(na_q_norm_g[0] * (NA_DH ** -0.5 * LOG2E), NA_H).reshape(1, NA_W)
    gk_t = jnp.tile(na_k_norm_g[0], NA_H).reshape(1, NA_W)
    bd = jnp.asarray(np.kron(np.eye(4), np.full((NA_DH, NA_DH), 1.0 / NA_DH)), bf16)

    mod3 = _adaln_mod(c, c_ctx, w_mod[0], b_mod[0])
    t2 = _rpb_expand(na_rpb[0])

    kc, vc, s0f, s0b = _ctx_side(ctx, mod3, norm1_g, w_t, w_a, b_a, gk_t, bd, B)
    later_ws = [w_branch_na[0], w_branch_gla[0], w_out[0], w_ffn_gate[0], w_ffn_up[0], w_ffn_down[0]]
    na_qkv, gla_f, gla_b, gates, dec, ktf, ktb, *later_bf = _in_proj(
        x, mod3, norm1_g, w_t, gq_t, gk_t, bd, w_a, b_a, later_ws)
    ona = _na_attn(na_qkv, kc, vc, t2)
    of, ob = _gla_scan(gla_f, gla_b, ktf, ktb, dec, s0f, s0b)
    return _merge_ffn(x, ona, of, ob, gates, mod3, norm2_g, gla_norm_g, *later_bf)
```

```python
import functools

import numpy as np
import jax
import jax.numpy as jnp
from jax import lax
from jax.experimental import pallas as pl
from jax.experimental.pallas import tpu as pltpu

D = 1024
GRID_W = 64
CTX = 256
NA_H = 8
NA_DH = 64
NA_WR = 8
NA_WC = 16
GLA_H = 4
GLA_DK = 64
GLA_DV = 128
GLA_RANK = 16
GLA_TAU = 16.0
GLA_LA_MIN = -1.0
CHUNK = 64
ROPE_THETA = 10000.0
D_FF = 2816
EPS = 1e-6
NEG_INF = -1e30
NA_W = NA_H * NA_DH
GQK = GLA_H * GLA_DK
GV = GLA_H * GLA_DV

OFF_QNA, OFF_KNA, OFF_VNA = 0, 512, 1024
OFF_QG, OFF_KG, OFF_VG, OFF_OG = 1536, 1792, 2048, 2560
OFF_LR, OFF_MA, OFF_MB = 3072, 3104, 4128
N_LR = 2 * GLA_RANK
N_PROJ_OUT = 7
NA_Q, NA_K, NA_V, W_NA = 0, 512, 1024, 1536
G_V, G_Q, G_KI, W_GF = 0, 512, 768, 1024
W_GB = W_GF - GV
MG_A, MG_B, MG_OG, W_MG = 0, 1024, 2048, 2560
D_IN = 5152
LOG2E = 1.4426950408889634

LANES = 128
SUBLANES = 8
BF16_SUBLANES = 16
MXU_DIM = 256
NA_GH = MXU_DIM // NA_DH
NA_NG = NA_H // NA_GH
VMEM_LIMIT = 56 * 1024 * 1024

W_IN_CAST_ROWS = 46 * BF16_SUBLANES
TM_PROJ = 512
PROJ_SUB = 256
TM_FFN = 512
FFN_SUB = 256
NA_R = 64
NA_RI = 4
GLA_CB = 16
GLA_HG = 2

f32 = jnp.float32
bf16 = jnp.bfloat16


def _const_spec(shape):
    nd = len(shape)
    return pl.BlockSpec(shape, lambda *_: (0,) * nd, pipeline_mode=pl.Buffered(1))


def _dot(a, b):
    return jnp.dot(a, b, preferred_element_type=f32)


def _dot_nt(a, b):
    return lax.dot_general(a, b, (((1,), (1,)), ((), ())), preferred_element_type=f32)


def _split2(x):
    hi = x.astype(bf16)
    lo = (x - hi.astype(f32)).astype(bf16)
    return hi, lo


def _sigmoid(x):
    return 0.5 * jnp.tanh(0.5 * x) + 0.5


def _silu(x):
    return x * _sigmoid(x)


def _split3_dot(r, oh):
    h1 = r.astype(bf16)
    r2 = r - h1.astype(f32)
    h2 = r2.astype(bf16)
    h3 = (r2 - h2.astype(f32)).astype(bf16)
    return _dot(h1, oh) + _dot(h2, oh) + _dot(h3, oh)


def _rpb_expand_kernel(r_ref, oh_ref, mask_ref, o_ref):
    tiles = _split3_dot(r_ref[...], oh_ref[...]) * LOG2E + mask_ref[...]
    for q in range(GRID_W):
        o_ref[:, q, :] = tiles[:o_ref.shape[0], q * LANES:(q + 1) * LANES]


def _rpb_expand(rpb):
    nd = 2 * NA_WR - 1
    nrow = NA_H * nd
    ncol = 2 * NA_WC - 1
    nxt = jnp.concatenate([rpb[:, 1:], jnp.zeros_like(rpb[:, :1])], axis=1)
    pad = ((0, LANES - nrow), (0, LANES - ncol))
    r01 = jnp.concatenate([jnp.pad(rpb.reshape(nrow, ncol), pad), jnp.pad(nxt.reshape(nrow, ncol), pad)], axis=1)
    q = np.arange(GRID_W)
    cs = np.clip(q - NA_WC // 2, 0, GRID_W - NA_WC)
    in_win = (q[None, :] >= cs[:, None]) & (q[None, :] < cs[:, None] + NA_WC)
    dc = np.clip(q[None, :] - q[:, None], -(NA_WC - 1), NA_WC - 1) + NA_WC - 1
    oh = np.zeros((2, LANES, GRID_W, LANES), np.float32)
    qq, kk = np.nonzero(in_win)
    oh[0, dc[qq, kk], qq, kk] = 1.0
    oh[1, dc[qq, kk], qq, kk + GRID_W] = 1.0
    oh = oh.reshape(2 * LANES, GRID_W * LANES)
    mask = np.where(np.concatenate([in_win, in_win], axis=1), 0.0, NEG_INF).reshape(1, GRID_W * LANES)
    out = pl.pallas_call(
        _rpb_expand_kernel,
        out_shape=jax.ShapeDtypeStruct((nrow, GRID_W, LANES), f32),
        name="rpb_expand",
    )(r01, jnp.asarray(oh, bf16), jnp.asarray(mask, f32))
    return out.reshape(NA_H, nd, GRID_W, LANES)


def _cast_kernel(x_ref, o_ref):
    o_ref[...] = x_ref[...].astype(bf16)


def _cast_bf16(w, rows):
    R, C = w.shape
    return pl.pallas_call(
        _cast_kernel,
        grid=(R // rows,),
        in_specs=[pl.BlockSpec((rows, C), lambda j: (j, 0))],
        out_specs=pl.BlockSpec((rows, C), lambda j: (j, 0)),
        out_shape=jax.ShapeDtypeStruct((R, C), bf16),
        name="cast_bf16",
    )(w)


def _mod_kernel(a_ref, w_ref, b_ref, o_ref):
    a = _silu(a_ref[...]).astype(bf16)
    o_ref[...] = (_dot(a, w_ref[...].astype(bf16)) + b_ref[...])[:, None, :]


def _adaln_mod(c, c_ctx, w_mod, b_mod):
    B = c.shape[0]
    assert B < SUBLANES
    a = jnp.concatenate([c, c_ctx[None], jnp.zeros((SUBLANES - B - 1, D), f32)], axis=0)
    n = w_mod.shape[1]
    return pl.pallas_call(
        _mod_kernel,
        grid=(n // D,),
        in_specs=[pl.BlockSpec((SUBLANES, D), lambda j: (0, 0)),
                  pl.BlockSpec((D, D), lambda j: (0, j)),
                  pl.BlockSpec((1, D), lambda j: (0, j))],
        out_specs=pl.BlockSpec((SUBLANES, 1, D), lambda j: (0, 0, j)),
        out_shape=jax.ShapeDtypeStruct((SUBLANES, 1, n), f32),
        name="adaln_mod",
    )(a, w_mod, b_mod.reshape(1, n))


def _norm_mod(x, g, sh, sc):
    ms = jnp.mean(x * x, axis=-1, keepdims=True)
    return (x * lax.rsqrt(ms + EPS)) * (g * (1.0 + sc)) + sh


def _head_rms(t, bd, g):
    sq = (t * t).astype(bf16)
    ms = jnp.concatenate([_dot(sq[:, :MXU_DIM], bd), _dot(sq[:, MXU_DIM:], bd)], axis=1)
    return t * lax.rsqrt(ms + EPS) * g


def _log_alpha(zl, wa_ref, ba_ref):
    z = _dot(zl.astype(bf16), wa_ref[...]) + ba_ref[...]
    soft = jnp.log2(1.0 + jnp.exp2(z * -LOG2E))
    return jnp.maximum(soft * (-1.0 / GLA_TAU), GLA_LA_MIN * LOG2E)


def _ctx_kernel(ctx_ref, sh_ref, sc_ref, g1_ref, wk_ref, wv_ref, wkg_ref, wvg_ref, wlr_ref,
                wa_ref, ba_ref, gk_ref, bd_ref, su_ref, sl_ref,
                kc_ref, vc_ref, sf_ref, sb_ref):
    h = _norm_mod(ctx_ref[0], g1_ref[...], sh_ref[0], sc_ref[0]).astype(bf16)
    k = _dot_nt(h, wk_ref[...])
    kc_ref[0] = _head_rms(k, bd_ref[...], gk_ref[...]).astype(bf16)
    vc_ref[0] = _dot_nt(h, wv_ref[...]).astype(bf16)
    kg = _dot_nt(h, wkg_ref[...])
    vg = _dot_nt(h, wvg_ref[...]).astype(bf16)
    la = _log_alpha(_dot_nt(h, wlr_ref[...]), wa_ref, ba_ref)
    hi, lo = _split2(la)
    ef = _dot(su_ref[...], hi[:, :GQK]) + _dot(su_ref[...], lo[:, :GQK])
    eb = _dot(sl_ref[...], hi[:, GQK:]) + _dot(sl_ref[...], lo[:, GQK:])
    r = lax.broadcasted_iota(jnp.int32, (GQK, GV), 0) // GLA_DK
    cc = lax.broadcasted_iota(jnp.int32, (GQK, GV), 1) // GLA_DV
    diag = r == cc
    for e, out in ((ef, sf_ref), (eb, sb_ref)):
        kw = (kg * jnp.exp2(e)).astype(bf16)
        u = _dot(kw.T, vg)
        out[0] = jnp.where(diag, u, 0.0)


def _ctx_side(ctx, mod3, norm1_g, w_t, w_a, b_a, gk_t, bd, n_b):
    su = jnp.asarray(np.triu(np.ones((CTX, CTX), np.float32), 1), bf16)
    sl = jnp.asarray(np.tril(np.ones((CTX, CTX), np.float32), -1), bf16)
    B = ctx.shape[0]
    cst = lambda shape, idx: pl.BlockSpec(shape, lambda b: idx)
    return pl.pallas_call(
        _ctx_kernel,
        grid=(B,),
        in_specs=[pl.BlockSpec((1, CTX, D), lambda b: (b, 0, 0)),
                  cst((1, 1, D), (n_b, 0, 0)), cst((1, 1, D), (n_b, 0, 1)),
                  cst((1, D), (0, 0)),
                  cst((NA_W, D), (OFF_KNA // NA_W, 0)), cst((NA_W, D), (OFF_VNA // NA_W, 0)),
                  cst((GQK, D), (OFF_KG // GQK, 0)), cst((GV, D), (OFF_VG // GV, 0)),
                  cst((N_LR, D), (OFF_LR // N_LR, 0)),
                  cst((N_LR, 2 * GQK), (0, 0)), cst((1, 2 * GQK), (0, 0)),
                  cst((1, NA_W), (0, 0)), cst((MXU_DIM, MXU_DIM), (0, 0)),
                  cst((CTX, CTX), (0, 0)), cst((CTX, CTX), (0, 0))],
        out_specs=[pl.BlockSpec((1, CTX, NA_W), lambda b: (b, 0, 0)),
                   pl.BlockSpec((1, CTX, NA_W), lambda b: (b, 0, 0)),
                   pl.BlockSpec((1, GQK, GV), lambda b: (b, 0, 0)),
                   pl.BlockSpec((1, GQK, GV), lambda b: (b, 0, 0))],
        out_shape=[jax.ShapeDtypeStruct((B, CTX, NA_W), bf16),
                   jax.ShapeDtypeStruct((B, CTX, NA_W), bf16),
                   jax.ShapeDtypeStruct((B, GQK, GV), f32),
                   jax.ShapeDtypeStruct((B, GQK, GV), f32)],
        compiler_params=pltpu.CompilerParams(vmem_limit_bytes=VMEM_LIMIT),
        name="ctx_side",
    )(ctx, mod3, mod3, norm1_g, w_t, w_t, w_t, w_t, w_t, w_a, b_a, gk_t, bd, su, sl)


def _proj_kernel(x_ref, sh_ref, sc_ref, g1_ref, w_ref, gq_ref, gk_ref, bd_ref,
                 rc_ref, rs_ref, cc_ref, cs_ref, wa_ref, ba_ref, tl_ref, *refs):
    n_cast = (len(refs) - N_PROJ_OUT) // 2
    cast_in, refs = refs[:n_cast], refs[n_cast:]
    na_ref, gf_ref, gb_ref, mg_ref, dec_ref, ktf_ref, ktb_ref = refs[:N_PROJ_OUT]
    cast_out = refs[N_PROJ_OUT:]
    for src, dst in zip(cast_in, cast_out):
        dst[...] = src[...].astype(bf16)

    bd = bd_ref[...]
    tl = tl_ref[...]
    nrow = PROJ_SUB // GRID_W
    nchunk = PROJ_SUB // CHUNK
    lane = lax.broadcasted_iota(jnp.int32, (PROJ_SUB, GQK), 1)
    first_half = (lane & 31) < 16

    for hs in range(0, x_ref.shape[1], PROJ_SUB):
        rows = slice(hs, hs + PROJ_SUB)
        hb = _norm_mod(x_ref[0, rows], g1_ref[...], sh_ref[0], sc_ref[0]).astype(bf16)

        def proj(off, n):
            return _dot_nt(hb, w_ref[off:off + n, :])

        zl = proj(OFF_LR, N_LR)
        q_raw = proj(OFF_QNA, NA_W)
        la = _log_alpha(zl, wa_ref, ba_ref)
        k_raw = proj(OFF_KNA, NA_W)
        hi, lo = _split2(la)
        na_ref[0, rows, NA_Q:NA_Q + NA_W] = _head_rms(q_raw, bd, gq_ref[...]).astype(bf16)
        cum = jnp.concatenate(
            [_dot(tl, hi[s:s + MXU_DIM]) + _dot(tl, lo[s:s + MXU_DIM]) for s in range(0, PROJ_SUB, MXU_DIM)], axis=0)
        na_ref[0, rows, NA_K:NA_K + NA_W] = _head_rms(k_raw, bd, gk_ref[...]).astype(bf16)
        qg_raw = proj(OFF_QG, GQK)
        kg_raw = proj(OFF_KG, GQK)
        na_ref[0, rows, NA_V:NA_V + NA_W] = proj(OFF_VNA, NA_W).astype(bf16)
        gf_ref[0, rows, G_V:G_V + GV] = proj(OFF_VG, GV).astype(bf16)

        gr = slice(hs // GRID_W, hs // GRID_W + nrow)
        cos_t = (rc_ref[gr][:, None, :] + cc_ref[...][None]).reshape(PROJ_SUB, GQK)
        sin_t = (rs_ref[gr][:, None, :] + cs_ref[...][None]).reshape(PROJ_SUB, GQK)

        def rope(t):
            partner = jnp.where(first_half, pltpu.roll(t, GQK - 16, 1), pltpu.roll(t, 16, 1))
            return t * cos_t + partner * sin_t

        qg = rope(qg_raw) * (GLA_DK ** -0.5)
        kg = rope(kg_raw)

        cum3 = cum.reshape(nchunk, CHUNK, 2 * GQK)
        tot3 = cum3[:, CHUNK - 1:CHUNK, :]
        la3 = la.reshape(nchunk, CHUNK, 2 * GQK)
        dec_ref[0, hs // CHUNK:hs // CHUNK + nchunk] = jnp.exp2(tot3).reshape(nchunk, 2 * GQK)
        b_f = cum3[:, :, :GQK].reshape(PROJ_SUB, GQK)
        e_f = (tot3 - cum3)[:, :, :GQK].reshape(PROJ_SUB, GQK)
        b_b = (tot3 - cum3 + la3)[:, :, GQK:].reshape(PROJ_SUB, GQK)
        e_b = (cum3 - la3)[:, :, GQK:].reshape(PROJ_SUB, GQK)
        mg_ref[0, rows, MG_OG:MG_OG + GV] = _silu(proj(OFF_OG, GV)).astype(bf16)
        gf_ref[0, rows, G_Q:G_Q + GQK] = (qg * jnp.exp2(b_f)).astype(bf16)
        gf_ref[0, rows, G_KI:G_KI + GQK] = (kg * jnp.exp2(-b_f)).astype(bf16)
        ktf_ref[0, :, rows] = (kg * jnp.exp2(e_f)).T.astype(bf16)
        mg_ref[0, rows, MG_A:MG_A + D] = _sigmoid(proj(OFF_MA, D)).astype(bf16)
        gb_ref[0, rows, G_Q - GV:G_Q - GV + GQK] = (qg * jnp.exp2(b_b)).astype(bf16)
        gb_ref[0, rows, G_KI - GV:G_KI - GV + GQK] = (kg * jnp.exp2(-b_b)).astype(bf16)
        ktb_ref[0, :, rows] = (kg * jnp.exp2(e_b)).T.astype(bf16)
        mg_ref[0, rows, MG_B:MG_B + D] = _sigmoid(proj(OFF_MB, D)).astype(bf16)


def _rope_tables(L):
    n = GLA_DK // 4
    freqs = ROPE_THETA ** (-np.arange(n, dtype=np.float64) / n)
    lane = np.arange(GQK)
    fl = freqs[lane % n]
    is_row = (lane % GLA_DK) < GLA_DK // 2
    sign = np.where((lane % 32) < 16, -1.0, 1.0)
    rows = np.arange(L // GRID_W)[:, None] * fl[None]
    cols = np.arange(GRID_W)[:, None] * fl[None]
    rc = np.where(is_row, np.cos(rows), 0.0)
    rs = np.where(is_row, np.sin(rows) * sign, 0.0)
    cc = np.where(~is_row, np.cos(cols), 0.0)
    cs = np.where(~is_row, np.sin(cols) * sign, 0.0)
    return [jnp.asarray(t, f32) for t in (rc, rs, cc, cs)]


def _in_proj(x, mod3, norm1_g, w_t, gq_t, gk_t, bd, w_a, b_a, cast_ws):
    B, L, _ = x.shape
    tm = TM_PROJ
    rows_per = tm // GRID_W
    nstep = L // tm
    total = B * nstep

    def slab(w):
        R, C = w.shape
        if R % (total * BF16_SUBLANES) == 0:
            return pl.BlockSpec((R // total, C), lambda b, i: (b * nstep + i, 0))
        assert (2 * R) % (total * BF16_SUBLANES) == 0 and C % (2 * LANES) == 0
        return pl.BlockSpec((2 * R // total, C // 2), lambda b, i: ((b * nstep + i) // 2, (b * nstep + i) % 2))
    rc, rs, cc, cs = _rope_tables(L)
    blk = np.kron(np.eye(MXU_DIM // CHUNK), np.tril(np.ones((CHUNK, CHUNK)))).astype(np.float32)
    tl = jnp.asarray(blk, bf16)
    tok = lambda n: pl.BlockSpec((1, tm, n), lambda b, i: (b, i, 0))
    sds = lambda n, dt: jax.ShapeDtypeStruct((B, L, n), dt)
    return pl.pallas_call(
        _proj_kernel,
        grid=(B, L // tm),
        in_specs=[tok(D),
                  pl.BlockSpec((1, 1, D), lambda b, i: (b, 0, 0)),
                  pl.BlockSpec((1, 1, D), lambda b, i: (b, 0, 1)),
                  _const_spec((1, D)), _const_spec((D_IN, D)),
                  _const_spec((1, NA_W)), _const_spec((1, NA_W)), _const_spec((MXU_DIM, MXU_DIM)),
                  pl.BlockSpec((rows_per, GQK), lambda b, i: (i, 0)),
                  pl.BlockSpec((rows_per, GQK), lambda b, i: (i, 0)),
                  _const_spec((GRID_W, GQK)), _const_spec((GRID_W, GQK)),
                  _const_spec((N_LR, 2 * GQK)), _const_spec((1, 2 * GQK)), _const_spec((MXU_DIM, MXU_DIM))]
                 + [slab(w) for w in cast_ws],
        out_specs=[tok(W_NA), tok(W_GF), tok(W_GB), tok(W_MG),
                   pl.BlockSpec((1, tm // CHUNK, 2 * GQK), lambda b, i: (b, i, 0)),
                   pl.BlockSpec((1, GQK, tm), lambda b, i: (b, 0, i)),
                   pl.BlockSpec((1, GQK, tm), lambda b, i: (b, 0, i))]
                  + [slab(w) for w in cast_ws],
        out_shape=[sds(W_NA, bf16), sds(W_GF, bf16), sds(W_GB, bf16), sds(W_MG, bf16),
                   jax.ShapeDtypeStruct((B, L // CHUNK, 2 * GQK), f32),
                   jax.ShapeDtypeStruct((B, GQK, L), bf16), jax.ShapeDtypeStruct((B, GQK, L), bf16)]
                  + [jax.ShapeDtypeStruct(w.shape, bf16) for w in cast_ws],
        compiler_params=pltpu.CompilerParams(
            dimension_semantics=("parallel", "parallel"), vmem_limit_bytes=VMEM_LIMIT),
        name="in_proj",
    )(x, mod3, mod3, norm1_g, w_t, gq_t, gk_t, bd, rc, rs, cc, cs, w_a, b_a, tl, *cast_ws)


def _lane_blocks(t):
    return [t[:, c:c + LANES] for c in range(0, t.shape[1], LANES)]


def _na_kernel(q_ref, k_ref, v_ref, kc_ref, vc_ref, t2_ref, o_ref, sl_ref, sc_ref, m_ref):
    i = pl.program_id(1)
    rows = k_ref.shape[1] // GRID_W
    lane_head = lax.broadcasted_iota(jnp.int32, (GRID_W, MXU_DIM), 1) // NA_DH
    hmask = [lane_head == h for h in range(NA_GH)]

    def window_start(r):
        rs = jnp.clip(r - NA_WR // 2, 0, rows - NA_WR)
        return rs, pl.multiple_of(rs * GRID_W, GRID_W)

    def produce(rr, bank, j):
        r = i * NA_R + rr
        rs, start = window_start(r)
        d0 = rs - r + (NA_WR - 1)
        row0 = pl.multiple_of(rr * GRID_W, GRID_W)
        for g in range(NA_NG):
            ls = slice(MXU_DIM * g, MXU_DIM * (g + 1))
            q = q_ref[0, pl.ds(row0, GRID_W), ls]
            qs = jnp.concatenate([jnp.where(m, q, jnp.zeros_like(q)) for m in hmask], axis=0)
            kw = k_ref[0, pl.ds(start, NA_WR * GRID_W), ls]
            bias = jnp.concatenate(
                [jnp.concatenate([t2_ref[NA_GH * g + h, d0 + 2 * jj] for jj in range(NA_WR // 2)], axis=1)
                 for h in range(NA_GH)], axis=0)
            s_loc = _dot_nt(qs, kw) + bias
            s_ctx = _dot_nt(qs, kc_ref[0, :, ls])
            mx = functools.reduce(jnp.maximum, _lane_blocks(s_loc) + _lane_blocks(s_ctx))
            sl_ref[bank, j, g] = s_loc
            sc_ref[bank, j, g] = s_ctx
            m_ref[bank, j, g] = jnp.broadcast_to(jnp.max(mx, axis=-1, keepdims=True), (NA_GH * GRID_W, LANES))

    def consume(rr, bank, j):
        _, start = window_start(i * NA_R + rr)
        row0 = pl.multiple_of(rr * GRID_W, GRID_W)
        outs = []
        for g in range(NA_NG):
            ls = slice(MXU_DIM * g, MXU_DIM * (g + 1))
            m = m_ref[bank, j, g]
            p_loc = [jnp.exp2(sl_ref[bank, j, g, :, c:c + LANES] - m) for c in range(0, NA_WR * GRID_W, LANES)]
            p_ctx = [jnp.exp2(sc_ref[bank, j, g, :, c:c + LANES] - m) for c in range(0, CTX, LANES)]
            den = jnp.sum(functools.reduce(jnp.add, p_loc + p_ctx), axis=-1, keepdims=True)
            vw = v_ref[0, pl.ds(start, NA_WR * GRID_W), ls]
            acc = (_dot(jnp.concatenate(p_loc, axis=1).astype(bf16), vw)
                   + _dot(jnp.concatenate(p_ctx, axis=1).astype(bf16), vc_ref[0, :, ls]))
            acc = acc * (1.0 / den)
            o = jnp.zeros((GRID_W, MXU_DIM), f32)
            for h in range(NA_GH):
                o = o + jnp.where(hmask[h], acc[h * GRID_W:(h + 1) * GRID_W], 0.0)
            outs.append(o)
        o_ref[0, pl.ds(row0, GRID_W), :] = jnp.concatenate(outs, axis=1).astype(bf16)

    for j in range(NA_RI):
        produce(jnp.int32(j), 0, j)

    def step(t, carry):
        bank = t & 1
        for j in range(NA_RI):
            consume(t * NA_RI + j, bank, j)
        for j in range(NA_RI):
            produce(jnp.minimum((t + 1) * NA_RI + j, NA_R - 1), 1 - bank, j)
        return carry

    lax.fori_loop(0, NA_R // NA_RI, step, 0)


def _na_attn(qkv, kc, vc, t2):
    B, L, _ = qkv.shape
    nblk = L // (NA_R * GRID_W)
    return pl.pallas_call(
        _na_kernel,
        grid=(B, nblk),
        in_specs=[pl.BlockSpec((1, NA_R * GRID_W, NA_W), lambda b, i: (b, i, NA_Q // NA_W)),
                  pl.BlockSpec((1, L, NA_W), lambda b, i: (b, 0, NA_K // NA_W), pipeline_mode=pl.Buffered(1)),
                  pl.BlockSpec((1, L, NA_W), lambda b, i: (b, 0, NA_V // NA_W), pipeline_mode=pl.Buffered(1)),
                  pl.BlockSpec((1, CTX, NA_W), lambda b, i: (b, 0, 0)),
                  pl.BlockSpec((1, CTX, NA_W), lambda b, i: (b, 0, 0)),
                  _const_spec((NA_H, 2 * NA_WR - 1, GRID_W, LANES))],
        out_specs=pl.BlockSpec((1, NA_R * GRID_W, NA_W), lambda b, i: (b, i, 0)),
        out_shape=jax.ShapeDtypeStruct((B, L, NA_W), bf16),
        scratch_shapes=[pltpu.VMEM((2, NA_RI, NA_NG, NA_GH * GRID_W, NA_WR * GRID_W), f32),
                        pltpu.VMEM((2, NA_RI, NA_NG, NA_GH * GRID_W, CTX), f32),
                        pltpu.VMEM((2, NA_RI, NA_NG, NA_GH * GRID_W, LANES), f32)],
        compiler_params=pltpu.CompilerParams(
            dimension_semantics=("parallel", "parallel"), vmem_limit_bytes=VMEM_LIMIT),
        name="na_attn",
    )(qkv, qkv, qkv, kc, vc, t2)


def _gla_chunk(q, ki, v, ket, dcol, s_ref, tri, hmask):
    nh = len(hmask)
    qs = jnp.concatenate([jnp.where(m, q, jnp.zeros_like(q)) for m in hmask], axis=0)
    att = jnp.where(tri, _dot_nt(qs, ki), 0.0).astype(bf16)
    o = _dot(q, s_ref[...].astype(bf16))
    intra = [_dot(att[h * CHUNK:(h + 1) * CHUNK], v[:, h * GLA_DV:(h + 1) * GLA_DV]) for h in range(nh)]
    o = o + jnp.concatenate(intra, axis=1)
    for h in range(nh):
        rs = slice(h * GLA_DK, (h + 1) * GLA_DK)
        cs = slice(h * GLA_DV, (h + 1) * GLA_DV)
        s_ref[rs, cs] = dcol[rs] * s_ref[rs, cs] + _dot(ket[rs], v[:, cs])
    return o


def _gla_kernel(gf_ref, ktf_ref, decf_ref, gb_ref, vb_ref, ktb_ref, decb_ref,
                s0f_ref, s0b_ref, of_ref, ob_ref, sf_ref, sb_ref):
    nbatch = gf_ref.shape[0]
    nh = GLA_H // GLA_HG
    wq, wv = nh * GLA_DK, nh * GLA_DV

    @pl.when(pl.program_id(0) == 0)
    def _():
        for b in range(nbatch):
            for p in range(GLA_HG):
                sf_ref[b, p] = s0f_ref[b, p * wq:(p + 1) * wq, p * wv:(p + 1) * wv]
                sb_ref[b, p] = s0b_ref[b, p * wq:(p + 1) * wq, p * wv:(p + 1) * wv]

    lane_head = lax.broadcasted_iota(jnp.int32, (CHUNK, wq), 1) // GLA_DK
    hmask = [lane_head == h for h in range(nh)]
    ri = lax.broadcasted_iota(jnp.int32, (nh * CHUNK, CHUNK), 0) & (CHUNK - 1)
    ci = lax.broadcasted_iota(jnp.int32, (nh * CHUNK, CHUNK), 1)
    tri_f = ri >= ci
    tri_b = ri <= ci
    dect_f = [decf_ref[b][:, :GQK].T for b in range(nbatch)]
    dect_b = [decb_ref[b][:, GQK:].T for b in range(nbatch)]

    def k_end(kt_ref, b, c):
        kt = kt_ref[b, :, (c // 2) * 2 * CHUNK:(c // 2 + 1) * 2 * CHUNK]
        if c % 2:
            kt = pltpu.roll(kt, CHUNK, 1)
        return kt[:, :CHUNK]

    for c in range(GLA_CB):
        ts = slice(c * CHUNK, (c + 1) * CHUNK)
        cb = GLA_CB - 1 - c
        tb = slice(cb * CHUNK, (cb + 1) * CHUNK)
        for b in range(nbatch):
            ket_f, ket_b = k_end(ktf_ref, b, c), k_end(ktb_ref, b, cb)
            for p in range(GLA_HG):
                ql = slice(p * wq, (p + 1) * wq)
                vl = slice(p * wv, (p + 1) * wv)
                of_ref[b, ts, vl] = _gla_chunk(
                    gf_ref[b, ts, G_Q + p * wq:G_Q + (p + 1) * wq], gf_ref[b, ts, G_KI + p * wq:G_KI + (p + 1) * wq],
                    gf_ref[b, ts, G_V + p * wv:G_V + (p + 1) * wv], ket_f[ql],
                    dect_f[b][ql, c:c + 1], sf_ref.at[b, p], tri_f, hmask)
                ob_ref[b, tb, vl] = _gla_chunk(
                    gb_ref[b, tb, G_Q - GV + p * wq:G_Q - GV + (p + 1) * wq],
                    gb_ref[b, tb, G_KI - GV + p * wq:G_KI - GV + (p + 1) * wq],
                    vb_ref[b, tb, vl], ket_b[ql],
                    dect_b[b][ql, cb:cb + 1], sb_ref.at[b, p], tri_b, hmask)


def _gla_scan(gf, gb, ktf, ktb, dec, s0f, s0b):
    B, L, _ = gf.shape
    tm = GLA_CB * CHUNK
    nb = L // tm
    fw = lambda n: pl.BlockSpec((B, tm, n), lambda i: (0, i, 0))
    bw = lambda n: pl.BlockSpec((B, tm, n), lambda i: (0, nb - 1 - i, 0))
    st = pl.BlockSpec((B, GQK, GV), lambda i: (0, 0, 0))
    return pl.pallas_call(
        _gla_kernel,
        grid=(nb,),
        in_specs=[fw(W_GF),
                  pl.BlockSpec((B, GQK, tm), lambda i: (0, 0, i)),
                  pl.BlockSpec((B, GLA_CB, 2 * GQK), lambda i: (0, i, 0)),
                  bw(W_GB), bw(GV),
                  pl.BlockSpec((B, GQK, tm), lambda i: (0, 0, nb - 1 - i)),
                  pl.BlockSpec((B, GLA_CB, 2 * GQK), lambda i: (0, nb - 1 - i, 0)),
                  st, st],
        out_specs=[fw(GV), bw(GV)],
        out_shape=[jax.ShapeDtypeStruct((B, L, GV), f32)] * 2,
        scratch_shapes=[pltpu.VMEM((B, GLA_HG, GQK // GLA_HG, GV // GLA_HG), f32)] * 2,
        compiler_params=pltpu.CompilerParams(
            dimension_semantics=("arbitrary",), vmem_limit_bytes=VMEM_LIMIT),
        name="gla_scan",
    )(gf, ktf, dec, gb, gf, ktb, dec, s0f, s0b)


def _merge_ffn_kernel(x_ref, ona_ref, of_ref, ob_ref, mg_ref,
                      g1_ref, sh2_ref, sc2_ref, g2_ref, n2_ref, gg_ref,
                      wna_ref, wgl_ref, wo_ref, wg_ref, wu_ref, wd_ref, o_ref):
    x1s, h2s = [], []
    for hs in range(0, x_ref.shape[1], FFN_SUB):
        rows = slice(hs, hs + FFN_SUB)
        og = of_ref[0, rows] + ob_ref[0, rows]
        parts = []
        for h in range(GLA_H):
            seg = og[:, h * GLA_DV:(h + 1) * GLA_DV]
            ms = jnp.mean(seg * seg, axis=-1, keepdims=True)
            parts.append(seg * lax.rsqrt(ms + EPS) * gg_ref[...])
        ogl = (jnp.concatenate(parts, axis=1) * mg_ref[0, rows, MG_OG:MG_OG + GV].astype(f32)).astype(bf16)
        y = (mg_ref[0, rows, MG_A:MG_A + D].astype(f32) * _dot(ona_ref[0, rows], wna_ref[...])
             + mg_ref[0, rows, MG_B:MG_B + D].astype(f32) * _dot(ogl, wgl_ref[...]))
        x1 = x_ref[0, rows] + g1_ref[0] * _dot(y.astype(bf16), wo_ref[...])
        x1s.append(x1)
        h2s.append(_norm_mod(x1, n2_ref[...], sh2_ref[0], sc2_ref[0]).astype(bf16))
    x1 = jnp.concatenate(x1s, axis=0)
    h2 = jnp.concatenate(h2s, axis=0)
    act = (_silu(_dot(h2, wg_ref[...])) * _dot(h2, wu_ref[...])).astype(bf16)
    o_ref[0] = x1 + g2_ref[0] * _dot(act, wd_ref[...])


def _merge_ffn(x, ona, of, ob, mg, mod3, norm2_g, gg, wna, wgl, wo, wg, wu, wd):
    B, L, _ = x.shape
    tm = TM_FFN
    tok = lambda n: pl.BlockSpec((1, tm, n), lambda b, i: (b, i, 0))
    modv = lambda j: pl.BlockSpec((1, 1, D), lambda b, i: (b, 0, j))
    return pl.pallas_call(
        _merge_ffn_kernel,
        grid=(B, L // tm),
        in_specs=[tok(D), tok(NA_W), tok(GV), tok(GV), tok(W_MG),
                  modv(2), modv(3), modv(4), modv(5),
                  _const_spec((1, D)), _const_spec((1, GLA_DV)),
                  _const_spec((NA_W, D)), _const_spec((GV, D)), _const_spec((D, D)),
                  _const_spec((D, D_FF)), _const_spec((D, D_FF)), _const_spec((D_FF, D))],
        out_specs=tok(D),
        out_shape=jax.ShapeDtypeStruct((B, L, D), f32),
        compiler_params=pltpu.CompilerParams(
            dimension_semantics=("parallel", "parallel"), vmem_limit_bytes=VMEM_LIMIT),
        name="merge_ffn",
    )(x, ona, of, ob, mg, mod3, mod3, mod3, mod3, norm2_g, gg, wna, wgl, wo, wg, wu, wd)


def kernel(x, c, ctx, c_ctx, w_mod, b_mod, norm1_g, norm2_g, w_in, na_q_norm_g, na_k_norm_g, na_rpb,
           gla_w_alpha, gla_b_alpha, gla_norm_g, w_branch_na, w_branch_gla, w_out,
           w_ffn_gate, w_ffn_up, w_ffn_down):
    B, L, _ = x.shape
    assert w_mod.shape[0] == 1 and L % (NA_R * GRID_W) == 0 and L % TM_PROJ == 0 and D_IN % W_IN_CAST_ROWS == 0

    w_t = _cast_bf16(jnp.transpose(w_in[0]), W_IN_CAST_ROWS)
    zero = jnp.zeros((GLA_RANK, GQK), f32)
    w_a = jnp.concatenate([jnp.concatenate([gla_w_alpha[0, 0], zero], axis=1),
                           jnp.concatenate([zero, gla_w_alpha[0, 1]], axis=1)], axis=0).astype(bf16)
    b_a = gla_b_alpha[0].reshape(1, 2 * GQK)
    gq_t = jnp.tile(na_q_norm_g[0] * (NA_DH ** -0.5 * LOG2E), NA_H).reshape(1, NA_W)
    gk_t = jnp.tile(na_k_norm_g[0], NA_H).reshape(1, NA_W)
    bd = jnp.asarray(np.kron(np.eye(4), np.full((NA_DH, NA_DH), 1.0 / NA_DH)), bf16)

    mod3 = _adaln_mod(c, c_ctx, w_mod[0], b_mod[0])
    t2 = _rpb_expand(na_rpb[0])

    kc, vc, s0f, s0b = _ctx_side(ctx, mod3, norm1_g, w_t, w_a, b_a, gk_t, bd, B)
    later_ws = [w_branch_na[0], w_branch_gla[0], w_out[0], w_ffn_gate[0], w_ffn_up[0], w_ffn_down[0]]
    na_qkv, gla_f, gla_b, gates, dec, ktf, ktb, *later_bf = _in_proj(
        x, mod3, norm1_g, w_t, gq_t, gk_t, bd, w_a, b_a, later_ws)
    ona = _na_attn(na_qkv, kc, vc, t2)
    of, ob = _gla_scan(gla_f, gla_b, ktf, ktb, dec, s0f, s0b)
    return _merge_ffn(x, ona, of, ob, gates, mod3, norm2_g, gla_norm_g, *later_bf)
```

```python
import functools

import numpy as np
import jax
import jax.numpy as jnp
from jax import lax
from jax.experimental import pallas as pl
from jax.experimental.pallas import tpu as pltpu

D = 1024
GRID_W = 64
CTX = 256
NA_H = 8
NA_DH = 64
NA_WR = 8
NA_WC = 16
GLA_H = 4
GLA_DK = 64
GLA_DV = 128
GLA_RANK = 16
GLA_TAU = 16.0
GLA_LA_MIN = -1.0
CHUNK = 64
ROPE_THETA = 10000.0
D_FF = 2816
EPS = 1e-6
NEG_INF = -1e30
NA_W = NA_H * NA_DH
GQK = GLA_H * GLA_DK
GV = GLA_H * GLA_DV

OFF_QNA, OFF_KNA, OFF_VNA = 0, 512, 1024
OFF_QG, OFF_KG, OFF_VG, OFF_OG = 1536, 1792, 2048, 2560
OFF_LR, OFF_MA, OFF_MB = 3072, 3104, 4128
N_LR = 2 * GLA_RANK
N_PROJ_OUT = 7
NA_Q, NA_K, NA_V, W_NA = 0, 512, 1024, 1536
G_V, G_Q, G_KI, W_GF = 0, 512, 768, 1024
W_GB = W_GF - GV
MG_A, MG_B, MG_OG, W_MG = 0, 1024, 2048, 2560
D_IN = 5152
LOG2E = 1.4426950408889634

LANES = 128
SUBLANES = 8
BF16_SUBLANES = 16
MXU_DIM = 256
NA_GH = 2
NA_GW = NA_GH * NA_DH
NA_NG = NA_H // NA_GH
VMEM_LIMIT = 56 * 1024 * 1024

W_IN_CAST_ROWS = 46 * BF16_SUBLANES
TM_PROJ = 512
PROJ_SUB = 256
TM_FFN = 512
FFN_SUB = 256
NA_R = 64
NA_RI = 4
GLA_CB = 16
GLA_HG = 2

f32 = jnp.float32
bf16 = jnp.bfloat16


def _const_spec(shape):
    nd = len(shape)
    return pl.BlockSpec(shape, lambda *_: (0,) * nd, pipeline_mode=pl.Buffered(1))


def _dot(a, b):
    return jnp.dot(a, b, preferred_element_type=f32)


def _dot_nt(a, b):
    return lax.dot_general(a, b, (((1,), (1,)), ((), ())), preferred_element_type=f32)


def _split2(x):
    hi = x.astype(bf16)
    lo = (x - hi.astype(f32)).astype(bf16)
    return hi, lo


def _sigmoid(x):
    return 0.5 * jnp.tanh(0.5 * x) + 0.5


def _silu(x):
    return x * _sigmoid(x)


def _split3_dot(r, oh):
    h1 = r.astype(bf16)
    r2 = r - h1.astype(f32)
    h2 = r2.astype(bf16)
    h3 = (r2 - h2.astype(f32)).astype(bf16)
    return _dot(h1, oh) + _dot(h2, oh) + _dot(h3, oh)


def _rpb_expand_kernel(r_ref, oh_ref, mask_ref, o_ref):
    tiles = _split3_dot(r_ref[...], oh_ref[...]) * LOG2E + mask_ref[...]
    for q in range(GRID_W):
        o_ref[:, q, :] = tiles[:o_ref.shape[0], q * LANES:(q + 1) * LANES]


def _rpb_expand(rpb):
    nd = 2 * NA_WR - 1
    nrow = NA_H * nd
    ncol = 2 * NA_WC - 1
    nxt = jnp.concatenate([rpb[:, 1:], jnp.zeros_like(rpb[:, :1])], axis=1)
    pad = ((0, LANES - nrow), (0, LANES - ncol))
    r01 = jnp.concatenate([jnp.pad(rpb.reshape(nrow, ncol), pad), jnp.pad(nxt.reshape(nrow, ncol), pad)], axis=1)
    q = np.arange(GRID_W)
    cs = np.clip(q - NA_WC // 2, 0, GRID_W - NA_WC)
    in_win = (q[None, :] >= cs[:, None]) & (q[None, :] < cs[:, None] + NA_WC)
    dc = np.clip(q[None, :] - q[:, None], -(NA_WC - 1), NA_WC - 1) + NA_WC - 1
    oh = np.zeros((2, LANES, GRID_W, LANES), np.float32)
    qq, kk = np.nonzero(in_win)
    oh[0, dc[qq, kk], qq, kk] = 1.0
    oh[1, dc[qq, kk], qq, kk + GRID_W] = 1.0
    oh = oh.reshape(2 * LANES, GRID_W * LANES)
    mask = np.where(np.concatenate([in_win, in_win], axis=1), 0.0, NEG_INF).reshape(1, GRID_W * LANES)
    out = pl.pallas_call(
        _rpb_expand_kernel,
        out_shape=jax.ShapeDtypeStruct((nrow, GRID_W, LANES), f32),
        name="rpb_expand",
    )(r01, jnp.asarray(oh, bf16), jnp.asarray(mask, f32))
    return out.reshape(NA_H, nd, GRID_W, LANES)


def _cast_kernel(x_ref, o_ref):
    o_ref[...] = x_ref[...].astype(bf16)


def _cast_bf16(w, rows):
    R, C = w.shape
    return pl.pallas_call(
        _cast_kernel,
        grid=(R // rows,),
        in_specs=[pl.BlockSpec((rows, C), lambda j: (j, 0))],
        out_specs=pl.BlockSpec((rows, C), lambda j: (j, 0)),
        out_shape=jax.ShapeDtypeStruct((R, C), bf16),
        name="cast_bf16",
    )(w)


def _mod_kernel(a_ref, w_ref, b_ref, o_ref):
    a = _silu(a_ref[...]).astype(bf16)
    o_ref[...] = (_dot(a, w_ref[...].astype(bf16)) + b_ref[...])[:, None, :]


def _adaln_mod(c, c_ctx, w_mod, b_mod):
    B = c.shape[0]
    assert B < SUBLANES
    a = jnp.concatenate([c, c_ctx[None], jnp.zeros((SUBLANES - B - 1, D), f32)], axis=0)
    n = w_mod.shape[1]
    return pl.pallas_call(
        _mod_kernel,
        grid=(n // D,),
        in_specs=[pl.BlockSpec((SUBLANES, D), lambda j: (0, 0)),
                  pl.BlockSpec((D, D), lambda j: (0, j)),
                  pl.BlockSpec((1, D), lambda j: (0, j))],
        out_specs=pl.BlockSpec((SUBLANES, 1, D), lambda j: (0, 0, j)),
        out_shape=jax.ShapeDtypeStruct((SUBLANES, 1, n), f32),
        name="adaln_mod",
    )(a, w_mod, b_mod.reshape(1, n))


def _norm_mod(x, g, sh, sc):
    ms = jnp.mean(x * x, axis=-1, keepdims=True)
    return (x * lax.rsqrt(ms + EPS)) * (g * (1.0 + sc)) + sh


def _head_rms(t, bd, g):
    sq = (t * t).astype(bf16)
    ms = jnp.concatenate([_dot(sq[:, :MXU_DIM], bd), _dot(sq[:, MXU_DIM:], bd)], axis=1)
    return t * lax.rsqrt(ms + EPS) * g


def _log_alpha(zl, wa_ref, ba_ref):
    z = _dot(zl.astype(bf16), wa_ref[...]) + ba_ref[...]
    soft = jnp.log2(1.0 + jnp.exp2(z * -LOG2E))
    return jnp.maximum(soft * (-1.0 / GLA_TAU), GLA_LA_MIN * LOG2E)


def _ctx_kernel(ctx_ref, sh_ref, sc_ref, g1_ref, wk_ref, wv_ref, wkg_ref, wvg_ref, wlr_ref,
                wa_ref, ba_ref, gk_ref, bd_ref, su_ref, sl_ref,
                kc_ref, vc_ref, sf_ref, sb_ref):
    h = _norm_mod(ctx_ref[0], g1_ref[...], sh_ref[0], sc_ref[0]).astype(bf16)
    k = _dot_nt(h, wk_ref[...])
    kc_ref[0] = _head_rms(k, bd_ref[...], gk_ref[...]).astype(bf16)
    vc_ref[0] = _dot_nt(h, wv_ref[...]).astype(bf16)
    kg = _dot_nt(h, wkg_ref[...])
    vg = _dot_nt(h, wvg_ref[...]).astype(bf16)
    la = _log_alpha(_dot_nt(h, wlr_ref[...]), wa_ref, ba_ref)
    hi, lo = _split2(la)
    ef = _dot(su_ref[...], hi[:, :GQK]) + _dot(su_ref[...], lo[:, :GQK])
    eb = _dot(sl_ref[...], hi[:, GQK:]) + _dot(sl_ref[...], lo[:, GQK:])
    r = lax.broadcasted_iota(jnp.int32, (GQK, GV), 0) // GLA_DK
    cc = lax.broadcasted_iota(jnp.int32, (GQK, GV), 1) // GLA_DV
    diag = r == cc
    for e, out in ((ef, sf_ref), (eb, sb_ref)):
        kw = (kg * jnp.exp2(e)).astype(bf16)
        u = _dot(kw.T, vg)
        out[0] = jnp.where(diag, u, 0.0)


def _ctx_side(ctx, mod3, norm1_g, w_t, w_a, b_a, gk_t, bd, n_b):
    su = jnp.asarray(np.triu(np.ones((CTX, CTX), np.float32), 1), bf16)
    sl = jnp.asarray(np.tril(np.ones((CTX, CTX), np.float32), -1), bf16)
    B = ctx.shape[0]
    cst = lambda shape, idx: pl.BlockSpec(shape, lambda b: idx)
    return pl.pallas_call(
        _ctx_kernel,
        grid=(B,),
        in_specs=[pl.BlockSpec((1, CTX, D), lambda b: (b, 0, 0)),
                  cst((1, 1, D), (n_b, 0, 0)), cst((1, 1, D), (n_b, 0, 1)),
                  cst((1, D), (0, 0)),
                  cst((NA_W, D), (OFF_KNA // NA_W, 0)), cst((NA_W, D), (OFF_VNA // NA_W, 0)),
                  cst((GQK, D), (OFF_KG // GQK, 0)), cst((GV, D), (OFF_VG // GV, 0)),
                  cst((N_LR, D), (OFF_LR // N_LR, 0)),
                  cst((N_LR, 2 * GQK), (0, 0)), cst((1, 2 * GQK), (0, 0)),
                  cst((1, NA_W), (0, 0)), cst((MXU_DIM, MXU_DIM), (0, 0)),
                  cst((CTX, CTX), (0, 0)), cst((CTX, CTX), (0, 0))],
        out_specs=[pl.BlockSpec((1, CTX, NA_W), lambda b: (b, 0, 0)),
                   pl.BlockSpec((1, CTX, NA_W), lambda b: (b, 0, 0)),
                   pl.BlockSpec((1, GQK, GV), lambda b: (b, 0, 0)),
                   pl.BlockSpec((1, GQK, GV), lambda b: (b, 0, 0))],
        out_shape=[jax.ShapeDtypeStruct((B, CTX, NA_W), bf16),
                   jax.ShapeDtypeStruct((B, CTX, NA_W), bf16),
                   jax.ShapeDtypeStruct((B, GQK, GV), f32),
                   jax.ShapeDtypeStruct((B, GQK, GV), f32)],
        compiler_params=pltpu.CompilerParams(vmem_limit_bytes=VMEM_LIMIT),
        name="ctx_side",
    )(ctx, mod3, mod3, norm1_g, w_t, w_t, w_t, w_t, w_t, w_a, b_a, gk_t, bd, su, sl)


def _proj_kernel(x_ref, sh_ref, sc_ref, g1_ref, w_ref, gq_ref, gk_ref, bd_ref,
                 rc_ref, rs_ref, cc_ref, cs_ref, wa_ref, ba_ref, tl_ref, *refs):
    n_cast = (len(refs) - N_PROJ_OUT) // 2
    cast_in, refs = refs[:n_cast], refs[n_cast:]
    na_ref, gf_ref, gb_ref, mg_ref, dec_ref, ktf_ref, ktb_ref = refs[:N_PROJ_OUT]
    cast_out = refs[N_PROJ_OUT:]
    for src, dst in zip(cast_in, cast_out):
        dst[...] = src[...].astype(bf16)

    bd = bd_ref[...]
    tl = tl_ref[...]
    nrow = PROJ_SUB // GRID_W
    nchunk = PROJ_SUB // CHUNK
    lane = lax.broadcasted_iota(jnp.int32, (PROJ_SUB, GQK), 1)
    first_half = (lane & 31) < 16

    for hs in range(0, x_ref.shape[1], PROJ_SUB):
        rows = slice(hs, hs + PROJ_SUB)
        hb = _norm_mod(x_ref[0, rows], g1_ref[...], sh_ref[0], sc_ref[0]).astype(bf16)

        def proj(off, n):
            return _dot_nt(hb, w_ref[off:off + n, :])

        zl = proj(OFF_LR, N_LR)
        q_raw = proj(OFF_QNA, NA_W)
        la = _log_alpha(zl, wa_ref, ba_ref)
        k_raw = proj(OFF_KNA, NA_W)
        hi, lo = _split2(la)
        na_ref[0, rows, NA_Q:NA_Q + NA_W] = _head_rms(q_raw, bd, gq_ref[...]).astype(bf16)
        cum = jnp.concatenate(
            [_dot(tl, hi[s:s + MXU_DIM]) + _dot(tl, lo[s:s + MXU_DIM]) for s in range(0, PROJ_SUB, MXU_DIM)], axis=0)
        na_ref[0, rows, NA_K:NA_K + NA_W] = _head_rms(k_raw, bd, gk_ref[...]).astype(bf16)
        qg_raw = proj(OFF_QG, GQK)
        kg_raw = proj(OFF_KG, GQK)
        na_ref[0, rows, NA_V:NA_V + NA_W] = proj(OFF_VNA, NA_W).astype(bf16)
        gf_ref[0, rows, G_V:G_V + GV] = proj(OFF_VG, GV).astype(bf16)

        gr = slice(hs // GRID_W, hs // GRID_W + nrow)
        cos_t = (rc_ref[gr][:, None, :] + cc_ref[...][None]).reshape(PROJ_SUB, GQK)
        sin_t = (rs_ref[gr][:, None, :] + cs_ref[...][None]).reshape(PROJ_SUB, GQK)

        def rope(t):
            partner = jnp.where(first_half, pltpu.roll(t, GQK - 16, 1), pltpu.roll(t, 16, 1))
            return t * cos_t + partner * sin_t

        qg = rope(qg_raw) * (GLA_DK ** -0.5)
        kg = rope(kg_raw)

        cum3 = cum.reshape(nchunk, CHUNK, 2 * GQK)
        tot3 = cum3[:, CHUNK - 1:CHUNK, :]
        la3 = la.reshape(nchunk, CHUNK, 2 * GQK)
        dec_ref[0, hs // CHUNK:hs // CHUNK + nchunk] = jnp.exp2(tot3).reshape(nchunk, 2 * GQK)
        b_f = cum3[:, :, :GQK].reshape(PROJ_SUB, GQK)
        e_f = (tot3 - cum3)[:, :, :GQK].reshape(PROJ_SUB, GQK)
        b_b = (tot3 - cum3 + la3)[:, :, GQK:].reshape(PROJ_SUB, GQK)
        e_b = (cum3 - la3)[:, :, GQK:].reshape(PROJ_SUB, GQK)
        mg_ref[0, rows, MG_OG:MG_OG + GV] = _silu(proj(OFF_OG, GV)).astype(bf16)
        gf_ref[0, rows, G_Q:G_Q + GQK] = (qg * jnp.exp2(b_f)).astype(bf16)
        gf_ref[0, rows, G_KI:G_KI + GQK] = (kg * jnp.exp2(-b_f)).astype(bf16)
        ktf_ref[0, :, rows] = (kg * jnp.exp2(e_f)).T.astype(bf16)
        mg_ref[0, rows, MG_A:MG_A + D] = _sigmoid(proj(OFF_MA, D)).astype(bf16)
        gb_ref[0, rows, G_Q - GV:G_Q - GV + GQK] = (qg * jnp.exp2(b_b)).astype(bf16)
        gb_ref[0, rows, G_KI - GV:G_KI - GV + GQK] = (kg * jnp.exp2(-b_b)).astype(bf16)
        ktb_ref[0, :, rows] = (kg * jnp.exp2(e_b)).T.astype(bf16)
        mg_ref[0, rows, MG_B:MG_B + D] = _sigmoid(proj(OFF_MB, D)).astype(bf16)


def _rope_tables(L):
    n = GLA_DK // 4
    freqs = ROPE_THETA ** (-np.arange(n, dtype=np.float64) / n)
    lane = np.arange(GQK)
    fl = freqs[lane % n]
    is_row = (lane % GLA_DK) < GLA_DK // 2
    sign = np.where((lane % 32) < 16, -1.0, 1.0)
    rows = np.arange(L // GRID_W)[:, None] * fl[None]
    cols = np.arange(GRID_W)[:, None] * fl[None]
    rc = np.where(is_row, np.cos(rows), 0.0)
    rs = np.where(is_row, np.sin(rows) * sign, 0.0)
    cc = np.where(~is_row, np.cos(cols), 0.0)
    cs = np.where(~is_row, np.sin(cols) * sign, 0.0)
    return [jnp.asarray(t, f32) for t in (rc, rs, cc, cs)]


def _in_proj(x, mod3, norm1_g, w_t, gq_t, gk_t, bd, w_a, b_a, cast_ws):
    B, L, _ = x.shape
    tm = TM_PROJ
    rows_per = tm // GRID_W
    nstep = L // tm
    total = B * nstep

    def slab(w):
        R, C = w.shape
        if R % (total * BF16_SUBLANES) == 0:
            return pl.BlockSpec((R // total, C), lambda b, i: (b * nstep + i, 0))
        assert (2 * R) % (total * BF16_SUBLANES) == 0 and C % (2 * LANES) == 0
        return pl.BlockSpec((2 * R // total, C // 2), lambda b, i: ((b * nstep + i) // 2, (b * nstep + i) % 2))
    rc, rs, cc, cs = _rope_tables(L)
    blk = np.kron(np.eye(MXU_DIM // CHUNK), np.tril(np.ones((CHUNK, CHUNK)))).astype(np.float32)
    tl = jnp.asarray(blk, bf16)
    tok = lambda n: pl.BlockSpec((1, tm, n), lambda b, i: (b, i, 0))
    sds = lambda n, dt: jax.ShapeDtypeStruct((B, L, n), dt)
    return pl.pallas_call(
        _proj_kernel,
        grid=(B, L // tm),
        in_specs=[tok(D),
                  pl.BlockSpec((1, 1, D), lambda b, i: (b, 0, 0)),
                  pl.BlockSpec((1, 1, D), lambda b, i: (b, 0, 1)),
                  _const_spec((1, D)), _const_spec((D_IN, D)),
                  _const_spec((1, NA_W)), _const_spec((1, NA_W)), _const_spec((MXU_DIM, MXU_DIM)),
                  pl.BlockSpec((rows_per, GQK), lambda b, i: (i, 0)),
                  pl.BlockSpec((rows_per, GQK), lambda b, i: (i, 0)),
                  _const_spec((GRID_W, GQK)), _const_spec((GRID_W, GQK)),
                  _const_spec((N_LR, 2 * GQK)), _const_spec((1, 2 * GQK)), _const_spec((MXU_DIM, MXU_DIM))]
                 + [slab(w) for w in cast_ws],
        out_specs=[tok(W_NA), tok(W_GF), tok(W_GB), tok(W_MG),
                   pl.BlockSpec((1, tm // CHUNK, 2 * GQK), lambda b, i: (b, i, 0)),
                   pl.BlockSpec((1, GQK, tm), lambda b, i: (b, 0, i)),
                   pl.BlockSpec((1, GQK, tm), lambda b, i: (b, 0, i))]
                  + [slab(w) for w in cast_ws],
        out_shape=[sds(W_NA, bf16), sds(W_GF, bf16), sds(W_GB, bf16), sds(W_MG, bf16),
                   jax.ShapeDtypeStruct((B, L // CHUNK, 2 * GQK), f32),
                   jax.ShapeDtypeStruct((B, GQK, L), bf16), jax.ShapeDtypeStruct((B, GQK, L), bf16)]
                  + [jax.ShapeDtypeStruct(w.shape, bf16) for w in cast_ws],
        compiler_params=pltpu.CompilerParams(
            dimension_semantics=("parallel", "parallel"), vmem_limit_bytes=VMEM_LIMIT),
        name="in_proj",
    )(x, mod3, mod3, norm1_g, w_t, gq_t, gk_t, bd, rc, rs, cc, cs, w_a, b_a, tl, *cast_ws)


def _lane_blocks(t):
    return [t[:, c:c + LANES] for c in range(0, t.shape[1], LANES)]


def _na_kernel(q_ref, k_ref, v_ref, kc_ref, vc_ref, t2_ref, o_ref, sl_ref, sc_ref, m_ref):
    i = pl.program_id(1)
    rows = k_ref.shape[1] // GRID_W
    lane_head = lax.broadcasted_iota(jnp.int32, (GRID_W, NA_GW), 1) // NA_DH
    hmask = [lane_head == h for h in range(NA_GH)]

    def window_start(r):
        rs = jnp.clip(r - NA_WR // 2, 0, rows - NA_WR)
        return rs, pl.multiple_of(rs * GRID_W, GRID_W)

    def produce(rr, bank, j):
        r = i * NA_R + rr
        rs, start = window_start(r)
        d0 = rs - r + (NA_WR - 1)
        row0 = pl.multiple_of(rr * GRID_W, GRID_W)
        for g in range(NA_NG):
            ls = slice(NA_GW * g, NA_GW * (g + 1))
            q = q_ref[0, pl.ds(row0, GRID_W), ls]
            qs = jnp.concatenate([jnp.where(m, q, jnp.zeros_like(q)) for m in hmask], axis=0)
            kw = k_ref[0, pl.ds(start, NA_WR * GRID_W), ls]
            bias = jnp.concatenate(
                [jnp.concatenate([t2_ref[NA_GH * g + h, d0 + 2 * jj] for jj in range(NA_WR // 2)], axis=1)
                 for h in range(NA_GH)], axis=0)
            s_loc = _dot_nt(qs, kw) + bias
            s_ctx = _dot_nt(qs, kc_ref[0, :, ls])
            mx = functools.reduce(jnp.maximum, _lane_blocks(s_loc) + _lane_blocks(s_ctx))
            sl_ref[bank, j, g] = s_loc
            sc_ref[bank, j, g] = s_ctx
            m_ref[bank, j, g] = jnp.broadcast_to(jnp.max(mx, axis=-1, keepdims=True), (NA_GH * GRID_W, LANES))

    def consume(rr, bank, j):
        _, start = window_start(i * NA_R + rr)
        row0 = pl.multiple_of(rr * GRID_W, GRID_W)
        outs = []
        for g in range(NA_NG):
            ls = slice(NA_GW * g, NA_GW * (g + 1))
            m = m_ref[bank, j, g]
            p_loc = [jnp.exp2(sl_ref[bank, j, g, :, c:c + LANES] - m) for c in range(0, NA_WR * GRID_W, LANES)]
            p_ctx = [jnp.exp2(sc_ref[bank, j, g, :, c:c + LANES] - m) for c in range(0, CTX, LANES)]
            den = jnp.sum(functools.reduce(jnp.add, p_loc + p_ctx), axis=-1, keepdims=True)
            vw = v_ref[0, pl.ds(start, NA_WR * GRID_W), ls]
            acc = (_dot(jnp.concatenate(p_loc, axis=1).astype(bf16), vw)
                   + _dot(jnp.concatenate(p_ctx, axis=1).astype(bf16), vc_ref[0, :, ls]))
            acc = acc * (1.0 / den)
            o = jnp.zeros((GRID_W, NA_GW), f32)
            for h in range(NA_GH):
                o = o + jnp.where(hmask[h], acc[h * GRID_W:(h + 1) * GRID_W], 0.0)
            outs.append(o)
        o_ref[0, pl.ds(row0, GRID_W), :] = jnp.concatenate(outs, axis=1).astype(bf16)

    for j in range(NA_RI):
        produce(jnp.int32(j), 0, j)

    def step(t, carry):
        bank = t & 1
        for j in range(NA_RI):
            consume(t * NA_RI + j, bank, j)
        for j in range(NA_RI):
            produce(jnp.minimum((t + 1) * NA_RI + j, NA_R - 1), 1 - bank, j)
        return carry

    lax.fori_loop(0, NA_R // NA_RI, step, 0)


def _na_attn(qkv, kc, vc, t2):
    B, L, _ = qkv.shape
    nblk = L // (NA_R * GRID_W)
    return pl.pallas_call(
        _na_kernel,
        grid=(B, nblk),
        in_specs=[pl.BlockSpec((1, NA_R * GRID_W, NA_W), lambda b, i: (b, i, NA_Q // NA_W)),
                  pl.BlockSpec((1, L, NA_W), lambda b, i: (b, 0, NA_K // NA_W), pipeline_mode=pl.Buffered(1)),
                  pl.BlockSpec((1, L, NA_W), lambda b, i: (b, 0, NA_V // NA_W), pipeline_mode=pl.Buffered(1)),
                  pl.BlockSpec((1, CTX, NA_W), lambda b, i: (b, 0, 0)),
                  pl.BlockSpec((1, CTX, NA_W), lambda b, i: (b, 0, 0)),
                  _const_spec((NA_H, 2 * NA_WR - 1, GRID_W, LANES))],
        out_specs=pl.BlockSpec((1, NA_R * GRID_W, NA_W), lambda b, i: (b, i, 0)),
        out_shape=jax.ShapeDtypeStruct((B, L, NA_W), bf16),
        scratch_shapes=[pltpu.VMEM((2, NA_RI, NA_NG, NA_GH * GRID_W, NA_WR * GRID_W), f32),
                        pltpu.VMEM((2, NA_RI, NA_NG, NA_GH * GRID_W, CTX), f32),
                        pltpu.VMEM((2, NA_RI, NA_NG, NA_GH * GRID_W, LANES), f32)],
        compiler_params=pltpu.CompilerParams(
            dimension_semantics=("parallel", "parallel"), vmem_limit_bytes=VMEM_LIMIT),
        name="na_attn",
    )(qkv, qkv, qkv, kc, vc, t2)


def _gla_chunk(q, ki, v, ket, dcol, s_ref, tri, hmask):
    nh = len(hmask)
    qs = jnp.concatenate([jnp.where(m, q, jnp.zeros_like(q)) for m in hmask], axis=0)
    att = jnp.where(tri, _dot_nt(qs, ki), 0.0).astype(bf16)
    o = _dot(q, s_ref[...].astype(bf16))
    intra = [_dot(att[h * CHUNK:(h + 1) * CHUNK], v[:, h * GLA_DV:(h + 1) * GLA_DV]) for h in range(nh)]
    o = o + jnp.concatenate(intra, axis=1)
    for h in range(nh):
        rs = slice(h * GLA_DK, (h + 1) * GLA_DK)
        cs = slice(h * GLA_DV, (h + 1) * GLA_DV)
        s_ref[rs, cs] = dcol[rs] * s_ref[rs, cs] + _dot(ket[rs], v[:, cs])
    return o


def _gla_kernel(gf_ref, ktf_ref, decf_ref, gb_ref, vb_ref, ktb_ref, decb_ref,
                s0f_ref, s0b_ref, of_ref, ob_ref, sf_ref, sb_ref):
    nbatch = gf_ref.shape[0]
    nh = GLA_H // GLA_HG
    wq, wv = nh * GLA_DK, nh * GLA_DV

    @pl.when(pl.program_id(0) == 0)
    def _():
        for b in range(nbatch):
            for p in range(GLA_HG):
                sf_ref[b, p] = s0f_ref[b, p * wq:(p + 1) * wq, p * wv:(p + 1) * wv]
                sb_ref[b, p] = s0b_ref[b, p * wq:(p + 1) * wq, p * wv:(p + 1) * wv]

    lane_head = lax.broadcasted_iota(jnp.int32, (CHUNK, wq), 1) // GLA_DK
    hmask = [lane_head == h for h in range(nh)]
    ri = lax.broadcasted_iota(jnp.int32, (nh * CHUNK, CHUNK), 0) & (CHUNK - 1)
    ci = lax.broadcasted_iota(jnp.int32, (nh * CHUNK, CHUNK), 1)
    tri_f = ri >= ci
    tri_b = ri <= ci
    dect_f = [decf_ref[b][:, :GQK].T for b in range(nbatch)]
    dect_b = [decb_ref[b][:, GQK:].T for b in range(nbatch)]

    def k_end(kt_ref, b, c):
        kt = kt_ref[b, :, (c // 2) * 2 * CHUNK:(c // 2 + 1) * 2 * CHUNK]
        if c % 2:
            kt = pltpu.roll(kt, CHUNK, 1)
        return kt[:, :CHUNK]

    for c in range(GLA_CB):
        ts = slice(c * CHUNK, (c + 1) * CHUNK)
        cb = GLA_CB - 1 - c
        tb = slice(cb * CHUNK, (cb + 1) * CHUNK)
        for b in range(nbatch):
            ket_f, ket_b = k_end(ktf_ref, b, c), k_end(ktb_ref, b, cb)
            for p in range(GLA_HG):
                ql = slice(p * wq, (p + 1) * wq)
                vl = slice(p * wv, (p + 1) * wv)
                of_ref[b, ts, vl] = _gla_chunk(
                    gf_ref[b, ts, G_Q + p * wq:G_Q + (p + 1) * wq], gf_ref[b, ts, G_KI + p * wq:G_KI + (p + 1) * wq],
                    gf_ref[b, ts, G_V + p * wv:G_V + (p + 1) * wv], ket_f[ql],
                    dect_f[b][ql, c:c + 1], sf_ref.at[b, p], tri_f, hmask)
                ob_ref[b, tb, vl] = _gla_chunk(
                    gb_ref[b, tb, G_Q - GV + p * wq:G_Q - GV + (p + 1) * wq],
                    gb_ref[b, tb, G_KI - GV + p * wq:G_KI - GV + (p + 1) * wq],
                    vb_ref[b, tb, vl], ket_b[ql],
                    dect_b[b][ql, cb:cb + 1], sb_ref.at[b, p], tri_b, hmask)


def _gla_scan(gf, gb, ktf, ktb, dec, s0f, s0b):
    B, L, _ = gf.shape
    tm = GLA_CB * CHUNK
    nb = L // tm
    fw = lambda n: pl.BlockSpec((B, tm, n), lambda i: (0, i, 0))
    bw = lambda n: pl.BlockSpec((B, tm, n), lambda i: (0, nb - 1 - i, 0))
    st = pl.BlockSpec((B, GQK, GV), lambda i: (0, 0, 0))
    return pl.pallas_call(
        _gla_kernel,
        grid=(nb,),
        in_specs=[fw(W_GF),
                  pl.BlockSpec((B, GQK, tm), lambda i: (0, 0, i)),
                  pl.BlockSpec((B, GLA_CB, 2 * GQK), lambda i: (0, i, 0)),
                  bw(W_GB), bw(GV),
                  pl.BlockSpec((B, GQK, tm), lambda i: (0, 0, nb - 1 - i)),
                  pl.BlockSpec((B, GLA_CB, 2 * GQK), lambda i: (0, nb - 1 - i, 0)),
                  st, st],
        out_specs=[fw(GV), bw(GV)],
        out_shape=[jax.ShapeDtypeStruct((B, L, GV), f32)] * 2,
        scratch_shapes=[pltpu.VMEM((B, GLA_HG, GQK // GLA_HG, GV // GLA_HG), f32)] * 2,
        compiler_params=pltpu.CompilerParams(
            dimension_semantics=("arbitrary",), vmem_limit_bytes=VMEM_LIMIT),
        name="gla_scan",
    )(gf, ktf, dec, gb, gf, ktb, dec, s0f, s0b)


def _merge_ffn_kernel(x_ref, ona_ref, of_ref, ob_ref, mg_ref,
                      g1_ref, sh2_ref, sc2_ref, g2_ref, n2_ref, gg_ref,
                      wna_ref, wgl_ref, wo_ref, wg_ref, wu_ref, wd_ref, o_ref):
    x1s, h2s = [], []
    for hs in range(0, x_ref.shape[1], FFN_SUB):
        rows = slice(hs, hs + FFN_SUB)
        og = of_ref[0, rows] + ob_ref[0, rows]
        parts = []
        for h in range(GLA_H):
            seg = og[:, h * GLA_DV:(h + 1) * GLA_DV]
            ms = jnp.mean(seg * seg, axis=-1, keepdims=True)
            parts.append(seg * lax.rsqrt(ms + EPS) * gg_ref[...])
        ogl = (jnp.concatenate(parts, axis=1) * mg_ref[0, rows, MG_OG:MG_OG + GV].astype(f32)).astype(bf16)
        y = (mg_ref[0, rows, MG_A:MG_A + D].astype(f32) * _dot(ona_ref[0, rows], wna_ref[...])
             + mg_ref[0, rows, MG_B:MG_B + D].astype(f32) * _dot(ogl, wgl_ref[...]))
        x1 = x_ref[0, rows] + g1_ref[0] * _dot(y.astype(bf16), wo_ref[...])
        x1s.append(x1)
        h2s.append(_norm_mod(x1, n2_ref[...], sh2_ref[0], sc2_ref[0]).astype(bf16))
    x1 = jnp.concatenate(x1s, axis=0)
    h2 = jnp.concatenate(h2s, axis=0)
    act = (_silu(_dot(h2, wg_ref[...])) * _dot(h2, wu_ref[...])).astype(bf16)
    o_ref[0] = x1 + g2_ref[0] * _dot(act, wd_ref[...])


def _merge_ffn(x, ona, of, ob, mg, mod3, norm2_g, gg, wna, wgl, wo, wg, wu, wd):
    B, L, _ = x.shape
    tm = TM_FFN
    tok = lambda n: pl.BlockSpec((1, tm, n), lambda b, i: (b, i, 0))
    modv = lambda j: pl.BlockSpec((1, 1, D), lambda b, i: (b, 0, j))
    return pl.pallas_call(
        _merge_ffn_kernel,
        grid=(B, L // tm),
        in_specs=[tok(D), tok(NA_W), tok(GV), tok(GV), tok(W_MG),
                  modv(2), modv(3), modv(4), modv(5),
                  _const_spec((1, D)), _const_spec((1, GLA_DV)),
                  _const_spec((NA_W, D)), _const_spec((GV, D)), _const_spec((D, D)),
                  _const_spec((D, D_FF)), _const_spec((D, D_FF)), _const_spec((D_FF, D))],
        out_specs=tok(D),
        out_shape=jax.ShapeDtypeStruct((B, L, D), f32),
        compiler_params=pltpu.CompilerParams(
            dimension_semantics=("parallel", "parallel"), vmem_limit_bytes=VMEM_LIMIT),
        name="merge_ffn",
    )(x, ona, of, ob, mg, mod3, mod3, mod3, mod3, norm2_g, gg, wna, wgl, wo, wg, wu, wd)


def kernel(x, c, ctx, c_ctx, w_mod, b_mod, norm1_g, norm2_g, w_in, na_q_norm_g, na_k_norm_g, na_rpb,
           gla_w_alpha, gla_b_alpha, gla_norm_g, w_branch_na, w_branch_gla, w_out,
           w_ffn_gate, w_ffn_up, w_ffn_down):
    B, L, _ = x.shape
    assert w_mod.shape[0] == 1 and L % (NA_R * GRID_W) == 0 and L % TM_PROJ == 0 and D_IN % W_IN_CAST_ROWS == 0

    w_t = _cast_bf16(jnp.transpose(w_in[0]), W_IN_CAST_ROWS)
    zero = jnp.zeros((GLA_RANK, GQK), f32)
    w_a = jnp.concatenate([jnp.concatenate([gla_w_alpha[0, 0], zero], axis=1),
                           jnp.concatenate([zero, gla_w_alpha[0, 1]], axis=1)], axis=0).astype(bf16)
    b_a = gla_b_alpha[0].reshape(1, 2 * GQK)
    gq_t = jnp.tile(na_q_norm_g[0] * (NA_DH ** -0.5 * LOG2E), NA_H).reshape(1, NA_W)
    gk_t = jnp.tile(na_k_norm_g[0], NA_H).reshape(1, NA_W)
    bd = jnp.asarray(np.kron(np.eye(4), np.full((NA_DH, NA_DH), 1.0 / NA_DH)), bf16)

    mod3 = _adaln_mod(c, c_ctx, w_mod[0], b_mod[0])
    t2 = _rpb_expand(na_rpb[0])

    kc, vc, s0f, s0b = _ctx_side(ctx, mod3, norm1_g, w_t, w_a, b_a, gk_t, bd, B)
    later_ws = [w_branch_na[0], w_branch_gla[0], w_out[0], w_ffn_gate[0], w_ffn_up[0], w_ffn_down[0]]
    na_qkv, gla_f, gla_b, gates, dec, ktf, ktb, *later_bf = _in_proj(
        x, mod3, norm1_g, w_t, gq_t, gk_t, bd, w_a, b_a, later_ws)
    ona = _na_attn(na_qkv, kc, vc, t2)
    of, ob = _gla_scan(gla_f, gla_b, ktf, ktb, dec, s0f, s0b)
    return _merge_ffn(x, ona, of, ob, gates, mod3, norm2_g, gla_norm_g, *later_bf)
```

```python
import functools

import numpy as np
import jax
import jax.numpy as jnp
from jax import lax
from jax.experimental import pallas as pl
from jax.experimental.pallas import tpu as pltpu

D = 1024
GRID_W = 64
CTX = 256
NA_H = 8
NA_DH = 64
NA_WR = 8
NA_WC = 16
GLA_H = 4
GLA_DK = 64
GLA_DV = 128
GLA_RANK = 16
GLA_TAU = 16.0
GLA_LA_MIN = -1.0
CHUNK = 64
ROPE_THETA = 10000.0
D_FF = 2816
EPS = 1e-6
NEG_INF = -1e30
NA_W = NA_H * NA_DH
GQK = GLA_H * GLA_DK
GV = GLA_H * GLA_DV

OFF_QNA, OFF_KNA, OFF_VNA = 0, 512, 1024
OFF_QG, OFF_KG, OFF_VG, OFF_OG = 1536, 1792, 2048, 2560
OFF_LR, OFF_MA, OFF_MB = 3072, 3104, 4128
N_LR = 2 * GLA_RANK
N_PROJ_OUT = 7
NA_Q, NA_K, NA_V, W_NA = 0, 512, 1024, 1536
G_V, G_Q, G_KI, W_GF = 0, 512, 768, 1024
W_GB = W_GF - GV
MG_A, MG_B, MG_OG, W_MG = 0, 1024, 2048, 2560
D_IN = 5152
LOG2E = 1.4426950408889634

LANES = 128
SUBLANES = 8
BF16_SUBLANES = 16
MXU_DIM = 256
NA_GH = 2
NA_GW = NA_GH * NA_DH
NA_NG = NA_H // NA_GH
VMEM_LIMIT = 56 * 1024 * 1024

W_IN_CAST_ROWS = 46 * BF16_SUBLANES
TM_PROJ = 512
PROJ_SUB = 256
TM_FFN = 512
FFN_SUB = 256
NA_R = 64
NA_RI = 4
GLA_CB = 16
GLA_HG = 2

f32 = jnp.float32
bf16 = jnp.bfloat16


def _const_spec(shape):
    nd = len(shape)
    return pl.BlockSpec(shape, lambda *_: (0,) * nd, pipeline_mode=pl.Buffered(1))


def _dot(a, b):
    return jnp.dot(a, b, preferred_element_type=f32)


def _dot_nt(a, b):
    return lax.dot_general(a, b, (((1,), (1,)), ((), ())), preferred_element_type=f32)


def _split2(x):
    hi = x.astype(bf16)
    lo = (x - hi.astype(f32)).astype(bf16)
    return hi, lo


def _sigmoid(x):
    return 0.5 * jnp.tanh(0.5 * x) + 0.5


def _silu(x):
    return x * _sigmoid(x)


def _split3_dot(r, oh):
    h1 = r.astype(bf16)
    r2 = r - h1.astype(f32)
    h2 = r2.astype(bf16)
    h3 = (r2 - h2.astype(f32)).astype(bf16)
    return _dot(h1, oh) + _dot(h2, oh) + _dot(h3, oh)


def _rpb_expand_kernel(r_ref, oh_ref, mask_ref, o_ref):
    tiles = _split3_dot(r_ref[...], oh_ref[...]) * LOG2E + mask_ref[...]
    for q in range(GRID_W):
        o_ref[:, q, :] = tiles[:o_ref.shape[0], q * LANES:(q + 1) * LANES]


def _rpb_expand(rpb):
    nd = 2 * NA_WR - 1
    nrow = NA_H * nd
    ncol = 2 * NA_WC - 1
    nxt = jnp.concatenate([rpb[:, 1:], jnp.zeros_like(rpb[:, :1])], axis=1)
    pad = ((0, LANES - nrow), (0, LANES - ncol))
    r01 = jnp.concatenate([jnp.pad(rpb.reshape(nrow, ncol), pad), jnp.pad(nxt.reshape(nrow, ncol), pad)], axis=1)
    q = np.arange(GRID_W)
    cs = np.clip(q - NA_WC // 2, 0, GRID_W - NA_WC)
    in_win = (q[None, :] >= cs[:, None]) & (q[None, :] < cs[:, None] + NA_WC)
    dc = np.clip(q[None, :] - q[:, None], -(NA_WC - 1), NA_WC - 1) + NA_WC - 1
    oh = np.zeros((2, LANES, GRID_W, LANES), np.float32)
    qq, kk = np.nonzero(in_win)
    oh[0, dc[qq, kk], qq, kk] = 1.0
    oh[1, dc[qq, kk], qq, kk + GRID_W] = 1.0
    oh = oh.reshape(2 * LANES, GRID_W * LANES)
    mask = np.where(np.concatenate([in_win, in_win], axis=1), 0.0, NEG_INF).reshape(1, GRID_W * LANES)
    out = pl.pallas_call(
        _rpb_expand_kernel,
        out_shape=jax.ShapeDtypeStruct((nrow, GRID_W, LANES), f32),
        name="rpb_expand",
    )(r01, jnp.asarray(oh, bf16), jnp.asarray(mask, f32))
    return out.reshape(NA_H, nd, GRID_W, LANES)


def _cast_kernel(x_ref, o_ref):
    o_ref[...] = x_ref[...].astype(bf16)


def _cast_bf16(w, rows):
    R, C = w.shape
    return pl.pallas_call(
        _cast_kernel,
        grid=(R // rows,),
        in_specs=[pl.BlockSpec((rows, C), lambda j: (j, 0))],
        out_specs=pl.BlockSpec((rows, C), lambda j: (j, 0)),
        out_shape=jax.ShapeDtypeStruct((R, C), bf16),
        name="cast_bf16",
    )(w)


def _mod_kernel(a_ref, w_ref, b_ref, o_ref):
    a = _silu(a_ref[...]).astype(bf16)
    o_ref[...] = (_dot(a, w_ref[...].astype(bf16)) + b_ref[...])[:, None, :]


def _adaln_mod(c, c_ctx, w_mod, b_mod):
    B = c.shape[0]
    assert B < SUBLANES
    a = jnp.concatenate([c, c_ctx[None], jnp.zeros((SUBLANES - B - 1, D), f32)], axis=0)
    n = w_mod.shape[1]
    return pl.pallas_call(
        _mod_kernel,
        grid=(n // D,),
        in_specs=[pl.BlockSpec((SUBLANES, D), lambda j: (0, 0)),
                  pl.BlockSpec((D, D), lambda j: (0, j)),
                  pl.BlockSpec((1, D), lambda j: (0, j))],
        out_specs=pl.BlockSpec((SUBLANES, 1, D), lambda j: (0, 0, j)),
        out_shape=jax.ShapeDtypeStruct((SUBLANES, 1, n), f32),
        name="adaln_mod",
    )(a, w_mod, b_mod.reshape(1, n))


def _norm_mod(x, g, sh, sc):
    ms = jnp.mean(x * x, axis=-1, keepdims=True)
    return (x * lax.rsqrt(ms + EPS)) * (g * (1.0 + sc)) + sh


def _head_rms(t, bd, g):
    sq = (t * t).astype(bf16)
    ms = jnp.concatenate([_dot(sq[:, :MXU_DIM], bd), _dot(sq[:, MXU_DIM:], bd)], axis=1)
    return t * lax.rsqrt(ms + EPS) * g


def _log_alpha(zl, wa_ref, ba_ref):
    z = _dot(zl.astype(bf16), wa_ref[...]) + ba_ref[...]
    soft = jnp.log2(1.0 + jnp.exp2(z * -LOG2E))
    return jnp.maximum(soft * (-1.0 / GLA_TAU), GLA_LA_MIN * LOG2E)


def _ctx_kernel(ctx_ref, sh_ref, sc_ref, g1_ref, wk_ref, wv_ref, wkg_ref, wvg_ref, wlr_ref,
                wa_ref, ba_ref, gk_ref, bd_ref, su_ref, sl_ref,
                kc_ref, vc_ref, sf_ref, sb_ref):
    h = _norm_mod(ctx_ref[0], g1_ref[...], sh_ref[0], sc_ref[0]).astype(bf16)
    k = _dot_nt(h, wk_ref[...])
    kc_ref[0] = _head_rms(k, bd_ref[...], gk_ref[...]).astype(bf16)
    vc_ref[0] = _dot_nt(h, wv_ref[...]).astype(bf16)
    kg = _dot_nt(h, wkg_ref[...])
    vg = _dot_nt(h, wvg_ref[...]).astype(bf16)
    la = _log_alpha(_dot_nt(h, wlr_ref[...]), wa_ref, ba_ref)
    hi, lo = _split2(la)
    ef = _dot(su_ref[...], hi[:, :GQK]) + _dot(su_ref[...], lo[:, :GQK])
    eb = _dot(sl_ref[...], hi[:, GQK:]) + _dot(sl_ref[...], lo[:, GQK:])
    r = lax.broadcasted_iota(jnp.int32, (GQK, GV), 0) // GLA_DK
    cc = lax.broadcasted_iota(jnp.int32, (GQK, GV), 1) // GLA_DV
    diag = r == cc
    for e, out in ((ef, sf_ref), (eb, sb_ref)):
        kw = (kg * jnp.exp2(e)).astype(bf16)
        u = _dot(kw.T, vg)
        out[0] = jnp.where(diag, u, 0.0)


def _ctx_side(ctx, mod3, norm1_g, w_t, w_a, b_a, gk_t, bd, n_b):
    su = jnp.asarray(np.triu(np.ones((CTX, CTX), np.float32), 1), bf16)
    sl = jnp.asarray(np.tril(np.ones((CTX, CTX), np.float32), -1), bf16)
    B = ctx.shape[0]
    cst = lambda shape, idx: pl.BlockSpec(shape, lambda b: idx)
    return pl.pallas_call(
        _ctx_kernel,
        grid=(B,),
        in_specs=[pl.BlockSpec((1, CTX, D), lambda b: (b, 0, 0)),
                  cst((1, 1, D), (n_b, 0, 0)), cst((1, 1, D), (n_b, 0, 1)),
                  cst((1, D), (0, 0)),
                  cst((NA_W, D), (OFF_KNA // NA_W, 0)), cst((NA_W, D), (OFF_VNA // NA_W, 0)),
                  cst((GQK, D), (OFF_KG // GQK, 0)), cst((GV, D), (OFF_VG // GV, 0)),
                  cst((N_LR, D), (OFF_LR // N_LR, 0)),
                  cst((N_LR, 2 * GQK), (0, 0)), cst((1, 2 * GQK), (0, 0)),
                  cst((1, NA_W), (0, 0)), cst((MXU_DIM, MXU_DIM), (0, 0)),
                  cst((CTX, CTX), (0, 0)), cst((CTX, CTX), (0, 0))],
        out_specs=[pl.BlockSpec((1, CTX, NA_W), lambda b: (b, 0, 0)),
                   pl.BlockSpec((1, CTX, NA_W), lambda b: (b, 0, 0)),
                   pl.BlockSpec((1, GQK, GV), lambda b: (b, 0, 0)),
                   pl.BlockSpec((1, GQK, GV), lambda b: (b, 0, 0))],
        out_shape=[jax.ShapeDtypeStruct((B, CTX, NA_W), bf16),
                   jax.ShapeDtypeStruct((B, CTX, NA_W), bf16),
                   jax.ShapeDtypeStruct((B, GQK, GV), f32),
                   jax.ShapeDtypeStruct((B, GQK, GV), f32)],
        compiler_params=pltpu.CompilerParams(vmem_limit_bytes=VMEM_LIMIT),
        name="ctx_side",
    )(ctx, mod3, mod3, norm1_g, w_t, w_t, w_t, w_t, w_t, w_a, b_a, gk_t, bd, su, sl)


def _proj_kernel(x_ref, sh_ref, sc_ref, g1_ref, w_ref, gq_ref, gk_ref, bd_ref,
                 rc_ref, rs_ref, cc_ref, cs_ref, wa_ref, ba_ref, tl_ref, *refs):
    n_cast = (len(refs) - N_PROJ_OUT) // 2
    cast_in, refs = refs[:n_cast], refs[n_cast:]
    na_ref, gf_ref, gb_ref, mg_ref, dec_ref, ktf_ref, ktb_ref = refs[:N_PROJ_OUT]
    cast_out = refs[N_PROJ_OUT:]
    for src, dst in zip(cast_in, cast_out):
        dst[...] = src[...].astype(bf16)

    bd = bd_ref[...]
    tl = tl_ref[...]
    nrow = PROJ_SUB // GRID_W
    nchunk = PROJ_SUB // CHUNK
    lane = lax.broadcasted_iota(jnp.int32, (PROJ_SUB, GQK), 1)
    first_half = (lane & 31) < 16

    for hs in range(0, x_ref.shape[1], PROJ_SUB):
        rows = slice(hs, hs + PROJ_SUB)
        hb = _norm_mod(x_ref[0, rows], g1_ref[...], sh_ref[0], sc_ref[0]).astype(bf16)

        def proj(off, n):
            return _dot_nt(hb, w_ref[off:off + n, :])

        zl = proj(OFF_LR, N_LR)
        q_raw = proj(OFF_QNA, NA_W)
        la = _log_alpha(zl, wa_ref, ba_ref)
        k_raw = proj(OFF_KNA, NA_W)
        hi, lo = _split2(la)
        na_ref[0, rows, NA_Q:NA_Q + NA_W] = _head_rms(q_raw, bd, gq_ref[...]).astype(bf16)
        cum = jnp.concatenate(
            [_dot(tl, hi[s:s + MXU_DIM]) + _dot(tl, lo[s:s + MXU_DIM]) for s in range(0, PROJ_SUB, MXU_DIM)], axis=0)
        na_ref[0, rows, NA_K:NA_K + NA_W] = _head_rms(k_raw, bd, gk_ref[...]).astype(bf16)
        qg_raw = proj(OFF_QG, GQK)
        kg_raw = proj(OFF_KG, GQK)
        na_ref[0, rows, NA_V:NA_V + NA_W] = proj(OFF_VNA, NA_W).astype(bf16)
        gf_ref[0, rows, G_V:G_V + GV] = proj(OFF_VG, GV).astype(bf16)

        gr = slice(hs // GRID_W, hs // GRID_W + nrow)
        cos_t = (rc_ref[gr][:, None, :] + cc_ref[...][None]).reshape(PROJ_SUB, GQK)
        sin_t = (rs_ref[gr][:, None, :] + cs_ref[...][None]).reshape(PROJ_SUB, GQK)

        def rope(t):
            partner = jnp.where(first_half, pltpu.roll(t, GQK - 16, 1), pltpu.roll(t, 16, 1))
            return t * cos_t + partner * sin_t

        qg = rope(qg_raw) * (GLA_DK ** -0.5)
        kg = rope(kg_raw)

        cum3 = cum.reshape(nchunk, CHUNK, 2 * GQK)
        tot3 = cum3[:, CHUNK - 1:CHUNK, :]
        la3 = la.reshape(nchunk, CHUNK, 2 * GQK)
        dec_ref[0, hs // CHUNK:hs // CHUNK + nchunk] = jnp.exp2(tot3).reshape(nchunk, 2 * GQK)
        b_f = cum3[:, :, :GQK].reshape(PROJ_SUB, GQK)
        e_f = (tot3 - cum3)[:, :, :GQK].reshape(PROJ_SUB, GQK)
        b_b = (tot3 - cum3 + la3)[:, :, GQK:].reshape(PROJ_SUB, GQK)
        e_b = (cum3 - la3)[:, :, GQK:].reshape(PROJ_SUB, GQK)
        mg_ref[0, rows, MG_OG:MG_OG + GV] = _silu(proj(OFF_OG, GV)).astype(bf16)
        gf_ref[0, rows, G_Q:G_Q + GQK] = (qg * jnp.exp2(b_f)).astype(bf16)
        gf_ref[0, rows, G_KI:G_KI + GQK] = (kg * jnp.exp2(-b_f)).astype(bf16)
        ktf_ref[0, :, rows] = (kg * jnp.exp2(e_f)).T.astype(bf16)
        mg_ref[0, rows, MG_A:MG_A + D] = _sigmoid(proj(OFF_MA, D)).astype(bf16)
        gb_ref[0, rows, G_Q - GV:G_Q - GV + GQK] = (qg * jnp.exp2(b_b)).astype(bf16)
        gb_ref[0, rows, G_KI - GV:G_KI - GV + GQK] = (kg * jnp.exp2(-b_b)).astype(bf16)
        ktb_ref[0, :, rows] = (kg * jnp.exp2(e_b)).T.astype(bf16)
        mg_ref[0, rows, MG_B:MG_B + D] = _sigmoid(proj(OFF_MB, D)).astype(bf16)


def _rope_tables(L):
    n = GLA_DK // 4
    freqs = ROPE_THETA ** (-np.arange(n, dtype=np.float64) / n)
    lane = np.arange(GQK)
    fl = freqs[lane % n]
    is_row = (lane % GLA_DK) < GLA_DK // 2
    sign = np.where((lane % 32) < 16, -1.0, 1.0)
    rows = np.arange(L // GRID_W)[:, None] * fl[None]
    cols = np.arange(GRID_W)[:, None] * fl[None]
    rc = np.where(is_row, np.cos(rows), 0.0)
    rs = np.where(is_row, np.sin(rows) * sign, 0.0)
    cc = np.where(~is_row, np.cos(cols), 0.0)
    cs = np.where(~is_row, np.sin(cols) * sign, 0.0)
    return [jnp.asarray(t, f32) for t in (rc, rs, cc, cs)]


def _in_proj(x, mod3, norm1_g, w_t, gq_t, gk_t, bd, w_a, b_a, cast_ws):
    B, L, _ = x.shape
    tm = TM_PROJ
    rows_per = tm // GRID_W
    nstep = L // tm
    total = B * nstep

    def slab(w):
        R, C = w.shape
        if R % (total * BF16_SUBLANES) == 0:
            return pl.BlockSpec((R // total, C), lambda b, i: (b * nstep + i, 0))
        assert (2 * R) % (total * BF16_SUBLANES) == 0 and C % (2 * LANES) == 0
        return pl.BlockSpec((2 * R // total, C // 2), lambda b, i: ((b * nstep + i) // 2, (b * nstep + i) % 2))
    rc, rs, cc, cs = _rope_tables(L)
    blk = np.kron(np.eye(MXU_DIM // CHUNK), np.tril(np.ones((CHUNK, CHUNK)))).astype(np.float32)
    tl = jnp.asarray(blk, bf16)
    tok = lambda n: pl.BlockSpec((1, tm, n), lambda b, i: (b, i, 0))
    sds = lambda n, dt: jax.ShapeDtypeStruct((B, L, n), dt)
    return pl.pallas_call(
        _proj_kernel,
        grid=(B, L // tm),
        in_specs=[tok(D),
                  pl.BlockSpec((1, 1, D), lambda b, i: (b, 0, 0)),
                  pl.BlockSpec((1, 1, D), lambda b, i: (b, 0, 1)),
                  _const_spec((1, D)), _const_spec((D_IN, D)),
                  _const_spec((1, NA_W)), _const_spec((1, NA_W)), _const_spec((MXU_DIM, MXU_DIM)),
                  pl.BlockSpec((rows_per, GQK), lambda b, i: (i, 0)),
                  pl.BlockSpec((rows_per, GQK), lambda b, i: (i, 0)),
                  _const_spec((GRID_W, GQK)), _const_spec((GRID_W, GQK)),
                  _const_spec((N_LR, 2 * GQK)), _const_spec((1, 2 * GQK)), _const_spec((MXU_DIM, MXU_DIM))]
                 + [slab(w) for w in cast_ws],
        out_specs=[tok(W_NA), tok(W_GF), tok(W_GB), tok(W_MG),
                   pl.BlockSpec((1, tm // CHUNK, 2 * GQK), lambda b, i: (b, i, 0)),
                   pl.BlockSpec((1, GQK, tm), lambda b, i: (b, 0, i)),
                   pl.BlockSpec((1, GQK, tm), lambda b, i: (b, 0, i))]
                  + [slab(w) for w in cast_ws],
        out_shape=[sds(W_NA, bf16), sds(W_GF, bf16), sds(W_GB, bf16), sds(W_MG, bf16),
                   jax.ShapeDtypeStruct((B, L // CHUNK, 2 * GQK), f32),
                   jax.ShapeDtypeStruct((B, GQK, L), bf16), jax.ShapeDtypeStruct((B, GQK, L), bf16)]
                  + [jax.ShapeDtypeStruct(w.shape, bf16) for w in cast_ws],
        compiler_params=pltpu.CompilerParams(
            dimension_semantics=("parallel", "parallel"), vmem_limit_bytes=VMEM_LIMIT),
        name="in_proj",
    )(x, mod3, mod3, norm1_g, w_t, gq_t, gk_t, bd, rc, rs, cc, cs, w_a, b_a, tl, *cast_ws)


def _lane_blocks(t):
    return [t[:, c:c + LANES] for c in range(0, t.shape[1], LANES)]


def _na_kernel(q_ref, k_ref, v_ref, kc_ref, vc_ref, t2_ref, o_ref, sl_ref, sc_ref, m_ref):
    i = pl.program_id(1)
    rows = k_ref.shape[1] // GRID_W
    lane_head = lax.broadcasted_iota(jnp.int32, (GRID_W, NA_GW), 1) // NA_DH
    hmask = [lane_head == h for h in range(NA_GH)]

    def window_start(r):
        rs = jnp.clip(r - NA_WR // 2, 0, rows - NA_WR)
        return rs, pl.multiple_of(rs * GRID_W, GRID_W)

    def produce(rr, bank, j):
        r = i * NA_R + rr
        rs, start = window_start(r)
        d0 = rs - r + (NA_WR - 1)
        row0 = pl.multiple_of(rr * GRID_W, GRID_W)
        for g in range(NA_NG):
            ls = slice(NA_GW * g, NA_GW * (g + 1))
            q = q_ref[0, pl.ds(row0, GRID_W), ls]
            qs = jnp.concatenate([jnp.where(m, q, jnp.zeros_like(q)) for m in hmask], axis=0)
            kw = k_ref[0, pl.ds(start, NA_WR * GRID_W), ls]
            s_raw = _dot_nt(qs, kw)
            s_ctx = _dot_nt(qs, kc_ref[0, :, ls])
            sc_ref[bank, j, g] = s_ctx
            for h in range(NA_GH):
                hr = slice(h * GRID_W, (h + 1) * GRID_W)
                mx = functools.reduce(jnp.maximum, _lane_blocks(s_ctx[hr]))
                for jj in range(NA_WR // 2):
                    cl = slice(jj * LANES, (jj + 1) * LANES)
                    blk = s_raw[hr, cl] + t2_ref[NA_GH * g + h, d0 + 2 * jj]
                    sl_ref[bank, j, g, hr, cl] = blk
                    mx = jnp.maximum(mx, blk)
                m_ref[bank, j, g, hr] = jnp.broadcast_to(jnp.max(mx, axis=-1, keepdims=True), (GRID_W, LANES))

    def consume(rr, bank, j):
        _, start = window_start(i * NA_R + rr)
        row0 = pl.multiple_of(rr * GRID_W, GRID_W)
        outs = []
        for g in range(NA_NG):
            ls = slice(NA_GW * g, NA_GW * (g + 1))
            m = m_ref[bank, j, g]
            p_loc = [jnp.exp2(sl_ref[bank, j, g, :, c:c + LANES] - m) for c in range(0, NA_WR * GRID_W, LANES)]
            p_ctx = [jnp.exp2(sc_ref[bank, j, g, :, c:c + LANES] - m) for c in range(0, CTX, LANES)]
            den = jnp.sum(functools.reduce(jnp.add, p_loc + p_ctx), axis=-1, keepdims=True)
            vw = v_ref[0, pl.ds(start, NA_WR * GRID_W), ls]
            acc = (_dot(jnp.concatenate(p_loc, axis=1).astype(bf16), vw)
                   + _dot(jnp.concatenate(p_ctx, axis=1).astype(bf16), vc_ref[0, :, ls]))
            acc = acc * (1.0 / den)
            o = jnp.zeros((GRID_W, NA_GW), f32)
            for h in range(NA_GH):
                o = o + jnp.where(hmask[h], acc[h * GRID_W:(h + 1) * GRID_W], 0.0)
            outs.append(o)
        o_ref[0, pl.ds(row0, GRID_W), :] = jnp.concatenate(outs, axis=1).astype(bf16)

    for j in range(NA_RI):
        produce(jnp.int32(j), 0, j)

    def step(t, carry):
        bank = t & 1
        for j in range(NA_RI):
            consume(t * NA_RI + j, bank, j)
        for j in range(NA_RI):
            produce(jnp.minimum((t + 1) * NA_RI + j, NA_R - 1), 1 - bank, j)
        return carry

    lax.fori_loop(0, NA_R // NA_RI, step, 0)


def _na_attn(qkv, kc, vc, t2):
    B, L, _ = qkv.shape
    nblk = L // (NA_R * GRID_W)
    return pl.pallas_call(
        _na_kernel,
        grid=(B, nblk),
        in_specs=[pl.BlockSpec((1, NA_R * GRID_W, NA_W), lambda b, i: (b, i, NA_Q // NA_W)),
                  pl.BlockSpec((1, L, NA_W), lambda b, i: (b, 0, NA_K // NA_W), pipeline_mode=pl.Buffered(1)),
                  pl.BlockSpec((1, L, NA_W), lambda b, i: (b, 0, NA_V // NA_W), pipeline_mode=pl.Buffered(1)),
                  pl.BlockSpec((1, CTX, NA_W), lambda b, i: (b, 0, 0)),
                  pl.BlockSpec((1, CTX, NA_W), lambda b, i: (b, 0, 0)),
                  _const_spec((NA_H, 2 * NA_WR - 1, GRID_W, LANES))],
        out_specs=pl.BlockSpec((1, NA_R * GRID_W, NA_W), lambda b, i: (b, i, 0)),
        out_shape=jax.ShapeDtypeStruct((B, L, NA_W), bf16),
        scratch_shapes=[pltpu.VMEM((2, NA_RI, NA_NG, NA_GH * GRID_W, NA_WR * GRID_W), f32),
                        pltpu.VMEM((2, NA_RI, NA_NG, NA_GH * GRID_W, CTX), f32),
                        pltpu.VMEM((2, NA_RI, NA_NG, NA_GH * GRID_W, LANES), f32)],
        compiler_params=pltpu.CompilerParams(
            dimension_semantics=("parallel", "parallel"), vmem_limit_bytes=VMEM_LIMIT),
        name="na_attn",
    )(qkv, qkv, qkv, kc, vc, t2)


def _gla_chunk(q, ki, v, ket, dcol, s_ref, tri, hmask):
    nh = len(hmask)
    qs = jnp.concatenate([jnp.where(m, q, jnp.zeros_like(q)) for m in hmask], axis=0)
    att = jnp.where(tri, _dot_nt(qs, ki), 0.0).astype(bf16)
    o = _dot(q, s_ref[...].astype(bf16))
    intra = [_dot(att[h * CHUNK:(h + 1) * CHUNK], v[:, h * GLA_DV:(h + 1) * GLA_DV]) for h in range(nh)]
    o = o + jnp.concatenate(intra, axis=1)
    for h in range(nh):
        rs = slice(h * GLA_DK, (h + 1) * GLA_DK)
        cs = slice(h * GLA_DV, (h + 1) * GLA_DV)
        s_ref[rs, cs] = dcol[rs] * s_ref[rs, cs] + _dot(ket[rs], v[:, cs])
    return o


def _gla_kernel(gf_ref, ktf_ref, decf_ref, gb_ref, vb_ref, ktb_ref, decb_ref,
                s0f_ref, s0b_ref, of_ref, ob_ref, sf_ref, sb_ref):
    nbatch = gf_ref.shape[0]
    nh = GLA_H // GLA_HG
    wq, wv = nh * GLA_DK, nh * GLA_DV

    @pl.when(pl.program_id(0) == 0)
    def _():
        for b in range(nbatch):
            for p in range(GLA_HG):
                sf_ref[b, p] = s0f_ref[b, p * wq:(p + 1) * wq, p * wv:(p + 1) * wv]
                sb_ref[b, p] = s0b_ref[b, p * wq:(p + 1) * wq, p * wv:(p + 1) * wv]

    lane_head = lax.broadcasted_iota(jnp.int32, (CHUNK, wq), 1) // GLA_DK
    hmask = [lane_head == h for h in range(nh)]
    ri = lax.broadcasted_iota(jnp.int32, (nh * CHUNK, CHUNK), 0) & (CHUNK - 1)
    ci = lax.broadcasted_iota(jnp.int32, (nh * CHUNK, CHUNK), 1)
    tri_f = ri >= ci
    tri_b = ri <= ci
    dect_f = [decf_ref[b][:, :GQK].T for b in range(nbatch)]
    dect_b = [decb_ref[b][:, GQK:].T for b in range(nbatch)]

    def k_end(kt_ref, b, c):
        kt = kt_ref[b, :, (c // 2) * 2 * CHUNK:(c // 2 + 1) * 2 * CHUNK]
        if c % 2:
            kt = pltpu.roll(kt, CHUNK, 1)
        return kt[:, :CHUNK]

    for c in range(GLA_CB):
        ts = slice(c * CHUNK, (c + 1) * CHUNK)
        cb = GLA_CB - 1 - c
        tb = slice(cb * CHUNK, (cb + 1) * CHUNK)
        for b in range(nbatch):
            ket_f, ket_b = k_end(ktf_ref, b, c), k_end(ktb_ref, b, cb)
            for p in range(GLA_HG):
                ql = slice(p * wq, (p + 1) * wq)
                vl = slice(p * wv, (p + 1) * wv)
                of_ref[b, ts, vl] = _gla_chunk(
                    gf_ref[b, ts, G_Q + p * wq:G_Q + (p + 1) * wq], gf_ref[b, ts, G_KI + p * wq:G_KI + (p + 1) * wq],
                    gf_ref[b, ts, G_V + p * wv:G_V + (p + 1) * wv], ket_f[ql],
                    dect_f[b][ql, c:c + 1], sf_ref.at[b, p], tri_f, hmask)
                ob_ref[b, tb, vl] = _gla_chunk(
                    gb_ref[b, tb, G_Q - GV + p * wq:G_Q - GV + (p + 1) * wq],
                    gb_ref[b, tb, G_KI - GV + p * wq:G_KI - GV + (p + 1) * wq],
                    vb_ref[b, tb, vl], ket_b[ql],
                    dect_b[b][ql, cb:cb + 1], sb_ref.at[b, p], tri_b, hmask)


def _gla_scan(gf, gb, ktf, ktb, dec, s0f, s0b):
    B, L, _ = gf.shape
    tm = GLA_CB * CHUNK
    nb = L // tm
    fw = lambda n: pl.BlockSpec((B, tm, n), lambda i: (0, i, 0))
    bw = lambda n: pl.BlockSpec((B, tm, n), lambda i: (0, nb - 1 - i, 0))
    st = pl.BlockSpec((B, GQK, GV), lambda i: (0, 0, 0))
    return pl.pallas_call(
        _gla_kernel,
        grid=(nb,),
        in_specs=[fw(W_GF),
                  pl.BlockSpec((B, GQK, tm), lambda i: (0, 0, i)),
                  pl.BlockSpec((B, GLA_CB, 2 * GQK), lambda i: (0, i, 0)),
                  bw(W_GB), bw(GV),
                  pl.BlockSpec((B, GQK, tm), lambda i: (0, 0, nb - 1 - i)),
                  pl.BlockSpec((B, GLA_CB, 2 * GQK), lambda i: (0, nb - 1 - i, 0)),
                  st, st],
        out_specs=[fw(GV), bw(GV)],
        out_shape=[jax.ShapeDtypeStruct((B, L, GV), f32)] * 2,
        scratch_shapes=[pltpu.VMEM((B, GLA_HG, GQK // GLA_HG, GV // GLA_HG), f32)] * 2,
        compiler_params=pltpu.CompilerParams(
            dimension_semantics=("arbitrary",), vmem_limit_bytes=VMEM_LIMIT),
        name="gla_scan",
    )(gf, ktf, dec, gb, gf, ktb, dec, s0f, s0b)


def _merge_ffn_kernel(x_ref, ona_ref, of_ref, ob_ref, mg_ref,
                      g1_ref, sh2_ref, sc2_ref, g2_ref, n2_ref, gg_ref,
                      wna_ref, wgl_ref, wo_ref, wg_ref, wu_ref, wd_ref, o_ref):
    x1s, h2s = [], []
    for hs in range(0, x_ref.shape[1], FFN_SUB):
        rows = slice(hs, hs + FFN_SUB)
        og = of_ref[0, rows] + ob_ref[0, rows]
        parts = []
        for h in range(GLA_H):
            seg = og[:, h * GLA_DV:(h + 1) * GLA_DV]
            ms = jnp.mean(seg * seg, axis=-1, keepdims=True)
            parts.append(seg * lax.rsqrt(ms + EPS) * gg_ref[...])
        ogl = (jnp.concatenate(parts, axis=1) * mg_ref[0, rows, MG_OG:MG_OG + GV].astype(f32)).astype(bf16)
        y = (mg_ref[0, rows, MG_A:MG_A + D].astype(f32) * _dot(ona_ref[0, rows], wna_ref[...])
             + mg_ref[0, rows, MG_B:MG_B + D].astype(f32) * _dot(ogl, wgl_ref[...]))
        x1 = x_ref[0, rows] + g1_ref[0] * _dot(y.astype(bf16), wo_ref[...])
        x1s.append(x1)
        h2s.append(_norm_mod(x1, n2_ref[...], sh2_ref[0], sc2_ref[0]).astype(bf16))
    x1 = jnp.concatenate(x1s, axis=0)
    h2 = jnp.concatenate(h2s, axis=0)
    act = (_silu(_dot(h2, wg_ref[...])) * _dot(h2, wu_ref[...])).astype(bf16)
    o_ref[0] = x1 + g2_ref[0] * _dot(act, wd_ref[...])


def _merge_ffn(x, ona, of, ob, mg, mod3, norm2_g, gg, wna, wgl, wo, wg, wu, wd):
    B, L, _ = x.shape
    tm = TM_FFN
    tok = lambda n: pl.BlockSpec((1, tm, n), lambda b, i: (b, i, 0))
    modv = lambda j: pl.BlockSpec((1, 1, D), lambda b, i: (b, 0, j))
    return pl.pallas_call(
        _merge_ffn_kernel,
        grid=(B, L // tm),
        in_specs=[tok(D), tok(NA_W), tok(GV), tok(GV), tok(W_MG),
                  modv(2), modv(3), modv(4), modv(5),
                  _const_spec((1, D)), _const_spec((1, GLA_DV)),
                  _const_spec((NA_W, D)), _const_spec((GV, D)), _const_spec((D, D)),
                  _const_spec((D, D_FF)), _const_spec((D, D_FF)), _const_spec((D_FF, D))],
        out_specs=tok(D),
        out_shape=jax.ShapeDtypeStruct((B, L, D), f32),
        compiler_params=pltpu.CompilerParams(
            dimension_semantics=("parallel", "parallel"), vmem_limit_bytes=VMEM_LIMIT),
        name="merge_ffn",
    )(x, ona, of, ob, mg, mod3, mod3, mod3, mod3, norm2_g, gg, wna, wgl, wo, wg, wu, wd)


def kernel(x, c, ctx, c_ctx, w_mod, b_mod, norm1_g, norm2_g, w_in, na_q_norm_g, na_k_norm_g, na_rpb,
           gla_w_alpha, gla_b_alpha, gla_norm_g, w_branch_na, w_branch_gla, w_out,
           w_ffn_gate, w_ffn_up, w_ffn_down):
    B, L, _ = x.shape
    assert w_mod.shape[0] == 1 and L % (NA_R * GRID_W) == 0 and L % TM_PROJ == 0 and D_IN % W_IN_CAST_ROWS == 0

    w_t = _cast_bf16(jnp.transpose(w_in[0]), W_IN_CAST_ROWS)
    zero = jnp.zeros((GLA_RANK, GQK), f32)
    w_a = jnp.concatenate([jnp.concatenate([gla_w_alpha[0, 0], zero], axis=1),
                           jnp.concatenate([zero, gla_w_alpha[0, 1]], axis=1)], axis=0).astype(bf16)
    b_a = gla_b_alpha[0].reshape(1, 2 * GQK)
    gq_t = jnp.tile(na_q_norm_g[0] * (NA_DH ** -0.5 * LOG2E), NA_H).reshape(1, NA_W)
    gk_t = jnp.tile(na_k_norm_g[0], NA_H).reshape(1, NA_W)
    bd = jnp.asarray(np.kron(np.eye(4), np.full((NA_DH, NA_DH), 1.0 / NA_DH)), bf16)

    mod3 = _adaln_mod(c, c_ctx, w_mod[0], b_mod[0])
    t2 = _rpb_expand(na_rpb[0])

    kc, vc, s0f, s0b = _ctx_side(ctx, mod3, norm1_g, w_t, w_a, b_a, gk_t, bd, B)
    later_ws = [w_branch_na[0], w_branch_gla[0], w_out[0], w_ffn_gate[0], w_ffn_up[0], w_ffn_down[0]]
    na_qkv, gla_f, gla_b, gates, dec, ktf, ktb, *later_bf = _in_proj(
        x, mod3, norm1_g, w_t, gq_t, gk_t, bd, w_a, b_a, later_ws)
    ona = _na_attn(na_qkv, kc, vc, t2)
    of, ob = _gla_scan(gla_f, gla_b, ktf, ktb, dec, s0f, s0b)
    return _merge_ffn(x, ona, of, ob, gates, mod3, norm2_g, gla_norm_g, *later_bf)
```
